```python
import math
import jax, jax.numpy as jnp
from jax import lax
import numpy as np

D_MODEL = 1024
BATCH = 8
SEQ = 2048
DEPTH = 1
DEC_BATCH = 128
DEC_SEQ = 1
PAST_LEN = 16384
PAGE_SIZE = 128

GM_CHUNK = 128
GM_GROUPS = 4
GM_GROUP_DIM = 128
GM_WIDTH = GM_GROUPS * GM_GROUP_DIM
SSD_HEADS = 16
SSD_HEAD_DIM = 64
SSD_INNER = SSD_HEADS * SSD_HEAD_DIM
SSD_GROUPS = 2
SSD_STATE = 128
SSD_CONV = 4
SSD_CHUNK = 128
SSD_CONV_DIM = SSD_INNER + 2 * SSD_GROUPS * SSD_STATE
MEM_LEN = 256
XA_HEADS = 4
XA_HEAD_DIM = 128
XA_WIDTH = XA_HEADS * XA_HEAD_DIM
N_BRANCH = 3
D_FF = 4 * D_MODEL
EPS = 1e-6
IN_PROJ_DIM = 2 * GM_WIDTH + SSD_INNER + SSD_CONV_DIM + SSD_HEADS + XA_WIDTH + N_BRANCH * D_MODEL

kernel_name = "hybrid_gmlp_ssd_memxattn_decode_step"


def rmsnorm(x, w):
    xf = x.astype(jnp.float32)
    y = xf * lax.rsqrt(jnp.mean(xf * xf, axis=-1, keepdims=True) + EPS)
    return (y * w.astype(jnp.float32)).astype(x.dtype)


def layernorm(x, w, b):
    xf = x.astype(jnp.float32)
    mu = jnp.mean(xf, axis=-1, keepdims=True)
    xc = xf - mu
    y = xc * lax.rsqrt(jnp.mean(xc * xc, axis=-1, keepdims=True) + EPS)
    return (y * w.astype(jnp.float32) + b.astype(jnp.float32)).astype(x.dtype)


def pad_seq(t, mult):
    n_pad = (-t.shape[1]) % mult
    return jnp.pad(t, [(0, 0), (0, n_pad)] + [(0, 0)] * (t.ndim - 2))


def split_points():
    sizes = (2 * GM_WIDTH, SSD_INNER, SSD_CONV_DIM, SSD_HEADS, XA_WIDTH, N_BRANCH * D_MODEL)
    pts, acc = [], 0
    for s in sizes[:-1]:
        acc += s
        pts.append(acc)
    return pts


def gmlp_mix(uv, ln_w, ln_b, w_s, b_s):
    u, v = jnp.split(uv, 2, axis=-1)
    v = layernorm(v, ln_w, ln_b)
    b, L, _ = v.shape
    vc = pad_seq(v, GM_CHUNK)
    n_c = vc.shape[1] // GM_CHUNK
    vc = vc.reshape(b, n_c, GM_CHUNK, GM_GROUPS, GM_GROUP_DIM)
    causal = jnp.tril(jnp.ones((GM_CHUNK, GM_CHUNK), dtype=bool))
    ws = jnp.where(causal[None], w_s, 0).astype(v.dtype)
    mixed = jnp.einsum('gts,bcsgd->bctgd', ws, vc) + b_s.T.astype(v.dtype)[None, None, :, :, None]
    mixed = mixed.reshape(b, n_c * GM_CHUNK, GM_WIDTH)[:, :L]
    return u * mixed, v


def causal_conv(prev, u, w, bias):
    full = jnp.concatenate([prev.astype(u.dtype), u], axis=1)
    L = u.shape[1]
    out = full[:, 0:L] * w[0]
    for k in range(1, SSD_CONV):
        out = out + full[:, k:k + L] * w[k]
    return out + bias, full[:, L:]


def ssd_chunked(x, dt, A, bm, cm, h0):
    b, L = x.shape[:2]
    E = SSD_HEADS // SSD_GROUPS
    Q = SSD_CHUNK
    x, dt, bm, cm = (pad_seq(t, Q) for t in (x, dt, bm, cm))
    n_c = x.shape[1] // Q
    x = x.reshape(b, n_c, Q, SSD_GROUPS, E, SSD_HEAD_DIM)
    dt = dt.reshape(b, n_c, Q, SSD_GROUPS, E)
    bm = bm.reshape(b, n_c, Q, SSD_GROUPS, SSD_STATE)
    cm = cm.reshape(b, n_c, Q, SSD_GROUPS, SSD_STATE)
    acum = jnp.cumsum(dt * A.reshape(SSD_GROUPS, E), axis=2)
    xdt = x * dt[..., None]
    seg = acum[:, :, :, None] - acum[:, :, None, :]
    causal = jnp.tril(jnp.ones((Q, Q), dtype=bool))[:, :, None, None]
    decay = jnp.exp(jnp.where(causal, seg, -jnp.inf))
    cb = jnp.einsum('bcign,bcjgn->bcijg', cm, bm)
    y_diag = jnp.einsum('bcijg,bcijge,bcjgep->bcigep', cb, decay, xdt)
    decay_end = jnp.exp(acum[:, :, -1:] - acum)
    states = jnp.einsum('bcjgn,bcjge,bcjgep->bcgepn', bm, decay_end, xdt)
    chunk_decay = jnp.exp(acum[:, :, -1])

    def step(h, inp):
        s, d = inp
        return h * d[..., None, None] + s, h

    h_init = h0.reshape(b, SSD_GROUPS, E, SSD_HEAD_DIM, SSD_STATE)
    h_final, h_prev = lax.scan(step, h_init, (jnp.moveaxis(states, 1, 0), jnp.moveaxis(chunk_decay, 1, 0)))
    h_prev = jnp.moveaxis(h_prev, 0, 1)
    y_off = jnp.einsum('bcign,bcgepn,bcige->bcigep', cm, h_prev, jnp.exp(acum))
    y = (y_diag + y_off).reshape(b, n_c * Q, SSD_HEADS, SSD_HEAD_DIM)[:, :L]
    return y, h_final.reshape(b, SSD_HEADS, SSD_HEAD_DIM, SSD_STATE)


def mem_kv(mem, mem_norm_w, w_k, w_v):
    b = mem.shape[0]
    mn = rmsnorm(mem, mem_norm_w)
    k = (mn @ w_k).reshape(b, MEM_LEN, XA_HEADS, XA_HEAD_DIM)
    v = (mn @ w_v).reshape(b, MEM_LEN, XA_HEADS, XA_HEAD_DIM)
    return k, v


def mem_attend(q, k, v):
    s = jnp.einsum('blhd,bmhd->bhlm', q, k).astype(jnp.float32) * (XA_HEAD_DIM ** -0.5)
    p = jax.nn.softmax(s, axis=-1).astype(v.dtype)
    o = jnp.einsum('bhlm,bmhd->blhd', p, v)
    return o.reshape(q.shape[0], q.shape[1], XA_WIDTH)


def decoder_block(h, conv_prev, ssm_prev, mem_k, mem_v,
                  norm_mix_w, w_in, gm_ln_w, gm_ln_b, gm_ws, gm_bs,
                  conv_w, conv_b, dt_bias, a_log, d_skip, ssd_norm_w,
                  w_br_gm, w_br_ssd, w_br_xa, w_out, norm_ffn_w, w_up, w_down):
    b, L, _ = h.shape
    f32 = jnp.float32
    xn = rmsnorm(h, norm_mix_w)
    proj = xn @ w_in
    gm_uv, z, xbc, dt_raw, q, gate_raw = jnp.split(proj, split_points(), axis=-1)
    gm_out, gm_v = gmlp_mix(jax.nn.gelu(gm_uv, approximate=False), gm_ln_w, gm_ln_b, gm_ws, gm_bs)
    xbc, conv_new = causal_conv(conv_prev, xbc, conv_w, conv_b)
    xbc = jax.nn.silu(xbc)
    xs, bm, cm = jnp.split(xbc, [SSD_INNER, SSD_INNER + SSD_GROUPS * SSD_STATE], axis=-1)
    xs_h = xs.reshape(b, L, SSD_HEADS, SSD_HEAD_DIM).astype(f32)
    dt = jax.nn.softplus(dt_raw.astype(f32) + dt_bias.astype(f32))
    A = -jnp.exp(a_log.astype(f32))
    y, ssm_new = ssd_chunked(xs_h, dt, A,
                             bm.reshape(b, L, SSD_GROUPS, SSD_STATE).astype(f32),
                             cm.reshape(b, L, SSD_GROUPS, SSD_STATE).astype(f32),
                             ssm_prev.astype(f32))
    y = y + xs_h * d_skip.astype(f32)[:, None]
    y = y.reshape(b, L, SSD_INNER).astype(h.dtype) * jax.nn.silu(z)
    y = rmsnorm(y.reshape(b, L, SSD_GROUPS, SSD_INNER // SSD_GROUPS),
                ssd_norm_w.reshape(SSD_GROUPS, SSD_INNER // SSD_GROUPS)).reshape(b, L, SSD_INNER)
    xa = mem_attend(q.reshape(b, L, XA_HEADS, XA_HEAD_DIM), mem_k, mem_v)
    gates = jax.nn.sigmoid(gate_raw.astype(f32)).astype(h.dtype).reshape(b, L, N_BRANCH, D_MODEL)
    merged = (gates[:, :, 0] * (gm_out @ w_br_gm)
              + gates[:, :, 1] * (y @ w_br_ssd)
              + gates[:, :, 2] * (xa @ w_br_xa))
    h = h + merged @ w_out
    hn = rmsnorm(h, norm_ffn_w)
    h = h + jnp.square(jax.nn.relu(hn @ w_up)) @ w_down
    return h, conv_new, ssm_new, gm_v


def setup_inputs(seed: int = 0) -> dict:
    key = jax.random.key(seed)
    ks = jax.random.split(key, 40)
    f32 = jnp.float32

    def nrm(k, shape, scale):
        return jax.random.normal(k, shape, f32) * scale

    Ld = DEPTH
    dt0 = jnp.exp(jax.random.uniform(ks[12], (Ld, SSD_HEADS), f32, math.log(1e-3), math.log(1e-1)))
    return {
        "x_prompt": nrm(ks[0], (BATCH, SEQ, D_MODEL), 1.0),
        "x_sample": nrm(ks[1], (DEC_BATCH, DEC_SEQ, D_MODEL), 1.0),
        "mem_prompt": nrm(ks[2], (BATCH, MEM_LEN, D_MODEL), 1.0),
        "cache_mem_k": nrm(ks[3], (Ld, DEC_BATCH, MEM_LEN, XA_HEADS, XA_HEAD_DIM), 1.0),
        "cache_mem_v": nrm(ks[4], (Ld, DEC_BATCH, MEM_LEN, XA_HEADS, XA_HEAD_DIM), 1.0),
        "state_conv": nrm(ks[5], (Ld, DEC_BATCH, SSD_CONV - 1, SSD_CONV_DIM), 1.0),
        "state_ssm": nrm(ks[6], (Ld, DEC_BATCH, SSD_HEADS, SSD_HEAD_DIM, SSD_STATE), 0.1),
        "norm_mix_w": 1.0 + nrm(ks[7], (Ld, D_MODEL), 0.02),
        "w_in": nrm(ks[8], (Ld, D_MODEL, IN_PROJ_DIM), D_MODEL ** -0.5),
        "gm_ln_w": 1.0 + nrm(ks[9], (Ld, GM_WIDTH), 0.02),
        "gm_ln_b": nrm(ks[10], (Ld, GM_WIDTH), 0.02),
        "gm_ws": nrm(ks[11], (Ld, GM_GROUPS, GM_CHUNK, GM_CHUNK), 0.5 * GM_CHUNK ** -0.5),
        "gm_bs": 1.0 + nrm(ks[13], (Ld, GM_GROUPS, GM_CHUNK), 0.02),
        "conv_w": nrm(ks[14], (Ld, SSD_CONV, SSD_CONV_DIM), SSD_CONV ** -0.5),
        "conv_b": nrm(ks[15], (Ld, SSD_CONV_DIM), 0.02),
        "dt_bias": dt0 + jnp.log(-jnp.expm1(-dt0)),
        "a_log": jnp.log(jax.random.uniform(ks[16], (Ld, SSD_HEADS), f32, 1.0, 16.0)),
        "d_skip": 1.0 + nrm(ks[17], (Ld, SSD_HEADS), 0.02),
        "ssd_norm_w": 1.0 + nrm(ks[18], (Ld, SSD_INNER), 0.02),
        "mem_norm_w": 1.0 + nrm(ks[19], (Ld, D_MODEL), 0.02),
        "w_mem_k": nrm(ks[20], (Ld, D_MODEL, XA_WIDTH), D_MODEL ** -0.5),
        "w_mem_v": nrm(ks[21], (Ld, D_MODEL, XA_WIDTH), D_MODEL ** -0.5),
        "w_br_gm": nrm(ks[22], (Ld, GM_WIDTH, D_MODEL), GM_WIDTH ** -0.5),
        "w_br_ssd": nrm(ks[23], (Ld, SSD_INNER, D_MODEL), SSD_INNER ** -0.5),
        "w_br_xa": nrm(ks[24], (Ld, XA_WIDTH, D_MODEL), XA_WIDTH ** -0.5),
        "w_out": nrm(ks[25], (Ld, D_MODEL, D_MODEL), D_MODEL ** -0.5),
        "norm_ffn_w": 1.0 + nrm(ks[26], (Ld, D_MODEL), 0.02),
        "w_up": nrm(ks[27], (Ld, D_MODEL, D_FF), D_MODEL ** -0.5),
        "w_down": nrm(ks[28], (Ld, D_FF, D_MODEL), D_FF ** -0.5),
        "norm_final_w": 1.0 + nrm(ks[29], (D_MODEL,), 0.02),
    }


def reference(x_prompt, x_sample, mem_prompt, cache_mem_k, cache_mem_v, state_conv, state_ssm,
              norm_mix_w, w_in, gm_ln_w, gm_ln_b, gm_ws, gm_bs, conv_w, conv_b, dt_bias, a_log,
              d_skip, ssd_norm_w, mem_norm_w, w_mem_k, w_mem_v, w_br_gm, w_br_ssd, w_br_xa, w_out,
              norm_ffn_w, w_up, w_down, norm_final_w):
    hp, hs = x_prompt, x_sample
    mk_p, mv_p, cv_p, ss_p, cv_s, ss_s, gv_s = [], [], [], [], [], [], []
    for l in range(DEPTH):
        layer_w = (norm_mix_w[l], w_in[l], gm_ln_w[l], gm_ln_b[l], gm_ws[l], gm_bs[l],
                   conv_w[l], conv_b[l], dt_bias[l], a_log[l], d_skip[l], ssd_norm_w[l],
                   w_br_gm[l], w_br_ssd[l], w_br_xa[l], w_out[l], norm_ffn_w[l], w_up[l], w_down[l])
        mk, mv = mem_kv(mem_prompt, mem_norm_w[l], w_mem_k[l], w_mem_v[l])
        conv0 = jnp.zeros((hp.shape[0], SSD_CONV - 1, SSD_CONV_DIM), hp.dtype)
        ssm0 = jnp.zeros((hp.shape[0], SSD_HEADS, SSD_HEAD_DIM, SSD_STATE), jnp.float32)
        hp, cvp, ssp, _ = decoder_block(hp, conv0, ssm0, mk, mv, *layer_w)
        hs, cvs, sss, gvs = decoder_block(hs, state_conv[l], state_ssm[l],
                                          cache_mem_k[l], cache_mem_v[l], *layer_w)
        mk_p.append(mk); mv_p.append(mv); cv_p.append(cvp); ss_p.append(ssp)
        cv_s.append(cvs); ss_s.append(sss); gv_s.append(gvs)
    y_prompt = rmsnorm(hp, norm_final_w)
    y_sample = rmsnorm(hs, norm_final_w)
    mem_k_prompt = jnp.stack(mk_p)
    mem_v_prompt = jnp.stack(mv_p)
    conv_prompt = jnp.stack(cv_p)
    ssm_prompt = jnp.stack(ss_p)
    conv_sample = jnp.stack(cv_s)
    ssm_sample = jnp.stack(ss_s)
    gmlp_v_sample = jnp.stack(gv_s)
    return (y_prompt, y_sample, mem_k_prompt, mem_v_prompt, conv_prompt, ssm_prompt,
            conv_sample, ssm_sample, gmlp_v_sample)
```

```python
import functools

import jax
import jax.numpy as jnp
from jax import lax
from jax.experimental import pallas as pl
from jax.experimental.pallas import tpu as pltpu

F32 = jnp.float32
BF16 = jnp.bfloat16

D_MODEL = 1024
GM_CHUNK = 128
GM_GROUPS = 4
GM_GROUP_DIM = 128
GM_WIDTH = GM_GROUPS * GM_GROUP_DIM
SSD_HEADS = 16
SSD_HEAD_DIM = 64
SSD_INNER = SSD_HEADS * SSD_HEAD_DIM
SSD_GROUPS = 2
SSD_GROUP_WIDTH = SSD_INNER // SSD_GROUPS
SSD_STATE = 128
SSD_CONV = 4
SSD_CHUNK = 128
SSD_CONV_DIM = SSD_INNER + 2 * SSD_GROUPS * SSD_STATE
MEM_LEN = 256
XA_HEADS = 4
XA_HEAD_DIM = 128
XA_WIDTH = XA_HEADS * XA_HEAD_DIM
N_BRANCH = 3
D_FF = 4 * D_MODEL
EPS = 1e-6

SUBLANES = 8
LANES = 128
HEADS_PER_LANE_TILE = LANES // SSD_HEAD_DIM
HEAD_PAIRS = SSD_HEADS // HEADS_PER_LANE_TILE
PAIRS_PER_GROUP = HEAD_PAIRS // SSD_GROUPS

DT_PAD = LANES
OFF_UV = 0
OFF_Z = OFF_UV + 2 * GM_WIDTH
OFF_XBC = OFF_Z + SSD_INNER
OFF_DT = OFF_XBC + SSD_CONV_DIM
OFF_Q = OFF_DT + DT_PAD
OFF_GATE = OFF_Q + XA_WIDTH
IN_PROJ_PACKED = OFF_GATE + N_BRANCH * D_MODEL

MIXER_ROWS = 256
FFN_ROWS = 512
FFN_COLS = 1024
MEMKV_ROWS = 512
SAMPLE_BLOCK = SUBLANES
SAMPLE_PROJ_COLS = IN_PROJ_PACKED // 3
VMEM_LIMIT_BYTES = 56 * 1024 * 1024


def _dot(a, b):
    return jnp.dot(a, b, preferred_element_type=F32)


def _dot_nt(a, b):
    return lax.dot_general(a, b, (((1,), (1,)), ((), ())), preferred_element_type=F32)


def _dot_tn(a, b):
    return lax.dot_general(a, b, (((0,), (0,)), ((), ())), preferred_element_type=F32)


def _rms(x, w):
    return x * lax.rsqrt(jnp.mean(x * x, axis=-1, keepdims=True) + EPS) * w


def _layernorm(x, w, b):
    xc = x - jnp.mean(x, axis=-1, keepdims=True)
    return xc * lax.rsqrt(jnp.mean(xc * xc, axis=-1, keepdims=True) + EPS) * w + b


def _gelu(x):
    return 0.5 * x * (1.0 + lax.erf(x * 0.7071067811865476))


def _silu(x):
    return x * jax.nn.sigmoid(x)


def _softplus(x):
    return jnp.maximum(x, 0.0) + jnp.log1p(jnp.exp(-jnp.abs(x)))


def _split3(x):
    p1 = x.astype(BF16)
    r1 = x - p1.astype(F32)
    p2 = r1.astype(BF16)
    p3 = (r1 - p2.astype(F32)).astype(BF16)
    return p1, p2, p3


def _group_rms(y, w):
    parts = []
    for g in range(SSD_GROUPS):
        cols = slice(g * SSD_GROUP_WIDTH, (g + 1) * SSD_GROUP_WIDTH)
        parts.append(_rms(y[:, cols], w[:, cols]))
    return jnp.concatenate(parts, axis=-1)


def _memkv_kernel(mem_ref, nw_ref, wkv_ref, k_ref, v_ref):
    mn = _rms(mem_ref[...], nw_ref[...]).astype(BF16)
    kv = _dot(mn, wkv_ref[...])
    k_ref[...] = kv[:, :XA_WIDTH]
    v_ref[...] = kv[:, XA_WIDTH:]


def _memkv(mem2d, norm_w, wkv):
    rows = mem2d.shape[0]
    tm = min(MEMKV_ROWS, rows)
    return pl.pallas_call(
        _memkv_kernel,
        grid=(rows // tm,),
        in_specs=[
            pl.BlockSpec((tm, D_MODEL), lambda i: (i, 0)),
            pl.BlockSpec((1, D_MODEL), lambda i: (0, 0)),
            pl.BlockSpec((D_MODEL, 2 * XA_WIDTH), lambda i: (0, 0)),
        ],
        out_specs=[
            pl.BlockSpec((tm, XA_WIDTH), lambda i: (i, 0)),
            pl.BlockSpec((tm, XA_WIDTH), lambda i: (i, 0)),
        ],
        out_shape=[jax.ShapeDtypeStruct((rows, XA_WIDTH), F32)] * 2,
        compiler_params=pltpu.CompilerParams(
            dimension_semantics=("arbitrary",), vmem_limit_bytes=VMEM_LIMIT_BYTES),
        name="memkv",
    )(mem2d, norm_w, wkv)


def _ssd_chunk(rows, xs_s, bc_s, dt_s, neg_a, hT_s, y_s):
    q = SSD_CHUNK
    ii = lax.broadcasted_iota(jnp.int32, (q, q), 0)
    jj = lax.broadcasted_iota(jnp.int32, (q, q), 1)
    causal = ii >= jj
    low_half = lax.broadcasted_iota(jnp.int32, (q, LANES), 1) < SSD_HEAD_DIM
    low_half_row = lax.broadcasted_iota(jnp.int32, (1, LANES), 1) < SSD_HEAD_DIM

    dt = dt_s[rows, :]
    tri = causal.astype(BF16)
    d1, d2, d3 = _split3(dt * neg_a)
    acum = _dot(tri, d1) + _dot(tri, d2) + _dot(tri, d3)
    acum_t = acum.T
    dt_t = dt.T
    w_t = dt_t * jnp.exp(acum_t[:, q - 1:q] - acum_t)

    xs_bf = xs_s[rows, :].astype(BF16)
    for g in range(SSD_GROUPS):
        b_g = bc_s[rows, g * SSD_STATE:(g + 1) * SSD_STATE]
        c_g = bc_s[rows, (SSD_GROUPS + g) * SSD_STATE:(SSD_GROUPS + g + 1) * SSD_STATE]
        cb = _dot_nt(c_g.astype(BF16), b_g.astype(BF16))
        b_t = b_g.T
        for pr in range(PAIRS_PER_GROUP):
            pair = g * PAIRS_PER_GROUP + pr
            lanes = slice(pair * LANES, (pair + 1) * LANES)
            xs_pair = xs_bf[:, lanes]
            h_prev = hT_s[:, lanes]
            rhs = jnp.concatenate([xs_pair, h_prev.astype(BF16)], axis=0)
            ys, ss, cds = [], [], []
            for k in range(HEADS_PER_LANE_TILE):
                h = pair * HEADS_PER_LANE_TILE + k
                col = jnp.broadcast_to(acum[:, h:h + 1], (q, q))
                seg = jnp.where(causal, col - acum_t[h:h + 1, :], -jnp.inf)
                m_h = cb * jnp.exp(seg) * dt_t[h:h + 1, :]
                c_h = c_g * jnp.exp(col)
                lhs = jnp.concatenate([m_h.astype(BF16), c_h.astype(BF16)], axis=1)
                ys.append(_dot(lhs, rhs))
                ss.append(_dot((b_t * w_t[h:h + 1, :]).astype(BF16), xs_pair))
                cds.append(jnp.exp(col[q - 1:q, :]))
            y_s[rows, lanes] = jnp.where(low_half, ys[0], ys[1])
            decay = jnp.where(low_half_row, cds[0], cds[1])
            hT_s[:, lanes] = h_prev * decay + jnp.where(low_half, ss[0], ss[1])


def _mixer_kernel(x_ref, k_ref, v_ref, nw_ref, win_ref, lnw_ref, lnb_ref, ws_ref, bst_ref,
                  cw_ref, cb_ref, dtb_ref, alog_ref, dsk_ref, snw_ref,
                  wgm_ref, wssd_ref, wxa_ref, wout_ref,
                  h_ref, conv_ref, ssm_ref,
                  ext_s, hT_s, xs_s, bc_s, dt_s, y_s, u_s, v_s, gm_s):
    tm = x_ref.shape[1]
    t = pl.program_id(1)
    head = SUBLANES

    @pl.when(t == 0)
    def _():
        ext_s[0:head, :] = jnp.zeros((head, SSD_CONV_DIM), F32)
        hT_s[...] = jnp.zeros(hT_s.shape, F32)

    x = x_ref[0]
    xn = _rms(x, nw_ref[...]).astype(BF16)

    uv = _gelu(_dot(xn, win_ref[:, OFF_UV:OFF_Z]))
    u_s[...] = uv[:, :GM_WIDTH]
    v_s[...] = _layernorm(uv[:, GM_WIDTH:], lnw_ref[...], lnb_ref[...]).astype(BF16)

    ext_s[head:head + tm, :] = _dot(xn, win_ref[:, OFF_XBC:OFF_DT])
    acc = ext_s[head - 3:head - 3 + tm, :] * cw_ref[0:1, :]
    for k in range(1, SSD_CONV):
        acc = acc + ext_s[head - 3 + k:head - 3 + k + tm, :] * cw_ref[k:k + 1, :]
    xbc = _silu(acc + cb_ref[...])
    xs_s[...] = xbc[:, :SSD_INNER]
    bc_s[...] = xbc[:, SSD_INNER:]
    conv_ref[0] = ext_s[tm + head - 3:tm + head, :]
    ext_s[0:head, :] = ext_s[tm:tm + head, :]
    dt_s[...] = _softplus(_dot(xn, win_ref[:, OFF_DT:OFF_Q]) + dtb_ref[...])
    neg_a = -jnp.exp(alog_ref[...])

    for c in range(tm // SSD_CHUNK):
        rows = slice(c * SSD_CHUNK, (c + 1) * SSD_CHUNK)
        _ssd_chunk(rows, xs_s, bc_s, dt_s, neg_a, hT_s, y_s)

    ii = lax.broadcasted_iota(jnp.int32, (GM_CHUNK, GM_CHUNK), 0)
    jj = lax.broadcasted_iota(jnp.int32, (GM_CHUNK, GM_CHUNK), 1)
    for g in range(GM_GROUPS):
        cols = slice(g * GM_GROUP_DIM, (g + 1) * GM_GROUP_DIM)
        w_tril = jnp.where(ii >= jj, ws_ref[g], 0.0).astype(BF16)
        bias = bst_ref[:, g:g + 1]
        for c in range(tm // GM_CHUNK):
            rows = slice(c * GM_CHUNK, (c + 1) * GM_CHUNK)
            mixed = _dot(w_tril, v_s[rows, cols]) + bias
            gm_s[rows, cols] = (u_s[rows, cols] * mixed).astype(BF16)

    zs = _silu(_dot(xn, win_ref[:, OFF_Z:OFF_XBC]))
    yv = (y_s[...] + xs_s[...] * dsk_ref[...]) * zs
    yn = _group_rms(yv, snw_ref[...]).astype(BF16)

    qv = _dot(xn, win_ref[:, OFF_Q:OFF_GATE])
    outs = []
    for hd in range(XA_HEADS):
        cols = slice(hd * XA_HEAD_DIM, (hd + 1) * XA_HEAD_DIM)
        s = _dot_nt(qv[:, cols].astype(BF16), k_ref[0, :, cols].astype(BF16)) * (XA_HEAD_DIM ** -0.5)
        p = jnp.exp(s - jnp.max(s, axis=-1, keepdims=True))
        o = _dot(p.astype(BF16), v_ref[0, :, cols].astype(BF16))
        outs.append(o / jnp.sum(p, axis=-1, keepdims=True))
    xa = jnp.concatenate(outs, axis=-1).astype(BF16)

    merged = None
    for i, (br, w_ref) in enumerate(((gm_s[...], wgm_ref), (yn, wssd_ref), (xa, wxa_ref))):
        gate = jax.nn.sigmoid(_dot(xn, win_ref[:, OFF_GATE + i * D_MODEL:OFF_GATE + (i + 1) * D_MODEL]))
        term = gate * _dot(br, w_ref[...])
        merged = term if merged is None else merged + term
    h_ref[0] = x + _dot(merged.astype(BF16), wout_ref[...])

    @pl.when(t == pl.num_programs(1) - 1)
    def _():
        for pair in range(HEAD_PAIRS):
            lanes = slice(pair * LANES, (pair + 1) * LANES)
            ssm_ref[0, lanes, :] = hT_s[:, lanes].T


def _full(shape):
    return pl.BlockSpec(shape, lambda *_: (0,) * len(shape))


def _mixer(x, mem_k, mem_v, p):
    b, seq, _ = x.shape
    tm = MIXER_ROWS
    assert seq % tm == 0 and tm % SSD_CHUNK == 0
    small = [p["norm_mix_w"], p["w_in"], p["gm_ln_w"], p["gm_ln_b"], p["gm_ws"], p["gm_bs_t"],
             p["conv_w"], p["conv_b"], p["dt_bias"], p["a_log"], p["d_skip"], p["ssd_norm_w"],
             p["w_br_gm"], p["w_br_ssd"], p["w_br_xa"], p["w_out"]]
    return pl.pallas_call(
        _mixer_kernel,
        grid=(b, seq // tm),
        in_specs=[
            pl.BlockSpec((1, tm, D_MODEL), lambda i, j: (i, j, 0)),
            pl.BlockSpec((1, MEM_LEN, XA_WIDTH), lambda i, j: (i, 0, 0)),
            pl.BlockSpec((1, MEM_LEN, XA_WIDTH), lambda i, j: (i, 0, 0)),
        ] + [_full(a.shape) for a in small],
        out_specs=[
            pl.BlockSpec((1, tm, D_MODEL), lambda i, j: (i, j, 0)),
            pl.BlockSpec((1, SSD_CONV - 1, SSD_CONV_DIM), lambda i, j: (i, 0, 0)),
            pl.BlockSpec((1, SSD_INNER, SSD_STATE), lambda i, j: (i, 0, 0)),
        ],
        out_shape=[
            jax.ShapeDtypeStruct((b, seq, D_MODEL), F32),
            jax.ShapeDtypeStruct((b, SSD_CONV - 1, SSD_CONV_DIM), F32),
            jax.ShapeDtypeStruct((b, SSD_INNER, SSD_STATE), F32),
        ],
        scratch_shapes=[
            pltpu.VMEM((tm + SUBLANES, SSD_CONV_DIM), F32),
            pltpu.VMEM((SSD_STATE, SSD_INNER), F32),
            pltpu.VMEM((tm, SSD_INNER), F32),
            pltpu.VMEM((tm, 2 * SSD_GROUPS * SSD_STATE), F32),
            pltpu.VMEM((tm, DT_PAD), F32),
            pltpu.VMEM((tm, SSD_INNER), F32),
            pltpu.VMEM((tm, GM_WIDTH), F32),
            pltpu.VMEM((tm, GM_WIDTH), BF16),
            pltpu.VMEM((tm, GM_WIDTH), BF16),
        ],
        compiler_params=pltpu.CompilerParams(
            dimension_semantics=("arbitrary", "arbitrary"), vmem_limit_bytes=VMEM_LIMIT_BYTES),
        name="prompt_mixer",
    )(x, mem_k, mem_v, *small)


def _ffn_kernel(h_ref, nw_ref, wup_ref, wdn_ref, fw_ref, o_ref):
    h = h_ref[...]
    hn = _rms(h, nw_ref[...]).astype(BF16)
    acc = h
    for j in range(D_FF // FFN_COLS):
        cols = slice(j * FFN_COLS, (j + 1) * FFN_COLS)
        a = jnp.square(jnp.maximum(_dot(hn, wup_ref[:, cols]), 0.0)).astype(BF16)
        acc = acc + _dot(a, wdn_ref[cols, :])
    o_ref[...] = _rms(acc, fw_ref[...])


def _ffn(h2d, p):
    rows = h2d.shape[0]
    tm = min(FFN_ROWS, rows)
    assert rows % tm == 0
    return pl.pallas_call(
        _ffn_kernel,
        grid=(rows // tm,),
        in_specs=[
            pl.BlockSpec((tm, D_MODEL), lambda i: (i, 0)),
            _full((1, D_MODEL)),
            _full((D_MODEL, D_FF)),
            _full((D_FF, D_MODEL)),
            _full((1, D_MODEL)),
        ],
        out_specs=pl.BlockSpec((tm, D_MODEL), lambda i: (i, 0)),
        out_shape=jax.ShapeDtypeStruct((rows, D_MODEL), F32),
        compiler_params=pltpu.CompilerParams(
            dimension_semantics=("arbitrary",), vmem_limit_bytes=VMEM_LIMIT_BYTES),
        name="ffn",
    )(h2d, p["norm_ffn_w"], p["w_up"], p["w_down"], p["norm_final_w"])


def _sample_proj_kernel(x_ref, nw_ref, w_ref, o_ref):
    xn = _rms(x_ref[...], nw_ref[...]).astype(BF16)
    o_ref[...] = _dot(xn, w_ref[...])


def _sample_proj(x2d, p):
    rows = x2d.shape[0]
    tn = SAMPLE_PROJ_COLS
    return pl.pallas_call(
        _sample_proj_kernel,
        grid=(IN_PROJ_PACKED // tn,),
        in_specs=[
            _full((rows, D_MODEL)),
            _full((1, D_MODEL)),
            pl.BlockSpec((D_MODEL, tn), lambda j: (0, j)),
        ],
        out_specs=pl.BlockSpec((rows, tn), lambda j: (0, j)),
        out_shape=jax.ShapeDtypeStruct((rows, IN_PROJ_PACKED), F32),
        compiler_params=pltpu.CompilerParams(
            dimension_semantics=("arbitrary",), vmem_limit_bytes=VMEM_LIMIT_BYTES),
        name="sample_proj",
    )(x2d, p["norm_mix_w"], p["w_in"])


def _sample_state_kernel(proj_ref, stc_ref, ssm_ref, k_ref, v_ref, lnw_ref, lnb_ref, ws_ref,
                         bs_ref, cw_ref, cb_ref, dtb_ref, alog_ref, dsk_ref, snw_ref,
                         gm_ref, yn_ref, xa_ref, convn_ref, ssmn_ref, gv_ref):
    nb = proj_ref.shape[0]
    proj = proj_ref[...]

    uv = _gelu(proj[:, OFF_UV:OFF_Z])
    v = _layernorm(uv[:, GM_WIDTH:], lnw_ref[...], lnb_ref[...])
    gv_ref[...] = v
    for g in range(GM_GROUPS):
        cols = slice(g * GM_GROUP_DIM, (g + 1) * GM_GROUP_DIM)
        mixed = v[:, cols] * ws_ref[g, 0:1, 0:1] + bs_ref[g:g + 1, 0:1]
        gm_ref[:, cols] = uv[:, cols] * mixed

    st = stc_ref[...]
    xnew = proj[:, OFF_XBC:OFF_DT]
    acc = st[:, 0:SSD_CONV_DIM] * cw_ref[0:1, :]
    for k in range(1, SSD_CONV - 1):
        acc = acc + st[:, k * SSD_CONV_DIM:(k + 1) * SSD_CONV_DIM] * cw_ref[k:k + 1, :]
    acc = acc + xnew * cw_ref[SSD_CONV - 1:SSD_CONV, :]
    convn_ref[:, 0:(SSD_CONV - 2) * SSD_CONV_DIM] = st[:, SSD_CONV_DIM:]
    convn_ref[:, (SSD_CONV - 2) * SSD_CONV_DIM:] = xnew
    xbc = _silu(acc + cb_ref[...])
    xs = xbc[:, :SSD_INNER]
    bm = xbc[:, SSD_INNER:SSD_INNER + SSD_GROUPS * SSD_STATE].astype(BF16)
    cm = xbc[:, SSD_INNER + SSD_GROUPS * SSD_STATE:].astype(BF16)

    dt = _softplus(proj[:, OFF_DT:OFF_Q] + dtb_ref[...])
    decay = jnp.exp(dt * -jnp.exp(alog_ref[...]))
    hh = lax.broadcasted_iota(jnp.int32, (DT_PAD, SSD_INNER), 0)
    cc = lax.broadcasted_iota(jnp.int32, (DT_PAD, SSD_INNER), 1)
    spread = (jnp.right_shift(cc, SSD_HEAD_DIM.bit_length() - 1) == hh).astype(BF16)
    dt_wide = sum(_dot(piece, spread) for piece in _split3(dt))
    decay_wide = sum(_dot(piece, spread) for piece in _split3(decay))
    xdt = xs * dt_wide

    rid = lax.broadcasted_iota(jnp.int32, (nb, 1), 0)
    ones = jnp.ones((nb, SSD_STATE), BF16)
    scale = XA_HEAD_DIM ** -0.5
    qv = proj[:, OFF_Q:OFF_GATE]
    y = jnp.zeros((nb, SSD_INNER), F32)
    for bi in range(nb):
        mine = rid == bi
        e1, e2, e3 = _split3(jnp.where(mine, decay_wide, 0.0))
        e_col = _dot_tn(e1, ones) + _dot_tn(e2, ones) + _dot_tn(e3, ones)
        x_row = jnp.where(mine, xdt, 0.0).astype(BF16)
        parts = []
        for g in range(SSD_GROUPS):
            cols = slice(g * SSD_GROUP_WIDTH, (g + 1) * SSD_GROUP_WIDTH)
            ncols = slice(g * SSD_STATE, (g + 1) * SSD_STATE)
            upd = _dot_tn(x_row[:, cols], bm[:, ncols])
            h_new = ssm_ref[bi, cols, :] * e_col[cols, :] + upd
            ssmn_ref[bi, cols, :] = h_new
            parts.append(_dot_nt(cm[:, ncols], h_new.astype(BF16)))
        y = y + jnp.where(mine, jnp.concatenate(parts, axis=-1), 0.0)

        kq = k_ref[bi] * qv[bi:bi + 1, :]
        for hd in range(XA_HEADS):
            cols = slice(hd * XA_HEAD_DIM, (hd + 1) * XA_HEAD_DIM)
            s = jnp.sum(kq[:, cols], axis=-1, keepdims=True) * scale
            pexp = jnp.exp(s - jnp.max(s, axis=0, keepdims=True))
            o = jnp.sum(pexp * v_ref[bi, :, cols], axis=0, keepdims=True)
            xa_ref[bi:bi + 1, cols] = o / jnp.sum(pexp, axis=0, keepdims=True)

    zs = _silu(proj[:, OFF_Z:OFF_XBC])
    yn_ref[...] = _group_rms((y + xs * dsk_ref[...]) * zs, snw_ref[...])


def _sample_state(proj, state_conv2d, state_ssm, cache_k, cache_v, p):
    n = proj.shape[0]
    nb = SAMPLE_BLOCK
    assert n % nb == 0
    conv_w = (SSD_CONV - 1) * SSD_CONV_DIM
    small = [p["gm_ln_w"], p["gm_ln_b"], p["gm_ws"], p["gm_bs"], p["conv_w"], p["conv_b"],
             p["dt_bias"], p["a_log"], p["d_skip"], p["ssd_norm_w"]]
    rows2 = lambda w: pl.BlockSpec((nb, w), lambda i: (i, 0))
    rows3 = lambda a, c: pl.BlockSpec((nb, a, c), lambda i: (i, 0, 0))
    return pl.pallas_call(
        _sample_state_kernel,
        grid=(n // nb,),
        in_specs=[rows2(IN_PROJ_PACKED), rows2(conv_w), rows3(SSD_INNER, SSD_STATE),
                  rows3(MEM_LEN, XA_WIDTH), rows3(MEM_LEN, XA_WIDTH)]
                 + [_full(a.shape) for a in small],
        out_specs=[rows2(GM_WIDTH), rows2(SSD_INNER), rows2(XA_WIDTH), rows2(conv_w),
                   rows3(SSD_INNER, SSD_STATE), rows2(GM_WIDTH)],
        out_shape=[
            jax.ShapeDtypeStruct((n, GM_WIDTH), F32),
            jax.ShapeDtypeStruct((n, SSD_INNER), F32),
            jax.ShapeDtypeStruct((n, XA_WIDTH), F32),
            jax.ShapeDtypeStruct((n, conv_w), F32),
            jax.ShapeDtypeStruct((n, SSD_INNER, SSD_STATE), F32),
            jax.ShapeDtypeStruct((n, GM_WIDTH), F32),
        ],
        compiler_params=pltpu.CompilerParams(
            dimension_semantics=("arbitrary",), vmem_limit_bytes=VMEM_LIMIT_BYTES),
        name="sample_state",
    )(proj, state_conv2d, state_ssm, cache_k, cache_v, *small)


def _sample_merge_kernel(x_ref, proj_ref, gm_ref, yn_ref, xa_ref, wgm_ref, wssd_ref, wxa_ref,
                         wout_ref, h_ref):
    merged = None
    for i, (br_ref, w_ref) in enumerate(((gm_ref, wgm_ref), (yn_ref, wssd_ref), (xa_ref, wxa_ref))):
        gate = jax.nn.sigmoid(proj_ref[:, OFF_GATE + i * D_MODEL:OFF_GATE + (i + 1) * D_MODEL])
        term = gate * _dot(br_ref[...].astype(BF16), w_ref[...])
        merged = term if merged is None else merged + term
    h_ref[...] = x_ref[...] + _dot(merged.astype(BF16), wout_ref[...])


def _sample_merge(x2d, proj, gm, yn, xa, p):
    args = [x2d, proj, gm, yn, xa, p["w_br_gm"], p["w_br_ssd"], p["w_br_xa"], p["w_out"]]
    return pl.pallas_call(
        _sample_merge_kernel,
        grid=(1,),
        in_specs=[_full(a.shape) for a in args],
        out_specs=_full(x2d.shape),
        out_shape=jax.ShapeDtypeStruct(x2d.shape, F32),
        compiler_params=pltpu.CompilerParams(
            dimension_semantics=("arbitrary",), vmem_limit_bytes=VMEM_LIMIT_BYTES),
        name="sample_merge",
    )(*args)


def _prep_layer(l, norm_mix_w, w_in, gm_ln_w, gm_ln_b, gm_ws, gm_bs, conv_w, conv_b, dt_bias, a_log,
                d_skip, ssd_norm_w, mem_norm_w, w_mem_k, w_mem_v, w_br_gm, w_br_ssd, w_br_xa, w_out,
                norm_ffn_w, w_up, w_down, norm_final_w):
    row = lambda a: a.reshape(1, -1)
    pad_heads = lambda a: jnp.pad(a, (0, DT_PAD - SSD_HEADS)).reshape(1, DT_PAD)
    dt_lo = OFF_DT
    w = w_in[l]
    w_packed = jnp.concatenate(
        [w[:, :dt_lo + SSD_HEADS],
         jnp.zeros((D_MODEL, DT_PAD - SSD_HEADS), w.dtype),
         w[:, dt_lo + SSD_HEADS:]], axis=1).astype(BF16)
    return {
        "norm_mix_w": row(norm_mix_w[l]), "w_in": w_packed,
        "gm_ln_w": row(gm_ln_w[l]), "gm_ln_b": row(gm_ln_b[l]),
        "gm_ws": gm_ws[l], "gm_bs": gm_bs[l], "gm_bs_t": gm_bs[l].T,
        "conv_w": conv_w[l], "conv_b": row(conv_b[l]),
        "dt_bias": pad_heads(dt_bias[l]), "a_log": pad_heads(a_log[l]),
        "d_skip": row(jnp.repeat(d_skip[l], SSD_HEAD_DIM)), "ssd_norm_w": row(ssd_norm_w[l]),
        "mem_norm_w": row(mem_norm_w[l]),
        "w_kv": jnp.concatenate([w_mem_k[l], w_mem_v[l]], axis=1).astype(BF16),
        "w_br_gm": w_br_gm[l].astype(BF16), "w_br_ssd": w_br_ssd[l].astype(BF16),
        "w_br_xa": w_br_xa[l].astype(BF16), "w_out": w_out[l].astype(BF16),
        "norm_ffn_w": row(norm_ffn_w[l]), "w_up": w_up[l].astype(BF16),
        "w_down": w_down[l].astype(BF16), "norm_final_w": row(norm_final_w),
    }


def kernel(x_prompt, x_sample, mem_prompt, cache_mem_k, cache_mem_v, state_conv, state_ssm, norm_mix_w, w_in, gm_ln_w, gm_ln_b, gm_ws, gm_bs, conv_w, conv_b, dt_bias, a_log, d_skip, ssd_norm_w, mem_norm_w, w_mem_k, w_mem_v, w_br_gm, w_br_ssd, w_br_xa, w_out, norm_ffn_w, w_up, w_down, norm_final_w):
    depth = w_in.shape[0]
    assert depth == 1, "the final norm is fused into the MLP kernel of the last (only) layer"
    b, seq, _ = x_prompt.shape
    n, dec_seq, _ = x_sample.shape
    assert dec_seq == 1
    p = _prep_layer(0, norm_mix_w, w_in, gm_ln_w, gm_ln_b, gm_ws, gm_bs, conv_w, conv_b, dt_bias,
                    a_log, d_skip, ssd_norm_w, mem_norm_w, w_mem_k, w_mem_v, w_br_gm, w_br_ssd,
                    w_br_xa, w_out, norm_ffn_w, w_up, w_down, norm_final_w)

    mem_k, mem_v = _memkv(mem_prompt.reshape(b * MEM_LEN, D_MODEL), p["mem_norm_w"], p["w_kv"])
    h1, conv_p, ssm_p = _mixer(x_prompt, mem_k.reshape(b, MEM_LEN, XA_WIDTH),
                               mem_v.reshape(b, MEM_LEN, XA_WIDTH), p)
    y_prompt = _ffn(h1.reshape(b * seq, D_MODEL), p).reshape(b, seq, D_MODEL)

    xs2d = x_sample.reshape(n, D_MODEL)
    proj = _sample_proj(xs2d, p)
    gm, yn, xa, conv_s, ssm_s, gv = _sample_state(
        proj, state_conv[0].reshape(n, (SSD_CONV - 1) * SSD_CONV_DIM), state_ssm[0].reshape(n, SSD_INNER, SSD_STATE),
        cache_mem_k[0].reshape(n, MEM_LEN, XA_WIDTH), cache_mem_v[0].reshape(n, MEM_LEN, XA_WIDTH), p)
    hs = _sample_merge(xs2d, proj, gm, yn, xa, p)
    y_sample = _ffn(hs, p).reshape(n, 1, D_MODEL)

    kv_shape = (1, b, MEM_LEN, XA_HEADS, XA_HEAD_DIM)
    state_shape = (SSD_HEADS, SSD_HEAD_DIM, SSD_STATE)
    return (y_prompt, y_sample,
            mem_k.reshape(kv_shape), mem_v.reshape(kv_shape),
            conv_p.reshape(1, b, SSD_CONV - 1, SSD_CONV_DIM), ssm_p.reshape((1, b) + state_shape),
            conv_s.reshape(1, n, SSD_CONV - 1, SSD_CONV_DIM), ssm_s.reshape((1, n) + state_shape),
            gv.reshape(1, n, 1, GM_WIDTH))
```

```python
import functools

import jax
import jax.numpy as jnp
from jax import lax
from jax.experimental import pallas as pl
from jax.experimental.pallas import tpu as pltpu

F32 = jnp.float32
BF16 = jnp.bfloat16

D_MODEL = 1024
GM_CHUNK = 128
GM_GROUPS = 4
GM_GROUP_DIM = 128
GM_WIDTH = GM_GROUPS * GM_GROUP_DIM
SSD_HEADS = 16
SSD_HEAD_DIM = 64
SSD_INNER = SSD_HEADS * SSD_HEAD_DIM
SSD_GROUPS = 2
SSD_GROUP_WIDTH = SSD_INNER // SSD_GROUPS
SSD_STATE = 128
SSD_CONV = 4
SSD_CHUNK = 128
SSD_CONV_DIM = SSD_INNER + 2 * SSD_GROUPS * SSD_STATE
MEM_LEN = 256
XA_HEADS = 4
XA_HEAD_DIM = 128
XA_WIDTH = XA_HEADS * XA_HEAD_DIM
N_BRANCH = 3
D_FF = 4 * D_MODEL
EPS = 1e-6

SUBLANES = 8
LANES = 128
HEADS_PER_LANE_TILE = LANES // SSD_HEAD_DIM
HEAD_PAIRS = SSD_HEADS // HEADS_PER_LANE_TILE
PAIRS_PER_GROUP = HEAD_PAIRS // SSD_GROUPS

DT_PAD = LANES
OFF_UV = 0
OFF_Z = OFF_UV + 2 * GM_WIDTH
OFF_XBC = OFF_Z + SSD_INNER
OFF_DT = OFF_XBC + SSD_CONV_DIM
OFF_Q = OFF_DT + DT_PAD
OFF_GATE = OFF_Q + XA_WIDTH
IN_PROJ_PACKED = OFF_GATE + N_BRANCH * D_MODEL

MIXER_ROWS = 256
FFN_ROWS = 512
FFN_COLS = 1024
MEMKV_ROWS = 512
SAMPLE_BLOCK = SUBLANES
VMEM_LIMIT_BYTES = 56 * 1024 * 1024


def _dot(a, b):
    return jnp.dot(a, b, preferred_element_type=F32)


def _dot_nt(a, b):
    return lax.dot_general(a, b, (((1,), (1,)), ((), ())), preferred_element_type=F32)


def _dot_tn(a, b):
    return lax.dot_general(a, b, (((0,), (0,)), ((), ())), preferred_element_type=F32)


def _rms(x, w):
    return x * lax.rsqrt(jnp.mean(x * x, axis=-1, keepdims=True) + EPS) * w


def _layernorm(x, w, b):
    xc = x - jnp.mean(x, axis=-1, keepdims=True)
    return xc * lax.rsqrt(jnp.mean(xc * xc, axis=-1, keepdims=True) + EPS) * w + b


def _gelu(x):
    return 0.5 * x * (1.0 + lax.erf(x * 0.7071067811865476))


def _silu(x):
    return x * jax.nn.sigmoid(x)


def _softplus(x):
    return jnp.maximum(x, 0.0) + jnp.log1p(jnp.exp(-jnp.abs(x)))


def _split3(x):
    p1 = x.astype(BF16)
    r1 = x - p1.astype(F32)
    p2 = r1.astype(BF16)
    p3 = (r1 - p2.astype(F32)).astype(BF16)
    return p1, p2, p3


REDUCE_CHAINS = 8


def _reduce_leading(x, op):
    n = x.shape[0]
    assert n % REDUCE_CHAINS == 0
    acc = [x[w] for w in range(REDUCE_CHAINS)]
    for i in range(REDUCE_CHAINS, n, REDUCE_CHAINS):
        acc = [op(a, x[i + w]) for w, a in enumerate(acc)]
    while len(acc) > 1:
        acc = [op(acc[2 * i], acc[2 * i + 1]) for i in range(len(acc) // 2)]
    return acc[0]


def _fold_heads(a, op):
    assert SUBLANES == 2 * XA_HEADS
    return op(a, pltpu.roll(a, XA_HEADS, axis=0))


def _group_rms(y, w):
    parts = []
    for g in range(SSD_GROUPS):
        cols = slice(g * SSD_GROUP_WIDTH, (g + 1) * SSD_GROUP_WIDTH)
        parts.append(_rms(y[:, cols], w[:, cols]))
    return jnp.concatenate(parts, axis=-1)


def _memkv_kernel(mem_ref, nw_ref, wkv_ref, k_ref, v_ref):
    mn = _rms(mem_ref[...], nw_ref[...]).astype(BF16)
    kv = _dot(mn, wkv_ref[...])
    k_ref[...] = kv[:, :XA_WIDTH]
    v_ref[...] = kv[:, XA_WIDTH:]


def _memkv(mem2d, norm_w, wkv):
    rows = mem2d.shape[0]
    tm = min(MEMKV_ROWS, rows)
    return pl.pallas_call(
        _memkv_kernel,
        grid=(rows // tm,),
        in_specs=[
            pl.BlockSpec((tm, D_MODEL), lambda i: (i, 0)),
            pl.BlockSpec((1, D_MODEL), lambda i: (0, 0)),
            pl.BlockSpec((D_MODEL, 2 * XA_WIDTH), lambda i: (0, 0)),
        ],
        out_specs=[
            pl.BlockSpec((tm, XA_WIDTH), lambda i: (i, 0)),
            pl.BlockSpec((tm, XA_WIDTH), lambda i: (i, 0)),
        ],
        out_shape=[jax.ShapeDtypeStruct((rows, XA_WIDTH), F32)] * 2,
        compiler_params=pltpu.CompilerParams(
            dimension_semantics=("arbitrary",), vmem_limit_bytes=VMEM_LIMIT_BYTES),
        name="memkv",
    )(mem2d, norm_w, wkv)


def _ssd_chunk(rows, xs_s, bc_s, dt_s, neg_a, hT_s, y_s):
    q = SSD_CHUNK
    ii = lax.broadcasted_iota(jnp.int32, (q, q), 0)
    jj = lax.broadcasted_iota(jnp.int32, (q, q), 1)
    causal = ii >= jj
    low_half = lax.broadcasted_iota(jnp.int32, (q, LANES), 1) < SSD_HEAD_DIM
    low_half_row = lax.broadcasted_iota(jnp.int32, (1, LANES), 1) < SSD_HEAD_DIM

    dt = dt_s[rows, :]
    tri = causal.astype(BF16)
    d1, d2, d3 = _split3(dt * neg_a)
    acum = _dot(tri, d1) + _dot(tri, d2) + _dot(tri, d3)
    acum_t = acum.T
    dt_t = dt.T
    w_c = dt * jnp.exp(acum[q - 1:q, :] - acum)

    def spread_pair(a, pair):
        h0 = pair * HEADS_PER_LANE_TILE
        return jnp.where(low_half, jnp.broadcast_to(a[:, h0:h0 + 1], (q, LANES)),
                         jnp.broadcast_to(a[:, h0 + 1:h0 + 2], (q, LANES)))

    for g in range(SSD_GROUPS):
        b_g = bc_s[rows, g * SSD_STATE:(g + 1) * SSD_STATE]
        c_g = bc_s[rows, (SSD_GROUPS + g) * SSD_STATE:(SSD_GROUPS + g + 1) * SSD_STATE]
        cb = _dot_nt(c_g.astype(BF16), b_g.astype(BF16))
        xw = []
        for pr in range(PAIRS_PER_GROUP):
            pair = g * PAIRS_PER_GROUP + pr
            lanes = slice(pair * LANES, (pair + 1) * LANES)
            xw.append((xs_s[rows, lanes] * spread_pair(w_c, pair)).astype(BF16))
        s_g = _dot(b_g.T.astype(BF16), jnp.concatenate(xw, axis=1))
        for pr in range(PAIRS_PER_GROUP):
            pair = g * PAIRS_PER_GROUP + pr
            lanes = slice(pair * LANES, (pair + 1) * LANES)
            h_prev = hT_s[:, lanes]
            rhs = jnp.concatenate([xs_s[rows, lanes].astype(BF16), h_prev.astype(BF16)], axis=0)
            ys, cds = [], []
            for k in range(HEADS_PER_LANE_TILE):
                h = pair * HEADS_PER_LANE_TILE + k
                col = jnp.broadcast_to(acum[:, h:h + 1], (q, q))
                seg = jnp.where(causal, col - acum_t[h:h + 1, :], -jnp.inf)
                m_h = cb * jnp.exp(seg) * dt_t[h:h + 1, :]
                c_h = c_g * jnp.exp(col)
                lhs = jnp.concatenate([m_h.astype(BF16), c_h.astype(BF16)], axis=1)
                ys.append(_dot(lhs, rhs))
                cds.append(jnp.exp(col[q - 1:q, :]))
            y_s[rows, lanes] = jnp.where(low_half, ys[0], ys[1])
            decay = jnp.where(low_half_row, cds[0], cds[1])
            hT_s[:, lanes] = h_prev * decay + s_g[:, pr * LANES:(pr + 1) * LANES]


def _in_proj_cols(w_refs, lo, hi):
    wa_ref, wdt_ref, wb_ref = w_refs
    if hi <= OFF_DT:
        return wa_ref[:, lo:hi]
    if lo >= OFF_Q:
        return wb_ref[:, lo - OFF_Q:hi - OFF_Q]
    assert (lo, hi) == (OFF_DT, OFF_Q)
    return wdt_ref[...]


def _mixer_kernel(x_ref, k_ref, v_ref, nw_ref, wa_ref, wdt_ref, wb_ref, lnw_ref, lnb_ref, ws_ref, bst_ref,
                  cw_ref, cb_ref, dtb_ref, alog_ref, dsk_ref, snw_ref,
                  wgm_ref, wssd_ref, wxa_ref, wout_ref,
                  h_ref, conv_ref, ssm_ref,
                  ext_s, hT_s, xs_s, bc_s, dt_s, y_s, u_s, v_s, gm_s):
    tm = x_ref.shape[1]
    t = pl.program_id(1)
    w_in = functools.partial(_in_proj_cols, (wa_ref, wdt_ref, wb_ref))
    head = SUBLANES

    @pl.when(t == 0)
    def _():
        ext_s[0:head, :] = jnp.zeros((head, SSD_CONV_DIM), F32)
        hT_s[...] = jnp.zeros(hT_s.shape, F32)

    x = x_ref[0]
    xn = _rms(x, nw_ref[...]).astype(BF16)

    uv = _gelu(_dot(xn, w_in(OFF_UV, OFF_Z)))
    u_s[...] = uv[:, :GM_WIDTH]
    v_s[...] = _layernorm(uv[:, GM_WIDTH:], lnw_ref[...], lnb_ref[...]).astype(BF16)

    ext_s[head:head + tm, :] = _dot(xn, w_in(OFF_XBC, OFF_DT))
    acc = ext_s[head - 3:head - 3 + tm, :] * cw_ref[0:1, :]
    for k in range(1, SSD_CONV):
        acc = acc + ext_s[head - 3 + k:head - 3 + k + tm, :] * cw_ref[k:k + 1, :]
    xbc = _silu(acc + cb_ref[...])
    xs_s[...] = xbc[:, :SSD_INNER]
    bc_s[...] = xbc[:, SSD_INNER:]
    conv_ref[0] = ext_s[tm + head - 3:tm + head, :]
    ext_s[0:head, :] = ext_s[tm:tm + head, :]
    dt_s[...] = _softplus(_dot(xn, w_in(OFF_DT, OFF_Q)) + dtb_ref[...])
    neg_a = -jnp.exp(alog_ref[...])

    for c in range(tm // SSD_CHUNK):
        rows = slice(c * SSD_CHUNK, (c + 1) * SSD_CHUNK)
        _ssd_chunk(rows, xs_s, bc_s, dt_s, neg_a, hT_s, y_s)

    ii = lax.broadcasted_iota(jnp.int32, (GM_CHUNK, GM_CHUNK), 0)
    jj = lax.broadcasted_iota(jnp.int32, (GM_CHUNK, GM_CHUNK), 1)
    for g in range(GM_GROUPS):
        cols = slice(g * GM_GROUP_DIM, (g + 1) * GM_GROUP_DIM)
        w_tril = jnp.where(ii >= jj, ws_ref[g], 0.0).astype(BF16)
        bias = bst_ref[:, g:g + 1]
        for c in range(tm // GM_CHUNK):
            rows = slice(c * GM_CHUNK, (c + 1) * GM_CHUNK)
            mixed = _dot(w_tril, v_s[rows, cols]) + bias
            gm_s[rows, cols] = (u_s[rows, cols] * mixed).astype(BF16)

    zs = _silu(_dot(xn, w_in(OFF_Z, OFF_XBC)))
    yv = (y_s[...] + xs_s[...] * dsk_ref[...]) * zs
    yn = _group_rms(yv, snw_ref[...]).astype(BF16)

    qv = _dot(xn, w_in(OFF_Q, OFF_GATE))
    outs = []
    for hd in range(XA_HEADS):
        cols = slice(hd * XA_HEAD_DIM, (hd + 1) * XA_HEAD_DIM)
        s = _dot_nt(qv[:, cols].astype(BF16), k_ref[0, :, cols].astype(BF16)) * (XA_HEAD_DIM ** -0.5)
        p = jnp.exp(s - jnp.max(s, axis=-1, keepdims=True))
        o = _dot(p.astype(BF16), v_ref[0, :, cols].astype(BF16))
        outs.append(o / jnp.sum(p, axis=-1, keepdims=True))
    xa = jnp.concatenate(outs, axis=-1).astype(BF16)

    merged = None
    for i, (br, w_ref) in enumerate(((gm_s[...], wgm_ref), (yn, wssd_ref), (xa, wxa_ref))):
        gate = jax.nn.sigmoid(_dot(xn, w_in(OFF_GATE + i * D_MODEL, OFF_GATE + (i + 1) * D_MODEL)))
        term = gate * _dot(br, w_ref[...])
        merged = term if merged is None else merged + term
    h_ref[0] = x + _dot(merged.astype(BF16), wout_ref[...])

    @pl.when(t == pl.num_programs(1) - 1)
    def _():
        for pair in range(HEAD_PAIRS):
            lanes = slice(pair * LANES, (pair + 1) * LANES)
            ssm_ref[0, lanes, :] = hT_s[:, lanes].T


def _full(shape):
    return pl.BlockSpec(shape, lambda *_: (0,) * len(shape))


def _mixer(x, mem_k, mem_v, p):
    b, seq, _ = x.shape
    tm = MIXER_ROWS
    assert seq % tm == 0 and tm % SSD_CHUNK == 0
    small = [p["norm_mix_w"], p["w_in_a"], p["w_in_dt"], p["w_in_b"],
             p["gm_ln_w"], p["gm_ln_b"], p["gm_ws"], p["gm_bs_t"],
             p["conv_w"], p["conv_b"], p["dt_bias"], p["a_log"], p["d_skip"], p["ssd_norm_w"],
             p["w_br_gm"], p["w_br_ssd"], p["w_br_xa"], p["w_out"]]
    return pl.pallas_call(
        _mixer_kernel,
        grid=(b, seq // tm),
        in_specs=[
            pl.BlockSpec((1, tm, D_MODEL), lambda i, j: (i, j, 0)),
            pl.BlockSpec((1, MEM_LEN, XA_WIDTH), lambda i, j: (i, 0, 0)),
            pl.BlockSpec((1, MEM_LEN, XA_WIDTH), lambda i, j: (i, 0, 0)),
        ] + [_full(a.shape) for a in small],
        out_specs=[
            pl.BlockSpec((1, tm, D_MODEL), lambda i, j: (i, j, 0)),
            pl.BlockSpec((1, SSD_CONV - 1, SSD_CONV_DIM), lambda i, j: (i, 0, 0)),
            pl.BlockSpec((1, SSD_INNER, SSD_STATE), lambda i, j: (i, 0, 0)),
        ],
        out_shape=[
            jax.ShapeDtypeStruct((b, seq, D_MODEL), F32),
            jax.ShapeDtypeStruct((b, SSD_CONV - 1, SSD_CONV_DIM), F32),
            jax.ShapeDtypeStruct((b, SSD_INNER, SSD_STATE), F32),
        ],
        scratch_shapes=[
            pltpu.VMEM((tm + SUBLANES, SSD_CONV_DIM), F32),
            pltpu.VMEM((SSD_STATE, SSD_INNER), F32),
            pltpu.VMEM((tm, SSD_INNER), F32),
            pltpu.VMEM((tm, 2 * SSD_GROUPS * SSD_STATE), F32),
            pltpu.VMEM((tm, DT_PAD), F32),
            pltpu.VMEM((tm, SSD_INNER), F32),
            pltpu.VMEM((tm, GM_WIDTH), F32),
            pltpu.VMEM((tm, GM_WIDTH), BF16),
            pltpu.VMEM((tm, GM_WIDTH), BF16),
        ],
        compiler_params=pltpu.CompilerParams(
            dimension_semantics=("arbitrary", "arbitrary"), vmem_limit_bytes=VMEM_LIMIT_BYTES),
        name="prompt_mixer",
    )(x, mem_k, mem_v, *small)


def _ffn_kernel(h_ref, nw_ref, wup_ref, wdn_ref, fw_ref, o_ref):
    h = h_ref[...]
    hn = _rms(h, nw_ref[...]).astype(BF16)
    acc = h
    for j in range(D_FF // FFN_COLS):
        cols = slice(j * FFN_COLS, (j + 1) * FFN_COLS)
        a = jnp.square(jnp.maximum(_dot(hn, wup_ref[:, cols]), 0.0)).astype(BF16)
        acc = acc + _dot(a, wdn_ref[cols, :])
    o_ref[...] = _rms(acc, fw_ref[...])


def _ffn(h2d, p):
    rows = h2d.shape[0]
    tm = min(FFN_ROWS, rows)
    assert rows % tm == 0
    return pl.pallas_call(
        _ffn_kernel,
        grid=(rows // tm,),
        in_specs=[
            pl.BlockSpec((tm, D_MODEL), lambda i: (i, 0)),
            _full((1, D_MODEL)),
            _full((D_MODEL, D_FF)),
            _full((D_FF, D_MODEL)),
            _full((1, D_MODEL)),
        ],
        out_specs=pl.BlockSpec((tm, D_MODEL), lambda i: (i, 0)),
        out_shape=jax.ShapeDtypeStruct((rows, D_MODEL), F32),
        compiler_params=pltpu.CompilerParams(
            dimension_semantics=("arbitrary",), vmem_limit_bytes=VMEM_LIMIT_BYTES),
        name="ffn",
    )(h2d, p["norm_ffn_w"], p["w_up"], p["w_down"], p["norm_final_w"])


def _sample_proj_kernel(x_ref, nw_ref, wa_ref, wdt_ref, wb_ref, o_ref):
    xn = _rms(x_ref[...], nw_ref[...]).astype(BF16)
    o_ref[:, :OFF_DT] = _dot(xn, wa_ref[...])
    o_ref[:, OFF_DT:OFF_Q] = _dot(xn, wdt_ref[...])
    o_ref[:, OFF_Q:] = _dot(xn, wb_ref[...])


def _sample_proj(x2d, p):
    rows = x2d.shape[0]
    args = [x2d, p["norm_mix_w"], p["w_in_a"], p["w_in_dt"], p["w_in_b"]]
    return pl.pallas_call(
        _sample_proj_kernel,
        grid=(1,),
        in_specs=[_full(a.shape) for a in args],
        out_specs=_full((rows, IN_PROJ_PACKED)),
        out_shape=jax.ShapeDtypeStruct((rows, IN_PROJ_PACKED), F32),
        compiler_params=pltpu.CompilerParams(
            dimension_semantics=("arbitrary",), vmem_limit_bytes=VMEM_LIMIT_BYTES),
        name="sample_proj",
    )(*args)


def _sample_state_kernel(proj_ref, stc_ref, ssm_ref, k_ref, v_ref, lnw_ref, lnb_ref, ws_ref,
                         bs_ref, cw_ref, cb_ref, dtb_ref, alog_ref, dsk_ref, snw_ref,
                         gm_ref, yn_ref, xa_ref, convn_ref, ssmn_ref, gv_ref):
    nb = proj_ref.shape[0]
    proj = proj_ref[...]

    uv = _gelu(proj[:, OFF_UV:OFF_Z])
    v = _layernorm(uv[:, GM_WIDTH:], lnw_ref[...], lnb_ref[...])
    gv_ref[...] = v
    for g in range(GM_GROUPS):
        cols = slice(g * GM_GROUP_DIM, (g + 1) * GM_GROUP_DIM)
        mixed = v[:, cols] * ws_ref[g, 0:1, 0:1] + bs_ref[g:g + 1, 0:1]
        gm_ref[:, cols] = uv[:, cols] * mixed

    st = stc_ref[...]
    xnew = proj[:, OFF_XBC:OFF_DT]
    acc = st[:, 0:SSD_CONV_DIM] * cw_ref[0:1, :]
    for k in range(1, SSD_CONV - 1):
        acc = acc + st[:, k * SSD_CONV_DIM:(k + 1) * SSD_CONV_DIM] * cw_ref[k:k + 1, :]
    acc = acc + xnew * cw_ref[SSD_CONV - 1:SSD_CONV, :]
    convn_ref[:, 0:(SSD_CONV - 2) * SSD_CONV_DIM] = st[:, SSD_CONV_DIM:]
    convn_ref[:, (SSD_CONV - 2) * SSD_CONV_DIM:] = xnew
    xbc = _silu(acc + cb_ref[...])
    xs = xbc[:, :SSD_INNER]
    bm = xbc[:, SSD_INNER:SSD_INNER + SSD_GROUPS * SSD_STATE].astype(BF16)
    cm = xbc[:, SSD_INNER + SSD_GROUPS * SSD_STATE:].astype(BF16)

    dt = _softplus(proj[:, OFF_DT:OFF_Q] + dtb_ref[...])
    decay = jnp.exp(dt * -jnp.exp(alog_ref[...]))
    hh = lax.broadcasted_iota(jnp.int32, (DT_PAD, SSD_INNER), 0)
    cc = lax.broadcasted_iota(jnp.int32, (DT_PAD, SSD_INNER), 1)
    spread = (jnp.right_shift(cc, SSD_HEAD_DIM.bit_length() - 1) == hh).astype(BF16)
    dt_wide = sum(_dot(piece, spread) for piece in _split3(dt))
    xdt = xs * dt_wide

    rid = lax.broadcasted_iota(jnp.int32, (nb, 1), 0)
    scale = XA_HEAD_DIM ** -0.5
    kv_vregs = MEM_LEN * XA_HEADS // SUBLANES
    lane_ones = jnp.ones((XA_HEAD_DIM, XA_HEAD_DIM), BF16)
    qv = proj[:, OFF_Q:OFF_GATE]
    y = jnp.zeros((nb, SSD_INNER), F32)
    for bi in range(nb):
        mine = rid == bi
        x_row = jnp.where(mine, xdt, 0.0).astype(BF16)
        parts = []
        for g in range(SSD_GROUPS):
            cols = slice(g * SSD_GROUP_WIDTH, (g + 1) * SSD_GROUP_WIDTH)
            ncols = slice(g * SSD_STATE, (g + 1) * SSD_STATE)
            upd = _dot_tn(x_row[:, cols], bm[:, ncols])
            heads = range(g * SSD_HEADS // SSD_GROUPS, (g + 1) * SSD_HEADS // SSD_GROUPS)
            e_rows = jnp.concatenate(
                [jnp.broadcast_to(decay[bi:bi + 1, h:h + 1], (SSD_HEAD_DIM, SSD_STATE)) for h in heads],
                axis=0)
            h_new = ssm_ref[bi, cols, :] * e_rows + upd
            ssmn_ref[bi, cols, :] = h_new
            parts.append(_dot_nt(cm[:, ncols], h_new.astype(BF16)))
        y = y + jnp.where(mine, jnp.concatenate(parts, axis=-1), 0.0)

        q_rows = jnp.concatenate(
            [qv[bi:bi + 1, hd * XA_HEAD_DIM:(hd + 1) * XA_HEAD_DIM] for hd in range(XA_HEADS)]
            * (SUBLANES // XA_HEADS), axis=0)
        kq = (k_ref[bi].reshape(kv_vregs, SUBLANES, XA_HEAD_DIM) * q_rows[None]).astype(BF16)
        s = _dot(kq.reshape(kv_vregs * SUBLANES, XA_HEAD_DIM), lane_ones) * scale
        s = s.reshape(kv_vregs, SUBLANES, XA_HEAD_DIM)
        pexp = jnp.exp(s - _fold_heads(_reduce_leading(s, jnp.maximum), jnp.maximum)[None])
        o = _reduce_leading(pexp * v_ref[bi].reshape(kv_vregs, SUBLANES, XA_HEAD_DIM), jnp.add)
        o = _fold_heads(o, jnp.add) / _fold_heads(_reduce_leading(pexp, jnp.add), jnp.add)
        for hd in range(XA_HEADS):
            xa_ref[bi:bi + 1, hd * XA_HEAD_DIM:(hd + 1) * XA_HEAD_DIM] = o[hd:hd + 1, :]

    zs = _silu(proj[:, OFF_Z:OFF_XBC])
    yn_ref[...] = _group_rms((y + xs * dsk_ref[...]) * zs, snw_ref[...])


def _sample_state(proj, state_conv2d, state_ssm, cache_k, cache_v, p):
    n = proj.shape[0]
    nb = SAMPLE_BLOCK
    assert n % nb == 0
    conv_w = (SSD_CONV - 1) * SSD_CONV_DIM
    small = [p["gm_ln_w"], p["gm_ln_b"], p["gm_ws"], p["gm_bs"], p["conv_w"], p["conv_b"],
             p["dt_bias"], p["a_log"], p["d_skip"], p["ssd_norm_w"]]
    rows2 = lambda w: pl.BlockSpec((nb, w), lambda i: (i, 0))
    rows3 = lambda a, c: pl.BlockSpec((nb, a, c), lambda i: (i, 0, 0))
    cache = rows3(MEM_LEN * XA_HEADS, XA_HEAD_DIM)
    return pl.pallas_call(
        _sample_state_kernel,
        grid=(n // nb,),
        in_specs=[rows2(IN_PROJ_PACKED), rows2(conv_w), rows3(SSD_INNER, SSD_STATE), cache, cache]
                 + [_full(a.shape) for a in small],
        out_specs=[rows2(GM_WIDTH), rows2(SSD_INNER), rows2(XA_WIDTH), rows2(conv_w),
                   rows3(SSD_INNER, SSD_STATE), rows2(GM_WIDTH)],
        out_shape=[
            jax.ShapeDtypeStruct((n, GM_WIDTH), F32),
            jax.ShapeDtypeStruct((n, SSD_INNER), F32),
            jax.ShapeDtypeStruct((n, XA_WIDTH), F32),
            jax.ShapeDtypeStruct((n, conv_w), F32),
            jax.ShapeDtypeStruct((n, SSD_INNER, SSD_STATE), F32),
            jax.ShapeDtypeStruct((n, GM_WIDTH), F32),
        ],
        compiler_params=pltpu.CompilerParams(
            dimension_semantics=("arbitrary",), vmem_limit_bytes=VMEM_LIMIT_BYTES),
        name="sample_state",
    )(proj, state_conv2d, state_ssm, cache_k, cache_v, *small)


def _sample_merge_kernel(x_ref, proj_ref, gm_ref, yn_ref, xa_ref, wgm_ref, wssd_ref, wxa_ref,
                         wout_ref, h_ref):
    merged = None
    for i, (br_ref, w_ref) in enumerate(((gm_ref, wgm_ref), (yn_ref, wssd_ref), (xa_ref, wxa_ref))):
        gate = jax.nn.sigmoid(proj_ref[:, OFF_GATE + i * D_MODEL:OFF_GATE + (i + 1) * D_MODEL])
        term = gate * _dot(br_ref[...].astype(BF16), w_ref[...])
        merged = term if merged is None else merged + term
    h_ref[...] = x_ref[...] + _dot(merged.astype(BF16), wout_ref[...])


def _sample_merge(x2d, proj, gm, yn, xa, p):
    args = [x2d, proj, gm, yn, xa, p["w_br_gm"], p["w_br_ssd"], p["w_br_xa"], p["w_out"]]
    return pl.pallas_call(
        _sample_merge_kernel,
        grid=(1,),
        in_specs=[_full(a.shape) for a in args],
        out_specs=_full(x2d.shape),
        out_shape=jax.ShapeDtypeStruct(x2d.shape, F32),
        compiler_params=pltpu.CompilerParams(
            dimension_semantics=("arbitrary",), vmem_limit_bytes=VMEM_LIMIT_BYTES),
        name="sample_merge",
    )(*args)


def _prep_layer(l, norm_mix_w, w_in, gm_ln_w, gm_ln_b, gm_ws, gm_bs, conv_w, conv_b, dt_bias, a_log,
                d_skip, ssd_norm_w, mem_norm_w, w_mem_k, w_mem_v, w_br_gm, w_br_ssd, w_br_xa, w_out,
                norm_ffn_w, w_up, w_down, norm_final_w):
    row = lambda a: a.reshape(1, -1)
    pad_heads = lambda a: jnp.pad(a, (0, DT_PAD - SSD_HEADS)).reshape(1, DT_PAD)
    w = w_in[l]
    w_dt = jnp.pad(w[:, OFF_DT:OFF_DT + SSD_HEADS], ((0, 0), (0, DT_PAD - SSD_HEADS)))
    return {
        "norm_mix_w": row(norm_mix_w[l]),
        "w_in_a": w[:, :OFF_DT].astype(BF16), "w_in_dt": w_dt.astype(BF16),
        "w_in_b": w[:, OFF_DT + SSD_HEADS:].astype(BF16),
        "gm_ln_w": row(gm_ln_w[l]), "gm_ln_b": row(gm_ln_b[l]),
        "gm_ws": gm_ws[l], "gm_bs": gm_bs[l], "gm_bs_t": gm_bs[l].T,
        "conv_w": conv_w[l], "conv_b": row(conv_b[l]),
        "dt_bias": pad_heads(dt_bias[l]), "a_log": pad_heads(a_log[l]),
        "d_skip": row(jnp.repeat(d_skip[l], SSD_HEAD_DIM)), "ssd_norm_w": row(ssd_norm_w[l]),
        "mem_norm_w": row(mem_norm_w[l]),
        "w_kv": jnp.concatenate([w_mem_k[l], w_mem_v[l]], axis=1).astype(BF16),
        "w_br_gm": w_br_gm[l].astype(BF16), "w_br_ssd": w_br_ssd[l].astype(BF16),
        "w_br_xa": w_br_xa[l].astype(BF16), "w_out": w_out[l].astype(BF16),
        "norm_ffn_w": row(norm_ffn_w[l]), "w_up": w_up[l].astype(BF16),
        "w_down": w_down[l].astype(BF16), "norm_final_w": row(norm_final_w),
    }


def kernel(x_prompt, x_sample, mem_prompt, cache_mem_k, cache_mem_v, state_conv, state_ssm, norm_mix_w, w_in, gm_ln_w, gm_ln_b, gm_ws, gm_bs, conv_w, conv_b, dt_bias, a_log, d_skip, ssd_norm_w, mem_norm_w, w_mem_k, w_mem_v, w_br_gm, w_br_ssd, w_br_xa, w_out, norm_ffn_w, w_up, w_down, norm_final_w):
    depth = w_in.shape[0]
    assert depth == 1, "the final norm is fused into the MLP kernel of the last (only) layer"
    b, seq, _ = x_prompt.shape
    n, dec_seq, _ = x_sample.shape
    assert dec_seq == 1
    p = _prep_layer(0, norm_mix_w, w_in, gm_ln_w, gm_ln_b, gm_ws, gm_bs, conv_w, conv_b, dt_bias,
                    a_log, d_skip, ssd_norm_w, mem_norm_w, w_mem_k, w_mem_v, w_br_gm, w_br_ssd,
                    w_br_xa, w_out, norm_ffn_w, w_up, w_down, norm_final_w)

    mem_k, mem_v = _memkv(mem_prompt.reshape(b * MEM_LEN, D_MODEL), p["mem_norm_w"], p["w_kv"])
    h1, conv_p, ssm_p = _mixer(x_prompt, mem_k.reshape(b, MEM_LEN, XA_WIDTH),
                               mem_v.reshape(b, MEM_LEN, XA_WIDTH), p)
    y_prompt = _ffn(h1.reshape(b * seq, D_MODEL), p).reshape(b, seq, D_MODEL)

    xs2d = x_sample.reshape(n, D_MODEL)
    proj = _sample_proj(xs2d, p)
    gm, yn, xa, conv_s, ssm_s, gv = _sample_state(
        proj, state_conv[0].reshape(n, (SSD_CONV - 1) * SSD_CONV_DIM), state_ssm[0].reshape(n, SSD_INNER, SSD_STATE),
        cache_mem_k[0].reshape(n, MEM_LEN * XA_HEADS, XA_HEAD_DIM),
        cache_mem_v[0].reshape(n, MEM_LEN * XA_HEADS, XA_HEAD_DIM), p)
    hs = _sample_merge(xs2d, proj, gm, yn, xa, p)
    y_sample = _ffn(hs, p).reshape(n, 1, D_MODEL)

    kv_shape = (1, b, MEM_LEN, XA_HEADS, XA_HEAD_DIM)
    state_shape = (SSD_HEADS, SSD_HEAD_DIM, SSD_STATE)
    return (y_prompt, y_sample,
            mem_k.reshape(kv_shape), mem_v.reshape(kv_shape),
            conv_p.reshape(1, b, SSD_CONV - 1, SSD_CONV_DIM), ssm_p.reshape((1, b) + state_shape),
            conv_s.reshape(1, n, SSD_CONV - 1, SSD_CONV_DIM), ssm_s.reshape((1, n) + state_shape),
            gv.reshape(1, n, 1, GM_WIDTH))
```

```python
import functools

import jax
import jax.numpy as jnp
from jax import lax
from jax.experimental import pallas as pl
from jax.experimental.pallas import tpu as pltpu

F32 = jnp.float32
BF16 = jnp.bfloat16

D_MODEL = 1024
GM_CHUNK = 128
GM_GROUPS = 4
GM_GROUP_DIM = 128
GM_WIDTH = GM_GROUPS * GM_GROUP_DIM
SSD_HEADS = 16
SSD_HEAD_DIM = 64
SSD_INNER = SSD_HEADS * SSD_HEAD_DIM
SSD_GROUPS = 2
SSD_GROUP_WIDTH = SSD_INNER // SSD_GROUPS
SSD_STATE = 128
SSD_CONV = 4
SSD_CHUNK = 128
SSD_CONV_DIM = SSD_INNER + 2 * SSD_GROUPS * SSD_STATE
MEM_LEN = 256
XA_HEADS = 4
XA_HEAD_DIM = 128
XA_WIDTH = XA_HEADS * XA_HEAD_DIM
N_BRANCH = 3
D_FF = 4 * D_MODEL
EPS = 1e-6

SUBLANES = 8
LANES = 128
HEADS_PER_LANE_TILE = LANES // SSD_HEAD_DIM
HEAD_PAIRS = SSD_HEADS // HEADS_PER_LANE_TILE
PAIRS_PER_GROUP = HEAD_PAIRS // SSD_GROUPS

DT_PAD = LANES
OFF_UV = 0
OFF_Z = OFF_UV + 2 * GM_WIDTH
OFF_XBC = OFF_Z + SSD_INNER
OFF_DT = OFF_XBC + SSD_CONV_DIM
OFF_Q = OFF_DT + DT_PAD
OFF_GATE = OFF_Q + XA_WIDTH
IN_PROJ_PACKED = OFF_GATE + N_BRANCH * D_MODEL

RAW_UV = 0
RAW_Z = RAW_UV + 2 * GM_WIDTH
RAW_DT = RAW_Z + SSD_INNER
RAW_GATE = RAW_DT + DT_PAD
RAW_WIDTH = RAW_GATE + N_BRANCH * D_MODEL

MIXER_ROWS = 256
PIECE_COLS = 256
MERGE_COLS = 512
FFN_ROWS = 512
FFN_COLS = 1024
MEMKV_ROWS = 512
SAMPLE_BLOCK = SUBLANES
VMEM_LIMIT_BYTES = 56 * 1024 * 1024


def _dot(a, b):
    return jnp.dot(a, b, preferred_element_type=F32)


def _dot_nt(a, b):
    return lax.dot_general(a, b, (((1,), (1,)), ((), ())), preferred_element_type=F32)


def _dot_tn(a, b):
    return lax.dot_general(a, b, (((0,), (0,)), ((), ())), preferred_element_type=F32)


def _rms(x, w):
    return x * lax.rsqrt(jnp.mean(x * x, axis=-1, keepdims=True) + EPS) * w


def _layernorm(x, w, b):
    xc = x - jnp.mean(x, axis=-1, keepdims=True)
    return xc * lax.rsqrt(jnp.mean(xc * xc, axis=-1, keepdims=True) + EPS) * w + b


def _gelu(x):
    return 0.5 * x * (1.0 + lax.erf(x * 0.7071067811865476))


def _silu(x):
    return x * jax.nn.sigmoid(x)


def _softplus(x):
    return jnp.maximum(x, 0.0) + jnp.log1p(jnp.exp(-jnp.abs(x)))


def _split3(x):
    p1 = x.astype(BF16)
    r1 = x - p1.astype(F32)
    p2 = r1.astype(BF16)
    p3 = (r1 - p2.astype(F32)).astype(BF16)
    return p1, p2, p3


REDUCE_CHAINS = 8


def _reduce_leading(x, op):
    n = x.shape[0]
    assert n % REDUCE_CHAINS == 0
    acc = [x[w] for w in range(REDUCE_CHAINS)]
    for i in range(REDUCE_CHAINS, n, REDUCE_CHAINS):
        acc = [op(a, x[i + w]) for w, a in enumerate(acc)]
    while len(acc) > 1:
        acc = [op(acc[2 * i], acc[2 * i + 1]) for i in range(len(acc) // 2)]
    return acc[0]


def _fold_heads(a, op):
    assert SUBLANES == 2 * XA_HEADS
    return op(a, pltpu.roll(a, XA_HEADS, axis=0))


def _group_rms(y, w):
    parts = []
    for g in range(SSD_GROUPS):
        cols = slice(g * SSD_GROUP_WIDTH, (g + 1) * SSD_GROUP_WIDTH)
        parts.append(_rms(y[:, cols], w[:, cols]))
    return jnp.concatenate(parts, axis=-1)


def _memkv_kernel(mem_ref, nw_ref, wkv_ref, k_ref, v_ref):
    mn = _rms(mem_ref[...], nw_ref[...]).astype(BF16)
    kv = _dot(mn, wkv_ref[...])
    k_ref[...] = kv[:, :XA_WIDTH]
    v_ref[...] = kv[:, XA_WIDTH:]


def _memkv(mem2d, norm_w, wkv):
    rows = mem2d.shape[0]
    tm = min(MEMKV_ROWS, rows)
    return pl.pallas_call(
        _memkv_kernel,
        grid=(rows // tm,),
        in_specs=[
            pl.BlockSpec((tm, D_MODEL), lambda i: (i, 0)),
            pl.BlockSpec((1, D_MODEL), lambda i: (0, 0)),
            pl.BlockSpec((D_MODEL, 2 * XA_WIDTH), lambda i: (0, 0)),
        ],
        out_specs=[
            pl.BlockSpec((tm, XA_WIDTH), lambda i: (i, 0)),
            pl.BlockSpec((tm, XA_WIDTH), lambda i: (i, 0)),
        ],
        out_shape=[jax.ShapeDtypeStruct((rows, XA_WIDTH), F32)] * 2,
        compiler_params=pltpu.CompilerParams(
            dimension_semantics=("arbitrary",), vmem_limit_bytes=VMEM_LIMIT_BYTES),
        name="memkv",
    )(mem2d, norm_w, wkv)


def _ssd_chunk_steps(rows, xs_s, bc_s, dt_s, neg_a, hT_s, y_s, scratch):
    acum_s, acumt_s, dtt_s, wc_s, cb_s, lhs_s, cd_s, xw_s, sg_s = scratch
    q = SSD_CHUNK

    def masks():
        ii = lax.broadcasted_iota(jnp.int32, (q, q), 0)
        jj = lax.broadcasted_iota(jnp.int32, (q, q), 1)
        low_half = lax.broadcasted_iota(jnp.int32, (q, LANES), 1) < SSD_HEAD_DIM
        low_half_row = lax.broadcasted_iota(jnp.int32, (1, LANES), 1) < SSD_HEAD_DIM
        return ii >= jj, low_half, low_half_row

    def prologue():
        causal, _, _ = masks()
        dt = dt_s[rows, :]
        d1, d2, d3 = _split3(dt * neg_a)
        tri = causal.astype(BF16)
        acum = _dot(tri, d1) + _dot(tri, d2) + _dot(tri, d3)
        acum_s[...] = acum
        acumt_s[...] = acum.T
        dtt_s[...] = dt.T
        wc_s[...] = dt * jnp.exp(acum[q - 1:q, :] - acum)

    def group_step(g):
        _, low_half, _ = masks()
        b_g = bc_s[rows, g * SSD_STATE:(g + 1) * SSD_STATE]
        c_g = bc_s[rows, (SSD_GROUPS + g) * SSD_STATE:(SSD_GROUPS + g + 1) * SSD_STATE]
        cb_s[g] = _dot_nt(c_g.astype(BF16), b_g.astype(BF16))
        for pr in range(PAIRS_PER_GROUP):
            pair = g * PAIRS_PER_GROUP + pr
            lanes = slice(pair * LANES, (pair + 1) * LANES)
            h0 = pair * HEADS_PER_LANE_TILE
            w_pair = jnp.where(low_half, jnp.broadcast_to(wc_s[:, h0:h0 + 1], (q, LANES)),
                               jnp.broadcast_to(wc_s[:, h0 + 1:h0 + 2], (q, LANES)))
            xw_s[:, lanes] = (xs_s[rows, lanes] * w_pair).astype(BF16)
        cols = slice(g * SSD_GROUP_WIDTH, (g + 1) * SSD_GROUP_WIDTH)
        sg_s[:, cols] = _dot(b_g.T.astype(BF16), xw_s[:, cols])

    def head_step(h):
        causal, _, _ = masks()
        g = h // (SSD_HEADS // SSD_GROUPS)
        c_g = bc_s[rows, (SSD_GROUPS + g) * SSD_STATE:(SSD_GROUPS + g + 1) * SSD_STATE]
        col = jnp.broadcast_to(acum_s[:, h:h + 1], (q, q))
        seg = jnp.where(causal, col - acumt_s[h:h + 1, :], -jnp.inf)
        lhs_s[h, :, 0:q] = (cb_s[g] * jnp.exp(seg) * dtt_s[h:h + 1, :]).astype(BF16)
        lhs_s[h, :, q:2 * q] = (c_g * jnp.exp(col)).astype(BF16)
        cd_s[h:h + 1, :] = jnp.exp(col[q - 1:q, :])

    def pair_step(pair):
        _, low_half, low_half_row = masks()
        lanes = slice(pair * LANES, (pair + 1) * LANES)
        h0 = pair * HEADS_PER_LANE_TILE
        h_prev = hT_s[:, lanes]
        rhs = jnp.concatenate([xs_s[rows, lanes].astype(BF16), h_prev.astype(BF16)], axis=0)
        y_s[rows, lanes] = jnp.where(low_half, _dot(lhs_s[h0], rhs), _dot(lhs_s[h0 + 1], rhs))
        decay = jnp.where(low_half_row, cd_s[h0:h0 + 1, :], cd_s[h0 + 1:h0 + 2, :])
        hT_s[:, lanes] = h_prev * decay + sg_s[:, lanes]

    steps = [(2, prologue)]
    steps += [(3, functools.partial(group_step, g)) for g in range(SSD_GROUPS)]
    steps += [(2, functools.partial(head_step, h)) for h in range(SSD_HEADS)]
    steps += [(1, functools.partial(pair_step, pair)) for pair in range(HEAD_PAIRS)]
    return steps


def _interleave(steps, pieces):
    total = sum(w for w, _ in steps)
    done = acc = 0
    for w, step in steps:
        acc += w
        upto = len(pieces) * acc // total
        for piece in pieces[done:upto]:
            piece()
        done = upto
        step()
    assert done == len(pieces)


def _in_proj_cols(w_refs, lo, hi):
    wa_ref, wdt_ref, wb_ref = w_refs
    if hi <= OFF_DT:
        return wa_ref[:, lo:hi]
    if lo >= OFF_Q:
        return wb_ref[:, lo - OFF_Q:hi - OFF_Q]
    assert (lo, hi) == (OFF_DT, OFF_Q)
    return wdt_ref[...]


def _mixer_kernel(x1_ref, x2_ref, k_ref, v_ref, nw_ref, wa_ref, wdt_ref, wb_ref, lnw_ref, lnb_ref, ws_ref,
                  bst_ref, cw_ref, cb_ref, dtb_ref, alog_ref, dsk_ref, snw_ref,
                  wgm_ref, wssd_ref, wxa_ref, wout_ref,
                  h_ref, conv_ref, ssm_ref,
                  xn_s, raw_s, q_s, ext_s, tail_s, hT_s, xs_s, bc_s, dt_s, y_s, u_s, v_s, gm_s, yn_s, xa_s, merged_s,
                  *ssd_scratch, tiles_per_seq):
    tm = x1_ref.shape[1]
    s = pl.program_id(0)
    t2 = lax.rem(jnp.maximum(s - 1, 0), tiles_per_seq)
    w_in = functools.partial(_in_proj_cols, (wa_ref, wdt_ref, wb_ref))
    head = SUBLANES

    @pl.when(s == 0)
    def _():
        raw_s[1] = jnp.zeros(raw_s.shape[1:], F32)
        q_s[1] = jnp.zeros(q_s.shape[1:], BF16)
        ext_s[1] = jnp.zeros(ext_s.shape[1:], F32)

    @pl.when(t2 == 0)
    def _():
        tail_s[...] = jnp.zeros(tail_s.shape, F32)
        hT_s[...] = jnp.zeros(hT_s.shape, F32)

    def step_body(wslot, rslot):
        xn_s[...] = _rms(x1_ref[0], nw_ref[...]).astype(BF16)

        def proj_piece(lo, width, store):
            return lambda: store(_dot(xn_s[...], w_in(lo, lo + width)))

        def raw_store(dst):
            def store(r):
                raw_s[wslot, :, dst:dst + r.shape[1]] = r
            return store

        def ext_store(dst):
            def store(r):
                ext_s[wslot, head:head + tm, dst:dst + r.shape[1]] = r
            return store

        def q_store(dst):
            def store(r):
                q_s[wslot, :, dst:dst + r.shape[1]] = r.astype(BF16)
            return store

        pc = PIECE_COLS
        pieces = ([proj_piece(OFF_XBC + j, pc, ext_store(j)) for j in range(0, SSD_CONV_DIM, pc)]
                  + [proj_piece(OFF_UV + j, pc, raw_store(RAW_UV + j)) for j in range(0, 2 * GM_WIDTH, pc)]
                  + [proj_piece(OFF_DT, DT_PAD, raw_store(RAW_DT))]
                  + [proj_piece(OFF_Z + j, pc, raw_store(RAW_Z + j)) for j in range(0, SSD_INNER, pc)]
                  + [proj_piece(OFF_Q + j, pc, q_store(j)) for j in range(0, XA_WIDTH, pc)]
                  + [proj_piece(OFF_GATE + j, pc, raw_store(RAW_GATE + j))
                     for j in range(0, N_BRANCH * D_MODEL, pc)])

        steps = []

        def conv_block(j):
            cols = slice(j, j + pc)
            acc = ext_s[rslot, head - 3:head - 3 + tm, cols] * cw_ref[0:1, cols]
            for k in range(1, SSD_CONV):
                acc = acc + ext_s[rslot, head - 3 + k:head - 3 + k + tm, cols] * cw_ref[k:k + 1, cols]
            xbc = _silu(acc + cb_ref[:, cols])
            if j < SSD_INNER:
                xs_s[:, cols] = xbc
            else:
                bc_s[:, j - SSD_INNER:j - SSD_INNER + pc] = xbc

        def conv_head():
            ext_s[rslot, 0:head, :] = tail_s[...]

        def conv_tail():
            conv_ref[0] = ext_s[rslot, tm + head - 3:tm + head, :]
            tail_s[...] = ext_s[rslot, tm:tm + head, :]
            dt_s[...] = _softplus(raw_s[rslot, :, RAW_DT:RAW_GATE] + dtb_ref[...])

        steps.append((0, conv_head))
        steps += [(3, functools.partial(conv_block, j)) for j in range(0, SSD_CONV_DIM, pc)]
        steps.append((1, conv_tail))

        def gelu_u(j):
            u_s[:, j:j + pc] = _gelu(raw_s[rslot, :, RAW_UV + j:RAW_UV + j + pc])

        def gelu_v():
            v = _gelu(raw_s[rslot, :, RAW_UV + GM_WIDTH:RAW_Z])
            v_s[...] = _layernorm(v, lnw_ref[...], lnb_ref[...]).astype(BF16)

        steps += [(2, functools.partial(gelu_u, j)) for j in range(0, GM_WIDTH, pc)]
        steps.append((4, gelu_v))

        neg_a = -jnp.exp(alog_ref[...])
        for c in range(tm // SSD_CHUNK):
            rows = slice(c * SSD_CHUNK, (c + 1) * SSD_CHUNK)
            steps += _ssd_chunk_steps(rows, xs_s, bc_s, dt_s, neg_a, hT_s, y_s, ssd_scratch)

        def gmlp_group(g):
            ii = lax.broadcasted_iota(jnp.int32, (GM_CHUNK, GM_CHUNK), 0)
            jj = lax.broadcasted_iota(jnp.int32, (GM_CHUNK, GM_CHUNK), 1)
            cols = slice(g * GM_GROUP_DIM, (g + 1) * GM_GROUP_DIM)
            w_tril = jnp.where(ii >= jj, ws_ref[g], 0.0).astype(BF16)
            bias = bst_ref[:, g:g + 1]
            for c in range(tm // GM_CHUNK):
                rows = slice(c * GM_CHUNK, (c + 1) * GM_CHUNK)
                mixed = _dot(w_tril, v_s[rows, cols]) + bias
                gm_s[rows, cols] = (u_s[rows, cols] * mixed).astype(BF16)

        steps += [(1, functools.partial(gmlp_group, g)) for g in range(GM_GROUPS)]

        def attn_head(hd):
            cols = slice(hd * XA_HEAD_DIM, (hd + 1) * XA_HEAD_DIM)
            sc = _dot_nt(q_s[rslot, :, cols], k_ref[0, :, cols].astype(BF16)) * (XA_HEAD_DIM ** -0.5)
            p = jnp.exp(sc - jnp.max(sc, axis=-1, keepdims=True))
            o = _dot(p.astype(BF16), v_ref[0, :, cols].astype(BF16))
            xa_s[:, cols] = (o / jnp.sum(p, axis=-1, keepdims=True)).astype(BF16)

        steps += [(2, functools.partial(attn_head, hd)) for hd in range(XA_HEADS)]

        def ssd_out_group(g):
            cols = slice(g * SSD_GROUP_WIDTH, (g + 1) * SSD_GROUP_WIDTH)
            zs = _silu(raw_s[rslot, :, RAW_Z + g * SSD_GROUP_WIDTH:RAW_Z + (g + 1) * SSD_GROUP_WIDTH])
            yv = (y_s[:, cols] + xs_s[:, cols] * dsk_ref[:, cols]) * zs
            yn_s[:, cols] = _rms(yv, snw_ref[:, cols]).astype(BF16)

        steps += [(3, functools.partial(ssd_out_group, g)) for g in range(SSD_GROUPS)]

        def merge_piece(i, br_s, w_ref, j):
            lo = RAW_GATE + i * D_MODEL + j
            term = jax.nn.sigmoid(raw_s[rslot, :, lo:lo + MERGE_COLS]) * _dot(br_s[...], w_ref[:, j:j + MERGE_COLS])
            if i == 0:
                merged_s[:, j:j + MERGE_COLS] = term
            else:
                merged_s[:, j:j + MERGE_COLS] += term

        for i, (br_s, w_ref) in enumerate(((gm_s, wgm_ref), (yn_s, wssd_ref), (xa_s, wxa_ref))):
            steps += [(2, functools.partial(merge_piece, i, br_s, w_ref, j)) for j in range(0, D_MODEL, MERGE_COLS)]

        def out_proj():
            h_ref[0] = x2_ref[0] + _dot(merged_s[...].astype(BF16), wout_ref[...])

        steps.append((1, out_proj))
        _interleave(steps, pieces)

    parity = lax.rem(s, 2)
    pl.when(parity == 0)(functools.partial(step_body, 0, 1))
    pl.when(parity == 1)(functools.partial(step_body, 1, 0))

    @pl.when(t2 == tiles_per_seq - 1)
    def _():
        for pair in range(HEAD_PAIRS):
            lanes = slice(pair * LANES, (pair + 1) * LANES)
            ssm_ref[0, lanes, :] = hT_s[:, lanes].T


def _full(shape):
    return pl.BlockSpec(shape, lambda *_: (0,) * len(shape))


def _mixer(x, mem_k, mem_v, p):
    b, seq, _ = x.shape
    tm = MIXER_ROWS
    assert seq % tm == 0 and tm % SSD_CHUNK == 0
    small = [p["norm_mix_w"], p["w_in_a"], p["w_in_dt"], p["w_in_b"],
             p["gm_ln_w"], p["gm_ln_b"], p["gm_ws"], p["gm_bs_t"],
             p["conv_w"], p["conv_b"], p["dt_bias"], p["a_log"], p["d_skip"], p["ssd_norm_w"],
             p["w_br_gm"], p["w_br_ssd"], p["w_br_xa"], p["w_out"]]
    tps = seq // tm
    n_tiles = b * tps
    tile1 = lambda s: jnp.minimum(s, n_tiles - 1)
    tile2 = lambda s: jnp.maximum(s - 1, 0)
    x_spec = lambda tile: pl.BlockSpec((1, tm, D_MODEL), lambda s: (tile(s) // tps, tile(s) % tps, 0))
    seq_spec = lambda r, c: pl.BlockSpec((1, r, c), lambda s: (tile2(s) // tps, 0, 0))
    return pl.pallas_call(
        functools.partial(_mixer_kernel, tiles_per_seq=tps),
        grid=(n_tiles + 1,),
        in_specs=[x_spec(tile1), x_spec(tile2), seq_spec(MEM_LEN, XA_WIDTH), seq_spec(MEM_LEN, XA_WIDTH)]
                 + [_full(a.shape) for a in small],
        out_specs=[x_spec(tile2), seq_spec(SSD_CONV - 1, SSD_CONV_DIM), seq_spec(SSD_INNER, SSD_STATE)],
        out_shape=[
            jax.ShapeDtypeStruct((b, seq, D_MODEL), F32),
            jax.ShapeDtypeStruct((b, SSD_CONV - 1, SSD_CONV_DIM), F32),
            jax.ShapeDtypeStruct((b, SSD_INNER, SSD_STATE), F32),
        ],
        scratch_shapes=[
            pltpu.VMEM((tm, D_MODEL), BF16),
            pltpu.VMEM((2, tm, RAW_WIDTH), F32),
            pltpu.VMEM((2, tm, XA_WIDTH), BF16),
            pltpu.VMEM((2, tm + SUBLANES, SSD_CONV_DIM), F32),
            pltpu.VMEM((SUBLANES, SSD_CONV_DIM), F32),
            pltpu.VMEM((SSD_STATE, SSD_INNER), F32),
            pltpu.VMEM((tm, SSD_INNER), F32),
            pltpu.VMEM((tm, 2 * SSD_GROUPS * SSD_STATE), F32),
            pltpu.VMEM((tm, DT_PAD), F32),
            pltpu.VMEM((tm, SSD_INNER), F32),
            pltpu.VMEM((tm, GM_WIDTH), F32),
            pltpu.VMEM((tm, GM_WIDTH), BF16),
            pltpu.VMEM((tm, GM_WIDTH), BF16),
            pltpu.VMEM((tm, SSD_INNER), BF16),
            pltpu.VMEM((tm, XA_WIDTH), BF16),
            pltpu.VMEM((tm, D_MODEL), F32),
            pltpu.VMEM((SSD_CHUNK, DT_PAD), F32),
            pltpu.VMEM((DT_PAD, SSD_CHUNK), F32),
            pltpu.VMEM((DT_PAD, SSD_CHUNK), F32),
            pltpu.VMEM((SSD_CHUNK, DT_PAD), F32),
            pltpu.VMEM((SSD_GROUPS, SSD_CHUNK, SSD_CHUNK), F32),
            pltpu.VMEM((SSD_HEADS, SSD_CHUNK, 2 * SSD_CHUNK), BF16),
            pltpu.VMEM((SSD_HEADS, SSD_CHUNK), F32),
            pltpu.VMEM((SSD_CHUNK, SSD_INNER), BF16),
            pltpu.VMEM((SSD_STATE, SSD_INNER), F32),
        ],
        compiler_params=pltpu.CompilerParams(
            dimension_semantics=("arbitrary",), vmem_limit_bytes=VMEM_LIMIT_BYTES),
        name="prompt_mixer",
    )(x, x, mem_k, mem_v, *small)


def _ffn_kernel(h_ref, nw_ref, wup_ref, wdn_ref, fw_ref, o_ref):
    h = h_ref[...]
    hn = _rms(h, nw_ref[...]).astype(BF16)
    acc = h
    for j in range(D_FF // FFN_COLS):
        cols = slice(j * FFN_COLS, (j + 1) * FFN_COLS)
        a = jnp.square(jnp.maximum(_dot(hn, wup_ref[:, cols]), 0.0)).astype(BF16)
        acc = acc + _dot(a, wdn_ref[cols, :])
    o_ref[...] = _rms(acc, fw_ref[...])


def _ffn(h2d, p):
    rows = h2d.shape[0]
    tm = min(FFN_ROWS, rows)
    assert rows % tm == 0
    return pl.pallas_call(
        _ffn_kernel,
        grid=(rows // tm,),
        in_specs=[
            pl.BlockSpec((tm, D_MODEL), lambda i: (i, 0)),
            _full((1, D_MODEL)),
            _full((D_MODEL, D_FF)),
            _full((D_FF, D_MODEL)),
            _full((1, D_MODEL)),
        ],
        out_specs=pl.BlockSpec((tm, D_MODEL), lambda i: (i, 0)),
        out_shape=jax.ShapeDtypeStruct((rows, D_MODEL), F32),
        compiler_params=pltpu.CompilerParams(
            dimension_semantics=("arbitrary",), vmem_limit_bytes=VMEM_LIMIT_BYTES),
        name="ffn",
    )(h2d, p["norm_ffn_w"], p["w_up"], p["w_down"], p["norm_final_w"])


def _sample_proj_kernel(x_ref, nw_ref, wa_ref, wdt_ref, wb_ref, o_ref):
    xn = _rms(x_ref[...], nw_ref[...]).astype(BF16)
    o_ref[:, :OFF_DT] = _dot(xn, wa_ref[...])
    o_ref[:, OFF_DT:OFF_Q] = _dot(xn, wdt_ref[...])
    o_ref[:, OFF_Q:] = _dot(xn, wb_ref[...])


def _sample_proj(x2d, p):
    rows = x2d.shape[0]
    args = [x2d, p["norm_mix_w"], p["w_in_a"], p["w_in_dt"], p["w_in_b"]]
    return pl.pallas_call(
        _sample_proj_kernel,
        grid=(1,),
        in_specs=[_full(a.shape) for a in args],
        out_specs=_full((rows, IN_PROJ_PACKED)),
        out_shape=jax.ShapeDtypeStruct((rows, IN_PROJ_PACKED), F32),
        compiler_params=pltpu.CompilerParams(
            dimension_semantics=("arbitrary",), vmem_limit_bytes=VMEM_LIMIT_BYTES),
        name="sample_proj",
    )(*args)


def _sample_state_kernel(proj_ref, stc_ref, ssm_ref, k_ref, v_ref, lnw_ref, lnb_ref, ws_ref,
                         bs_ref, cw_ref, cb_ref, dtb_ref, alog_ref, dsk_ref, snw_ref,
                         gm_ref, yn_ref, xa_ref, convn_ref, ssmn_ref, gv_ref):
    nb = proj_ref.shape[0]
    proj = proj_ref[...]

    uv = _gelu(proj[:, OFF_UV:OFF_Z])
    v = _layernorm(uv[:, GM_WIDTH:], lnw_ref[...], lnb_ref[...])
    gv_ref[...] = v
    for g in range(GM_GROUPS):
        cols = slice(g * GM_GROUP_DIM, (g + 1) * GM_GROUP_DIM)
        mixed = v[:, cols] * ws_ref[g, 0:1, 0:1] + bs_ref[g:g + 1, 0:1]
        gm_ref[:, cols] = uv[:, cols] * mixed

    st = stc_ref[...]
    xnew = proj[:, OFF_XBC:OFF_DT]
    acc = st[:, 0:SSD_CONV_DIM] * cw_ref[0:1, :]
    for k in range(1, SSD_CONV - 1):
        acc = acc + st[:, k * SSD_CONV_DIM:(k + 1) * SSD_CONV_DIM] * cw_ref[k:k + 1, :]
    acc = acc + xnew * cw_ref[SSD_CONV - 1:SSD_CONV, :]
    convn_ref[:, 0:(SSD_CONV - 2) * SSD_CONV_DIM] = st[:, SSD_CONV_DIM:]
    convn_ref[:, (SSD_CONV - 2) * SSD_CONV_DIM:] = xnew
    xbc = _silu(acc + cb_ref[...])
    xs = xbc[:, :SSD_INNER]
    bm = xbc[:, SSD_INNER:SSD_INNER + SSD_GROUPS * SSD_STATE].astype(BF16)
    cm = xbc[:, SSD_INNER + SSD_GROUPS * SSD_STATE:].astype(BF16)

    dt = _softplus(proj[:, OFF_DT:OFF_Q] + dtb_ref[...])
    decay = jnp.exp(dt * -jnp.exp(alog_ref[...]))
    hh = lax.broadcasted_iota(jnp.int32, (DT_PAD, SSD_INNER), 0)
    cc = lax.broadcasted_iota(jnp.int32, (DT_PAD, SSD_INNER), 1)
    spread = (jnp.right_shift(cc, SSD_HEAD_DIM.bit_length() - 1) == hh).astype(BF16)
    dt_wide = sum(_dot(piece, spread) for piece in _split3(dt))
    xdt = xs * dt_wide

    rid = lax.broadcasted_iota(jnp.int32, (nb, 1), 0)
    scale = XA_HEAD_DIM ** -0.5
    kv_vregs = MEM_LEN * XA_HEADS // SUBLANES
    lane_ones = jnp.ones((XA_HEAD_DIM, XA_HEAD_DIM), BF16)
    qv = proj[:, OFF_Q:OFF_GATE]
    y = jnp.zeros((nb, SSD_INNER), F32)
    for bi in range(nb):
        mine = rid == bi
        x_row = jnp.where(mine, xdt, 0.0).astype(BF16)
        parts = []
        for g in range(SSD_GROUPS):
            cols = slice(g * SSD_GROUP_WIDTH, (g + 1) * SSD_GROUP_WIDTH)
            ncols = slice(g * SSD_STATE, (g + 1) * SSD_STATE)
            upd = _dot_tn(x_row[:, cols], bm[:, ncols])
            heads = range(g * SSD_HEADS // SSD_GROUPS, (g + 1) * SSD_HEADS // SSD_GROUPS)
            e_rows = jnp.concatenate(
                [jnp.broadcast_to(decay[bi:bi + 1, h:h + 1], (SSD_HEAD_DIM, SSD_STATE)) for h in heads],
                axis=0)
            h_new = ssm_ref[bi, cols, :] * e_rows + upd
            ssmn_ref[bi, cols, :] = h_new
            parts.append(_dot_nt(cm[:, ncols], h_new.astype(BF16)))
        y = y + jnp.where(mine, jnp.concatenate(parts, axis=-1), 0.0)

        q_rows = jnp.concatenate(
            [qv[bi:bi + 1, hd * XA_HEAD_DIM:(hd + 1) * XA_HEAD_DIM] for hd in range(XA_HEADS)]
            * (SUBLANES // XA_HEADS), axis=0)
        kq = (k_ref[bi].reshape(kv_vregs, SUBLANES, XA_HEAD_DIM) * q_rows[None]).astype(BF16)
        s = _dot(kq.reshape(kv_vregs * SUBLANES, XA_HEAD_DIM), lane_ones) * scale
        s = s.reshape(kv_vregs, SUBLANES, XA_HEAD_DIM)
        pexp = jnp.exp(s - _fold_heads(_reduce_leading(s, jnp.maximum), jnp.maximum)[None])
        o = _reduce_leading(pexp * v_ref[bi].reshape(kv_vregs, SUBLANES, XA_HEAD_DIM), jnp.add)
        o = _fold_heads(o, jnp.add) / _fold_heads(_reduce_leading(pexp, jnp.add), jnp.add)
        for hd in range(XA_HEADS):
            xa_ref[bi:bi + 1, hd * XA_HEAD_DIM:(hd + 1) * XA_HEAD_DIM] = o[hd:hd + 1, :]

    zs = _silu(proj[:, OFF_Z:OFF_XBC])
    yn_ref[...] = _group_rms((y + xs * dsk_ref[...]) * zs, snw_ref[...])


def _sample_state(proj, state_conv2d, state_ssm, cache_k, cache_v, p):
    n = proj.shape[0]
    nb = SAMPLE_BLOCK
    assert n % nb == 0
    conv_w = (SSD_CONV - 1) * SSD_CONV_DIM
    small = [p["gm_ln_w"], p["gm_ln_b"], p["gm_ws"], p["gm_bs"], p["conv_w"], p["conv_b"],
             p["dt_bias"], p["a_log"], p["d_skip"], p["ssd_norm_w"]]
    rows2 = lambda w: pl.BlockSpec((nb, w), lambda i: (i, 0))
    rows3 = lambda a, c: pl.BlockSpec((nb, a, c), lambda i: (i, 0, 0))
    cache = rows3(MEM_LEN * XA_HEADS, XA_HEAD_DIM)
    return pl.pallas_call(
        _sample_state_kernel,
        grid=(n // nb,),
        in_specs=[rows2(IN_PROJ_PACKED), rows2(conv_w), rows3(SSD_INNER, SSD_STATE), cache, cache]
                 + [_full(a.shape) for a in small],
        out_specs=[rows2(GM_WIDTH), rows2(SSD_INNER), rows2(XA_WIDTH), rows2(conv_w),
                   rows3(SSD_INNER, SSD_STATE), rows2(GM_WIDTH)],
        out_shape=[
            jax.ShapeDtypeStruct((n, GM_WIDTH), F32),
            jax.ShapeDtypeStruct((n, SSD_INNER), F32),
            jax.ShapeDtypeStruct((n, XA_WIDTH), F32),
            jax.ShapeDtypeStruct((n, conv_w), F32),
            jax.ShapeDtypeStruct((n, SSD_INNER, SSD_STATE), F32),
            jax.ShapeDtypeStruct((n, GM_WIDTH), F32),
        ],
        compiler_params=pltpu.CompilerParams(
            dimension_semantics=("arbitrary",), vmem_limit_bytes=VMEM_LIMIT_BYTES),
        name="sample_state",
    )(proj, state_conv2d, state_ssm, cache_k, cache_v, *small)


def _sample_merge_kernel(x_ref, proj_ref, gm_ref, yn_ref, xa_ref, wgm_ref, wssd_ref, wxa_ref,
                         wout_ref, h_ref):
    merged = None
    for i, (br_ref, w_ref) in enumerate(((gm_ref, wgm_ref), (yn_ref, wssd_ref), (xa_ref, wxa_ref))):
        gate = jax.nn.sigmoid(proj_ref[:, OFF_GATE + i * D_MODEL:OFF_GATE + (i + 1) * D_MODEL])
        term = gate * _dot(br_ref[...].astype(BF16), w_ref[...])
        merged = term if merged is None else merged + term
    h_ref[...] = x_ref[...] + _dot(merged.astype(BF16), wout_ref[...])


def _sample_merge(x2d, proj, gm, yn, xa, p):
    args = [x2d, proj, gm, yn, xa, p["w_br_gm"], p["w_br_ssd"], p["w_br_xa"], p["w_out"]]
    return pl.pallas_call(
        _sample_merge_kernel,
        grid=(1,),
        in_specs=[_full(a.shape) for a in args],
        out_specs=_full(x2d.shape),
        out_shape=jax.ShapeDtypeStruct(x2d.shape, F32),
        compiler_params=pltpu.CompilerParams(
            dimension_semantics=("arbitrary",), vmem_limit_bytes=VMEM_LIMIT_BYTES),
        name="sample_merge",
    )(*args)


def _prep_layer(l, norm_mix_w, w_in, gm_ln_w, gm_ln_b, gm_ws, gm_bs, conv_w, conv_b, dt_bias, a_log,
                d_skip, ssd_norm_w, mem_norm_w, w_mem_k, w_mem_v, w_br_gm, w_br_ssd, w_br_xa, w_out,
                norm_ffn_w, w_up, w_down, norm_final_w):
    row = lambda a: a.reshape(1, -1)
    pad_heads = lambda a: jnp.pad(a, (0, DT_PAD - SSD_HEADS)).reshape(1, DT_PAD)
    w = w_in[l]
    w_dt = jnp.pad(w[:, OFF_DT:OFF_DT + SSD_HEADS], ((0, 0), (0, DT_PAD - SSD_HEADS)))
    return {
        "norm_mix_w": row(norm_mix_w[l]),
        "w_in_a": w[:, :OFF_DT].astype(BF16), "w_in_dt": w_dt.astype(BF16),
        "w_in_b": w[:, OFF_DT + SSD_HEADS:].astype(BF16),
        "gm_ln_w": row(gm_ln_w[l]), "gm_ln_b": row(gm_ln_b[l]),
        "gm_ws": gm_ws[l], "gm_bs": gm_bs[l], "gm_bs_t": gm_bs[l].T,
        "conv_w": conv_w[l], "conv_b": row(conv_b[l]),
        "dt_bias": pad_heads(dt_bias[l]), "a_log": pad_heads(a_log[l]),
        "d_skip": row(jnp.repeat(d_skip[l], SSD_HEAD_DIM)), "ssd_norm_w": row(ssd_norm_w[l]),
        "mem_norm_w": row(mem_norm_w[l]),
        "w_kv": jnp.concatenate([w_mem_k[l], w_mem_v[l]], axis=1).astype(BF16),
        "w_br_gm": w_br_gm[l].astype(BF16), "w_br_ssd": w_br_ssd[l].astype(BF16),
        "w_br_xa": w_br_xa[l].astype(BF16), "w_out": w_out[l].astype(BF16),
        "norm_ffn_w": row(norm_ffn_w[l]), "w_up": w_up[l].astype(BF16),
        "w_down": w_down[l].astype(BF16), "norm_final_w": row(norm_final_w),
    }


def kernel(x_prompt, x_sample, mem_prompt, cache_mem_k, cache_mem_v, state_conv, state_ssm, norm_mix_w, w_in, gm_ln_w, gm_ln_b, gm_ws, gm_bs, conv_w, conv_b, dt_bias, a_log, d_skip, ssd_norm_w, mem_norm_w, w_mem_k, w_mem_v, w_br_gm, w_br_ssd, w_br_xa, w_out, norm_ffn_w, w_up, w_down, norm_final_w):
    depth = w_in.shape[0]
    assert depth == 1, "the final norm is fused into the MLP kernel of the last (only) layer"
    b, seq, _ = x_prompt.shape
    n, dec_seq, _ = x_sample.shape
    assert dec_seq == 1
    p = _prep_layer(0, norm_mix_w, w_in, gm_ln_w, gm_ln_b, gm_ws, gm_bs, conv_w, conv_b, dt_bias,
                    a_log, d_skip, ssd_norm_w, mem_norm_w, w_mem_k, w_mem_v, w_br_gm, w_br_ssd,
                    w_br_xa, w_out, norm_ffn_w, w_up, w_down, norm_final_w)

    mem_k, mem_v = _memkv(mem_prompt.reshape(b * MEM_LEN, D_MODEL), p["mem_norm_w"], p["w_kv"])
    h1, conv_p, ssm_p = _mixer(x_prompt, mem_k.reshape(b, MEM_LEN, XA_WIDTH),
                               mem_v.reshape(b, MEM_LEN, XA_WIDTH), p)
    y_prompt = _ffn(h1.reshape(b * seq, D_MODEL), p).reshape(b, seq, D_MODEL)

    xs2d = x_sample.reshape(n, D_MODEL)
    proj = _sample_proj(xs2d, p)
    gm, yn, xa, conv_s, ssm_s, gv = _sample_state(
        proj, state_conv[0].reshape(n, (SSD_CONV - 1) * SSD_CONV_DIM), state_ssm[0].reshape(n, SSD_INNER, SSD_STATE),
        cache_mem_k[0].reshape(n, MEM_LEN * XA_HEADS, XA_HEAD_DIM),
        cache_mem_v[0].reshape(n, MEM_LEN * XA_HEADS, XA_HEAD_DIM), p)
    hs = _sample_merge(xs2d, proj, gm, yn, xa, p)
    y_sample = _ffn(hs, p).reshape(n, 1, D_MODEL)

    kv_shape = (1, b, MEM_LEN, XA_HEADS, XA_HEAD_DIM)
    state_shape = (SSD_HEADS, SSD_HEAD_DIM, SSD_STATE)
    return (y_prompt, y_sample,
            mem_k.reshape(kv_shape), mem_v.reshape(kv_shape),
            conv_p.reshape(1, b, SSD_CONV - 1, SSD_CONV_DIM), ssm_p.reshape((1, b) + state_shape),
            conv_s.reshape(1, n, SSD_CONV - 1, SSD_CONV_DIM), ssm_s.reshape((1, n) + state_shape),
            gv.reshape(1, n, 1, GM_WIDTH))
```

```python
import functools

import jax
import jax.numpy as jnp
from jax import lax
from jax.experimental import pallas as pl
from jax.experimental.pallas import tpu as pltpu

F32 = jnp.float32
BF16 = jnp.bfloat16

D_MODEL = 1024
GM_CHUNK = 128
GM_GROUPS = 4
GM_GROUP_DIM = 128
GM_WIDTH = GM_GROUPS * GM_GROUP_DIM
SSD_HEADS = 16
SSD_HEAD_DIM = 64
SSD_INNER = SSD_HEADS * SSD_HEAD_DIM
SSD_GROUPS = 2
SSD_GROUP_WIDTH = SSD_INNER // SSD_GROUPS
SSD_STATE = 128
SSD_CONV = 4
SSD_CHUNK = 128
SSD_CONV_DIM = SSD_INNER + 2 * SSD_GROUPS * SSD_STATE
MEM_LEN = 256
XA_HEADS = 4
XA_HEAD_DIM = 128
XA_WIDTH = XA_HEADS * XA_HEAD_DIM
N_BRANCH = 3
D_FF = 4 * D_MODEL
EPS = 1e-6

SUBLANES = 8
LANES = 128
HEADS_PER_LANE_TILE = LANES // SSD_HEAD_DIM
HEAD_PAIRS = SSD_HEADS // HEADS_PER_LANE_TILE
PAIRS_PER_GROUP = HEAD_PAIRS // SSD_GROUPS

DT_PAD = LANES
OFF_UV = 0
OFF_Z = OFF_UV + 2 * GM_WIDTH
OFF_XBC = OFF_Z + SSD_INNER
OFF_DT = OFF_XBC + SSD_CONV_DIM
OFF_Q = OFF_DT + DT_PAD
OFF_GATE = OFF_Q + XA_WIDTH
IN_PROJ_PACKED = OFF_GATE + N_BRANCH * D_MODEL

RAW_UV = 0
RAW_Z = RAW_UV + 2 * GM_WIDTH
RAW_DT = RAW_Z + SSD_INNER
RAW_GATE = RAW_DT + DT_PAD
RAW_WIDTH = RAW_GATE + N_BRANCH * D_MODEL

MIXER_ROWS = 256
PIECE_COLS = 256
MERGE_COLS = 512
FFN_ROWS = 512
FFN_COLS = 1024
MEMKV_ROWS = 512
PACK_ROWS = 256
SAMPLE_BLOCK = SUBLANES
VMEM_LIMIT_BYTES = 56 * 1024 * 1024


def _dot(a, b):
    return jnp.dot(a, b, preferred_element_type=F32)


def _dot_nt(a, b):
    return lax.dot_general(a, b, (((1,), (1,)), ((), ())), preferred_element_type=F32)


def _dot_tn(a, b):
    return lax.dot_general(a, b, (((0,), (0,)), ((), ())), preferred_element_type=F32)


def _rms(x, w):
    return x * lax.rsqrt(jnp.mean(x * x, axis=-1, keepdims=True) + EPS) * w


def _layernorm(x, w, b):
    xc = x - jnp.mean(x, axis=-1, keepdims=True)
    return xc * lax.rsqrt(jnp.mean(xc * xc, axis=-1, keepdims=True) + EPS) * w + b


def _gelu(x):
    return 0.5 * x * (1.0 + lax.erf(x * 0.7071067811865476))


def _silu(x):
    return x * jax.nn.sigmoid(x)


def _softplus(x):
    return jnp.maximum(x, 0.0) + jnp.log1p(jnp.exp(-jnp.abs(x)))


def _split3(x):
    p1 = x.astype(BF16)
    r1 = x - p1.astype(F32)
    p2 = r1.astype(BF16)
    p3 = (r1 - p2.astype(F32)).astype(BF16)
    return p1, p2, p3


REDUCE_CHAINS = 8


def _reduce_leading(x, op):
    n = x.shape[0]
    assert n % REDUCE_CHAINS == 0
    acc = [x[w] for w in range(REDUCE_CHAINS)]
    for i in range(REDUCE_CHAINS, n, REDUCE_CHAINS):
        acc = [op(a, x[i + w]) for w, a in enumerate(acc)]
    while len(acc) > 1:
        acc = [op(acc[2 * i], acc[2 * i + 1]) for i in range(len(acc) // 2)]
    return acc[0]


def _fold_heads(a, op):
    assert SUBLANES == 2 * XA_HEADS
    return op(a, pltpu.roll(a, XA_HEADS, axis=0))


def _group_rms(y, w):
    parts = []
    for g in range(SSD_GROUPS):
        cols = slice(g * SSD_GROUP_WIDTH, (g + 1) * SSD_GROUP_WIDTH)
        parts.append(_rms(y[:, cols], w[:, cols]))
    return jnp.concatenate(parts, axis=-1)


def _memkv_kernel(mem_ref, nw_ref, wkv_ref, k_ref, v_ref):
    mn = _rms(mem_ref[...], nw_ref[...]).astype(BF16)
    kv = _dot(mn, wkv_ref[...])
    k_ref[...] = kv[:, :XA_WIDTH]
    v_ref[...] = kv[:, XA_WIDTH:]


def _memkv(mem2d, norm_w, wkv):
    rows = mem2d.shape[0]
    tm = min(MEMKV_ROWS, rows)
    return pl.pallas_call(
        _memkv_kernel,
        grid=(rows // tm,),
        in_specs=[
            pl.BlockSpec((tm, D_MODEL), lambda i: (i, 0)),
            pl.BlockSpec((1, D_MODEL), lambda i: (0, 0)),
            pl.BlockSpec((D_MODEL, 2 * XA_WIDTH), lambda i: (0, 0)),
        ],
        out_specs=[
            pl.BlockSpec((tm, XA_WIDTH), lambda i: (i, 0)),
            pl.BlockSpec((tm, XA_WIDTH), lambda i: (i, 0)),
        ],
        out_shape=[jax.ShapeDtypeStruct((rows, XA_WIDTH), F32)] * 2,
        compiler_params=pltpu.CompilerParams(
            dimension_semantics=("arbitrary",), vmem_limit_bytes=VMEM_LIMIT_BYTES),
        name="memkv",
    )(mem2d, norm_w, wkv)


def _ssd_chunk_steps(rows, xs_s, bc_s, dt_s, neg_a, hT_s, y_s, scratch):
    acum_s, acumt_s, dtt_s, wc_s, cb_s, lhs_s, cd_s, xw_s, sg_s = scratch
    q = SSD_CHUNK

    def masks():
        ii = lax.broadcasted_iota(jnp.int32, (q, q), 0)
        jj = lax.broadcasted_iota(jnp.int32, (q, q), 1)
        low_half = lax.broadcasted_iota(jnp.int32, (q, LANES), 1) < SSD_HEAD_DIM
        low_half_row = lax.broadcasted_iota(jnp.int32, (1, LANES), 1) < SSD_HEAD_DIM
        return ii >= jj, low_half, low_half_row

    def prologue():
        causal, _, _ = masks()
        dt = dt_s[rows, :]
        d1, d2, d3 = _split3(dt * neg_a)
        tri = causal.astype(BF16)
        acum = _dot(tri, d1) + _dot(tri, d2) + _dot(tri, d3)
        acum_s[...] = acum
        acumt_s[...] = acum.T
        dtt_s[...] = dt.T
        wc_s[...] = dt * jnp.exp(acum[q - 1:q, :] - acum)

    def group_step(g):
        _, low_half, _ = masks()
        b_g = bc_s[rows, g * SSD_STATE:(g + 1) * SSD_STATE]
        c_g = bc_s[rows, (SSD_GROUPS + g) * SSD_STATE:(SSD_GROUPS + g + 1) * SSD_STATE]
        cb_s[g] = _dot_nt(c_g.astype(BF16), b_g.astype(BF16))
        for pr in range(PAIRS_PER_GROUP):
            pair = g * PAIRS_PER_GROUP + pr
            lanes = slice(pair * LANES, (pair + 1) * LANES)
            h0 = pair * HEADS_PER_LANE_TILE
            w_pair = jnp.where(low_half, jnp.broadcast_to(wc_s[:, h0:h0 + 1], (q, LANES)),
                               jnp.broadcast_to(wc_s[:, h0 + 1:h0 + 2], (q, LANES)))
            xw_s[:, lanes] = (xs_s[rows, lanes] * w_pair).astype(BF16)
        cols = slice(g * SSD_GROUP_WIDTH, (g + 1) * SSD_GROUP_WIDTH)
        sg_s[:, cols] = _dot(b_g.T.astype(BF16), xw_s[:, cols])

    def head_step(h):
        causal, _, _ = masks()
        g = h // (SSD_HEADS // SSD_GROUPS)
        c_g = bc_s[rows, (SSD_GROUPS + g) * SSD_STATE:(SSD_GROUPS + g + 1) * SSD_STATE]
        col = jnp.broadcast_to(acum_s[:, h:h + 1], (q, q))
        seg = jnp.where(causal, col - acumt_s[h:h + 1, :], -jnp.inf)
        lhs_s[h, :, 0:q] = (cb_s[g] * jnp.exp(seg) * dtt_s[h:h + 1, :]).astype(BF16)
        lhs_s[h, :, q:2 * q] = (c_g * jnp.exp(col)).astype(BF16)
        cd_s[h:h + 1, :] = jnp.exp(col[q - 1:q, :])

    def pair_step(pair):
        _, low_half, low_half_row = masks()
        lanes = slice(pair * LANES, (pair + 1) * LANES)
        h0 = pair * HEADS_PER_LANE_TILE
        h_prev = hT_s[:, lanes]
        rhs = jnp.concatenate([xs_s[rows, lanes].astype(BF16), h_prev.astype(BF16)], axis=0)
        y_s[rows, lanes] = jnp.where(low_half, _dot(lhs_s[h0], rhs), _dot(lhs_s[h0 + 1], rhs))
        decay = jnp.where(low_half_row, cd_s[h0:h0 + 1, :], cd_s[h0 + 1:h0 + 2, :])
        hT_s[:, lanes] = h_prev * decay + sg_s[:, lanes]

    steps = [(2, prologue)]
    steps += [(3, functools.partial(group_step, g)) for g in range(SSD_GROUPS)]
    steps += [(2, functools.partial(head_step, h)) for h in range(SSD_HEADS)]
    steps += [(1, functools.partial(pair_step, pair)) for pair in range(HEAD_PAIRS)]
    return steps


def _interleave(steps, pieces):
    total = sum(w for w, _ in steps)
    done = acc = 0
    for w, step in steps:
        acc += w
        upto = len(pieces) * acc // total
        for piece in pieces[done:upto]:
            piece()
        done = upto
        step()
    assert done == len(pieces)


def _in_proj_cols(w_refs, lo, hi):
    wa_ref, wdt_ref, wb_ref = w_refs
    if hi <= OFF_DT:
        return wa_ref[:, lo:hi]
    if lo >= OFF_Q:
        return wb_ref[:, lo - OFF_Q:hi - OFF_Q]
    assert (lo, hi) == (OFF_DT, OFF_Q)
    return wdt_ref[...]


def _mixer_kernel(x1_ref, x2_ref, k_ref, v_ref, nw_ref, wa_ref, wdt_ref, wb_ref, lnw_ref, lnb_ref, ws_ref,
                  bst_ref, cw_ref, cb_ref, dtb_ref, alog_ref, dsk_ref, snw_ref,
                  wgm_ref, wssd_ref, wxa_ref, wout_ref,
                  h_ref, conv_ref, ssm_ref,
                  xn_s, raw_s, q_s, ext_s, tail_s, hT_s, xs_s, bc_s, dt_s, y_s, u_s, v_s, gm_s, yn_s, xa_s, merged_s,
                  *ssd_scratch, tiles_per_seq):
    tm = x1_ref.shape[1]
    s = pl.program_id(0)
    t2 = lax.rem(jnp.maximum(s - 1, 0), tiles_per_seq)
    w_in = functools.partial(_in_proj_cols, (wa_ref, wdt_ref, wb_ref))
    head = SUBLANES

    @pl.when(s == 0)
    def _():
        raw_s[1] = jnp.zeros(raw_s.shape[1:], F32)
        q_s[1] = jnp.zeros(q_s.shape[1:], BF16)
        ext_s[1] = jnp.zeros(ext_s.shape[1:], F32)

    @pl.when(t2 == 0)
    def _():
        tail_s[...] = jnp.zeros(tail_s.shape, F32)
        hT_s[...] = jnp.zeros(hT_s.shape, F32)

    def step_body(wslot, rslot):
        xn_s[...] = _rms(x1_ref[0], nw_ref[...]).astype(BF16)

        def proj_piece(lo, width, store):
            return lambda: store(_dot(xn_s[...], w_in(lo, lo + width)))

        def raw_store(dst):
            def store(r):
                raw_s[wslot, :, dst:dst + r.shape[1]] = r
            return store

        def ext_store(dst):
            def store(r):
                ext_s[wslot, head:head + tm, dst:dst + r.shape[1]] = r
            return store

        def q_store(dst):
            def store(r):
                q_s[wslot, :, dst:dst + r.shape[1]] = r.astype(BF16)
            return store

        pc = PIECE_COLS
        pieces = ([proj_piece(OFF_XBC + j, pc, ext_store(j)) for j in range(0, SSD_CONV_DIM, pc)]
                  + [proj_piece(OFF_UV + j, pc, raw_store(RAW_UV + j)) for j in range(0, 2 * GM_WIDTH, pc)]
                  + [proj_piece(OFF_DT, DT_PAD, raw_store(RAW_DT))]
                  + [proj_piece(OFF_Z + j, pc, raw_store(RAW_Z + j)) for j in range(0, SSD_INNER, pc)]
                  + [proj_piece(OFF_Q + j, pc, q_store(j)) for j in range(0, XA_WIDTH, pc)]
                  + [proj_piece(OFF_GATE + j, pc, raw_store(RAW_GATE + j))
                     for j in range(0, N_BRANCH * D_MODEL, pc)])

        steps = []

        def conv_block(j):
            cols = slice(j, j + pc)
            acc = ext_s[rslot, head - 3:head - 3 + tm, cols] * cw_ref[0:1, cols]
            for k in range(1, SSD_CONV):
                acc = acc + ext_s[rslot, head - 3 + k:head - 3 + k + tm, cols] * cw_ref[k:k + 1, cols]
            xbc = _silu(acc + cb_ref[:, cols])
            if j < SSD_INNER:
                xs_s[:, cols] = xbc
            else:
                bc_s[:, j - SSD_INNER:j - SSD_INNER + pc] = xbc

        def conv_head():
            ext_s[rslot, 0:head, :] = tail_s[...]

        def conv_tail():
            conv_ref[0] = ext_s[rslot, tm + head - 3:tm + head, :]
            tail_s[...] = ext_s[rslot, tm:tm + head, :]
            dt_s[...] = _softplus(raw_s[rslot, :, RAW_DT:RAW_GATE] + dtb_ref[...])

        steps.append((0, conv_head))
        steps += [(3, functools.partial(conv_block, j)) for j in range(0, SSD_CONV_DIM, pc)]
        steps.append((1, conv_tail))

        def gelu_u(j):
            u_s[:, j:j + pc] = _gelu(raw_s[rslot, :, RAW_UV + j:RAW_UV + j + pc])

        def gelu_v():
            v = _gelu(raw_s[rslot, :, RAW_UV + GM_WIDTH:RAW_Z])
            v_s[...] = _layernorm(v, lnw_ref[...], lnb_ref[...]).astype(BF16)

        steps += [(2, functools.partial(gelu_u, j)) for j in range(0, GM_WIDTH, pc)]
        steps.append((4, gelu_v))

        neg_a = -jnp.exp(alog_ref[...])
        for c in range(tm // SSD_CHUNK):
            rows = slice(c * SSD_CHUNK, (c + 1) * SSD_CHUNK)
            steps += _ssd_chunk_steps(rows, xs_s, bc_s, dt_s, neg_a, hT_s, y_s, ssd_scratch)

        def gmlp_group(g):
            ii = lax.broadcasted_iota(jnp.int32, (GM_CHUNK, GM_CHUNK), 0)
            jj = lax.broadcasted_iota(jnp.int32, (GM_CHUNK, GM_CHUNK), 1)
            cols = slice(g * GM_GROUP_DIM, (g + 1) * GM_GROUP_DIM)
            w_tril = jnp.where(ii >= jj, ws_ref[g], 0.0).astype(BF16)
            bias = bst_ref[:, g:g + 1]
            for c in range(tm // GM_CHUNK):
                rows = slice(c * GM_CHUNK, (c + 1) * GM_CHUNK)
                mixed = _dot(w_tril, v_s[rows, cols]) + bias
                gm_s[rows, cols] = (u_s[rows, cols] * mixed).astype(BF16)

        steps += [(1, functools.partial(gmlp_group, g)) for g in range(GM_GROUPS)]

        def attn_head(hd):
            cols = slice(hd * XA_HEAD_DIM, (hd + 1) * XA_HEAD_DIM)
            sc = _dot_nt(q_s[rslot, :, cols], k_ref[0, :, cols].astype(BF16)) * (XA_HEAD_DIM ** -0.5)
            p = jnp.exp(sc - jnp.max(sc, axis=-1, keepdims=True))
            o = _dot(p.astype(BF16), v_ref[0, :, cols].astype(BF16))
            xa_s[:, cols] = (o / jnp.sum(p, axis=-1, keepdims=True)).astype(BF16)

        steps += [(2, functools.partial(attn_head, hd)) for hd in range(XA_HEADS)]

        def ssd_out_group(g):
            cols = slice(g * SSD_GROUP_WIDTH, (g + 1) * SSD_GROUP_WIDTH)
            zs = _silu(raw_s[rslot, :, RAW_Z + g * SSD_GROUP_WIDTH:RAW_Z + (g + 1) * SSD_GROUP_WIDTH])
            yv = (y_s[:, cols] + xs_s[:, cols] * dsk_ref[:, cols]) * zs
            yn_s[:, cols] = _rms(yv, snw_ref[:, cols]).astype(BF16)

        steps += [(3, functools.partial(ssd_out_group, g)) for g in range(SSD_GROUPS)]

        def merge_piece(i, br_s, w_ref, j):
            lo = RAW_GATE + i * D_MODEL + j
            term = jax.nn.sigmoid(raw_s[rslot, :, lo:lo + MERGE_COLS]) * _dot(br_s[...], w_ref[:, j:j + MERGE_COLS])
            if i == 0:
                merged_s[:, j:j + MERGE_COLS] = term
            else:
                merged_s[:, j:j + MERGE_COLS] += term

        for i, (br_s, w_ref) in enumerate(((gm_s, wgm_ref), (yn_s, wssd_ref), (xa_s, wxa_ref))):
            steps += [(2, functools.partial(merge_piece, i, br_s, w_ref, j)) for j in range(0, D_MODEL, MERGE_COLS)]

        def out_proj():
            h_ref[0] = x2_ref[0] + _dot(merged_s[...].astype(BF16), wout_ref[...])

        steps.append((1, out_proj))
        _interleave(steps, pieces)

    parity = lax.rem(s, 2)
    pl.when(parity == 0)(functools.partial(step_body, 0, 1))
    pl.when(parity == 1)(functools.partial(step_body, 1, 0))

    @pl.when(t2 == tiles_per_seq - 1)
    def _():
        for pair in range(HEAD_PAIRS):
            lanes = slice(pair * LANES, (pair + 1) * LANES)
            ssm_ref[0, lanes, :] = hT_s[:, lanes].T


def _full(shape):
    return pl.BlockSpec(shape, lambda *_: (0,) * len(shape))


def _mixer(x, mem_k, mem_v, p):
    b, seq, _ = x.shape
    tm = MIXER_ROWS
    assert seq % tm == 0 and tm % SSD_CHUNK == 0
    small = [p["norm_mix_w"], p["w_in_a"], p["w_in_dt"], p["w_in_b"],
             p["gm_ln_w"], p["gm_ln_b"], p["gm_ws"], p["gm_bs_t"],
             p["conv_w"], p["conv_b"], p["dt_bias"], p["a_log"], p["d_skip"], p["ssd_norm_w"],
             p["w_br_gm"], p["w_br_ssd"], p["w_br_xa"], p["w_out"]]
    tps = seq // tm
    n_tiles = b * tps
    tile1 = lambda s: jnp.minimum(s, n_tiles - 1)
    tile2 = lambda s: jnp.maximum(s - 1, 0)
    x_spec = lambda tile: pl.BlockSpec((1, tm, D_MODEL), lambda s: (tile(s) // tps, tile(s) % tps, 0))
    seq_spec = lambda r, c: pl.BlockSpec((1, r, c), lambda s: (tile2(s) // tps, 0, 0))
    return pl.pallas_call(
        functools.partial(_mixer_kernel, tiles_per_seq=tps),
        grid=(n_tiles + 1,),
        in_specs=[x_spec(tile1), x_spec(tile2), seq_spec(MEM_LEN, XA_WIDTH), seq_spec(MEM_LEN, XA_WIDTH)]
                 + [_full(a.shape) for a in small],
        out_specs=[x_spec(tile2), seq_spec(SSD_CONV - 1, SSD_CONV_DIM), seq_spec(SSD_INNER, SSD_STATE)],
        out_shape=[
            jax.ShapeDtypeStruct((b, seq, D_MODEL), F32),
            jax.ShapeDtypeStruct((b, SSD_CONV - 1, SSD_CONV_DIM), F32),
            jax.ShapeDtypeStruct((b, SSD_INNER, SSD_STATE), F32),
        ],
        scratch_shapes=[
            pltpu.VMEM((tm, D_MODEL), BF16),
            pltpu.VMEM((2, tm, RAW_WIDTH), F32),
            pltpu.VMEM((2, tm, XA_WIDTH), BF16),
            pltpu.VMEM((2, tm + SUBLANES, SSD_CONV_DIM), F32),
            pltpu.VMEM((SUBLANES, SSD_CONV_DIM), F32),
            pltpu.VMEM((SSD_STATE, SSD_INNER), F32),
            pltpu.VMEM((tm, SSD_INNER), F32),
            pltpu.VMEM((tm, 2 * SSD_GROUPS * SSD_STATE), F32),
            pltpu.VMEM((tm, DT_PAD), F32),
            pltpu.VMEM((tm, SSD_INNER), F32),
            pltpu.VMEM((tm, GM_WIDTH), F32),
            pltpu.VMEM((tm, GM_WIDTH), BF16),
            pltpu.VMEM((tm, GM_WIDTH), BF16),
            pltpu.VMEM((tm, SSD_INNER), BF16),
            pltpu.VMEM((tm, XA_WIDTH), BF16),
            pltpu.VMEM((tm, D_MODEL), F32),
            pltpu.VMEM((SSD_CHUNK, DT_PAD), F32),
            pltpu.VMEM((DT_PAD, SSD_CHUNK), F32),
            pltpu.VMEM((DT_PAD, SSD_CHUNK), F32),
            pltpu.VMEM((SSD_CHUNK, DT_PAD), F32),
            pltpu.VMEM((SSD_GROUPS, SSD_CHUNK, SSD_CHUNK), F32),
            pltpu.VMEM((SSD_HEADS, SSD_CHUNK, 2 * SSD_CHUNK), BF16),
            pltpu.VMEM((SSD_HEADS, SSD_CHUNK), F32),
            pltpu.VMEM((SSD_CHUNK, SSD_INNER), BF16),
            pltpu.VMEM((SSD_STATE, SSD_INNER), F32),
        ],
        compiler_params=pltpu.CompilerParams(
            dimension_semantics=("arbitrary",), vmem_limit_bytes=VMEM_LIMIT_BYTES),
        name="prompt_mixer",
    )(x, x, mem_k, mem_v, *small)


def _ffn_kernel(h_ref, nw_ref, wup_ref, wdn_ref, fw_ref, o_ref):
    h = h_ref[...]
    hn = _rms(h, nw_ref[...]).astype(BF16)
    acc = h
    for j in range(D_FF // FFN_COLS):
        cols = slice(j * FFN_COLS, (j + 1) * FFN_COLS)
        a = jnp.square(jnp.maximum(_dot(hn, wup_ref[:, cols]), 0.0)).astype(BF16)
        acc = acc + _dot(a, wdn_ref[cols, :])
    o_ref[...] = _rms(acc, fw_ref[...])


def _ffn(h2d, p):
    rows = h2d.shape[0]
    tm = min(FFN_ROWS, rows)
    assert rows % tm == 0
    return pl.pallas_call(
        _ffn_kernel,
        grid=(rows // tm,),
        in_specs=[
            pl.BlockSpec((tm, D_MODEL), lambda i: (i, 0)),
            _full((1, D_MODEL)),
            _full((D_MODEL, D_FF)),
            _full((D_FF, D_MODEL)),
            _full((1, D_MODEL)),
        ],
        out_specs=pl.BlockSpec((tm, D_MODEL), lambda i: (i, 0)),
        out_shape=jax.ShapeDtypeStruct((rows, D_MODEL), F32),
        compiler_params=pltpu.CompilerParams(
            dimension_semantics=("arbitrary",), vmem_limit_bytes=VMEM_LIMIT_BYTES),
        name="ffn",
    )(h2d, p["norm_ffn_w"], p["w_up"], p["w_down"], p["norm_final_w"])


def _sample_proj_kernel(x_ref, nw_ref, wa_ref, wdt_ref, wb_ref, o_ref):
    xn = _rms(x_ref[...], nw_ref[...]).astype(BF16)
    o_ref[:, :OFF_DT] = _dot(xn, wa_ref[...])
    o_ref[:, OFF_DT:OFF_Q] = _dot(xn, wdt_ref[...])
    o_ref[:, OFF_Q:] = _dot(xn, wb_ref[...])


def _sample_proj(x2d, p):
    rows = x2d.shape[0]
    args = [x2d, p["norm_mix_w"], p["w_in_a"], p["w_in_dt"], p["w_in_b"]]
    return pl.pallas_call(
        _sample_proj_kernel,
        grid=(1,),
        in_specs=[_full(a.shape) for a in args],
        out_specs=_full((rows, IN_PROJ_PACKED)),
        out_shape=jax.ShapeDtypeStruct((rows, IN_PROJ_PACKED), F32),
        compiler_params=pltpu.CompilerParams(
            dimension_semantics=("arbitrary",), vmem_limit_bytes=VMEM_LIMIT_BYTES),
        name="sample_proj",
    )(*args)


def _sample_state_kernel(proj_ref, stc_ref, ssm_ref, k_ref, v_ref, lnw_ref, lnb_ref, ws_ref,
                         bs_ref, cw_ref, cb_ref, dtb_ref, alog_ref, dsk_ref, snw_ref,
                         gm_ref, yn_ref, xa_ref, convn_ref, ssmn_ref, gv_ref):
    nb = proj_ref.shape[0]
    proj = proj_ref[...]

    uv = _gelu(proj[:, OFF_UV:OFF_Z])
    v = _layernorm(uv[:, GM_WIDTH:], lnw_ref[...], lnb_ref[...])
    gv_ref[...] = v
    for g in range(GM_GROUPS):
        cols = slice(g * GM_GROUP_DIM, (g + 1) * GM_GROUP_DIM)
        mixed = v[:, cols] * ws_ref[g, 0:1, 0:1] + bs_ref[g:g + 1, 0:1]
        gm_ref[:, cols] = uv[:, cols] * mixed

    st = stc_ref[...]
    xnew = proj[:, OFF_XBC:OFF_DT]
    acc = st[:, 0:SSD_CONV_DIM] * cw_ref[0:1, :]
    for k in range(1, SSD_CONV - 1):
        acc = acc + st[:, k * SSD_CONV_DIM:(k + 1) * SSD_CONV_DIM] * cw_ref[k:k + 1, :]
    acc = acc + xnew * cw_ref[SSD_CONV - 1:SSD_CONV, :]
    convn_ref[:, 0:(SSD_CONV - 2) * SSD_CONV_DIM] = st[:, SSD_CONV_DIM:]
    convn_ref[:, (SSD_CONV - 2) * SSD_CONV_DIM:] = xnew
    xbc = _silu(acc + cb_ref[...])
    xs = xbc[:, :SSD_INNER]
    bm = xbc[:, SSD_INNER:SSD_INNER + SSD_GROUPS * SSD_STATE].astype(BF16)
    cm = xbc[:, SSD_INNER + SSD_GROUPS * SSD_STATE:].astype(BF16)

    dt = _softplus(proj[:, OFF_DT:OFF_Q] + dtb_ref[...])
    decay = jnp.exp(dt * -jnp.exp(alog_ref[...]))
    hh = lax.broadcasted_iota(jnp.int32, (DT_PAD, SSD_INNER), 0)
    cc = lax.broadcasted_iota(jnp.int32, (DT_PAD, SSD_INNER), 1)
    spread = (jnp.right_shift(cc, SSD_HEAD_DIM.bit_length() - 1) == hh).astype(BF16)
    dt_wide = sum(_dot(piece, spread) for piece in _split3(dt))
    xdt = xs * dt_wide

    rid = lax.broadcasted_iota(jnp.int32, (nb, 1), 0)
    scale = XA_HEAD_DIM ** -0.5
    kv_vregs = MEM_LEN * XA_HEADS // SUBLANES
    lane_ones = jnp.ones((XA_HEAD_DIM, XA_HEAD_DIM), BF16)
    qv = proj[:, OFF_Q:OFF_GATE]
    y = jnp.zeros((nb, SSD_INNER), F32)
    for bi in range(nb):
        mine = rid == bi
        x_row = jnp.where(mine, xdt, 0.0).astype(BF16)
        parts = []
        for g in range(SSD_GROUPS):
            cols = slice(g * SSD_GROUP_WIDTH, (g + 1) * SSD_GROUP_WIDTH)
            ncols = slice(g * SSD_STATE, (g + 1) * SSD_STATE)
            upd = _dot_tn(x_row[:, cols], bm[:, ncols])
            heads = range(g * SSD_HEADS // SSD_GROUPS, (g + 1) * SSD_HEADS // SSD_GROUPS)
            e_rows = jnp.concatenate(
                [jnp.broadcast_to(decay[bi:bi + 1, h:h + 1], (SSD_HEAD_DIM, SSD_STATE)) for h in heads],
                axis=0)
            h_new = ssm_ref[bi, cols, :] * e_rows + upd
            ssmn_ref[bi, cols, :] = h_new
            parts.append(_dot_nt(cm[:, ncols], h_new.astype(BF16)))
        y = y + jnp.where(mine, jnp.concatenate(parts, axis=-1), 0.0)

        q_rows = jnp.concatenate(
            [qv[bi:bi + 1, hd * XA_HEAD_DIM:(hd + 1) * XA_HEAD_DIM] for hd in range(XA_HEADS)]
            * (SUBLANES // XA_HEADS), axis=0)
        kq = (k_ref[bi].reshape(kv_vregs, SUBLANES, XA_HEAD_DIM) * q_rows[None]).astype(BF16)
        s = _dot(kq.reshape(kv_vregs * SUBLANES, XA_HEAD_DIM), lane_ones) * scale
        s = s.reshape(kv_vregs, SUBLANES, XA_HEAD_DIM)
        pexp = jnp.exp(s - _fold_heads(_reduce_leading(s, jnp.maximum), jnp.maximum)[None])
        o = _reduce_leading(pexp * v_ref[bi].reshape(kv_vregs, SUBLANES, XA_HEAD_DIM), jnp.add)
        o = _fold_heads(o, jnp.add) / _fold_heads(_reduce_leading(pexp, jnp.add), jnp.add)
        for hd in range(XA_HEADS):
            xa_ref[bi:bi + 1, hd * XA_HEAD_DIM:(hd + 1) * XA_HEAD_DIM] = o[hd:hd + 1, :]

    zs = _silu(proj[:, OFF_Z:OFF_XBC])
    yn_ref[...] = _group_rms((y + xs * dsk_ref[...]) * zs, snw_ref[...])


def _sample_state(proj, state_conv2d, state_ssm, cache_k, cache_v, p):
    n = proj.shape[0]
    nb = SAMPLE_BLOCK
    assert n % nb == 0
    conv_w = (SSD_CONV - 1) * SSD_CONV_DIM
    small = [p["gm_ln_w"], p["gm_ln_b"], p["gm_ws"], p["gm_bs"], p["conv_w"], p["conv_b"],
             p["dt_bias"], p["a_log"], p["d_skip"], p["ssd_norm_w"]]
    rows2 = lambda w: pl.BlockSpec((nb, w), lambda i: (i, 0))
    rows3 = lambda a, c: pl.BlockSpec((nb, a, c), lambda i: (i, 0, 0))
    cache = rows3(MEM_LEN * XA_HEADS, XA_HEAD_DIM)
    return pl.pallas_call(
        _sample_state_kernel,
        grid=(n // nb,),
        in_specs=[rows2(IN_PROJ_PACKED), rows2(conv_w), rows3(SSD_INNER, SSD_STATE), cache, cache]
                 + [_full(a.shape) for a in small],
        out_specs=[rows2(GM_WIDTH), rows2(SSD_INNER), rows2(XA_WIDTH), rows2(conv_w),
                   rows3(SSD_INNER, SSD_STATE), rows2(GM_WIDTH)],
        out_shape=[
            jax.ShapeDtypeStruct((n, GM_WIDTH), F32),
            jax.ShapeDtypeStruct((n, SSD_INNER), F32),
            jax.ShapeDtypeStruct((n, XA_WIDTH), F32),
            jax.ShapeDtypeStruct((n, conv_w), F32),
            jax.ShapeDtypeStruct((n, SSD_INNER, SSD_STATE), F32),
            jax.ShapeDtypeStruct((n, GM_WIDTH), F32),
        ],
        compiler_params=pltpu.CompilerParams(
            dimension_semantics=("arbitrary",), vmem_limit_bytes=VMEM_LIMIT_BYTES),
        name="sample_state",
    )(proj, state_conv2d, state_ssm, cache_k, cache_v, *small)


def _sample_merge_kernel(x_ref, proj_ref, gm_ref, yn_ref, xa_ref, wgm_ref, wssd_ref, wxa_ref,
                         wout_ref, h_ref):
    merged = None
    for i, (br_ref, w_ref) in enumerate(((gm_ref, wgm_ref), (yn_ref, wssd_ref), (xa_ref, wxa_ref))):
        gate = jax.nn.sigmoid(proj_ref[:, OFF_GATE + i * D_MODEL:OFF_GATE + (i + 1) * D_MODEL])
        term = gate * _dot(br_ref[...].astype(BF16), w_ref[...])
        merged = term if merged is None else merged + term
    h_ref[...] = x_ref[...] + _dot(merged.astype(BF16), wout_ref[...])


def _sample_merge(x2d, proj, gm, yn, xa, p):
    args = [x2d, proj, gm, yn, xa, p["w_br_gm"], p["w_br_ssd"], p["w_br_xa"], p["w_out"]]
    return pl.pallas_call(
        _sample_merge_kernel,
        grid=(1,),
        in_specs=[_full(a.shape) for a in args],
        out_specs=_full(x2d.shape),
        out_shape=jax.ShapeDtypeStruct(x2d.shape, F32),
        compiler_params=pltpu.CompilerParams(
            dimension_semantics=("arbitrary",), vmem_limit_bytes=VMEM_LIMIT_BYTES),
        name="sample_merge",
    )(*args)


def _pack_w_in_kernel(w_ref, wa_ref, wdt_ref, wb_ref):
    wa_ref[...] = w_ref[0, :, :OFF_DT].astype(BF16)
    dt_tile = w_ref[0, :, OFF_DT:OFF_DT + DT_PAD]
    is_dt = lax.broadcasted_iota(jnp.int32, dt_tile.shape, 1) < SSD_HEADS
    wdt_ref[...] = jnp.where(is_dt, dt_tile, 0.0).astype(BF16)
    wb_ref[...] = w_ref[0, :, OFF_DT + SSD_HEADS:].astype(BF16)


def _pack_w_in(w_in, l):
    d, width = w_in.shape[1:]
    rest = width - OFF_DT - SSD_HEADS
    assert rest == IN_PROJ_PACKED - OFF_Q and d % PACK_ROWS == 0
    return pl.pallas_call(
        _pack_w_in_kernel,
        grid=(d // PACK_ROWS,),
        in_specs=[pl.BlockSpec((1, PACK_ROWS, width), lambda i: (l, i, 0))],
        out_specs=[pl.BlockSpec((PACK_ROWS, OFF_DT), lambda i: (i, 0)),
                   pl.BlockSpec((PACK_ROWS, DT_PAD), lambda i: (i, 0)),
                   pl.BlockSpec((PACK_ROWS, rest), lambda i: (i, 0))],
        out_shape=[jax.ShapeDtypeStruct((d, OFF_DT), BF16),
                   jax.ShapeDtypeStruct((d, DT_PAD), BF16),
                   jax.ShapeDtypeStruct((d, rest), BF16)],
        compiler_params=pltpu.CompilerParams(
            dimension_semantics=("arbitrary",), vmem_limit_bytes=VMEM_LIMIT_BYTES),
        name="pack_w_in",
    )(w_in)
def _prep_layer(l, norm_mix_w, w_in, gm_ln_w, gm_ln_b, gm_ws, gm_bs, conv_w, conv_b, dt_bias, a_log,
                d_skip, ssd_norm_w, mem_norm_w, w_mem_k, w_mem_v, w_br_gm, w_br_ssd, w_br_xa, w_out,
                norm_ffn_w, w_up, w_down, norm_final_w):
    row = lambda a: a.reshape(1, -1)
    pad_heads = lambda a: jnp.pad(a, (0, DT_PAD - SSD_HEADS)).reshape(1, DT_PAD)
    w_in_a, w_in_dt, w_in_b = _pack_w_in(w_in, l)
    return {
        "norm_mix_w": row(norm_mix_w[l]),
        "w_in_a": w_in_a, "w_in_dt": w_in_dt, "w_in_b": w_in_b,
        "gm_ln_w": row(gm_ln_w[l]), "gm_ln_b": row(gm_ln_b[l]),
        "gm_ws": gm_ws[l], "gm_bs": gm_bs[l], "gm_bs_t": gm_bs[l].T,
        "conv_w": conv_w[l], "conv_b": row(conv_b[l]),
        "dt_bias": pad_heads(dt_bias[l]), "a_log": pad_heads(a_log[l]),
        "d_skip": row(jnp.repeat(d_skip[l], SSD_HEAD_DIM)), "ssd_norm_w": row(ssd_norm_w[l]),
        "mem_norm_w": row(mem_norm_w[l]),
        "w_kv": jnp.concatenate([w_mem_k[l], w_mem_v[l]], axis=1).astype(BF16),
        "w_br_gm": w_br_gm[l].astype(BF16), "w_br_ssd": w_br_ssd[l].astype(BF16),
        "w_br_xa": w_br_xa[l].astype(BF16), "w_out": w_out[l].astype(BF16),
        "norm_ffn_w": row(norm_ffn_w[l]), "w_up": w_up[l].astype(BF16),
        "w_down": w_down[l].astype(BF16), "norm_final_w": row(norm_final_w),
    }


def kernel(x_prompt, x_sample, mem_prompt, cache_mem_k, cache_mem_v, state_conv, state_ssm, norm_mix_w, w_in, gm_ln_w, gm_ln_b, gm_ws, gm_bs, conv_w, conv_b, dt_bias, a_log, d_skip, ssd_norm_w, mem_norm_w, w_mem_k, w_mem_v, w_br_gm, w_br_ssd, w_br_xa, w_out, norm_ffn_w, w_up, w_down, norm_final_w):
    depth = w_in.shape[0]
    assert depth == 1, "the final norm is fused into the MLP kernel of the last (only) layer"
    b, seq, _ = x_prompt.shape
    n, dec_seq, _ = x_sample.shape
    assert dec_seq == 1
    p = _prep_layer(0, norm_mix_w, w_in, gm_ln_w, gm_ln_b, gm_ws, gm_bs, conv_w, conv_b, dt_bias,
                    a_log, d_skip, ssd_norm_w, mem_norm_w, w_mem_k, w_mem_v, w_br_gm, w_br_ssd,
                    w_br_xa, w_out, norm_ffn_w, w_up, w_down, norm_final_w)

    mem_k, mem_v = _memkv(mem_prompt.reshape(b * MEM_LEN, D_MODEL), p["mem_norm_w"], p["w_kv"])
    h1, conv_p, ssm_p = _mixer(x_prompt, mem_k.reshape(b, MEM_LEN, XA_WIDTH),
                               mem_v.reshape(b, MEM_LEN, XA_WIDTH), p)
    y_prompt = _ffn(h1.reshape(b * seq, D_MODEL), p).reshape(b, seq, D_MODEL)

    xs2d = x_sample.reshape(n, D_MODEL)
    proj = _sample_proj(xs2d, p)
    gm, yn, xa, conv_s, ssm_s, gv = _sample_state(
        proj, state_conv[0].reshape(n, (SSD_CONV - 1) * SSD_CONV_DIM), state_ssm[0].reshape(n, SSD_INNER, SSD_STATE),
        cache_mem_k[0].reshape(n, MEM_LEN * XA_HEADS, XA_HEAD_DIM),
        cache_mem_v[0].reshape(n, MEM_LEN * XA_HEADS, XA_HEAD_DIM), p)
    hs = _sample_merge(xs2d, proj, gm, yn, xa, p)
    y_sample = _ffn(hs, p).reshape(n, 1, D_MODEL)

    kv_shape = (1, b, MEM_LEN, XA_HEADS, XA_HEAD_DIM)
    state_shape = (SSD_HEADS, SSD_HEAD_DIM, SSD_STATE)
    return (y_prompt, y_sample,
            mem_k.reshape(kv_shape), mem_v.reshape(kv_shape),
            conv_p.reshape(1, b, SSD_CONV - 1, SSD_CONV_DIM), ssm_p.reshape((1, b) + state_shape),
            conv_s.reshape(1, n, SSD_CONV - 1, SSD_CONV_DIM), ssm_s.reshape((1, n) + state_shape),
            gv.reshape(1, n, 1, GM_WIDTH))
```

```python
import functools

import jax
import jax.numpy as jnp
from jax import lax
from jax.experimental import pallas as pl
from jax.experimental.pallas import tpu as pltpu

F32 = jnp.float32
BF16 = jnp.bfloat16

D_MODEL = 1024
GM_CHUNK = 128
GM_GROUPS = 4
GM_GROUP_DIM = 128
GM_WIDTH = GM_GROUPS * GM_GROUP_DIM
SSD_HEADS = 16
SSD_HEAD_DIM = 64
SSD_INNER = SSD_HEADS * SSD_HEAD_DIM
SSD_GROUPS = 2
SSD_GROUP_WIDTH = SSD_INNER // SSD_GROUPS
SSD_STATE = 128
SSD_CONV = 4
SSD_CHUNK = 128
SSD_CONV_DIM = SSD_INNER + 2 * SSD_GROUPS * SSD_STATE
MEM_LEN = 256
XA_HEADS = 4
XA_HEAD_DIM = 128
XA_WIDTH = XA_HEADS * XA_HEAD_DIM
N_BRANCH = 3
D_FF = 4 * D_MODEL
EPS = 1e-6

SUBLANES = 8
LANES = 128
HEADS_PER_LANE_TILE = LANES // SSD_HEAD_DIM
HEAD_PAIRS = SSD_HEADS // HEADS_PER_LANE_TILE
PAIRS_PER_GROUP = HEAD_PAIRS // SSD_GROUPS

DT_PAD = LANES
OFF_UV = 0
OFF_Z = OFF_UV + 2 * GM_WIDTH
OFF_XBC = OFF_Z + SSD_INNER
OFF_DT = OFF_XBC + SSD_CONV_DIM
OFF_Q = OFF_DT + DT_PAD
OFF_GATE = OFF_Q + XA_WIDTH
IN_PROJ_PACKED = OFF_GATE + N_BRANCH * D_MODEL

RAW_UV = 0
RAW_Z = RAW_UV + 2 * GM_WIDTH
RAW_DT = RAW_Z + SSD_INNER
RAW_GATE = RAW_DT + DT_PAD
RAW_WIDTH = RAW_GATE + N_BRANCH * D_MODEL

MIXER_ROWS = 256
PIECE_COLS = 256
MERGE_COLS = 512
FFN_ROWS = 512
FFN_COLS = 1024
MEMKV_ROWS = 512
PACK_COLS = 512
SAMPLE_BLOCK = SUBLANES
VMEM_LIMIT_BYTES = 56 * 1024 * 1024


def _dot(a, b):
    return jnp.dot(a, b, preferred_element_type=F32)


def _dot_nt(a, b):
    return lax.dot_general(a, b, (((1,), (1,)), ((), ())), preferred_element_type=F32)


def _dot_tn(a, b):
    return lax.dot_general(a, b, (((0,), (0,)), ((), ())), preferred_element_type=F32)


def _rms(x, w):
    return x * lax.rsqrt(jnp.mean(x * x, axis=-1, keepdims=True) + EPS) * w


def _layernorm(x, w, b):
    xc = x - jnp.mean(x, axis=-1, keepdims=True)
    return xc * lax.rsqrt(jnp.mean(xc * xc, axis=-1, keepdims=True) + EPS) * w + b


def _gelu(x):
    return 0.5 * x * (1.0 + lax.erf(x * 0.7071067811865476))


def _silu(x):
    return x * jax.nn.sigmoid(x)


def _softplus(x):
    return jnp.maximum(x, 0.0) + jnp.log1p(jnp.exp(-jnp.abs(x)))


def _split3(x):
    p1 = x.astype(BF16)
    r1 = x - p1.astype(F32)
    p2 = r1.astype(BF16)
    p3 = (r1 - p2.astype(F32)).astype(BF16)
    return p1, p2, p3


REDUCE_CHAINS = 8


def _reduce_leading(x, op):
    n = x.shape[0]
    assert n % REDUCE_CHAINS == 0
    acc = [x[w] for w in range(REDUCE_CHAINS)]
    for i in range(REDUCE_CHAINS, n, REDUCE_CHAINS):
        acc = [op(a, x[i + w]) for w, a in enumerate(acc)]
    while len(acc) > 1:
        acc = [op(acc[2 * i], acc[2 * i + 1]) for i in range(len(acc) // 2)]
    return acc[0]


def _fold_heads(a, op):
    assert SUBLANES == 2 * XA_HEADS
    return op(a, pltpu.roll(a, XA_HEADS, axis=0))


def _group_rms(y, w):
    parts = []
    for g in range(SSD_GROUPS):
        cols = slice(g * SSD_GROUP_WIDTH, (g + 1) * SSD_GROUP_WIDTH)
        parts.append(_rms(y[:, cols], w[:, cols]))
    return jnp.concatenate(parts, axis=-1)


def _memkv_kernel(mem_ref, nw_ref, wkv_ref, k_ref, v_ref):
    mn = _rms(mem_ref[...], nw_ref[...]).astype(BF16)
    kv = _dot(mn, wkv_ref[...])
    k_ref[...] = kv[:, :XA_WIDTH]
    v_ref[...] = kv[:, XA_WIDTH:]


def _memkv(mem2d, norm_w, wkv):
    rows = mem2d.shape[0]
    tm = min(MEMKV_ROWS, rows)
    return pl.pallas_call(
        _memkv_kernel,
        grid=(rows // tm,),
        in_specs=[
            pl.BlockSpec((tm, D_MODEL), lambda i: (i, 0)),
            pl.BlockSpec((1, D_MODEL), lambda i: (0, 0)),
            pl.BlockSpec((D_MODEL, 2 * XA_WIDTH), lambda i: (0, 0)),
        ],
        out_specs=[
            pl.BlockSpec((tm, XA_WIDTH), lambda i: (i, 0)),
            pl.BlockSpec((tm, XA_WIDTH), lambda i: (i, 0)),
        ],
        out_shape=[jax.ShapeDtypeStruct((rows, XA_WIDTH), F32)] * 2,
        compiler_params=pltpu.CompilerParams(
            dimension_semantics=("arbitrary",), vmem_limit_bytes=VMEM_LIMIT_BYTES),
        name="memkv",
    )(mem2d, norm_w, wkv)


def _ssd_chunk_steps(rows, xs_s, bc_s, dt_s, neg_a, hT_s, y_s, scratch):
    acum_s, acumt_s, dtt_s, wc_s, cb_s, lhs_s, cd_s, xw_s, sg_s = scratch
    q = SSD_CHUNK

    def masks():
        ii = lax.broadcasted_iota(jnp.int32, (q, q), 0)
        jj = lax.broadcasted_iota(jnp.int32, (q, q), 1)
        low_half = lax.broadcasted_iota(jnp.int32, (q, LANES), 1) < SSD_HEAD_DIM
        low_half_row = lax.broadcasted_iota(jnp.int32, (1, LANES), 1) < SSD_HEAD_DIM
        return ii >= jj, low_half, low_half_row

    def prologue():
        causal, _, _ = masks()
        dt = dt_s[rows, :]
        d1, d2, d3 = _split3(dt * neg_a)
        tri = causal.astype(BF16)
        acum = _dot(tri, d1) + _dot(tri, d2) + _dot(tri, d3)
        acum_s[...] = acum
        acumt_s[...] = acum.T
        dtt_s[...] = dt.T
        wc_s[...] = dt * jnp.exp(acum[q - 1:q, :] - acum)

    def group_step(g):
        _, low_half, _ = masks()
        b_g = bc_s[rows, g * SSD_STATE:(g + 1) * SSD_STATE]
        c_g = bc_s[rows, (SSD_GROUPS + g) * SSD_STATE:(SSD_GROUPS + g + 1) * SSD_STATE]
        cb_s[g] = _dot_nt(c_g.astype(BF16), b_g.astype(BF16))
        for pr in range(PAIRS_PER_GROUP):
            pair = g * PAIRS_PER_GROUP + pr
            lanes = slice(pair * LANES, (pair + 1) * LANES)
            h0 = pair * HEADS_PER_LANE_TILE
            w_pair = jnp.where(low_half, jnp.broadcast_to(wc_s[:, h0:h0 + 1], (q, LANES)),
                               jnp.broadcast_to(wc_s[:, h0 + 1:h0 + 2], (q, LANES)))
            xw_s[:, lanes] = (xs_s[rows, lanes] * w_pair).astype(BF16)
        cols = slice(g * SSD_GROUP_WIDTH, (g + 1) * SSD_GROUP_WIDTH)
        sg_s[:, cols] = _dot(b_g.T.astype(BF16), xw_s[:, cols])

    def head_step(h):
        causal, _, _ = masks()
        g = h // (SSD_HEADS // SSD_GROUPS)
        c_g = bc_s[rows, (SSD_GROUPS + g) * SSD_STATE:(SSD_GROUPS + g + 1) * SSD_STATE]
        col = jnp.broadcast_to(acum_s[:, h:h + 1], (q, q))
        seg = jnp.where(causal, col - acumt_s[h:h + 1, :], -jnp.inf)
        lhs_s[h, :, 0:q] = (cb_s[g] * jnp.exp(seg) * dtt_s[h:h + 1, :]).astype(BF16)
        lhs_s[h, :, q:2 * q] = (c_g * jnp.exp(col)).astype(BF16)
        cd_s[h:h + 1, :] = jnp.exp(col[q - 1:q, :])

    def pair_step(pair):
        _, low_half, low_half_row = masks()
        lanes = slice(pair * LANES, (pair + 1) * LANES)
        h0 = pair * HEADS_PER_LANE_TILE
        h_prev = hT_s[:, lanes]
        rhs = jnp.concatenate([xs_s[rows, lanes].astype(BF16), h_prev.astype(BF16)], axis=0)
        y_s[rows, lanes] = jnp.where(low_half, _dot(lhs_s[h0], rhs), _dot(lhs_s[h0 + 1], rhs))
        decay = jnp.where(low_half_row, cd_s[h0:h0 + 1, :], cd_s[h0 + 1:h0 + 2, :])
        hT_s[:, lanes] = h_prev * decay + sg_s[:, lanes]

    steps = [(2, prologue)]
    steps += [(3, functools.partial(group_step, g)) for g in range(SSD_GROUPS)]
    steps += [(2, functools.partial(head_step, h)) for h in range(SSD_HEADS)]
    steps += [(1, functools.partial(pair_step, pair)) for pair in range(HEAD_PAIRS)]
    return steps


def _interleave(steps, pieces):
    total = sum(w for w, _ in steps)
    done = acc = 0
    for w, step in steps:
        acc += w
        upto = len(pieces) * acc // total
        for piece in pieces[done:upto]:
            piece()
        done = upto
        step()
    assert done == len(pieces)


def _in_proj_cols(w_refs, lo, hi):
    wa_ref, wdt_ref, wb_ref = w_refs
    if hi <= OFF_DT:
        return wa_ref[:, lo:hi]
    if lo >= OFF_Q:
        return wb_ref[:, lo - OFF_Q:hi - OFF_Q]
    assert (lo, hi) == (OFF_DT, OFF_Q)
    return wdt_ref[...]


def _mixer_kernel(x1_ref, x2_ref, k_ref, v_ref, nw_ref, wa_ref, wdt_ref, wb_ref, lnw_ref, lnb_ref, ws_ref,
                  bst_ref, cw_ref, cb_ref, dtb_ref, alog_ref, dsk_ref, snw_ref,
                  wgm_ref, wssd_ref, wxa_ref, wout_ref,
                  h_ref, conv_ref, ssm_ref,
                  xn_s, raw_s, q_s, ext_s, tail_s, hT_s, xs_s, bc_s, dt_s, y_s, u_s, v_s, gm_s, yn_s, xa_s, merged_s,
                  *ssd_scratch, tiles_per_seq):
    tm = x1_ref.shape[1]
    s = pl.program_id(0)
    t2 = lax.rem(jnp.maximum(s - 1, 0), tiles_per_seq)
    w_in = functools.partial(_in_proj_cols, (wa_ref, wdt_ref, wb_ref))
    head = SUBLANES

    @pl.when(s == 0)
    def _():
        raw_s[1] = jnp.zeros(raw_s.shape[1:], F32)
        q_s[1] = jnp.zeros(q_s.shape[1:], BF16)
        ext_s[1] = jnp.zeros(ext_s.shape[1:], F32)

    @pl.when(t2 == 0)
    def _():
        tail_s[...] = jnp.zeros(tail_s.shape, F32)
        hT_s[...] = jnp.zeros(hT_s.shape, F32)

    def step_body(wslot, rslot):
        xn_s[...] = _rms(x1_ref[0], nw_ref[...]).astype(BF16)

        def proj_piece(lo, width, store):
            return lambda: store(_dot(xn_s[...], w_in(lo, lo + width)))

        def raw_store(dst):
            def store(r):
                raw_s[wslot, :, dst:dst + r.shape[1]] = r
            return store

        def ext_store(dst):
            def store(r):
                ext_s[wslot, head:head + tm, dst:dst + r.shape[1]] = r
            return store

        def q_store(dst):
            def store(r):
                q_s[wslot, :, dst:dst + r.shape[1]] = r.astype(BF16)
            return store

        pc = PIECE_COLS
        pieces = ([proj_piece(OFF_XBC + j, pc, ext_store(j)) for j in range(0, SSD_CONV_DIM, pc)]
                  + [proj_piece(OFF_UV + j, pc, raw_store(RAW_UV + j)) for j in range(0, 2 * GM_WIDTH, pc)]
                  + [proj_piece(OFF_DT, DT_PAD, raw_store(RAW_DT))]
                  + [proj_piece(OFF_Z + j, pc, raw_store(RAW_Z + j)) for j in range(0, SSD_INNER, pc)]
                  + [proj_piece(OFF_Q + j, pc, q_store(j)) for j in range(0, XA_WIDTH, pc)]
                  + [proj_piece(OFF_GATE + j, pc, raw_store(RAW_GATE + j))
                     for j in range(0, N_BRANCH * D_MODEL, pc)])

        steps = []

        def conv_block(j):
            cols = slice(j, j + pc)
            acc = ext_s[rslot, head - 3:head - 3 + tm, cols] * cw_ref[0:1, cols]
            for k in range(1, SSD_CONV):
                acc = acc + ext_s[rslot, head - 3 + k:head - 3 + k + tm, cols] * cw_ref[k:k + 1, cols]
            xbc = _silu(acc + cb_ref[:, cols])
            if j < SSD_INNER:
                xs_s[:, cols] = xbc
            else:
                bc_s[:, j - SSD_INNER:j - SSD_INNER + pc] = xbc

        def conv_head():
            ext_s[rslot, 0:head, :] = tail_s[...]

        def conv_tail():
            conv_ref[0] = ext_s[rslot, tm + head - 3:tm + head, :]
            tail_s[...] = ext_s[rslot, tm:tm + head, :]
            dt_s[...] = _softplus(raw_s[rslot, :, RAW_DT:RAW_GATE] + dtb_ref[...])

        steps.append((0, conv_head))
        steps += [(3, functools.partial(conv_block, j)) for j in range(0, SSD_CONV_DIM, pc)]
        steps.append((1, conv_tail))

        def gelu_u(j):
            u_s[:, j:j + pc] = _gelu(raw_s[rslot, :, RAW_UV + j:RAW_UV + j + pc])

        def gelu_v():
            v = _gelu(raw_s[rslot, :, RAW_UV + GM_WIDTH:RAW_Z])
            v_s[...] = _layernorm(v, lnw_ref[...], lnb_ref[...]).astype(BF16)

        steps += [(2, functools.partial(gelu_u, j)) for j in range(0, GM_WIDTH, pc)]
        steps.append((4, gelu_v))

        neg_a = -jnp.exp(alog_ref[...])
        for c in range(tm // SSD_CHUNK):
            rows = slice(c * SSD_CHUNK, (c + 1) * SSD_CHUNK)
            steps += _ssd_chunk_steps(rows, xs_s, bc_s, dt_s, neg_a, hT_s, y_s, ssd_scratch)

        def gmlp_group(g):
            ii = lax.broadcasted_iota(jnp.int32, (GM_CHUNK, GM_CHUNK), 0)
            jj = lax.broadcasted_iota(jnp.int32, (GM_CHUNK, GM_CHUNK), 1)
            cols = slice(g * GM_GROUP_DIM, (g + 1) * GM_GROUP_DIM)
            w_tril = jnp.where(ii >= jj, ws_ref[g], 0.0).astype(BF16)
            bias = bst_ref[:, g:g + 1]
            for c in range(tm // GM_CHUNK):
                rows = slice(c * GM_CHUNK, (c + 1) * GM_CHUNK)
                mixed = _dot(w_tril, v_s[rows, cols]) + bias
                gm_s[rows, cols] = (u_s[rows, cols] * mixed).astype(BF16)

        steps += [(1, functools.partial(gmlp_group, g)) for g in range(GM_GROUPS)]

        def attn_head(hd):
            cols = slice(hd * XA_HEAD_DIM, (hd + 1) * XA_HEAD_DIM)
            sc = _dot_nt(q_s[rslot, :, cols], k_ref[0, :, cols].astype(BF16)) * (XA_HEAD_DIM ** -0.5)
            p = jnp.exp(sc - jnp.max(sc, axis=-1, keepdims=True))
            o = _dot(p.astype(BF16), v_ref[0, :, cols].astype(BF16))
            xa_s[:, cols] = (o / jnp.sum(p, axis=-1, keepdims=True)).astype(BF16)

        steps += [(2, functools.partial(attn_head, hd)) for hd in range(XA_HEADS)]

        def ssd_out_group(g):
            cols = slice(g * SSD_GROUP_WIDTH, (g + 1) * SSD_GROUP_WIDTH)
            zs = _silu(raw_s[rslot, :, RAW_Z + g * SSD_GROUP_WIDTH:RAW_Z + (g + 1) * SSD_GROUP_WIDTH])
            yv = (y_s[:, cols] + xs_s[:, cols] * dsk_ref[:, cols]) * zs
            yn_s[:, cols] = _rms(yv, snw_ref[:, cols]).astype(BF16)

        steps += [(3, functools.partial(ssd_out_group, g)) for g in range(SSD_GROUPS)]

        def merge_piece(i, br_s, w_ref, j):
            lo = RAW_GATE + i * D_MODEL + j
            term = jax.nn.sigmoid(raw_s[rslot, :, lo:lo + MERGE_COLS]) * _dot(br_s[...], w_ref[:, j:j + MERGE_COLS])
            if i == 0:
                merged_s[:, j:j + MERGE_COLS] = term
            else:
                merged_s[:, j:j + MERGE_COLS] += term

        for i, (br_s, w_ref) in enumerate(((gm_s, wgm_ref), (yn_s, wssd_ref), (xa_s, wxa_ref))):
            steps += [(2, functools.partial(merge_piece, i, br_s, w_ref, j)) for j in range(0, D_MODEL, MERGE_COLS)]

        def out_proj():
            h_ref[0] = x2_ref[0] + _dot(merged_s[...].astype(BF16), wout_ref[...])

        steps.append((1, out_proj))
        _interleave(steps, pieces)

    parity = lax.rem(s, 2)
    pl.when(parity == 0)(functools.partial(step_body, 0, 1))
    pl.when(parity == 1)(functools.partial(step_body, 1, 0))

    @pl.when(t2 == tiles_per_seq - 1)
    def _():
        for pair in range(HEAD_PAIRS):
            lanes = slice(pair * LANES, (pair + 1) * LANES)
            ssm_ref[0, lanes, :] = hT_s[:, lanes].T


def _full(shape):
    return pl.BlockSpec(shape, lambda *_: (0,) * len(shape))


def _mixer(x, mem_k, mem_v, p):
    b, seq, _ = x.shape
    tm = MIXER_ROWS
    assert seq % tm == 0 and tm % SSD_CHUNK == 0
    small = [p["norm_mix_w"], p["w_in_a"], p["w_in_dt"], p["w_in_b"],
             p["gm_ln_w"], p["gm_ln_b"], p["gm_ws"], p["gm_bs_t"],
             p["conv_w"], p["conv_b"], p["dt_bias"], p["a_log"], p["d_skip"], p["ssd_norm_w"],
             p["w_br_gm"], p["w_br_ssd"], p["w_br_xa"], p["w_out"]]
    tps = seq // tm
    n_tiles = b * tps
    tile1 = lambda s: jnp.minimum(s, n_tiles - 1)
    tile2 = lambda s: jnp.maximum(s - 1, 0)
    x_spec = lambda tile: pl.BlockSpec((1, tm, D_MODEL), lambda s: (tile(s) // tps, tile(s) % tps, 0))
    seq_spec = lambda r, c: pl.BlockSpec((1, r, c), lambda s: (tile2(s) // tps, 0, 0))
    return pl.pallas_call(
        functools.partial(_mixer_kernel, tiles_per_seq=tps),
        grid=(n_tiles + 1,),
        in_specs=[x_spec(tile1), x_spec(tile2), seq_spec(MEM_LEN, XA_WIDTH), seq_spec(MEM_LEN, XA_WIDTH)]
                 + [_full(a.shape) for a in small],
        out_specs=[x_spec(tile2), seq_spec(SSD_CONV - 1, SSD_CONV_DIM), seq_spec(SSD_INNER, SSD_STATE)],
        out_shape=[
            jax.ShapeDtypeStruct((b, seq, D_MODEL), F32),
            jax.ShapeDtypeStruct((b, SSD_CONV - 1, SSD_CONV_DIM), F32),
            jax.ShapeDtypeStruct((b, SSD_INNER, SSD_STATE), F32),
        ],
        scratch_shapes=[
            pltpu.VMEM((tm, D_MODEL), BF16),
            pltpu.VMEM((2, tm, RAW_WIDTH), F32),
            pltpu.VMEM((2, tm, XA_WIDTH), BF16),
            pltpu.VMEM((2, tm + SUBLANES, SSD_CONV_DIM), F32),
            pltpu.VMEM((SUBLANES, SSD_CONV_DIM), F32),
            pltpu.VMEM((SSD_STATE, SSD_INNER), F32),
            pltpu.VMEM((tm, SSD_INNER), F32),
            pltpu.VMEM((tm, 2 * SSD_GROUPS * SSD_STATE), F32),
            pltpu.VMEM((tm, DT_PAD), F32),
            pltpu.VMEM((tm, SSD_INNER), F32),
            pltpu.VMEM((tm, GM_WIDTH), F32),
            pltpu.VMEM((tm, GM_WIDTH), BF16),
            pltpu.VMEM((tm, GM_WIDTH), BF16),
            pltpu.VMEM((tm, SSD_INNER), BF16),
            pltpu.VMEM((tm, XA_WIDTH), BF16),
            pltpu.VMEM((tm, D_MODEL), F32),
            pltpu.VMEM((SSD_CHUNK, DT_PAD), F32),
            pltpu.VMEM((DT_PAD, SSD_CHUNK), F32),
            pltpu.VMEM((DT_PAD, SSD_CHUNK), F32),
            pltpu.VMEM((SSD_CHUNK, DT_PAD), F32),
            pltpu.VMEM((SSD_GROUPS, SSD_CHUNK, SSD_CHUNK), F32),
            pltpu.VMEM((SSD_HEADS, SSD_CHUNK, 2 * SSD_CHUNK), BF16),
            pltpu.VMEM((SSD_HEADS, SSD_CHUNK), F32),
            pltpu.VMEM((SSD_CHUNK, SSD_INNER), BF16),
            pltpu.VMEM((SSD_STATE, SSD_INNER), F32),
        ],
        compiler_params=pltpu.CompilerParams(
            dimension_semantics=("arbitrary",), vmem_limit_bytes=VMEM_LIMIT_BYTES),
        name="prompt_mixer",
    )(x, x, mem_k, mem_v, *small)


def _ffn_kernel(h_ref, nw_ref, wup_ref, wdn_ref, fw_ref, o_ref):
    h = h_ref[...]
    hn = _rms(h, nw_ref[...]).astype(BF16)
    acc = h
    for j in range(D_FF // FFN_COLS):
        cols = slice(j * FFN_COLS, (j + 1) * FFN_COLS)
        a = jnp.square(jnp.maximum(_dot(hn, wup_ref[:, cols]), 0.0)).astype(BF16)
        acc = acc + _dot(a, wdn_ref[cols, :])
    o_ref[...] = _rms(acc, fw_ref[...])


def _ffn(h2d, p):
    rows = h2d.shape[0]
    tm = min(FFN_ROWS, rows)
    assert rows % tm == 0
    return pl.pallas_call(
        _ffn_kernel,
        grid=(rows // tm,),
        in_specs=[
            pl.BlockSpec((tm, D_MODEL), lambda i: (i, 0)),
            _full((1, D_MODEL)),
            _full((D_MODEL, D_FF)),
            _full((D_FF, D_MODEL)),
            _full((1, D_MODEL)),
        ],
        out_specs=pl.BlockSpec((tm, D_MODEL), lambda i: (i, 0)),
        out_shape=jax.ShapeDtypeStruct((rows, D_MODEL), F32),
        compiler_params=pltpu.CompilerParams(
            dimension_semantics=("arbitrary",), vmem_limit_bytes=VMEM_LIMIT_BYTES),
        name="ffn",
    )(h2d, p["norm_ffn_w"], p["w_up"], p["w_down"], p["norm_final_w"])


def _sample_proj_kernel(x_ref, nw_ref, wa_ref, wdt_ref, wb_ref, o_ref):
    xn = _rms(x_ref[...], nw_ref[...]).astype(BF16)
    o_ref[:, :OFF_DT] = _dot(xn, wa_ref[...])
    o_ref[:, OFF_DT:OFF_Q] = _dot(xn, wdt_ref[...])
    o_ref[:, OFF_Q:] = _dot(xn, wb_ref[...])


def _sample_proj(x2d, p):
    rows = x2d.shape[0]
    args = [x2d, p["norm_mix_w"], p["w_in_a"], p["w_in_dt"], p["w_in_b"]]
    return pl.pallas_call(
        _sample_proj_kernel,
        grid=(1,),
        in_specs=[_full(a.shape) for a in args],
        out_specs=_full((rows, IN_PROJ_PACKED)),
        out_shape=jax.ShapeDtypeStruct((rows, IN_PROJ_PACKED), F32),
        compiler_params=pltpu.CompilerParams(
            dimension_semantics=("arbitrary",), vmem_limit_bytes=VMEM_LIMIT_BYTES),
        name="sample_proj",
    )(*args)


def _sample_state_kernel(proj_ref, stc_ref, ssm_ref, k_ref, v_ref, lnw_ref, lnb_ref, ws_ref,
                         bs_ref, cw_ref, cb_ref, dtb_ref, alog_ref, dsk_ref, snw_ref,
                         gm_ref, yn_ref, xa_ref, convn_ref, ssmn_ref, gv_ref):
    nb = proj_ref.shape[0]
    proj = proj_ref[...]

    uv = _gelu(proj[:, OFF_UV:OFF_Z])
    v = _layernorm(uv[:, GM_WIDTH:], lnw_ref[...], lnb_ref[...])
    gv_ref[...] = v
    for g in range(GM_GROUPS):
        cols = slice(g * GM_GROUP_DIM, (g + 1) * GM_GROUP_DIM)
        mixed = v[:, cols] * ws_ref[g, 0:1, 0:1] + bs_ref[g:g + 1, 0:1]
        gm_ref[:, cols] = uv[:, cols] * mixed

    st = stc_ref[...]
    xnew = proj[:, OFF_XBC:OFF_DT]
    acc = st[:, 0:SSD_CONV_DIM] * cw_ref[0:1, :]
    for k in range(1, SSD_CONV - 1):
        acc = acc + st[:, k * SSD_CONV_DIM:(k + 1) * SSD_CONV_DIM] * cw_ref[k:k + 1, :]
    acc = acc + xnew * cw_ref[SSD_CONV - 1:SSD_CONV, :]
    convn_ref[:, 0:(SSD_CONV - 2) * SSD_CONV_DIM] = st[:, SSD_CONV_DIM:]
    convn_ref[:, (SSD_CONV - 2) * SSD_CONV_DIM:] = xnew
    xbc = _silu(acc + cb_ref[...])
    xs = xbc[:, :SSD_INNER]
    bm = xbc[:, SSD_INNER:SSD_INNER + SSD_GROUPS * SSD_STATE].astype(BF16)
    cm = xbc[:, SSD_INNER + SSD_GROUPS * SSD_STATE:].astype(BF16)

    dt = _softplus(proj[:, OFF_DT:OFF_Q] + dtb_ref[...])
    decay = jnp.exp(dt * -jnp.exp(alog_ref[...]))
    hh = lax.broadcasted_iota(jnp.int32, (DT_PAD, SSD_INNER), 0)
    cc = lax.broadcasted_iota(jnp.int32, (DT_PAD, SSD_INNER), 1)
    spread = (jnp.right_shift(cc, SSD_HEAD_DIM.bit_length() - 1) == hh).astype(BF16)
    dt_wide = sum(_dot(piece, spread) for piece in _split3(dt))
    xdt = xs * dt_wide

    rid = lax.broadcasted_iota(jnp.int32, (nb, 1), 0)
    scale = XA_HEAD_DIM ** -0.5
    kv_vregs = MEM_LEN * XA_HEADS // SUBLANES
    lane_ones = jnp.ones((XA_HEAD_DIM, XA_HEAD_DIM), BF16)
    qv = proj[:, OFF_Q:OFF_GATE]
    y = jnp.zeros((nb, SSD_INNER), F32)
    for bi in range(nb):
        mine = rid == bi
        x_row = jnp.where(mine, xdt, 0.0).astype(BF16)
        parts = []
        for g in range(SSD_GROUPS):
            cols = slice(g * SSD_GROUP_WIDTH, (g + 1) * SSD_GROUP_WIDTH)
            ncols = slice(g * SSD_STATE, (g + 1) * SSD_STATE)
            upd = _dot_tn(x_row[:, cols], bm[:, ncols])
            heads = range(g * SSD_HEADS // SSD_GROUPS, (g + 1) * SSD_HEADS // SSD_GROUPS)
            e_rows = jnp.concatenate(
                [jnp.broadcast_to(decay[bi:bi + 1, h:h + 1], (SSD_HEAD_DIM, SSD_STATE)) for h in heads],
                axis=0)
            h_new = ssm_ref[bi, cols, :] * e_rows + upd
            ssmn_ref[bi, cols, :] = h_new
            parts.append(_dot_nt(cm[:, ncols], h_new.astype(BF16)))
        y = y + jnp.where(mine, jnp.concatenate(parts, axis=-1), 0.0)

        q_rows = jnp.concatenate(
            [qv[bi:bi + 1, hd * XA_HEAD_DIM:(hd + 1) * XA_HEAD_DIM] for hd in range(XA_HEADS)]
            * (SUBLANES // XA_HEADS), axis=0)
        kq = (k_ref[bi].reshape(kv_vregs, SUBLANES, XA_HEAD_DIM) * q_rows[None]).astype(BF16)
        s = _dot(kq.reshape(kv_vregs * SUBLANES, XA_HEAD_DIM), lane_ones) * scale
        s = s.reshape(kv_vregs, SUBLANES, XA_HEAD_DIM)
        pexp = jnp.exp(s - _fold_heads(_reduce_leading(s, jnp.maximum), jnp.maximum)[None])
        o = _reduce_leading(pexp * v_ref[bi].reshape(kv_vregs, SUBLANES, XA_HEAD_DIM), jnp.add)
        o = _fold_heads(o, jnp.add) / _fold_heads(_reduce_leading(pexp, jnp.add), jnp.add)
        for hd in range(XA_HEADS):
            xa_ref[bi:bi + 1, hd * XA_HEAD_DIM:(hd + 1) * XA_HEAD_DIM] = o[hd:hd + 1, :]

    zs = _silu(proj[:, OFF_Z:OFF_XBC])
    yn_ref[...] = _group_rms((y + xs * dsk_ref[...]) * zs, snw_ref[...])


def _sample_state(proj, state_conv2d, state_ssm, cache_k, cache_v, p):
    n = proj.shape[0]
    nb = SAMPLE_BLOCK
    assert n % nb == 0
    conv_w = (SSD_CONV - 1) * SSD_CONV_DIM
    small = [p["gm_ln_w"], p["gm_ln_b"], p["gm_ws"], p["gm_bs"], p["conv_w"], p["conv_b"],
             p["dt_bias"], p["a_log"], p["d_skip"], p["ssd_norm_w"]]
    rows2 = lambda w: pl.BlockSpec((nb, w), lambda i: (i, 0))
    rows3 = lambda a, c: pl.BlockSpec((nb, a, c), lambda i: (i, 0, 0))
    cache = rows3(MEM_LEN * XA_HEADS, XA_HEAD_DIM)
    return pl.pallas_call(
        _sample_state_kernel,
        grid=(n // nb,),
        in_specs=[rows2(IN_PROJ_PACKED), rows2(conv_w), rows3(SSD_INNER, SSD_STATE), cache, cache]
                 + [_full(a.shape) for a in small],
        out_specs=[rows2(GM_WIDTH), rows2(SSD_INNER), rows2(XA_WIDTH), rows2(conv_w),
                   rows3(SSD_INNER, SSD_STATE), rows2(GM_WIDTH)],
        out_shape=[
            jax.ShapeDtypeStruct((n, GM_WIDTH), F32),
            jax.ShapeDtypeStruct((n, SSD_INNER), F32),
            jax.ShapeDtypeStruct((n, XA_WIDTH), F32),
            jax.ShapeDtypeStruct((n, conv_w), F32),
            jax.ShapeDtypeStruct((n, SSD_INNER, SSD_STATE), F32),
            jax.ShapeDtypeStruct((n, GM_WIDTH), F32),
        ],
        compiler_params=pltpu.CompilerParams(
            dimension_semantics=("arbitrary",), vmem_limit_bytes=VMEM_LIMIT_BYTES),
        name="sample_state",
    )(proj, state_conv2d, state_ssm, cache_k, cache_v, *small)


def _sample_merge_kernel(x_ref, proj_ref, gm_ref, yn_ref, xa_ref, wgm_ref, wssd_ref, wxa_ref,
                         wout_ref, h_ref):
    merged = None
    for i, (br_ref, w_ref) in enumerate(((gm_ref, wgm_ref), (yn_ref, wssd_ref), (xa_ref, wxa_ref))):
        gate = jax.nn.sigmoid(proj_ref[:, OFF_GATE + i * D_MODEL:OFF_GATE + (i + 1) * D_MODEL])
        term = gate * _dot(br_ref[...].astype(BF16), w_ref[...])
        merged = term if merged is None else merged + term
    h_ref[...] = x_ref[...] + _dot(merged.astype(BF16), wout_ref[...])


def _sample_merge(x2d, proj, gm, yn, xa, p):
    args = [x2d, proj, gm, yn, xa, p["w_br_gm"], p["w_br_ssd"], p["w_br_xa"], p["w_out"]]
    return pl.pallas_call(
        _sample_merge_kernel,
        grid=(1,),
        in_specs=[_full(a.shape) for a in args],
        out_specs=_full(x2d.shape),
        out_shape=jax.ShapeDtypeStruct(x2d.shape, F32),
        compiler_params=pltpu.CompilerParams(
            dimension_semantics=("arbitrary",), vmem_limit_bytes=VMEM_LIMIT_BYTES),
        name="sample_merge",
    )(*args)


def _pack_w_in_kernel(wt_ref, wa_ref, wdt_ref, wb_ref):
    rest = wb_ref.shape[1]
    for j in range(0, OFF_DT, PACK_COLS):
        wa_ref[:, j:j + PACK_COLS] = wt_ref[0, j:j + PACK_COLS, :].T.astype(BF16)
    dt_rows = wt_ref[0, OFF_DT:OFF_DT + DT_PAD, :]
    is_dt = lax.broadcasted_iota(jnp.int32, dt_rows.shape, 0) < SSD_HEADS
    wdt_ref[...] = jnp.where(is_dt, dt_rows, 0.0).T.astype(BF16)
    base = OFF_DT + SSD_HEADS
    for j in range(0, rest, PACK_COLS):
        wb_ref[:, j:j + PACK_COLS] = wt_ref[0, base + j:base + j + PACK_COLS, :].T.astype(BF16)


def _pack_w_in(w_in, l):
    d, width = w_in.shape[1:]
    rest = width - OFF_DT - SSD_HEADS
    assert rest == IN_PROJ_PACKED - OFF_Q and OFF_DT % PACK_COLS == 0 and rest % PACK_COLS == 0
    w_t = jnp.swapaxes(w_in, 1, 2)
    return pl.pallas_call(
        _pack_w_in_kernel,
        grid=(1,),
        in_specs=[pl.BlockSpec((1, width, d), lambda i: (l, 0, 0))],
        out_specs=[_full((d, OFF_DT)), _full((d, DT_PAD)), _full((d, rest))],
        out_shape=[jax.ShapeDtypeStruct((d, OFF_DT), BF16),
                   jax.ShapeDtypeStruct((d, DT_PAD), BF16),
                   jax.ShapeDtypeStruct((d, rest), BF16)],
        compiler_params=pltpu.CompilerParams(
            dimension_semantics=("arbitrary",), vmem_limit_bytes=VMEM_LIMIT_BYTES),
        name="pack_w_in",
    )(w_t)


def _prep_layer(l, norm_mix_w, w_in, gm_ln_w, gm_ln_b, gm_ws, gm_bs, conv_w, conv_b, dt_bias, a_log,
                d_skip, ssd_norm_w, mem_norm_w, w_mem_k, w_mem_v, w_br_gm, w_br_ssd, w_br_xa, w_out,
                norm_ffn_w, w_up, w_down, norm_final_w):
    row = lambda a: a.reshape(1, -1)
    pad_heads = lambda a: jnp.pad(a, (0, DT_PAD - SSD_HEADS)).reshape(1, DT_PAD)
    w_in_a, w_in_dt, w_in_b = _pack_w_in(w_in, l)
    return {
        "norm_mix_w": row(norm_mix_w[l]),
        "w_in_a": w_in_a, "w_in_dt": w_in_dt, "w_in_b": w_in_b,
        "gm_ln_w": row(gm_ln_w[l]), "gm_ln_b": row(gm_ln_b[l]),
        "gm_ws": gm_ws[l], "gm_bs": gm_bs[l], "gm_bs_t": gm_bs[l].T,
        "conv_w": conv_w[l], "conv_b": row(conv_b[l]),
        "dt_bias": pad_heads(dt_bias[l]), "a_log": pad_heads(a_log[l]),
        "d_skip": row(jnp.repeat(d_skip[l], SSD_HEAD_DIM)), "ssd_norm_w": row(ssd_norm_w[l]),
        "mem_norm_w": row(mem_norm_w[l]),
        "w_kv": jnp.concatenate([w_mem_k[l], w_mem_v[l]], axis=1).astype(BF16),
        "w_br_gm": w_br_gm[l].astype(BF16), "w_br_ssd": w_br_ssd[l].astype(BF16),
        "w_br_xa": w_br_xa[l].astype(BF16), "w_out": w_out[l].astype(BF16),
        "norm_ffn_w": row(norm_ffn_w[l]), "w_up": w_up[l].astype(BF16),
        "w_down": w_down[l].astype(BF16), "norm_final_w": row(norm_final_w),
    }


def kernel(x_prompt, x_sample, mem_prompt, cache_mem_k, cache_mem_v, state_conv, state_ssm, norm_mix_w, w_in, gm_ln_w, gm_ln_b, gm_ws, gm_bs, conv_w, conv_b, dt_bias, a_log, d_skip, ssd_norm_w, mem_norm_w, w_mem_k, w_mem_v, w_br_gm, w_br_ssd, w_br_xa, w_out, norm_ffn_w, w_up, w_down, norm_final_w):
    depth = w_in.shape[0]
    assert depth == 1, "the final norm is fused into the MLP kernel of the last (only) layer"
    b, seq, _ = x_prompt.shape
    n, dec_seq, _ = x_sample.shape
    assert dec_seq == 1
    p = _prep_layer(0, norm_mix_w, w_in, gm_ln_w, gm_ln_b, gm_ws, gm_bs, conv_w, conv_b, dt_bias,
                    a_log, d_skip, ssd_norm_w, mem_norm_w, w_mem_k, w_mem_v, w_br_gm, w_br_ssd,
                    w_br_xa, w_out, norm_ffn_w, w_up, w_down, norm_final_w)

    mem_k, mem_v = _memkv(mem_prompt.reshape(b * MEM_LEN, D_MODEL), p["mem_norm_w"], p["w_kv"])
    h1, conv_p, ssm_p = _mixer(x_prompt, mem_k.reshape(b, MEM_LEN, XA_WIDTH),
                               mem_v.reshape(b, MEM_LEN, XA_WIDTH), p)
    y_prompt = _ffn(h1.reshape(b * seq, D_MODEL), p).reshape(b, seq, D_MODEL)

    xs2d = x_sample.reshape(n, D_MODEL)
    proj = _sample_proj(xs2d, p)
    gm, yn, xa, conv_s, ssm_s, gv = _sample_state(
        proj, state_conv[0].reshape(n, (SSD_CONV - 1) * SSD_CONV_DIM), state_ssm[0].reshape(n, SSD_INNER, SSD_STATE),
        cache_mem_k[0].reshape(n, MEM_LEN * XA_HEADS, XA_HEAD_DIM),
        cache_mem_v[0].reshape(n, MEM_LEN * XA_HEADS, XA_HEAD_DIM), p)
    hs = _sample_merge(xs2d, proj, gm, yn, xa, p)
    y_sample = _ffn(hs, p).reshape(n, 1, D_MODEL)

    kv_shape = (1, b, MEM_LEN, XA_HEADS, XA_HEAD_DIM)
    state_shape = (SSD_HEADS, SSD_HEAD_DIM, SSD_STATE)
    return (y_prompt, y_sample,
            mem_k.reshape(kv_shape), mem_v.reshape(kv_shape),
            conv_p.reshape(1, b, SSD_CONV - 1, SSD_CONV_DIM), ssm_p.reshape((1, b) + state_shape),
            conv_s.reshape(1, n, SSD_CONV - 1, SSD_CONV_DIM), ssm_s.reshape((1, n) + state_shape),
            gv.reshape(1, n, 1, GM_WIDTH))
```

```python
import functools

import jax
import jax.numpy as jnp
from jax import lax
from jax.experimental import pallas as pl
from jax.experimental.pallas import tpu as pltpu

F32 = jnp.float32
BF16 = jnp.bfloat16

D_MODEL = 1024
GM_CHUNK = 128
GM_GROUPS = 4
GM_GROUP_DIM = 128
GM_WIDTH = GM_GROUPS * GM_GROUP_DIM
SSD_HEADS = 16
SSD_HEAD_DIM = 64
SSD_INNER = SSD_HEADS * SSD_HEAD_DIM
SSD_GROUPS = 2
SSD_GROUP_WIDTH = SSD_INNER // SSD_GROUPS
SSD_STATE = 128
SSD_CONV = 4
SSD_CHUNK = 128
SSD_CONV_DIM = SSD_INNER + 2 * SSD_GROUPS * SSD_STATE
MEM_LEN = 256
XA_HEADS = 4
XA_HEAD_DIM = 128
XA_WIDTH = XA_HEADS * XA_HEAD_DIM
N_BRANCH = 3
D_FF = 4 * D_MODEL
EPS = 1e-6

SUBLANES = 8
LANES = 128
HEADS_PER_LANE_TILE = LANES // SSD_HEAD_DIM
HEAD_PAIRS = SSD_HEADS // HEADS_PER_LANE_TILE
PAIRS_PER_GROUP = HEAD_PAIRS // SSD_GROUPS

DT_PAD = LANES
OFF_UV = 0
OFF_Z = OFF_UV + 2 * GM_WIDTH
OFF_XBC = OFF_Z + SSD_INNER
OFF_DT = OFF_XBC + SSD_CONV_DIM
OFF_Q = OFF_DT + DT_PAD
OFF_GATE = OFF_Q + XA_WIDTH
IN_PROJ_PACKED = OFF_GATE + N_BRANCH * D_MODEL

RAW_UV = 0
RAW_Z = RAW_UV + 2 * GM_WIDTH
RAW_DT = RAW_Z + SSD_INNER
RAW_GATE = RAW_DT + DT_PAD
RAW_WIDTH = RAW_GATE + N_BRANCH * D_MODEL

MIXER_ROWS = 256
PIECE_COLS = 256
MERGE_COLS = 512
FFN_ROWS = 512
FFN_COLS = 1024
MEMKV_ROWS = 512
PACK_COLS = 512
SAMPLE_BLOCK = SUBLANES
VMEM_LIMIT_BYTES = 56 * 1024 * 1024


def _dot(a, b):
    return jnp.dot(a, b, preferred_element_type=F32)


def _dot_nt(a, b):
    return lax.dot_general(a, b, (((1,), (1,)), ((), ())), preferred_element_type=F32)


def _dot_tn(a, b):
    return lax.dot_general(a, b, (((0,), (0,)), ((), ())), preferred_element_type=F32)


def _rms(x, w):
    return x * lax.rsqrt(jnp.mean(x * x, axis=-1, keepdims=True) + EPS) * w


def _layernorm(x, w, b):
    xc = x - jnp.mean(x, axis=-1, keepdims=True)
    return xc * lax.rsqrt(jnp.mean(xc * xc, axis=-1, keepdims=True) + EPS) * w + b


def _gelu(x):
    return 0.5 * x * (1.0 + lax.erf(x * 0.7071067811865476))


def _silu(x):
    return x * jax.nn.sigmoid(x)


def _softplus(x):
    return jnp.maximum(x, 0.0) + jnp.log1p(jnp.exp(-jnp.abs(x)))


def _split3(x):
    p1 = x.astype(BF16)
    r1 = x - p1.astype(F32)
    p2 = r1.astype(BF16)
    p3 = (r1 - p2.astype(F32)).astype(BF16)
    return p1, p2, p3


REDUCE_CHAINS = 8


def _reduce_leading(x, op):
    n = x.shape[0]
    assert n % REDUCE_CHAINS == 0
    acc = [x[w] for w in range(REDUCE_CHAINS)]
    for i in range(REDUCE_CHAINS, n, REDUCE_CHAINS):
        acc = [op(a, x[i + w]) for w, a in enumerate(acc)]
    while len(acc) > 1:
        acc = [op(acc[2 * i], acc[2 * i + 1]) for i in range(len(acc) // 2)]
    return acc[0]


def _fold_heads(a, op):
    assert SUBLANES == 2 * XA_HEADS
    return op(a, pltpu.roll(a, XA_HEADS, axis=0))


def _group_rms(y, w):
    parts = []
    for g in range(SSD_GROUPS):
        cols = slice(g * SSD_GROUP_WIDTH, (g + 1) * SSD_GROUP_WIDTH)
        parts.append(_rms(y[:, cols], w[:, cols]))
    return jnp.concatenate(parts, axis=-1)


def _memkv_kernel(mem_ref, nw_ref, wkv_ref, k_ref, v_ref):
    mn = _rms(mem_ref[...], nw_ref[...]).astype(BF16)
    kv = _dot(mn, wkv_ref[...])
    k_ref[...] = kv[:, :XA_WIDTH]
    v_ref[...] = kv[:, XA_WIDTH:]


def _memkv(mem2d, norm_w, wkv):
    rows = mem2d.shape[0]
    tm = min(MEMKV_ROWS, rows)
    return pl.pallas_call(
        _memkv_kernel,
        grid=(rows // tm,),
        in_specs=[
            pl.BlockSpec((tm, D_MODEL), lambda i: (i, 0)),
            pl.BlockSpec((1, D_MODEL), lambda i: (0, 0)),
            pl.BlockSpec((D_MODEL, 2 * XA_WIDTH), lambda i: (0, 0)),
        ],
        out_specs=[
            pl.BlockSpec((tm, XA_WIDTH), lambda i: (i, 0)),
            pl.BlockSpec((tm, XA_WIDTH), lambda i: (i, 0)),
        ],
        out_shape=[jax.ShapeDtypeStruct((rows, XA_WIDTH), F32)] * 2,
        compiler_params=pltpu.CompilerParams(
            dimension_semantics=("arbitrary",), vmem_limit_bytes=VMEM_LIMIT_BYTES),
        name="memkv",
    )(mem2d, norm_w, wkv)


def _ssd_chunk_steps(rows, xs_s, bc_s, dt_s, neg_a, hT_s, y_s, scratch):
    acum_s, acumt_s, dtt_s, wc_s, cb_s, lhs_s, cd_s, xw_s, sg_s = scratch
    q = SSD_CHUNK

    def masks():
        ii = lax.broadcasted_iota(jnp.int32, (q, q), 0)
        jj = lax.broadcasted_iota(jnp.int32, (q, q), 1)
        low_half = lax.broadcasted_iota(jnp.int32, (q, LANES), 1) < SSD_HEAD_DIM
        low_half_row = lax.broadcasted_iota(jnp.int32, (1, LANES), 1) < SSD_HEAD_DIM
        return ii >= jj, low_half, low_half_row

    def prologue():
        causal, _, _ = masks()
        dt = dt_s[rows, :]
        d1, d2, d3 = _split3(dt * neg_a)
        tri = causal.astype(BF16)
        acum = _dot(tri, d1) + _dot(tri, d2) + _dot(tri, d3)
        acum_s[...] = acum
        acumt_s[...] = acum.T
        dtt_s[...] = dt.T
        wc_s[...] = dt * jnp.exp(acum[q - 1:q, :] - acum)

    def group_step(g):
        _, low_half, _ = masks()
        b_g = bc_s[rows, g * SSD_STATE:(g + 1) * SSD_STATE]
        c_g = bc_s[rows, (SSD_GROUPS + g) * SSD_STATE:(SSD_GROUPS + g + 1) * SSD_STATE]
        cb_s[g] = _dot_nt(c_g.astype(BF16), b_g.astype(BF16))
        for pr in range(PAIRS_PER_GROUP):
            pair = g * PAIRS_PER_GROUP + pr
            lanes = slice(pair * LANES, (pair + 1) * LANES)
            h0 = pair * HEADS_PER_LANE_TILE
            w_pair = jnp.where(low_half, jnp.broadcast_to(wc_s[:, h0:h0 + 1], (q, LANES)),
                               jnp.broadcast_to(wc_s[:, h0 + 1:h0 + 2], (q, LANES)))
            xw_s[:, lanes] = (xs_s[rows, lanes] * w_pair).astype(BF16)
        cols = slice(g * SSD_GROUP_WIDTH, (g + 1) * SSD_GROUP_WIDTH)
        sg_s[:, cols] = _dot(b_g.T.astype(BF16), xw_s[:, cols])

    def head_step(h):
        causal, _, _ = masks()
        g = h // (SSD_HEADS // SSD_GROUPS)
        c_g = bc_s[rows, (SSD_GROUPS + g) * SSD_STATE:(SSD_GROUPS + g + 1) * SSD_STATE]
        col = jnp.broadcast_to(acum_s[:, h:h + 1], (q, q))
        seg = jnp.where(causal, col - acumt_s[h:h + 1, :], -jnp.inf)
        lhs_s[h, :, 0:q] = (cb_s[g] * jnp.exp(seg) * dtt_s[h:h + 1, :]).astype(BF16)
        lhs_s[h, :, q:2 * q] = (c_g * jnp.exp(col)).astype(BF16)
        cd_s[h:h + 1, :] = jnp.exp(col[q - 1:q, :])

    def pair_step(pair):
        _, low_half, low_half_row = masks()
        lanes = slice(pair * LANES, (pair + 1) * LANES)
        h0 = pair * HEADS_PER_LANE_TILE
        h_prev = hT_s[:, lanes]
        rhs = jnp.concatenate([xs_s[rows, lanes].astype(BF16), h_prev.astype(BF16)], axis=0)
        y_s[rows, lanes] = jnp.where(low_half, _dot(lhs_s[h0], rhs), _dot(lhs_s[h0 + 1], rhs))
        decay = jnp.where(low_half_row, cd_s[h0:h0 + 1, :], cd_s[h0 + 1:h0 + 2, :])
        hT_s[:, lanes] = h_prev * decay + sg_s[:, lanes]

    steps = [(2, prologue)]
    steps += [(3, functools.partial(group_step, g)) for g in range(SSD_GROUPS)]
    steps += [(2, functools.partial(head_step, h)) for h in range(SSD_HEADS)]
    steps += [(1, functools.partial(pair_step, pair)) for pair in range(HEAD_PAIRS)]
    return steps


def _interleave(steps, pieces):
    total = sum(w for w, _ in steps)
    done = acc = 0
    for w, step in steps:
        acc += w
        upto = len(pieces) * acc // total
        for piece in pieces[done:upto]:
            piece()
        done = upto
        step()
    assert done == len(pieces)


def _in_proj_cols(w_refs, lo, hi):
    wa_ref, wdt_ref, wb_ref = w_refs
    if hi <= OFF_DT:
        return wa_ref[:, lo:hi]
    if lo >= OFF_Q:
        return wb_ref[:, lo - OFF_Q:hi - OFF_Q]
    assert (lo, hi) == (OFF_DT, OFF_Q)
    return wdt_ref[...]


def _mixer_kernel(x1_ref, x2_ref, k_ref, v_ref, nw_ref, wa_ref, wdt_ref, wb_ref, lnw_ref, lnb_ref, ws_ref,
                  bst_ref, cw_ref, cb_ref, dtb_ref, alog_ref, dsk_ref, snw_ref,
                  wgm_ref, wssd_ref, wxa_ref, wout_ref,
                  h_ref, conv_ref, ssm_ref,
                  xn_s, raw_s, q_s, ext_s, hT_s, xs_s, bc_s, dt_s, y_s, u_s, v_s, gm_s, yn_s, xa_s, merged_s,
                  zs_s, gate_s, qb_s, *ssd_scratch, tiles_per_seq):
    tm = x1_ref.shape[1]
    s = pl.program_id(0)
    t2 = lax.rem(jnp.maximum(s - 1, 0), tiles_per_seq)
    w_in = functools.partial(_in_proj_cols, (wa_ref, wdt_ref, wb_ref))
    head = SUBLANES
    pc = PIECE_COLS

    @pl.when(s == 0)
    def _():
        raw_s[...] = jnp.zeros(raw_s.shape, F32)
        ext_s[...] = jnp.zeros(ext_s.shape, F32)
        gate_s[...] = jnp.zeros(gate_s.shape, F32)
        zs_s[...] = jnp.zeros(zs_s.shape, F32)
        qb_s[...] = jnp.zeros(qb_s.shape, BF16)

    @pl.when(t2 == 0)
    def _():
        ext_s[0:head, :] = jnp.zeros((head, SSD_CONV_DIM), F32)
        hT_s[...] = jnp.zeros(hT_s.shape, F32)

    xn_s[...] = _rms(x1_ref[0], nw_ref[...]).astype(BF16)

    def project(lo, width):
        return _dot(xn_s[...], w_in(lo, lo + width))

    def raw_piece(dst, src, width):
        def run():
            raw_s[:, dst:dst + width] = project(src, width)
        return run

    def q_piece(j):
        def run():
            q_s[:, j:j + pc] = project(OFF_Q + j, pc).astype(BF16)
        return run

    def xbc_piece(j):
        def run():
            ext_s[head:head + tm, j:j + pc] = project(OFF_XBC + j, pc)
        return run

    early = ([raw_piece(RAW_GATE + j, OFF_GATE + j, pc) for j in range(0, N_BRANCH * D_MODEL, pc)]
             + [raw_piece(RAW_Z + j, OFF_Z + j, pc) for j in range(0, SSD_INNER, pc)]
             + [q_piece(j) for j in range(0, XA_WIDTH, pc)])

    def conv_block(j):
        cols = slice(j, j + pc)
        acc = ext_s[head - 3:head - 3 + tm, cols] * cw_ref[0:1, cols]
        for k in range(1, SSD_CONV):
            acc = acc + ext_s[head - 3 + k:head - 3 + k + tm, cols] * cw_ref[k:k + 1, cols]
        xbc = _silu(acc + cb_ref[:, cols])
        if j < SSD_INNER:
            xs_s[:, cols] = xbc
        else:
            bc_s[:, j - SSD_INNER:j - SSD_INNER + pc] = xbc
        conv_ref[0, :, cols] = ext_s[tm + head - 3:tm + head, cols]
        ext_s[0:head, cols] = ext_s[tm:tm + head, cols]

    def dt_act():
        dt_s[...] = _softplus(raw_s[:, RAW_DT:RAW_GATE] + dtb_ref[...])

    def gelu_u(j):
        u_s[:, j:j + pc] = _gelu(raw_s[:, RAW_UV + j:RAW_UV + j + pc])

    def gelu_v():
        v = _gelu(raw_s[:, RAW_UV + GM_WIDTH:RAW_Z])
        v_s[...] = _layernorm(v, lnw_ref[...], lnb_ref[...]).astype(BF16)

    consume = ([(3, functools.partial(conv_block, j)) for j in range(0, SSD_CONV_DIM, pc)] + [(1, dt_act)]
               + [(2, functools.partial(gelu_u, j)) for j in range(0, GM_WIDTH, pc)] + [(4, gelu_v)])

    pieces = early + ([xbc_piece(j) for j in range(0, SSD_CONV_DIM, pc)] + [raw_piece(RAW_DT, OFF_DT, DT_PAD)]
                      + [raw_piece(RAW_UV + j, OFF_UV + j, pc) for j in range(0, 2 * GM_WIDTH, pc)])
    steps = list(consume)
    neg_a = -jnp.exp(alog_ref[...])
    for c in range(tm // SSD_CHUNK):
        rows = slice(c * SSD_CHUNK, (c + 1) * SSD_CHUNK)
        steps += _ssd_chunk_steps(rows, xs_s, bc_s, dt_s, neg_a, hT_s, y_s, ssd_scratch)

    def gmlp_group(g):
        ii = lax.broadcasted_iota(jnp.int32, (GM_CHUNK, GM_CHUNK), 0)
        jj = lax.broadcasted_iota(jnp.int32, (GM_CHUNK, GM_CHUNK), 1)
        cols = slice(g * GM_GROUP_DIM, (g + 1) * GM_GROUP_DIM)
        w_tril = jnp.where(ii >= jj, ws_ref[g], 0.0).astype(BF16)
        bias = bst_ref[:, g:g + 1]
        for c in range(tm // GM_CHUNK):
            rows = slice(c * GM_CHUNK, (c + 1) * GM_CHUNK)
            mixed = _dot(w_tril, v_s[rows, cols]) + bias
            gm_s[rows, cols] = (u_s[rows, cols] * mixed).astype(BF16)

    steps += [(1, functools.partial(gmlp_group, g)) for g in range(GM_GROUPS)]

    def attn_head(hd):
        cols = slice(hd * XA_HEAD_DIM, (hd + 1) * XA_HEAD_DIM)
        sc = _dot_nt(qb_s[:, cols], k_ref[0, :, cols].astype(BF16)) * (XA_HEAD_DIM ** -0.5)
        p = jnp.exp(sc - jnp.max(sc, axis=-1, keepdims=True))
        o = _dot(p.astype(BF16), v_ref[0, :, cols].astype(BF16))
        xa_s[:, cols] = (o / jnp.sum(p, axis=-1, keepdims=True)).astype(BF16)

    steps += [(2, functools.partial(attn_head, hd)) for hd in range(XA_HEADS)]

    def ssd_out_group(g):
        cols = slice(g * SSD_GROUP_WIDTH, (g + 1) * SSD_GROUP_WIDTH)
        yv = (y_s[:, cols] + xs_s[:, cols] * dsk_ref[:, cols]) * zs_s[:, cols]
        yn_s[:, cols] = _rms(yv, snw_ref[:, cols]).astype(BF16)

    steps += [(2, functools.partial(ssd_out_group, g)) for g in range(SSD_GROUPS)]

    def merge_piece(i, br_s, w_ref, j):
        lo = i * D_MODEL + j
        term = gate_s[:, lo:lo + MERGE_COLS] * _dot(br_s[...], w_ref[:, j:j + MERGE_COLS])
        if i == 0:
            merged_s[:, j:j + MERGE_COLS] = term
        else:
            merged_s[:, j:j + MERGE_COLS] += term

    for i, (br_s, w_ref) in enumerate(((gm_s, wgm_ref), (yn_s, wssd_ref), (xa_s, wxa_ref))):
        steps += [(2, functools.partial(merge_piece, i, br_s, w_ref, j)) for j in range(0, D_MODEL, MERGE_COLS)]

    def out_proj():
        h_ref[0] = x2_ref[0] + _dot(merged_s[...].astype(BF16), wout_ref[...])

    steps.append((1, out_proj))
    _interleave(steps, pieces)

    for j in range(0, N_BRANCH * D_MODEL, pc):
        gate_s[:, j:j + pc] = jax.nn.sigmoid(raw_s[:, RAW_GATE + j:RAW_GATE + j + pc])
    for j in range(0, SSD_INNER, pc):
        zs_s[:, j:j + pc] = _silu(raw_s[:, RAW_Z + j:RAW_Z + j + pc])
    qb_s[...] = q_s[...]

    @pl.when(t2 == tiles_per_seq - 1)
    def _():
        for pair in range(HEAD_PAIRS):
            lanes = slice(pair * LANES, (pair + 1) * LANES)
            ssm_ref[0, lanes, :] = hT_s[:, lanes].T


def _full(shape):
    return pl.BlockSpec(shape, lambda *_: (0,) * len(shape))


def _mixer(x, mem_k, mem_v, p):
    b, seq, _ = x.shape
    tm = MIXER_ROWS
    assert seq % tm == 0 and tm % SSD_CHUNK == 0
    small = [p["norm_mix_w"], p["w_in_a"], p["w_in_dt"], p["w_in_b"],
             p["gm_ln_w"], p["gm_ln_b"], p["gm_ws"], p["gm_bs_t"],
             p["conv_w"], p["conv_b"], p["dt_bias"], p["a_log"], p["d_skip"], p["ssd_norm_w"],
             p["w_br_gm"], p["w_br_ssd"], p["w_br_xa"], p["w_out"]]
    tps = seq // tm
    n_tiles = b * tps
    tile1 = lambda s: jnp.minimum(s, n_tiles - 1)
    tile2 = lambda s: jnp.maximum(s - 1, 0)
    x_spec = lambda tile: pl.BlockSpec((1, tm, D_MODEL), lambda s: (tile(s) // tps, tile(s) % tps, 0))
    seq_spec = lambda r, c: pl.BlockSpec((1, r, c), lambda s: (tile2(s) // tps, 0, 0))
    return pl.pallas_call(
        functools.partial(_mixer_kernel, tiles_per_seq=tps),
        grid=(n_tiles + 1,),
        in_specs=[x_spec(tile1), x_spec(tile2), seq_spec(MEM_LEN, XA_WIDTH), seq_spec(MEM_LEN, XA_WIDTH)]
                 + [_full(a.shape) for a in small],
        out_specs=[x_spec(tile2), seq_spec(SSD_CONV - 1, SSD_CONV_DIM), seq_spec(SSD_INNER, SSD_STATE)],
        out_shape=[
            jax.ShapeDtypeStruct((b, seq, D_MODEL), F32),
            jax.ShapeDtypeStruct((b, SSD_CONV - 1, SSD_CONV_DIM), F32),
            jax.ShapeDtypeStruct((b, SSD_INNER, SSD_STATE), F32),
        ],
        scratch_shapes=[
            pltpu.VMEM((tm, D_MODEL), BF16),
            pltpu.VMEM((tm, RAW_WIDTH), F32),
            pltpu.VMEM((tm, XA_WIDTH), BF16),
            pltpu.VMEM((tm + SUBLANES, SSD_CONV_DIM), F32),
            pltpu.VMEM((SSD_STATE, SSD_INNER), F32),
            pltpu.VMEM((tm, SSD_INNER), F32),
            pltpu.VMEM((tm, 2 * SSD_GROUPS * SSD_STATE), F32),
            pltpu.VMEM((tm, DT_PAD), F32),
            pltpu.VMEM((tm, SSD_INNER), F32),
            pltpu.VMEM((tm, GM_WIDTH), F32),
            pltpu.VMEM((tm, GM_WIDTH), BF16),
            pltpu.VMEM((tm, GM_WIDTH), BF16),
            pltpu.VMEM((tm, SSD_INNER), BF16),
            pltpu.VMEM((tm, XA_WIDTH), BF16),
            pltpu.VMEM((tm, D_MODEL), F32),
            pltpu.VMEM((tm, SSD_INNER), F32),
            pltpu.VMEM((tm, N_BRANCH * D_MODEL), F32),
            pltpu.VMEM((tm, XA_WIDTH), BF16),
            pltpu.VMEM((SSD_CHUNK, DT_PAD), F32),
            pltpu.VMEM((DT_PAD, SSD_CHUNK), F32),
            pltpu.VMEM((DT_PAD, SSD_CHUNK), F32),
            pltpu.VMEM((SSD_CHUNK, DT_PAD), F32),
            pltpu.VMEM((SSD_GROUPS, SSD_CHUNK, SSD_CHUNK), F32),
            pltpu.VMEM((SSD_HEADS, SSD_CHUNK, 2 * SSD_CHUNK), BF16),
            pltpu.VMEM((SSD_HEADS, SSD_CHUNK), F32),
            pltpu.VMEM((SSD_CHUNK, SSD_INNER), BF16),
            pltpu.VMEM((SSD_STATE, SSD_INNER), F32),
        ],
        compiler_params=pltpu.CompilerParams(
            dimension_semantics=("arbitrary",), vmem_limit_bytes=VMEM_LIMIT_BYTES),
        name="prompt_mixer",
    )(x, x, mem_k, mem_v, *small)


def _ffn_kernel(h_ref, nw_ref, wup_ref, wdn_ref, fw_ref, o_ref):
    h = h_ref[...]
    hn = _rms(h, nw_ref[...]).astype(BF16)
    acc = h
    for j in range(D_FF // FFN_COLS):
        cols = slice(j * FFN_COLS, (j + 1) * FFN_COLS)
        a = jnp.square(jnp.maximum(_dot(hn, wup_ref[:, cols]), 0.0)).astype(BF16)
        acc = acc + _dot(a, wdn_ref[cols, :])
    o_ref[...] = _rms(acc, fw_ref[...])


def _ffn(h2d, p):
    rows = h2d.shape[0]
    tm = min(FFN_ROWS, rows)
    assert rows % tm == 0
    return pl.pallas_call(
        _ffn_kernel,
        grid=(rows // tm,),
        in_specs=[
            pl.BlockSpec((tm, D_MODEL), lambda i: (i, 0)),
            _full((1, D_MODEL)),
            _full((D_MODEL, D_FF)),
            _full((D_FF, D_MODEL)),
            _full((1, D_MODEL)),
        ],
        out_specs=pl.BlockSpec((tm, D_MODEL), lambda i: (i, 0)),
        out_shape=jax.ShapeDtypeStruct((rows, D_MODEL), F32),
        compiler_params=pltpu.CompilerParams(
            dimension_semantics=("arbitrary",), vmem_limit_bytes=VMEM_LIMIT_BYTES),
        name="ffn",
    )(h2d, p["norm_ffn_w"], p["w_up"], p["w_down"], p["norm_final_w"])


def _sample_proj_kernel(x_ref, nw_ref, wa_ref, wdt_ref, wb_ref, o_ref):
    xn = _rms(x_ref[...], nw_ref[...]).astype(BF16)
    o_ref[:, :OFF_DT] = _dot(xn, wa_ref[...])
    o_ref[:, OFF_DT:OFF_Q] = _dot(xn, wdt_ref[...])
    o_ref[:, OFF_Q:] = _dot(xn, wb_ref[...])


def _sample_proj(x2d, p):
    rows = x2d.shape[0]
    args = [x2d, p["norm_mix_w"], p["w_in_a"], p["w_in_dt"], p["w_in_b"]]
    return pl.pallas_call(
        _sample_proj_kernel,
        grid=(1,),
        in_specs=[_full(a.shape) for a in args],
        out_specs=_full((rows, IN_PROJ_PACKED)),
        out_shape=jax.ShapeDtypeStruct((rows, IN_PROJ_PACKED), F32),
        compiler_params=pltpu.CompilerParams(
            dimension_semantics=("arbitrary",), vmem_limit_bytes=VMEM_LIMIT_BYTES),
        name="sample_proj",
    )(*args)


def _sample_state_kernel(proj_ref, stc_ref, ssm_ref, k_ref, v_ref, lnw_ref, lnb_ref, ws_ref,
                         bs_ref, cw_ref, cb_ref, dtb_ref, alog_ref, dsk_ref, snw_ref,
                         gm_ref, yn_ref, xa_ref, convn_ref, ssmn_ref, gv_ref):
    nb = proj_ref.shape[0]
    proj = proj_ref[...]

    uv = _gelu(proj[:, OFF_UV:OFF_Z])
    v = _layernorm(uv[:, GM_WIDTH:], lnw_ref[...], lnb_ref[...])
    gv_ref[...] = v
    for g in range(GM_GROUPS):
        cols = slice(g * GM_GROUP_DIM, (g + 1) * GM_GROUP_DIM)
        mixed = v[:, cols] * ws_ref[g, 0:1, 0:1] + bs_ref[g:g + 1, 0:1]
        gm_ref[:, cols] = uv[:, cols] * mixed

    st = stc_ref[...]
    xnew = proj[:, OFF_XBC:OFF_DT]
    acc = st[:, 0:SSD_CONV_DIM] * cw_ref[0:1, :]
    for k in range(1, SSD_CONV - 1):
        acc = acc + st[:, k * SSD_CONV_DIM:(k + 1) * SSD_CONV_DIM] * cw_ref[k:k + 1, :]
    acc = acc + xnew * cw_ref[SSD_CONV - 1:SSD_CONV, :]
    convn_ref[:, 0:(SSD_CONV - 2) * SSD_CONV_DIM] = st[:, SSD_CONV_DIM:]
    convn_ref[:, (SSD_CONV - 2) * SSD_CONV_DIM:] = xnew
    xbc = _silu(acc + cb_ref[...])
    xs = xbc[:, :SSD_INNER]
    bm = xbc[:, SSD_INNER:SSD_INNER + SSD_GROUPS * SSD_STATE].astype(BF16)
    cm = xbc[:, SSD_INNER + SSD_GROUPS * SSD_STATE:].astype(BF16)

    dt = _softplus(proj[:, OFF_DT:OFF_Q] + dtb_ref[...])
    decay = jnp.exp(dt * -jnp.exp(alog_ref[...]))
    hh = lax.broadcasted_iota(jnp.int32, (DT_PAD, SSD_INNER), 0)
    cc = lax.broadcasted_iota(jnp.int32, (DT_PAD, SSD_INNER), 1)
    spread = (jnp.right_shift(cc, SSD_HEAD_DIM.bit_length() - 1) == hh).astype(BF16)
    dt_wide = sum(_dot(piece, spread) for piece in _split3(dt))
    xdt = xs * dt_wide

    rid = lax.broadcasted_iota(jnp.int32, (nb, 1), 0)
    scale = XA_HEAD_DIM ** -0.5
    kv_vregs = MEM_LEN * XA_HEADS // SUBLANES
    lane_ones = jnp.ones((XA_HEAD_DIM, XA_HEAD_DIM), BF16)
    qv = proj[:, OFF_Q:OFF_GATE]
    y = jnp.zeros((nb, SSD_INNER), F32)
    for bi in range(nb):
        mine = rid == bi
        x_row = jnp.where(mine, xdt, 0.0).astype(BF16)
        parts = []
        for g in range(SSD_GROUPS):
            cols = slice(g * SSD_GROUP_WIDTH, (g + 1) * SSD_GROUP_WIDTH)
            ncols = slice(g * SSD_STATE, (g + 1) * SSD_STATE)
            upd = _dot_tn(x_row[:, cols], bm[:, ncols])
            heads = range(g * SSD_HEADS // SSD_GROUPS, (g + 1) * SSD_HEADS // SSD_GROUPS)
            e_rows = jnp.concatenate(
                [jnp.broadcast_to(decay[bi:bi + 1, h:h + 1], (SSD_HEAD_DIM, SSD_STATE)) for h in heads],
                axis=0)
            h_new = ssm_ref[bi, cols, :] * e_rows + upd
            ssmn_ref[bi, cols, :] = h_new
            parts.append(_dot_nt(cm[:, ncols], h_new.astype(BF16)))
        y = y + jnp.where(mine, jnp.concatenate(parts, axis=-1), 0.0)

        q_rows = jnp.concatenate(
            [qv[bi:bi + 1, hd * XA_HEAD_DIM:(hd + 1) * XA_HEAD_DIM] for hd in range(XA_HEADS)]
            * (SUBLANES // XA_HEADS), axis=0)
        kq = (k_ref[bi].reshape(kv_vregs, SUBLANES, XA_HEAD_DIM) * q_rows[None]).astype(BF16)
        s = _dot(kq.reshape(kv_vregs * SUBLANES, XA_HEAD_DIM), lane_ones) * scale
        s = s.reshape(kv_vregs, SUBLANES, XA_HEAD_DIM)
        pexp = jnp.exp(s - _fold_heads(_reduce_leading(s, jnp.maximum), jnp.maximum)[None])
        o = _reduce_leading(pexp * v_ref[bi].reshape(kv_vregs, SUBLANES, XA_HEAD_DIM), jnp.add)
        o = _fold_heads(o, jnp.add) / _fold_heads(_reduce_leading(pexp, jnp.add), jnp.add)
        for hd in range(XA_HEADS):
            xa_ref[bi:bi + 1, hd * XA_HEAD_DIM:(hd + 1) * XA_HEAD_DIM] = o[hd:hd + 1, :]

    zs = _silu(proj[:, OFF_Z:OFF_XBC])
    yn_ref[...] = _group_rms((y + xs * dsk_ref[...]) * zs, snw_ref[...])


def _sample_state(proj, state_conv2d, state_ssm, cache_k, cache_v, p):
    n = proj.shape[0]
    nb = SAMPLE_BLOCK
    assert n % nb == 0
    conv_w = (SSD_CONV - 1) * SSD_CONV_DIM
    small = [p["gm_ln_w"], p["gm_ln_b"], p["gm_ws"], p["gm_bs"], p["conv_w"], p["conv_b"],
             p["dt_bias"], p["a_log"], p["d_skip"], p["ssd_norm_w"]]
    rows2 = lambda w: pl.BlockSpec((nb, w), lambda i: (i, 0))
    rows3 = lambda a, c: pl.BlockSpec((nb, a, c), lambda i: (i, 0, 0))
    cache = rows3(MEM_LEN * XA_HEADS, XA_HEAD_DIM)
    return pl.pallas_call(
        _sample_state_kernel,
        grid=(n // nb,),
        in_specs=[rows2(IN_PROJ_PACKED), rows2(conv_w), rows3(SSD_INNER, SSD_STATE), cache, cache]
                 + [_full(a.shape) for a in small],
        out_specs=[rows2(GM_WIDTH), rows2(SSD_INNER), rows2(XA_WIDTH), rows2(conv_w),
                   rows3(SSD_INNER, SSD_STATE), rows2(GM_WIDTH)],
        out_shape=[
            jax.ShapeDtypeStruct((n, GM_WIDTH), F32),
            jax.ShapeDtypeStruct((n, SSD_INNER), F32),
            jax.ShapeDtypeStruct((n, XA_WIDTH), F32),
            jax.ShapeDtypeStruct((n, conv_w), F32),
            jax.ShapeDtypeStruct((n, SSD_INNER, SSD_STATE), F32),
            jax.ShapeDtypeStruct((n, GM_WIDTH), F32),
        ],
        compiler_params=pltpu.CompilerParams(
            dimension_semantics=("arbitrary",), vmem_limit_bytes=VMEM_LIMIT_BYTES),
        name="sample_state",
    )(proj, state_conv2d, state_ssm, cache_k, cache_v, *small)


def _sample_merge_kernel(x_ref, proj_ref, gm_ref, yn_ref, xa_ref, wgm_ref, wssd_ref, wxa_ref,
                         wout_ref, h_ref):
    merged = None
    for i, (br_ref, w_ref) in enumerate(((gm_ref, wgm_ref), (yn_ref, wssd_ref), (xa_ref, wxa_ref))):
        gate = jax.nn.sigmoid(proj_ref[:, OFF_GATE + i * D_MODEL:OFF_GATE + (i + 1) * D_MODEL])
        term = gate * _dot(br_ref[...].astype(BF16), w_ref[...])
        merged = term if merged is None else merged + term
    h_ref[...] = x_ref[...] + _dot(merged.astype(BF16), wout_ref[...])


def _sample_merge(x2d, proj, gm, yn, xa, p):
    args = [x2d, proj, gm, yn, xa, p["w_br_gm"], p["w_br_ssd"], p["w_br_xa"], p["w_out"]]
    return pl.pallas_call(
        _sample_merge_kernel,
        grid=(1,),
        in_specs=[_full(a.shape) for a in args],
        out_specs=_full(x2d.shape),
        out_shape=jax.ShapeDtypeStruct(x2d.shape, F32),
        compiler_params=pltpu.CompilerParams(
            dimension_semantics=("arbitrary",), vmem_limit_bytes=VMEM_LIMIT_BYTES),
        name="sample_merge",
    )(*args)


def _pack_w_in_kernel(wt_ref, wa_ref, wdt_ref, wb_ref):
    rest = wb_ref.shape[1]
    for j in range(0, OFF_DT, PACK_COLS):
        wa_ref[:, j:j + PACK_COLS] = wt_ref[0, j:j + PACK_COLS, :].T.astype(BF16)
    dt_rows = wt_ref[0, OFF_DT:OFF_DT + DT_PAD, :]
    is_dt = lax.broadcasted_iota(jnp.int32, dt_rows.shape, 0) < SSD_HEADS
    wdt_ref[...] = jnp.where(is_dt, dt_rows, 0.0).T.astype(BF16)
    base = OFF_DT + SSD_HEADS
    for j in range(0, rest, PACK_COLS):
        wb_ref[:, j:j + PACK_COLS] = wt_ref[0, base + j:base + j + PACK_COLS, :].T.astype(BF16)


def _pack_w_in(w_in, l):
    d, width = w_in.shape[1:]
    rest = width - OFF_DT - SSD_HEADS
    assert rest == IN_PROJ_PACKED - OFF_Q and OFF_DT % PACK_COLS == 0 and rest % PACK_COLS == 0
    w_t = jnp.swapaxes(w_in, 1, 2)
    return pl.pallas_call(
        _pack_w_in_kernel,
        grid=(1,),
        in_specs=[pl.BlockSpec((1, width, d), lambda i: (l, 0, 0))],
        out_specs=[_full((d, OFF_DT)), _full((d, DT_PAD)), _full((d, rest))],
        out_shape=[jax.ShapeDtypeStruct((d, OFF_DT), BF16),
                   jax.ShapeDtypeStruct((d, DT_PAD), BF16),
                   jax.ShapeDtypeStruct((d, rest), BF16)],
        compiler_params=pltpu.CompilerParams(
            dimension_semantics=("arbitrary",), vmem_limit_bytes=VMEM_LIMIT_BYTES),
        name="pack_w_in",
    )(w_t)


def _prep_layer(l, norm_mix_w, w_in, gm_ln_w, gm_ln_b, gm_ws, gm_bs, conv_w, conv_b, dt_bias, a_log,
                d_skip, ssd_norm_w, mem_norm_w, w_mem_k, w_mem_v, w_br_gm, w_br_ssd, w_br_xa, w_out,
                norm_ffn_w, w_up, w_down, norm_final_w):
    row = lambda a: a.reshape(1, -1)
    pad_heads = lambda a: jnp.pad(a, (0, DT_PAD - SSD_HEADS)).reshape(1, DT_PAD)
    w_in_a, w_in_dt, w_in_b = _pack_w_in(w_in, l)
    return {
        "norm_mix_w": row(norm_mix_w[l]),
        "w_in_a": w_in_a, "w_in_dt": w_in_dt, "w_in_b": w_in_b,
        "gm_ln_w": row(gm_ln_w[l]), "gm_ln_b": row(gm_ln_b[l]),
        "gm_ws": gm_ws[l], "gm_bs": gm_bs[l], "gm_bs_t": gm_bs[l].T,
        "conv_w": conv_w[l], "conv_b": row(conv_b[l]),
        "dt_bias": pad_heads(dt_bias[l]), "a_log": pad_heads(a_log[l]),
        "d_skip": row(jnp.repeat(d_skip[l], SSD_HEAD_DIM)), "ssd_norm_w": row(ssd_norm_w[l]),
        "mem_norm_w": row(mem_norm_w[l]),
        "w_kv": jnp.concatenate([w_mem_k[l], w_mem_v[l]], axis=1).astype(BF16),
        "w_br_gm": w_br_gm[l].astype(BF16), "w_br_ssd": w_br_ssd[l].astype(BF16),
        "w_br_xa": w_br_xa[l].astype(BF16), "w_out": w_out[l].astype(BF16),
        "norm_ffn_w": row(norm_ffn_w[l]), "w_up": w_up[l].astype(BF16),
        "w_down": w_down[l].astype(BF16), "norm_final_w": row(norm_final_w),
    }


def kernel(x_prompt, x_sample, mem_prompt, cache_mem_k, cache_mem_v, state_conv, state_ssm, norm_mix_w, w_in, gm_ln_w, gm_ln_b, gm_ws, gm_bs, conv_w, conv_b, dt_bias, a_log, d_skip, ssd_norm_w, mem_norm_w, w_mem_k, w_mem_v, w_br_gm, w_br_ssd, w_br_xa, w_out, norm_ffn_w, w_up, w_down, norm_final_w):
    depth = w_in.shape[0]
    assert depth == 1, "the final norm is fused into the MLP kernel of the last (only) layer"
    b, seq, _ = x_prompt.shape
    n, dec_seq, _ = x_sample.shape
    assert dec_seq == 1
    p = _prep_layer(0, norm_mix_w, w_in, gm_ln_w, gm_ln_b, gm_ws, gm_bs, conv_w, conv_b, dt_bias,
                    a_log, d_skip, ssd_norm_w, mem_norm_w, w_mem_k, w_mem_v, w_br_gm, w_br_ssd,
                    w_br_xa, w_out, norm_ffn_w, w_up, w_down, norm_final_w)

    mem_k, mem_v = _memkv(mem_prompt.reshape(b * MEM_LEN, D_MODEL), p["mem_norm_w"], p["w_kv"])
    h1, conv_p, ssm_p = _mixer(x_prompt, mem_k.reshape(b, MEM_LEN, XA_WIDTH),
                               mem_v.reshape(b, MEM_LEN, XA_WIDTH), p)
    y_prompt = _ffn(h1.reshape(b * seq, D_MODEL), p).reshape(b, seq, D_MODEL)

    xs2d = x_sample.reshape(n, D_MODEL)
    proj = _sample_proj(xs2d, p)
    gm, yn, xa, conv_s, ssm_s, gv = _sample_state(
        proj, state_conv[0].reshape(n, (SSD_CONV - 1) * SSD_CONV_DIM), state_ssm[0].reshape(n, SSD_INNER, SSD_STATE),
        cache_mem_k[0].reshape(n, MEM_LEN * XA_HEADS, XA_HEAD_DIM),
        cache_mem_v[0].reshape(n, MEM_LEN * XA_HEADS, XA_HEAD_DIM), p)
    hs = _sample_merge(xs2d, proj, gm, yn, xa, p)
    y_sample = _ffn(hs, p).reshape(n, 1, D_MODEL)

    kv_shape = (1, b, MEM_LEN, XA_HEADS, XA_HEAD_DIM)
    state_shape = (SSD_HEADS, SSD_HEAD_DIM, SSD_STATE)
    return (y_prompt, y_sample,
            mem_k.reshape(kv_shape), mem_v.reshape(kv_shape),
            conv_p.reshape(1, b, SSD_CONV - 1, SSD_CONV_DIM), ssm_p.reshape((1, b) + state_shape),
            conv_s.reshape(1, n, SSD_CONV - 1, SSD_CONV_DIM), ssm_s.reshape((1, n) + state_shape),
            gv.reshape(1, n, 1, GM_WIDTH))
```

```python
import functools

import jax
import jax.numpy as jnp
from jax import lax
from jax.experimental import pallas as pl
from jax.experimental.pallas import tpu as pltpu

F32 = jnp.float32
BF16 = jnp.bfloat16

D_MODEL = 1024
GM_CHUNK = 128
GM_GROUPS = 4
GM_GROUP_DIM = 128
GM_WIDTH = GM_GROUPS * GM_GROUP_DIM
SSD_HEADS = 16
SSD_HEAD_DIM = 64
SSD_INNER = SSD_HEADS * SSD_HEAD_DIM
SSD_GROUPS = 2
SSD_GROUP_WIDTH = SSD_INNER // SSD_GROUPS
SSD_STATE = 128
SSD_CONV = 4
SSD_CHUNK = 128
SSD_CONV_DIM = SSD_INNER + 2 * SSD_GROUPS * SSD_STATE
MEM_LEN = 256
XA_HEADS = 4
XA_HEAD_DIM = 128
XA_WIDTH = XA_HEADS * XA_HEAD_DIM
N_BRANCH = 3
D_FF = 4 * D_MODEL
EPS = 1e-6

SUBLANES = 8
LANES = 128
HEADS_PER_LANE_TILE = LANES // SSD_HEAD_DIM
HEAD_PAIRS = SSD_HEADS // HEADS_PER_LANE_TILE
PAIRS_PER_GROUP = HEAD_PAIRS // SSD_GROUPS

DT_PAD = LANES
OFF_UV = 0
OFF_Z = OFF_UV + 2 * GM_WIDTH
OFF_XBC = OFF_Z + SSD_INNER
OFF_DT = OFF_XBC + SSD_CONV_DIM
OFF_Q = OFF_DT + DT_PAD
OFF_GATE = OFF_Q + XA_WIDTH
IN_PROJ_PACKED = OFF_GATE + N_BRANCH * D_MODEL

RAW_UV = 0
RAW_Z = RAW_UV + 2 * GM_WIDTH
RAW_DT = RAW_Z + SSD_INNER
RAW_GATE = RAW_DT + DT_PAD
RAW_WIDTH = RAW_GATE + N_BRANCH * D_MODEL

MIXER_ROWS = 256
PIECE_COLS = 256
MERGE_COLS = 512
FFN_ROWS = 1024
FFN_COLS = 1024
MEMKV_ROWS = 512
PACK_COLS = 512
SAMPLE_BLOCK = SUBLANES
VMEM_LIMIT_BYTES = 56 * 1024 * 1024


def _dot(a, b):
    return jnp.dot(a, b, preferred_element_type=F32)


def _dot_nt(a, b):
    return lax.dot_general(a, b, (((1,), (1,)), ((), ())), preferred_element_type=F32)


def _dot_tn(a, b):
    return lax.dot_general(a, b, (((0,), (0,)), ((), ())), preferred_element_type=F32)


def _rms(x, w):
    return x * lax.rsqrt(jnp.mean(x * x, axis=-1, keepdims=True) + EPS) * w


def _layernorm(x, w, b):
    xc = x - jnp.mean(x, axis=-1, keepdims=True)
    return xc * lax.rsqrt(jnp.mean(xc * xc, axis=-1, keepdims=True) + EPS) * w + b


def _gelu(x):
    return 0.5 * x * (1.0 + lax.erf(x * 0.7071067811865476))


def _silu(x):
    return x * jax.nn.sigmoid(x)


def _softplus(x):
    return jnp.maximum(x, 0.0) + jnp.log1p(jnp.exp(-jnp.abs(x)))


def _split3(x):
    p1 = x.astype(BF16)
    r1 = x - p1.astype(F32)
    p2 = r1.astype(BF16)
    p3 = (r1 - p2.astype(F32)).astype(BF16)
    return p1, p2, p3


REDUCE_CHAINS = 8


def _reduce_leading(x, op):
    n = x.shape[0]
    assert n % REDUCE_CHAINS == 0
    acc = [x[w] for w in range(REDUCE_CHAINS)]
    for i in range(REDUCE_CHAINS, n, REDUCE_CHAINS):
        acc = [op(a, x[i + w]) for w, a in enumerate(acc)]
    while len(acc) > 1:
        acc = [op(acc[2 * i], acc[2 * i + 1]) for i in range(len(acc) // 2)]
    return acc[0]


def _fold_heads(a, op):
    assert SUBLANES == 2 * XA_HEADS
    return op(a, pltpu.roll(a, XA_HEADS, axis=0))


def _group_rms(y, w):
    parts = []
    for g in range(SSD_GROUPS):
        cols = slice(g * SSD_GROUP_WIDTH, (g + 1) * SSD_GROUP_WIDTH)
        parts.append(_rms(y[:, cols], w[:, cols]))
    return jnp.concatenate(parts, axis=-1)


def _memkv_kernel(mem_ref, nw_ref, wkv_ref, k_ref, v_ref):
    mn = _rms(mem_ref[...], nw_ref[...]).astype(BF16)
    kv = _dot(mn, wkv_ref[...])
    k_ref[...] = kv[:, :XA_WIDTH]
    v_ref[...] = kv[:, XA_WIDTH:]


def _memkv(mem2d, norm_w, wkv):
    rows = mem2d.shape[0]
    tm = min(MEMKV_ROWS, rows)
    return pl.pallas_call(
        _memkv_kernel,
        grid=(rows // tm,),
        in_specs=[
            pl.BlockSpec((tm, D_MODEL), lambda i: (i, 0)),
            pl.BlockSpec((1, D_MODEL), lambda i: (0, 0)),
            pl.BlockSpec((D_MODEL, 2 * XA_WIDTH), lambda i: (0, 0)),
        ],
        out_specs=[
            pl.BlockSpec((tm, XA_WIDTH), lambda i: (i, 0)),
            pl.BlockSpec((tm, XA_WIDTH), lambda i: (i, 0)),
        ],
        out_shape=[jax.ShapeDtypeStruct((rows, XA_WIDTH), F32)] * 2,
        compiler_params=pltpu.CompilerParams(
            dimension_semantics=("arbitrary",), vmem_limit_bytes=VMEM_LIMIT_BYTES),
        name="memkv",
    )(mem2d, norm_w, wkv)


def _ssd_chunk_steps(rows, xs_s, bc_s, dt_s, neg_a, hT_s, y_s, scratch):
    acum_s, acumt_s, dtt_s, wc_s, cb_s, lhs_s, cd_s, xw_s, sg_s = scratch
    q = SSD_CHUNK

    def masks():
        ii = lax.broadcasted_iota(jnp.int32, (q, q), 0)
        jj = lax.broadcasted_iota(jnp.int32, (q, q), 1)
        low_half = lax.broadcasted_iota(jnp.int32, (q, LANES), 1) < SSD_HEAD_DIM
        low_half_row = lax.broadcasted_iota(jnp.int32, (1, LANES), 1) < SSD_HEAD_DIM
        return ii >= jj, low_half, low_half_row

    def prologue():
        causal, _, _ = masks()
        dt = dt_s[rows, :]
        d1, d2, d3 = _split3(dt * neg_a)
        tri = causal.astype(BF16)
        acum = _dot(tri, d1) + _dot(tri, d2) + _dot(tri, d3)
        acum_s[...] = acum
        acumt_s[...] = acum.T
        dtt_s[...] = dt.T
        wc_s[...] = dt * jnp.exp(acum[q - 1:q, :] - acum)

    def group_step(g):
        _, low_half, _ = masks()
        b_g = bc_s[rows, g * SSD_STATE:(g + 1) * SSD_STATE]
        c_g = bc_s[rows, (SSD_GROUPS + g) * SSD_STATE:(SSD_GROUPS + g + 1) * SSD_STATE]
        cb_s[g] = _dot_nt(c_g.astype(BF16), b_g.astype(BF16))
        for pr in range(PAIRS_PER_GROUP):
            pair = g * PAIRS_PER_GROUP + pr
            lanes = slice(pair * LANES, (pair + 1) * LANES)
            h0 = pair * HEADS_PER_LANE_TILE
            w_pair = jnp.where(low_half, jnp.broadcast_to(wc_s[:, h0:h0 + 1], (q, LANES)),
                               jnp.broadcast_to(wc_s[:, h0 + 1:h0 + 2], (q, LANES)))
            xw_s[:, lanes] = (xs_s[rows, lanes] * w_pair).astype(BF16)
        cols = slice(g * SSD_GROUP_WIDTH, (g + 1) * SSD_GROUP_WIDTH)
        sg_s[:, cols] = _dot(b_g.T.astype(BF16), xw_s[:, cols])

    def head_step(h):
        causal, _, _ = masks()
        g = h // (SSD_HEADS // SSD_GROUPS)
        c_g = bc_s[rows, (SSD_GROUPS + g) * SSD_STATE:(SSD_GROUPS + g + 1) * SSD_STATE]
        col = jnp.broadcast_to(acum_s[:, h:h + 1], (q, q))
        seg = jnp.where(causal, col - acumt_s[h:h + 1, :], -jnp.inf)
        lhs_s[h, :, 0:q] = (cb_s[g] * jnp.exp(seg) * dtt_s[h:h + 1, :]).astype(BF16)
        lhs_s[h, :, q:2 * q] = (c_g * jnp.exp(col)).astype(BF16)
        cd_s[h:h + 1, :] = jnp.exp(col[q - 1:q, :])

    def pair_step(pair):
        _, low_half, low_half_row = masks()
        lanes = slice(pair * LANES, (pair + 1) * LANES)
        h0 = pair * HEADS_PER_LANE_TILE
        h_prev = hT_s[:, lanes]
        rhs = jnp.concatenate([xs_s[rows, lanes].astype(BF16), h_prev.astype(BF16)], axis=0)
        y_s[rows, lanes] = jnp.where(low_half, _dot(lhs_s[h0], rhs), _dot(lhs_s[h0 + 1], rhs))
        decay = jnp.where(low_half_row, cd_s[h0:h0 + 1, :], cd_s[h0 + 1:h0 + 2, :])
        hT_s[:, lanes] = h_prev * decay + sg_s[:, lanes]

    steps = [(2, prologue)]
    steps += [(3, functools.partial(group_step, g)) for g in range(SSD_GROUPS)]
    steps += [(2, functools.partial(head_step, h)) for h in range(SSD_HEADS)]
    steps += [(1, functools.partial(pair_step, pair)) for pair in range(HEAD_PAIRS)]
    return steps


def _interleave(steps, pieces):
    total = sum(w for w, _ in steps)
    done = acc = 0
    for w, step in steps:
        acc += w
        upto = len(pieces) * acc // total
        for piece in pieces[done:upto]:
            piece()
        done = upto
        step()
    assert done == len(pieces)


def _in_proj_cols(w_refs, lo, hi):
    wa_ref, wdt_ref, wb_ref = w_refs
    if hi <= OFF_DT:
        return wa_ref[:, lo:hi]
    if lo >= OFF_Q:
        return wb_ref[:, lo - OFF_Q:hi - OFF_Q]
    assert (lo, hi) == (OFF_DT, OFF_Q)
    return wdt_ref[...]


def _mixer_kernel(x1_ref, x2_ref, k_ref, v_ref, nw_ref, wa_ref, wdt_ref, wb_ref, lnw_ref, lnb_ref, ws_ref,
                  bst_ref, cw_ref, cb_ref, dtb_ref, alog_ref, dsk_ref, snw_ref,
                  wgm_ref, wssd_ref, wxa_ref, wout_ref,
                  h_ref, conv_ref, ssm_ref,
                  xn_s, raw_s, q_s, ext_s, tail_s, hT_s, xs_s, bc_s, dt_s, y_s, u_s, v_s, gm_s, yn_s, xa_s, merged_s,
                  *ssd_scratch, tiles_per_seq):
    tm = x1_ref.shape[1]
    s = pl.program_id(0)
    t2 = lax.rem(jnp.maximum(s - 1, 0), tiles_per_seq)
    w_in = functools.partial(_in_proj_cols, (wa_ref, wdt_ref, wb_ref))
    head = SUBLANES

    @pl.when(s == 0)
    def _():
        raw_s[1] = jnp.zeros(raw_s.shape[1:], F32)
        q_s[1] = jnp.zeros(q_s.shape[1:], BF16)
        ext_s[1] = jnp.zeros(ext_s.shape[1:], F32)

    @pl.when(t2 == 0)
    def _():
        tail_s[...] = jnp.zeros(tail_s.shape, F32)
        hT_s[...] = jnp.zeros(hT_s.shape, F32)

    def step_body(wslot, rslot):
        xn_s[...] = _rms(x1_ref[0], nw_ref[...]).astype(BF16)

        def proj_piece(lo, width, store):
            return lambda: store(_dot(xn_s[...], w_in(lo, lo + width)))

        def raw_store(dst):
            def store(r):
                raw_s[wslot, :, dst:dst + r.shape[1]] = r
            return store

        def ext_store(dst):
            def store(r):
                ext_s[wslot, head:head + tm, dst:dst + r.shape[1]] = r
            return store

        def q_store(dst):
            def store(r):
                q_s[wslot, :, dst:dst + r.shape[1]] = r.astype(BF16)
            return store

        pc = PIECE_COLS
        pieces = ([proj_piece(OFF_XBC + j, pc, ext_store(j)) for j in range(0, SSD_CONV_DIM, pc)]
                  + [proj_piece(OFF_UV + j, pc, raw_store(RAW_UV + j)) for j in range(0, 2 * GM_WIDTH, pc)]
                  + [proj_piece(OFF_DT, DT_PAD, raw_store(RAW_DT))]
                  + [proj_piece(OFF_Z + j, pc, raw_store(RAW_Z + j)) for j in range(0, SSD_INNER, pc)]
                  + [proj_piece(OFF_Q + j, pc, q_store(j)) for j in range(0, XA_WIDTH, pc)]
                  + [proj_piece(OFF_GATE + j, pc, raw_store(RAW_GATE + j))
                     for j in range(0, N_BRANCH * D_MODEL, pc)])

        steps = []

        def conv_block(j):
            cols = slice(j, j + pc)
            acc = ext_s[rslot, head - 3:head - 3 + tm, cols] * cw_ref[0:1, cols]
            for k in range(1, SSD_CONV):
                acc = acc + ext_s[rslot, head - 3 + k:head - 3 + k + tm, cols] * cw_ref[k:k + 1, cols]
            xbc = _silu(acc + cb_ref[:, cols])
            if j < SSD_INNER:
                xs_s[:, cols] = xbc
            else:
                bc_s[:, j - SSD_INNER:j - SSD_INNER + pc] = xbc

        def conv_head():
            ext_s[rslot, 0:head, :] = tail_s[...]

        def conv_tail():
            conv_ref[0] = ext_s[rslot, tm + head - 3:tm + head, :]
            tail_s[...] = ext_s[rslot, tm:tm + head, :]
            dt_s[...] = _softplus(raw_s[rslot, :, RAW_DT:RAW_GATE] + dtb_ref[...])

        steps.append((0, conv_head))
        steps += [(3, functools.partial(conv_block, j)) for j in range(0, SSD_CONV_DIM, pc)]
        steps.append((1, conv_tail))

        def gelu_u(j):
            u_s[:, j:j + pc] = _gelu(raw_s[rslot, :, RAW_UV + j:RAW_UV + j + pc])

        def gelu_v():
            v = _gelu(raw_s[rslot, :, RAW_UV + GM_WIDTH:RAW_Z])
            v_s[...] = _layernorm(v, lnw_ref[...], lnb_ref[...]).astype(BF16)

        steps += [(2, functools.partial(gelu_u, j)) for j in range(0, GM_WIDTH, pc)]
        steps.append((4, gelu_v))

        neg_a = -jnp.exp(alog_ref[...])
        for c in range(tm // SSD_CHUNK):
            rows = slice(c * SSD_CHUNK, (c + 1) * SSD_CHUNK)
            steps += _ssd_chunk_steps(rows, xs_s, bc_s, dt_s, neg_a, hT_s, y_s, ssd_scratch)

        def gmlp_group(g):
            ii = lax.broadcasted_iota(jnp.int32, (GM_CHUNK, GM_CHUNK), 0)
            jj = lax.broadcasted_iota(jnp.int32, (GM_CHUNK, GM_CHUNK), 1)
            cols = slice(g * GM_GROUP_DIM, (g + 1) * GM_GROUP_DIM)
            w_tril = jnp.where(ii >= jj, ws_ref[g], 0.0).astype(BF16)
            bias = bst_ref[:, g:g + 1]
            for c in range(tm // GM_CHUNK):
                rows = slice(c * GM_CHUNK, (c + 1) * GM_CHUNK)
                mixed = _dot(w_tril, v_s[rows, cols]) + bias
                gm_s[rows, cols] = (u_s[rows, cols] * mixed).astype(BF16)

        steps += [(1, functools.partial(gmlp_group, g)) for g in range(GM_GROUPS)]

        def attn_head(hd):
            cols = slice(hd * XA_HEAD_DIM, (hd + 1) * XA_HEAD_DIM)
            sc = _dot_nt(q_s[rslot, :, cols], k_ref[0, :, cols].astype(BF16)) * (XA_HEAD_DIM ** -0.5)
            p = jnp.exp(sc - jnp.max(sc, axis=-1, keepdims=True))
            o = _dot(p.astype(BF16), v_ref[0, :, cols].astype(BF16))
            xa_s[:, cols] = (o / jnp.sum(p, axis=-1, keepdims=True)).astype(BF16)

        steps += [(2, functools.partial(attn_head, hd)) for hd in range(XA_HEADS)]

        def ssd_out_group(g):
            cols = slice(g * SSD_GROUP_WIDTH, (g + 1) * SSD_GROUP_WIDTH)
            zs = _silu(raw_s[rslot, :, RAW_Z + g * SSD_GROUP_WIDTH:RAW_Z + (g + 1) * SSD_GROUP_WIDTH])
            yv = (y_s[:, cols] + xs_s[:, cols] * dsk_ref[:, cols]) * zs
            yn_s[:, cols] = _rms(yv, snw_ref[:, cols]).astype(BF16)

        steps += [(3, functools.partial(ssd_out_group, g)) for g in range(SSD_GROUPS)]

        def merge_piece(i, br_s, w_ref, j):
            lo = RAW_GATE + i * D_MODEL + j
            term = jax.nn.sigmoid(raw_s[rslot, :, lo:lo + MERGE_COLS]) * _dot(br_s[...], w_ref[:, j:j + MERGE_COLS])
            if i == 0:
                merged_s[:, j:j + MERGE_COLS] = term
            else:
                merged_s[:, j:j + MERGE_COLS] += term

        for i, (br_s, w_ref) in enumerate(((gm_s, wgm_ref), (yn_s, wssd_ref), (xa_s, wxa_ref))):
            steps += [(2, functools.partial(merge_piece, i, br_s, w_ref, j)) for j in range(0, D_MODEL, MERGE_COLS)]

        def out_proj():
            h_ref[0] = x2_ref[0] + _dot(merged_s[...].astype(BF16), wout_ref[...])

        steps.append((1, out_proj))
        _interleave(steps, pieces)

    parity = lax.rem(s, 2)
    pl.when(parity == 0)(functools.partial(step_body, 0, 1))
    pl.when(parity == 1)(functools.partial(step_body, 1, 0))

    @pl.when(t2 == tiles_per_seq - 1)
    def _():
        for pair in range(HEAD_PAIRS):
            lanes = slice(pair * LANES, (pair + 1) * LANES)
            ssm_ref[0, lanes, :] = hT_s[:, lanes].T


def _full(shape):
    return pl.BlockSpec(shape, lambda *_: (0,) * len(shape))


def _mixer(x, mem_k, mem_v, p):
    b, seq, _ = x.shape
    tm = MIXER_ROWS
    assert seq % tm == 0 and tm % SSD_CHUNK == 0
    small = [p["norm_mix_w"], p["w_in_a"], p["w_in_dt"], p["w_in_b"],
             p["gm_ln_w"], p["gm_ln_b"], p["gm_ws"], p["gm_bs_t"],
             p["conv_w"], p["conv_b"], p["dt_bias"], p["a_log"], p["d_skip"], p["ssd_norm_w"],
             p["w_br_gm"], p["w_br_ssd"], p["w_br_xa"], p["w_out"]]
    tps = seq // tm
    n_tiles = b * tps
    tile1 = lambda s: jnp.minimum(s, n_tiles - 1)
    tile2 = lambda s: jnp.maximum(s - 1, 0)
    x_spec = lambda tile: pl.BlockSpec((1, tm, D_MODEL), lambda s: (tile(s) // tps, tile(s) % tps, 0))
    seq_spec = lambda r, c: pl.BlockSpec((1, r, c), lambda s: (tile2(s) // tps, 0, 0))
    return pl.pallas_call(
        functools.partial(_mixer_kernel, tiles_per_seq=tps),
        grid=(n_tiles + 1,),
        in_specs=[x_spec(tile1), x_spec(tile2), seq_spec(MEM_LEN, XA_WIDTH), seq_spec(MEM_LEN, XA_WIDTH)]
                 + [_full(a.shape) for a in small],
        out_specs=[x_spec(tile2), seq_spec(SSD_CONV - 1, SSD_CONV_DIM), seq_spec(SSD_INNER, SSD_STATE)],
        out_shape=[
            jax.ShapeDtypeStruct((b, seq, D_MODEL), F32),
            jax.ShapeDtypeStruct((b, SSD_CONV - 1, SSD_CONV_DIM), F32),
            jax.ShapeDtypeStruct((b, SSD_INNER, SSD_STATE), F32),
        ],
        scratch_shapes=[
            pltpu.VMEM((tm, D_MODEL), BF16),
            pltpu.VMEM((2, tm, RAW_WIDTH), F32),
            pltpu.VMEM((2, tm, XA_WIDTH), BF16),
            pltpu.VMEM((2, tm + SUBLANES, SSD_CONV_DIM), F32),
            pltpu.VMEM((SUBLANES, SSD_CONV_DIM), F32),
            pltpu.VMEM((SSD_STATE, SSD_INNER), F32),
            pltpu.VMEM((tm, SSD_INNER), F32),
            pltpu.VMEM((tm, 2 * SSD_GROUPS * SSD_STATE), F32),
            pltpu.VMEM((tm, DT_PAD), F32),
            pltpu.VMEM((tm, SSD_INNER), F32),
            pltpu.VMEM((tm, GM_WIDTH), F32),
            pltpu.VMEM((tm, GM_WIDTH), BF16),
            pltpu.VMEM((tm, GM_WIDTH), BF16),
            pltpu.VMEM((tm, SSD_INNER), BF16),
            pltpu.VMEM((tm, XA_WIDTH), BF16),
            pltpu.VMEM((tm, D_MODEL), F32),
            pltpu.VMEM((SSD_CHUNK, DT_PAD), F32),
            pltpu.VMEM((DT_PAD, SSD_CHUNK), F32),
            pltpu.VMEM((DT_PAD, SSD_CHUNK), F32),
            pltpu.VMEM((SSD_CHUNK, DT_PAD), F32),
            pltpu.VMEM((SSD_GROUPS, SSD_CHUNK, SSD_CHUNK), F32),
            pltpu.VMEM((SSD_HEADS, SSD_CHUNK, 2 * SSD_CHUNK), BF16),
            pltpu.VMEM((SSD_HEADS, SSD_CHUNK), F32),
            pltpu.VMEM((SSD_CHUNK, SSD_INNER), BF16),
            pltpu.VMEM((SSD_STATE, SSD_INNER), F32),
        ],
        compiler_params=pltpu.CompilerParams(
            dimension_semantics=("arbitrary",), vmem_limit_bytes=VMEM_LIMIT_BYTES),
        name="prompt_mixer",
    )(x, x, mem_k, mem_v, *small)


def _ffn_kernel(h_ref, nw_ref, wup_ref, wdn_ref, fw_ref, o_ref):
    h = h_ref[...]
    hn = _rms(h, nw_ref[...]).astype(BF16)
    acc = h
    for j in range(D_FF // FFN_COLS):
        cols = slice(j * FFN_COLS, (j + 1) * FFN_COLS)
        a = jnp.square(jnp.maximum(_dot(hn, wup_ref[:, cols]), 0.0)).astype(BF16)
        acc = acc + _dot(a, wdn_ref[cols, :])
    o_ref[...] = _rms(acc, fw_ref[...])


def _ffn(h2d, p):
    rows = h2d.shape[0]
    tm = min(FFN_ROWS, rows)
    assert rows % tm == 0
    return pl.pallas_call(
        _ffn_kernel,
        grid=(rows // tm,),
        in_specs=[
            pl.BlockSpec((tm, D_MODEL), lambda i: (i, 0)),
            _full((1, D_MODEL)),
            _full((D_MODEL, D_FF)),
            _full((D_FF, D_MODEL)),
            _full((1, D_MODEL)),
        ],
        out_specs=pl.BlockSpec((tm, D_MODEL), lambda i: (i, 0)),
        out_shape=jax.ShapeDtypeStruct((rows, D_MODEL), F32),
        compiler_params=pltpu.CompilerParams(
            dimension_semantics=("arbitrary",), vmem_limit_bytes=VMEM_LIMIT_BYTES),
        name="ffn",
    )(h2d, p["norm_ffn_w"], p["w_up"], p["w_down"], p["norm_final_w"])


def _sample_proj_kernel(x_ref, nw_ref, wa_ref, wdt_ref, wb_ref, o_ref):
    xn = _rms(x_ref[...], nw_ref[...]).astype(BF16)
    o_ref[:, :OFF_DT] = _dot(xn, wa_ref[...])
    o_ref[:, OFF_DT:OFF_Q] = _dot(xn, wdt_ref[...])
    o_ref[:, OFF_Q:] = _dot(xn, wb_ref[...])


def _sample_proj(x2d, p):
    rows = x2d.shape[0]
    args = [x2d, p["norm_mix_w"], p["w_in_a"], p["w_in_dt"], p["w_in_b"]]
    return pl.pallas_call(
        _sample_proj_kernel,
        grid=(1,),
        in_specs=[_full(a.shape) for a in args],
        out_specs=_full((rows, IN_PROJ_PACKED)),
        out_shape=jax.ShapeDtypeStruct((rows, IN_PROJ_PACKED), F32),
        compiler_params=pltpu.CompilerParams(
            dimension_semantics=("arbitrary",), vmem_limit_bytes=VMEM_LIMIT_BYTES),
        name="sample_proj",
    )(*args)


def _sample_state_kernel(proj_ref, stc_ref, ssm_ref, k_ref, v_ref, lnw_ref, lnb_ref, ws_ref,
                         bs_ref, cw_ref, cb_ref, dtb_ref, alog_ref, dsk_ref, snw_ref,
                         gm_ref, yn_ref, xa_ref, convn_ref, ssmn_ref, gv_ref):
    nb = proj_ref.shape[0]
    proj = proj_ref[...]

    uv = _gelu(proj[:, OFF_UV:OFF_Z])
    v = _layernorm(uv[:, GM_WIDTH:], lnw_ref[...], lnb_ref[...])
    gv_ref[...] = v
    for g in range(GM_GROUPS):
        cols = slice(g * GM_GROUP_DIM, (g + 1) * GM_GROUP_DIM)
        mixed = v[:, cols] * ws_ref[g, 0:1, 0:1] + bs_ref[g:g + 1, 0:1]
        gm_ref[:, cols] = uv[:, cols] * mixed

    st = stc_ref[...]
    xnew = proj[:, OFF_XBC:OFF_DT]
    acc = st[:, 0:SSD_CONV_DIM] * cw_ref[0:1, :]
    for k in range(1, SSD_CONV - 1):
        acc = acc + st[:, k * SSD_CONV_DIM:(k + 1) * SSD_CONV_DIM] * cw_ref[k:k + 1, :]
    acc = acc + xnew * cw_ref[SSD_CONV - 1:SSD_CONV, :]
    convn_ref[:, 0:(SSD_CONV - 2) * SSD_CONV_DIM] = st[:, SSD_CONV_DIM:]
    convn_ref[:, (SSD_CONV - 2) * SSD_CONV_DIM:] = xnew
    xbc = _silu(acc + cb_ref[...])
    xs = xbc[:, :SSD_INNER]
    bm = xbc[:, SSD_INNER:SSD_INNER + SSD_GROUPS * SSD_STATE].astype(BF16)
    cm = xbc[:, SSD_INNER + SSD_GROUPS * SSD_STATE:].astype(BF16)

    dt = _softplus(proj[:, OFF_DT:OFF_Q] + dtb_ref[...])
    decay = jnp.exp(dt * -jnp.exp(alog_ref[...]))
    hh = lax.broadcasted_iota(jnp.int32, (DT_PAD, SSD_INNER), 0)
    cc = lax.broadcasted_iota(jnp.int32, (DT_PAD, SSD_INNER), 1)
    spread = (jnp.right_shift(cc, SSD_HEAD_DIM.bit_length() - 1) == hh).astype(BF16)
    dt_wide = sum(_dot(piece, spread) for piece in _split3(dt))
    xdt = xs * dt_wide

    rid = lax.broadcasted_iota(jnp.int32, (nb, 1), 0)
    scale = XA_HEAD_DIM ** -0.5
    kv_vregs = MEM_LEN * XA_HEADS // SUBLANES
    lane_ones = jnp.ones((XA_HEAD_DIM, XA_HEAD_DIM), BF16)
    qv = proj[:, OFF_Q:OFF_GATE]
    y = jnp.zeros((nb, SSD_INNER), F32)
    for bi in range(nb):
        mine = rid == bi
        x_row = jnp.where(mine, xdt, 0.0).astype(BF16)
        parts = []
        for g in range(SSD_GROUPS):
            cols = slice(g * SSD_GROUP_WIDTH, (g + 1) * SSD_GROUP_WIDTH)
            ncols = slice(g * SSD_STATE, (g + 1) * SSD_STATE)
            upd = _dot_tn(x_row[:, cols], bm[:, ncols])
            heads = range(g * SSD_HEADS // SSD_GROUPS, (g + 1) * SSD_HEADS // SSD_GROUPS)
            e_rows = jnp.concatenate(
                [jnp.broadcast_to(decay[bi:bi + 1, h:h + 1], (SSD_HEAD_DIM, SSD_STATE)) for h in heads],
                axis=0)
            h_new = ssm_ref[bi, cols, :] * e_rows + upd
            ssmn_ref[bi, cols, :] = h_new
            parts.append(_dot_nt(cm[:, ncols], h_new.astype(BF16)))
        y = y + jnp.where(mine, jnp.concatenate(parts, axis=-1), 0.0)

        q_rows = jnp.concatenate(
            [qv[bi:bi + 1, hd * XA_HEAD_DIM:(hd + 1) * XA_HEAD_DIM] for hd in range(XA_HEADS)]
            * (SUBLANES // XA_HEADS), axis=0)
        kq = (k_ref[bi].reshape(kv_vregs, SUBLANES, XA_HEAD_DIM) * q_rows[None]).astype(BF16)
        s = _dot(kq.reshape(kv_vregs * SUBLANES, XA_HEAD_DIM), lane_ones) * scale
        s = s.reshape(kv_vregs, SUBLANES, XA_HEAD_DIM)
        pexp = jnp.exp(s - _fold_heads(_reduce_leading(s, jnp.maximum), jnp.maximum)[None])
        o = _reduce_leading(pexp * v_ref[bi].reshape(kv_vregs, SUBLANES, XA_HEAD_DIM), jnp.add)
        o = _fold_heads(o, jnp.add) / _fold_heads(_reduce_leading(pexp, jnp.add), jnp.add)
        for hd in range(XA_HEADS):
            xa_ref[bi:bi + 1, hd * XA_HEAD_DIM:(hd + 1) * XA_HEAD_DIM] = o[hd:hd + 1, :]

    zs = _silu(proj[:, OFF_Z:OFF_XBC])
    yn_ref[...] = _group_rms((y + xs * dsk_ref[...]) * zs, snw_ref[...])


def _sample_state(proj, state_conv2d, state_ssm, cache_k, cache_v, p):
    n = proj.shape[0]
    nb = SAMPLE_BLOCK
    assert n % nb == 0
    conv_w = (SSD_CONV - 1) * SSD_CONV_DIM
    small = [p["gm_ln_w"], p["gm_ln_b"], p["gm_ws"], p["gm_bs"], p["conv_w"], p["conv_b"],
             p["dt_bias"], p["a_log"], p["d_skip"], p["ssd_norm_w"]]
    rows2 = lambda w: pl.BlockSpec((nb, w), lambda i: (i, 0))
    rows3 = lambda a, c: pl.BlockSpec((nb, a, c), lambda i: (i, 0, 0))
    cache = rows3(MEM_LEN * XA_HEADS, XA_HEAD_DIM)
    return pl.pallas_call(
        _sample_state_kernel,
        grid=(n // nb,),
        in_specs=[rows2(IN_PROJ_PACKED), rows2(conv_w), rows3(SSD_INNER, SSD_STATE), cache, cache]
                 + [_full(a.shape) for a in small],
        out_specs=[rows2(GM_WIDTH), rows2(SSD_INNER), rows2(XA_WIDTH), rows2(conv_w),
                   rows3(SSD_INNER, SSD_STATE), rows2(GM_WIDTH)],
        out_shape=[
            jax.ShapeDtypeStruct((n, GM_WIDTH), F32),
            jax.ShapeDtypeStruct((n, SSD_INNER), F32),
            jax.ShapeDtypeStruct((n, XA_WIDTH), F32),
            jax.ShapeDtypeStruct((n, conv_w), F32),
            jax.ShapeDtypeStruct((n, SSD_INNER, SSD_STATE), F32),
            jax.ShapeDtypeStruct((n, GM_WIDTH), F32),
        ],
        compiler_params=pltpu.CompilerParams(
            dimension_semantics=("arbitrary",), vmem_limit_bytes=VMEM_LIMIT_BYTES),
        name="sample_state",
    )(proj, state_conv2d, state_ssm, cache_k, cache_v, *small)


def _sample_merge_kernel(x_ref, proj_ref, gm_ref, yn_ref, xa_ref, wgm_ref, wssd_ref, wxa_ref,
                         wout_ref, h_ref):
    merged = None
    for i, (br_ref, w_ref) in enumerate(((gm_ref, wgm_ref), (yn_ref, wssd_ref), (xa_ref, wxa_ref))):
        gate = jax.nn.sigmoid(proj_ref[:, OFF_GATE + i * D_MODEL:OFF_GATE + (i + 1) * D_MODEL])
        term = gate * _dot(br_ref[...].astype(BF16), w_ref[...])
        merged = term if merged is None else merged + term
    h_ref[...] = x_ref[...] + _dot(merged.astype(BF16), wout_ref[...])


def _sample_merge(x2d, proj, gm, yn, xa, p):
    args = [x2d, proj, gm, yn, xa, p["w_br_gm"], p["w_br_ssd"], p["w_br_xa"], p["w_out"]]
    return pl.pallas_call(
        _sample_merge_kernel,
        grid=(1,),
        in_specs=[_full(a.shape) for a in args],
        out_specs=_full(x2d.shape),
        out_shape=jax.ShapeDtypeStruct(x2d.shape, F32),
        compiler_params=pltpu.CompilerParams(
            dimension_semantics=("arbitrary",), vmem_limit_bytes=VMEM_LIMIT_BYTES),
        name="sample_merge",
    )(*args)


def _pack_w_in_kernel(wt_ref, wa_ref, wdt_ref, wb_ref):
    rest = wb_ref.shape[1]
    for j in range(0, OFF_DT, PACK_COLS):
        wa_ref[:, j:j + PACK_COLS] = wt_ref[0, j:j + PACK_COLS, :].T.astype(BF16)
    dt_rows = wt_ref[0, OFF_DT:OFF_DT + DT_PAD, :]
    is_dt = lax.broadcasted_iota(jnp.int32, dt_rows.shape, 0) < SSD_HEADS
    wdt_ref[...] = jnp.where(is_dt, dt_rows, 0.0).T.astype(BF16)
    base = OFF_DT + SSD_HEADS
    for j in range(0, rest, PACK_COLS):
        wb_ref[:, j:j + PACK_COLS] = wt_ref[0, base + j:base + j + PACK_COLS, :].T.astype(BF16)


def _pack_w_in(w_in, l):
    d, width = w_in.shape[1:]
    rest = width - OFF_DT - SSD_HEADS
    assert rest == IN_PROJ_PACKED - OFF_Q and OFF_DT % PACK_COLS == 0 and rest % PACK_COLS == 0
    w_t = jnp.swapaxes(w_in, 1, 2)
    return pl.pallas_call(
        _pack_w_in_kernel,
        grid=(1,),
        in_specs=[pl.BlockSpec((1, width, d), lambda i: (l, 0, 0))],
        out_specs=[_full((d, OFF_DT)), _full((d, DT_PAD)), _full((d, rest))],
        out_shape=[jax.ShapeDtypeStruct((d, OFF_DT), BF16),
                   jax.ShapeDtypeStruct((d, DT_PAD), BF16),
                   jax.ShapeDtypeStruct((d, rest), BF16)],
        compiler_params=pltpu.CompilerParams(
            dimension_semantics=("arbitrary",), vmem_limit_bytes=VMEM_LIMIT_BYTES),
        name="pack_w_in",
    )(w_t)


def _prep_layer(l, norm_mix_w, w_in, gm_ln_w, gm_ln_b, gm_ws, gm_bs, conv_w, conv_b, dt_bias, a_log,
                d_skip, ssd_norm_w, mem_norm_w, w_mem_k, w_mem_v, w_br_gm, w_br_ssd, w_br_xa, w_out,
                norm_ffn_w, w_up, w_down, norm_final_w):
    row = lambda a: a.reshape(1, -1)
    pad_heads = lambda a: jnp.pad(a, (0, DT_PAD - SSD_HEADS)).reshape(1, DT_PAD)
    w_in_a, w_in_dt, w_in_b = _pack_w_in(w_in, l)
    return {
        "norm_mix_w": row(norm_mix_w[l]),
        "w_in_a": w_in_a, "w_in_dt": w_in_dt, "w_in_b": w_in_b,
        "gm_ln_w": row(gm_ln_w[l]), "gm_ln_b": row(gm_ln_b[l]),
        "gm_ws": gm_ws[l], "gm_bs": gm_bs[l], "gm_bs_t": gm_bs[l].T,
        "conv_w": conv_w[l], "conv_b": row(conv_b[l]),
        "dt_bias": pad_heads(dt_bias[l]), "a_log": pad_heads(a_log[l]),
        "d_skip": row(jnp.repeat(d_skip[l], SSD_HEAD_DIM)), "ssd_norm_w": row(ssd_norm_w[l]),
        "mem_norm_w": row(mem_norm_w[l]),
        "w_kv": jnp.concatenate([w_mem_k[l], w_mem_v[l]], axis=1).astype(BF16),
        "w_br_gm": w_br_gm[l].astype(BF16), "w_br_ssd": w_br_ssd[l].astype(BF16),
        "w_br_xa": w_br_xa[l].astype(BF16), "w_out": w_out[l].astype(BF16),
        "norm_ffn_w": row(norm_ffn_w[l]), "w_up": w_up[l].astype(BF16),
        "w_down": w_down[l].astype(BF16), "norm_final_w": row(norm_final_w),
    }


def kernel(x_prompt, x_sample, mem_prompt, cache_mem_k, cache_mem_v, state_conv, state_ssm, norm_mix_w, w_in, gm_ln_w, gm_ln_b, gm_ws, gm_bs, conv_w, conv_b, dt_bias, a_log, d_skip, ssd_norm_w, mem_norm_w, w_mem_k, w_mem_v, w_br_gm, w_br_ssd, w_br_xa, w_out, norm_ffn_w, w_up, w_down, norm_final_w):
    depth = w_in.shape[0]
    assert depth == 1, "the final norm is fused into the MLP kernel of the last (only) layer"
    b, seq, _ = x_prompt.shape
    n, dec_seq, _ = x_sample.shape
    assert dec_seq == 1
    p = _prep_layer(0, norm_mix_w, w_in, gm_ln_w, gm_ln_b, gm_ws, gm_bs, conv_w, conv_b, dt_bias,
                    a_log, d_skip, ssd_norm_w, mem_norm_w, w_mem_k, w_mem_v, w_br_gm, w_br_ssd,
                    w_br_xa, w_out, norm_ffn_w, w_up, w_down, norm_final_w)

    mem_k, mem_v = _memkv(mem_prompt.reshape(b * MEM_LEN, D_MODEL), p["mem_norm_w"], p["w_kv"])
    h1, conv_p, ssm_p = _mixer(x_prompt, mem_k.reshape(b, MEM_LEN, XA_WIDTH),
                               mem_v.reshape(b, MEM_LEN, XA_WIDTH), p)
    y_prompt = _ffn(h1.reshape(b * seq, D_MODEL), p).reshape(b, seq, D_MODEL)

    xs2d = x_sample.reshape(n, D_MODEL)
    proj = _sample_proj(xs2d, p)
    gm, yn, xa, conv_s, ssm_s, gv = _sample_state(
        proj, state_conv[0].reshape(n, (SSD_CONV - 1) * SSD_CONV_DIM), state_ssm[0].reshape(n, SSD_INNER, SSD_STATE),
        cache_mem_k[0].reshape(n, MEM_LEN * XA_HEADS, XA_HEAD_DIM),
        cache_mem_v[0].reshape(n, MEM_LEN * XA_HEADS, XA_HEAD_DIM), p)
    hs = _sample_merge(xs2d, proj, gm, yn, xa, p)
    y_sample = _ffn(hs, p).reshape(n, 1, D_MODEL)

    kv_shape = (1, b, MEM_LEN, XA_HEADS, XA_HEAD_DIM)
    state_shape = (SSD_HEADS, SSD_HEAD_DIM, SSD_STATE)
    return (y_prompt, y_sample,
            mem_k.reshape(kv_shape), mem_v.reshape(kv_shape),
            conv_p.reshape(1, b, SSD_CONV - 1, SSD_CONV_DIM), ssm_p.reshape((1, b) + state_shape),
            conv_s.reshape(1, n, SSD_CONV - 1, SSD_CONV_DIM), ssm_s.reshape((1, n) + state_shape),
            gv.reshape(1, n, 1, GM_WIDTH))
```

```python
import functools

import jax
import jax.numpy as jnp
from jax import lax
from jax.experimental import pallas as pl
from jax.experimental.pallas import tpu as pltpu

F32 = jnp.float32
BF16 = jnp.bfloat16

D_MODEL = 1024
GM_CHUNK = 128
GM_GROUPS = 4
GM_GROUP_DIM = 128
GM_WIDTH = GM_GROUPS * GM_GROUP_DIM
SSD_HEADS = 16
SSD_HEAD_DIM = 64
SSD_INNER = SSD_HEADS * SSD_HEAD_DIM
SSD_GROUPS = 2
SSD_GROUP_WIDTH = SSD_INNER // SSD_GROUPS
SSD_STATE = 128
SSD_CONV = 4
SSD_CHUNK = 128
SSD_CONV_DIM = SSD_INNER + 2 * SSD_GROUPS * SSD_STATE
MEM_LEN = 256
XA_HEADS = 4
XA_HEAD_DIM = 128
XA_WIDTH = XA_HEADS * XA_HEAD_DIM
N_BRANCH = 3
D_FF = 4 * D_MODEL
EPS = 1e-6

SUBLANES = 8
LANES = 128
HEADS_PER_LANE_TILE = LANES // SSD_HEAD_DIM
HEAD_PAIRS = SSD_HEADS // HEADS_PER_LANE_TILE
PAIRS_PER_GROUP = HEAD_PAIRS // SSD_GROUPS

DT_PAD = LANES
OFF_UV = 0
OFF_Z = OFF_UV + 2 * GM_WIDTH
OFF_XBC = OFF_Z + SSD_INNER
OFF_DT = OFF_XBC + SSD_CONV_DIM
OFF_Q = OFF_DT + DT_PAD
OFF_GATE = OFF_Q + XA_WIDTH
IN_PROJ_PACKED = OFF_GATE + N_BRANCH * D_MODEL

RAW_UV = 0
RAW_Z = RAW_UV + 2 * GM_WIDTH
RAW_DT = RAW_Z + SSD_INNER
RAW_GATE = RAW_DT + DT_PAD
RAW_WIDTH = RAW_GATE + N_BRANCH * D_MODEL

MIXER_ROWS = 256
PIECE_COLS = 256
MERGE_COLS = 512
FFN_ROWS = 1024
FFN_COLS = 1024
MEMKV_ROWS = 512
PACK_COLS = 512
SAMPLE_BLOCK = SUBLANES
VMEM_LIMIT_BYTES = 56 * 1024 * 1024


def _dot(a, b):
    return jnp.dot(a, b, preferred_element_type=F32)


def _dot_nt(a, b):
    return lax.dot_general(a, b, (((1,), (1,)), ((), ())), preferred_element_type=F32)


def _dot_tn(a, b):
    return lax.dot_general(a, b, (((0,), (0,)), ((), ())), preferred_element_type=F32)


def _rms(x, w):
    return x * lax.rsqrt(jnp.mean(x * x, axis=-1, keepdims=True) + EPS) * w


def _layernorm(x, w, b):
    xc = x - jnp.mean(x, axis=-1, keepdims=True)
    return xc * lax.rsqrt(jnp.mean(xc * xc, axis=-1, keepdims=True) + EPS) * w + b


def _gelu(x):
    return 0.5 * x * (1.0 + lax.erf(x * 0.7071067811865476))


def _silu(x):
    return x * jax.nn.sigmoid(x)


def _softplus(x):
    return jnp.maximum(x, 0.0) + jnp.log1p(jnp.exp(-jnp.abs(x)))


def _split3(x):
    p1 = x.astype(BF16)
    r1 = x - p1.astype(F32)
    p2 = r1.astype(BF16)
    p3 = (r1 - p2.astype(F32)).astype(BF16)
    return p1, p2, p3


REDUCE_CHAINS = 8


def _reduce_leading(x, op):
    n = x.shape[0]
    assert n % REDUCE_CHAINS == 0
    acc = [x[w] for w in range(REDUCE_CHAINS)]
    for i in range(REDUCE_CHAINS, n, REDUCE_CHAINS):
        acc = [op(a, x[i + w]) for w, a in enumerate(acc)]
    while len(acc) > 1:
        acc = [op(acc[2 * i], acc[2 * i + 1]) for i in range(len(acc) // 2)]
    return acc[0]


def _fold_heads(a, op):
    assert SUBLANES == 2 * XA_HEADS
    return op(a, pltpu.roll(a, XA_HEADS, axis=0))


def _group_rms(y, w):
    parts = []
    for g in range(SSD_GROUPS):
        cols = slice(g * SSD_GROUP_WIDTH, (g + 1) * SSD_GROUP_WIDTH)
        parts.append(_rms(y[:, cols], w[:, cols]))
    return jnp.concatenate(parts, axis=-1)


def _memkv_kernel(mem_ref, nw_ref, wkv_ref, k_ref, v_ref):
    mn = _rms(mem_ref[...], nw_ref[...]).astype(BF16)
    kv = _dot(mn, wkv_ref[...])
    k_ref[...] = kv[:, :XA_WIDTH]
    v_ref[...] = kv[:, XA_WIDTH:]


def _memkv(mem2d, norm_w, wkv):
    rows = mem2d.shape[0]
    tm = min(MEMKV_ROWS, rows)
    return pl.pallas_call(
        _memkv_kernel,
        grid=(rows // tm,),
        in_specs=[
            pl.BlockSpec((tm, D_MODEL), lambda i: (i, 0)),
            pl.BlockSpec((1, D_MODEL), lambda i: (0, 0)),
            pl.BlockSpec((D_MODEL, 2 * XA_WIDTH), lambda i: (0, 0)),
        ],
        out_specs=[
            pl.BlockSpec((tm, XA_WIDTH), lambda i: (i, 0)),
            pl.BlockSpec((tm, XA_WIDTH), lambda i: (i, 0)),
        ],
        out_shape=[jax.ShapeDtypeStruct((rows, XA_WIDTH), F32)] * 2,
        compiler_params=pltpu.CompilerParams(
            dimension_semantics=("arbitrary",), vmem_limit_bytes=VMEM_LIMIT_BYTES),
        name="memkv",
    )(mem2d, norm_w, wkv)


def _ssd_chunk_steps(rows, xs_s, bc_s, dt_s, neg_a, hT_s, y_s, scratch):
    acum_s, acumt_s, dtt_s, wc_s, cb_s, lhs_s, cd_s, xw_s, sg_s = scratch
    q = SSD_CHUNK

    def masks():
        ii = lax.broadcasted_iota(jnp.int32, (q, q), 0)
        jj = lax.broadcasted_iota(jnp.int32, (q, q), 1)
        low_half = lax.broadcasted_iota(jnp.int32, (q, LANES), 1) < SSD_HEAD_DIM
        low_half_row = lax.broadcasted_iota(jnp.int32, (1, LANES), 1) < SSD_HEAD_DIM
        return ii >= jj, low_half, low_half_row

    def prologue():
        causal, _, _ = masks()
        dt = dt_s[rows, :]
        d1, d2, d3 = _split3(dt * neg_a)
        tri = causal.astype(BF16)
        acum = _dot(tri, d1) + _dot(tri, d2) + _dot(tri, d3)
        acum_s[...] = acum
        acumt_s[...] = acum.T
        dtt_s[...] = dt.T
        wc_s[...] = dt * jnp.exp(acum[q - 1:q, :] - acum)

    def group_step(g):
        _, low_half, _ = masks()
        b_g = bc_s[rows, g * SSD_STATE:(g + 1) * SSD_STATE]
        c_g = bc_s[rows, (SSD_GROUPS + g) * SSD_STATE:(SSD_GROUPS + g + 1) * SSD_STATE]
        cb_s[g] = _dot_nt(c_g.astype(BF16), b_g.astype(BF16))
        for pr in range(PAIRS_PER_GROUP):
            pair = g * PAIRS_PER_GROUP + pr
            lanes = slice(pair * LANES, (pair + 1) * LANES)
            h0 = pair * HEADS_PER_LANE_TILE
            w_pair = jnp.where(low_half, jnp.broadcast_to(wc_s[:, h0:h0 + 1], (q, LANES)),
                               jnp.broadcast_to(wc_s[:, h0 + 1:h0 + 2], (q, LANES)))
            xw_s[:, lanes] = (xs_s[rows, lanes] * w_pair).astype(BF16)
        cols = slice(g * SSD_GROUP_WIDTH, (g + 1) * SSD_GROUP_WIDTH)
        sg_s[:, cols] = _dot(b_g.T.astype(BF16), xw_s[:, cols])

    def head_step(h):
        causal, _, _ = masks()
        g = h // (SSD_HEADS // SSD_GROUPS)
        c_g = bc_s[rows, (SSD_GROUPS + g) * SSD_STATE:(SSD_GROUPS + g + 1) * SSD_STATE]
        col = jnp.broadcast_to(acum_s[:, h:h + 1], (q, q))
        seg = jnp.where(causal, col - acumt_s[h:h + 1, :], -jnp.inf)
        lhs_s[h, :, 0:q] = (cb_s[g] * jnp.exp(seg) * dtt_s[h:h + 1, :]).astype(BF16)
        lhs_s[h, :, q:2 * q] = (c_g * jnp.exp(col)).astype(BF16)
        cd_s[h:h + 1, :] = jnp.exp(col[q - 1:q, :])

    def pair_step(pair):
        _, low_half, low_half_row = masks()
        lanes = slice(pair * LANES, (pair + 1) * LANES)
        h0 = pair * HEADS_PER_LANE_TILE
        h_prev = hT_s[:, lanes]
        rhs = jnp.concatenate([xs_s[rows, lanes].astype(BF16), h_prev.astype(BF16)], axis=0)
        y_s[rows, lanes] = jnp.where(low_half, _dot(lhs_s[h0], rhs), _dot(lhs_s[h0 + 1], rhs))
        decay = jnp.where(low_half_row, cd_s[h0:h0 + 1, :], cd_s[h0 + 1:h0 + 2, :])
        hT_s[:, lanes] = h_prev * decay + sg_s[:, lanes]

    steps = [(2, prologue)]
    steps += [(3, functools.partial(group_step, g)) for g in range(SSD_GROUPS)]
    steps += [(2, functools.partial(head_step, h)) for h in range(SSD_HEADS)]
    steps += [(1, functools.partial(pair_step, pair)) for pair in range(HEAD_PAIRS)]
    return steps


def _interleave(steps, pieces):
    total = sum(w for w, _ in steps)
    done = acc = 0
    for w, step in steps:
        acc += w
        upto = len(pieces) * acc // total
        for piece in pieces[done:upto]:
            piece()
        done = upto
        step()
    assert done == len(pieces)


def _in_proj_cols(w_refs, lo, hi):
    wa_ref, wdt_ref, wb_ref = w_refs
    if hi <= OFF_DT:
        return wa_ref[:, lo:hi]
    if lo >= OFF_Q:
        return wb_ref[:, lo - OFF_Q:hi - OFF_Q]
    assert (lo, hi) == (OFF_DT, OFF_Q)
    return wdt_ref[...]


def _mixer_kernel(x1_ref, x2_ref, k_ref, v_ref, nw_ref, wa_ref, wdt_ref, wb_ref, lnw_ref, lnb_ref, ws_ref,
                  bst_ref, cw_ref, cb_ref, dtb_ref, alog_ref, dsk_ref, snw_ref,
                  wgm_ref, wssd_ref, wxa_ref, wout_ref,
                  h_ref, conv_ref, ssm_ref,
                  xn_s, raw_s, q_s, ext_s, hT_s, xs_s, bc_s, dt_s, y_s, u_s, v_s, gm_s, yn_s, xa_s, merged_s,
                  zs_s, gate_s, qb_s, *ssd_scratch, tiles_per_seq):
    tm = x1_ref.shape[1]
    s = pl.program_id(0)
    t2 = lax.rem(jnp.maximum(s - 1, 0), tiles_per_seq)
    w_in = functools.partial(_in_proj_cols, (wa_ref, wdt_ref, wb_ref))
    head = SUBLANES
    pc = PIECE_COLS

    @pl.when(s == 0)
    def _():
        raw_s[...] = jnp.zeros(raw_s.shape, F32)
        ext_s[...] = jnp.zeros(ext_s.shape, F32)
        gate_s[...] = jnp.zeros(gate_s.shape, F32)
        zs_s[...] = jnp.zeros(zs_s.shape, F32)
        qb_s[...] = jnp.zeros(qb_s.shape, BF16)

    @pl.when(t2 == 0)
    def _():
        ext_s[0:head, :] = jnp.zeros((head, SSD_CONV_DIM), F32)
        hT_s[...] = jnp.zeros(hT_s.shape, F32)

    xn_s[...] = _rms(x1_ref[0], nw_ref[...]).astype(BF16)

    def project(lo, width):
        return _dot(xn_s[...], w_in(lo, lo + width))

    def raw_piece(dst, src, width):
        def run():
            raw_s[:, dst:dst + width] = project(src, width)
        return run

    def q_piece(j):
        def run():
            q_s[:, j:j + pc] = project(OFF_Q + j, pc).astype(BF16)
        return run

    def xbc_piece(j):
        def run():
            ext_s[head:head + tm, j:j + pc] = project(OFF_XBC + j, pc)
        return run

    for j in range(0, SSD_CONV_DIM, pc):
        cols = slice(j, j + pc)
        acc = ext_s[head - 3:head - 3 + tm, cols] * cw_ref[0:1, cols]
        for k in range(1, SSD_CONV):
            acc = acc + ext_s[head - 3 + k:head - 3 + k + tm, cols] * cw_ref[k:k + 1, cols]
        xbc = _silu(acc + cb_ref[:, cols])
        if j < SSD_INNER:
            xs_s[:, cols] = xbc
        else:
            bc_s[:, j - SSD_INNER:j - SSD_INNER + pc] = xbc
        conv_ref[0, :, cols] = ext_s[tm + head - 3:tm + head, cols]
        ext_s[0:head, cols] = ext_s[tm:tm + head, cols]
    dt_s[...] = _softplus(raw_s[:, RAW_DT:RAW_GATE] + dtb_ref[...])
    for j in range(0, GM_WIDTH, pc):
        u_s[:, j:j + pc] = _gelu(raw_s[:, RAW_UV + j:RAW_UV + j + pc])
    v = _gelu(raw_s[:, RAW_UV + GM_WIDTH:RAW_Z])
    v_s[...] = _layernorm(v, lnw_ref[...], lnb_ref[...]).astype(BF16)

    pieces = ([raw_piece(RAW_GATE + j, OFF_GATE + j, pc) for j in range(0, N_BRANCH * D_MODEL, pc)]
              + [raw_piece(RAW_Z + j, OFF_Z + j, pc) for j in range(0, SSD_INNER, pc)]
              + [q_piece(j) for j in range(0, XA_WIDTH, pc)]
              + [xbc_piece(j) for j in range(0, SSD_CONV_DIM, pc)] + [raw_piece(RAW_DT, OFF_DT, DT_PAD)]
              + [raw_piece(RAW_UV + j, OFF_UV + j, pc) for j in range(0, 2 * GM_WIDTH, pc)])
    steps = []
    neg_a = -jnp.exp(alog_ref[...])
    for c in range(tm // SSD_CHUNK):
        rows = slice(c * SSD_CHUNK, (c + 1) * SSD_CHUNK)
        steps += _ssd_chunk_steps(rows, xs_s, bc_s, dt_s, neg_a, hT_s, y_s, ssd_scratch)

    def gmlp_group(g):
        ii = lax.broadcasted_iota(jnp.int32, (GM_CHUNK, GM_CHUNK), 0)
        jj = lax.broadcasted_iota(jnp.int32, (GM_CHUNK, GM_CHUNK), 1)
        cols = slice(g * GM_GROUP_DIM, (g + 1) * GM_GROUP_DIM)
        w_tril = jnp.where(ii >= jj, ws_ref[g], 0.0).astype(BF16)
        bias = bst_ref[:, g:g + 1]
        for c in range(tm // GM_CHUNK):
            rows = slice(c * GM_CHUNK, (c + 1) * GM_CHUNK)
            mixed = _dot(w_tril, v_s[rows, cols]) + bias
            gm_s[rows, cols] = (u_s[rows, cols] * mixed).astype(BF16)

    steps += [(1, functools.partial(gmlp_group, g)) for g in range(GM_GROUPS)]

    def attn_head(hd):
        cols = slice(hd * XA_HEAD_DIM, (hd + 1) * XA_HEAD_DIM)
        sc = _dot_nt(qb_s[:, cols], k_ref[0, :, cols].astype(BF16)) * (XA_HEAD_DIM ** -0.5)
        p = jnp.exp(sc - jnp.max(sc, axis=-1, keepdims=True))
        o = _dot(p.astype(BF16), v_ref[0, :, cols].astype(BF16))
        xa_s[:, cols] = (o / jnp.sum(p, axis=-1, keepdims=True)).astype(BF16)

    steps += [(2, functools.partial(attn_head, hd)) for hd in range(XA_HEADS)]

    def ssd_out_group(g):
        cols = slice(g * SSD_GROUP_WIDTH, (g + 1) * SSD_GROUP_WIDTH)
        yv = (y_s[:, cols] + xs_s[:, cols] * dsk_ref[:, cols]) * zs_s[:, cols]
        yn_s[:, cols] = _rms(yv, snw_ref[:, cols]).astype(BF16)

    steps += [(2, functools.partial(ssd_out_group, g)) for g in range(SSD_GROUPS)]

    def merge_piece(i, br_s, w_ref, j):
        lo = i * D_MODEL + j
        term = gate_s[:, lo:lo + MERGE_COLS] * _dot(br_s[...], w_ref[:, j:j + MERGE_COLS])
        if i == 0:
            merged_s[:, j:j + MERGE_COLS] = term
        else:
            merged_s[:, j:j + MERGE_COLS] += term

    for i, (br_s, w_ref) in enumerate(((gm_s, wgm_ref), (yn_s, wssd_ref), (xa_s, wxa_ref))):
        steps += [(2, functools.partial(merge_piece, i, br_s, w_ref, j)) for j in range(0, D_MODEL, MERGE_COLS)]

    def out_proj():
        h_ref[0] = x2_ref[0] + _dot(merged_s[...].astype(BF16), wout_ref[...])

    steps.append((1, out_proj))
    _interleave(steps, pieces)

    for j in range(0, N_BRANCH * D_MODEL, pc):
        gate_s[:, j:j + pc] = jax.nn.sigmoid(raw_s[:, RAW_GATE + j:RAW_GATE + j + pc])
    for j in range(0, SSD_INNER, pc):
        zs_s[:, j:j + pc] = _silu(raw_s[:, RAW_Z + j:RAW_Z + j + pc])
    qb_s[...] = q_s[...]

    @pl.when(t2 == tiles_per_seq - 1)
    def _():
        for pair in range(HEAD_PAIRS):
            lanes = slice(pair * LANES, (pair + 1) * LANES)
            ssm_ref[0, lanes, :] = hT_s[:, lanes].T


def _full(shape):
    return pl.BlockSpec(shape, lambda *_: (0,) * len(shape))


def _mixer(x, mem_k, mem_v, p):
    b, seq, _ = x.shape
    tm = MIXER_ROWS
    assert seq % tm == 0 and tm % SSD_CHUNK == 0
    small = [p["norm_mix_w"], p["w_in_a"], p["w_in_dt"], p["w_in_b"],
             p["gm_ln_w"], p["gm_ln_b"], p["gm_ws"], p["gm_bs_t"],
             p["conv_w"], p["conv_b"], p["dt_bias"], p["a_log"], p["d_skip"], p["ssd_norm_w"],
             p["w_br_gm"], p["w_br_ssd"], p["w_br_xa"], p["w_out"]]
    tps = seq // tm
    n_tiles = b * tps
    tile1 = lambda s: jnp.minimum(s, n_tiles - 1)
    tile2 = lambda s: jnp.maximum(s - 1, 0)
    x_spec = lambda tile: pl.BlockSpec((1, tm, D_MODEL), lambda s: (tile(s) // tps, tile(s) % tps, 0))
    seq_spec = lambda r, c: pl.BlockSpec((1, r, c), lambda s: (tile2(s) // tps, 0, 0))
    return pl.pallas_call(
        functools.partial(_mixer_kernel, tiles_per_seq=tps),
        grid=(n_tiles + 1,),
        in_specs=[x_spec(tile1), x_spec(tile2), seq_spec(MEM_LEN, XA_WIDTH), seq_spec(MEM_LEN, XA_WIDTH)]
                 + [_full(a.shape) for a in small],
        out_specs=[x_spec(tile2), seq_spec(SSD_CONV - 1, SSD_CONV_DIM), seq_spec(SSD_INNER, SSD_STATE)],
        out_shape=[
            jax.ShapeDtypeStruct((b, seq, D_MODEL), F32),
            jax.ShapeDtypeStruct((b, SSD_CONV - 1, SSD_CONV_DIM), F32),
            jax.ShapeDtypeStruct((b, SSD_INNER, SSD_STATE), F32),
        ],
        scratch_shapes=[
            pltpu.VMEM((tm, D_MODEL), BF16),
            pltpu.VMEM((tm, RAW_WIDTH), F32),
            pltpu.VMEM((tm, XA_WIDTH), BF16),
            pltpu.VMEM((tm + SUBLANES, SSD_CONV_DIM), F32),
            pltpu.VMEM((SSD_STATE, SSD_INNER), F32),
            pltpu.VMEM((tm, SSD_INNER), F32),
            pltpu.VMEM((tm, 2 * SSD_GROUPS * SSD_STATE), F32),
            pltpu.VMEM((tm, DT_PAD), F32),
            pltpu.VMEM((tm, SSD_INNER), F32),
            pltpu.VMEM((tm, GM_WIDTH), F32),
            pltpu.VMEM((tm, GM_WIDTH), BF16),
            pltpu.VMEM((tm, GM_WIDTH), BF16),
            pltpu.VMEM((tm, SSD_INNER), BF16),
            pltpu.VMEM((tm, XA_WIDTH), BF16),
            pltpu.VMEM((tm, D_MODEL), F32),
            pltpu.VMEM((tm, SSD_INNER), F32),
            pltpu.VMEM((tm, N_BRANCH * D_MODEL), F32),
            pltpu.VMEM((tm, XA_WIDTH), BF16),
            pltpu.VMEM((SSD_CHUNK, DT_PAD), F32),
            pltpu.VMEM((DT_PAD, SSD_CHUNK), F32),
            pltpu.VMEM((DT_PAD, SSD_CHUNK), F32),
            pltpu.VMEM((SSD_CHUNK, DT_PAD), F32),
            pltpu.VMEM((SSD_GROUPS, SSD_CHUNK, SSD_CHUNK), F32),
            pltpu.VMEM((SSD_HEADS, SSD_CHUNK, 2 * SSD_CHUNK), BF16),
            pltpu.VMEM((SSD_HEADS, SSD_CHUNK), F32),
            pltpu.VMEM((SSD_CHUNK, SSD_INNER), BF16),
            pltpu.VMEM((SSD_STATE, SSD_INNER), F32),
        ],
        compiler_params=pltpu.CompilerParams(
            dimension_semantics=("arbitrary",), vmem_limit_bytes=VMEM_LIMIT_BYTES),
        name="prompt_mixer",
    )(x, x, mem_k, mem_v, *small)


def _ffn_kernel(h_ref, nw_ref, wup_ref, wdn_ref, fw_ref, o_ref):
    h = h_ref[...]
    hn = _rms(h, nw_ref[...]).astype(BF16)
    acc = h
    for j in range(D_FF // FFN_COLS):
        cols = slice(j * FFN_COLS, (j + 1) * FFN_COLS)
        a = jnp.square(jnp.maximum(_dot(hn, wup_ref[:, cols]), 0.0)).astype(BF16)
        acc = acc + _dot(a, wdn_ref[cols, :])
    o_ref[...] = _rms(acc, fw_ref[...])


def _ffn(h2d, p):
    rows = h2d.shape[0]
    tm = min(FFN_ROWS, rows)
    assert rows % tm == 0
    return pl.pallas_call(
        _ffn_kernel,
        grid=(rows // tm,),
        in_specs=[
            pl.BlockSpec((tm, D_MODEL), lambda i: (i, 0)),
            _full((1, D_MODEL)),
            _full((D_MODEL, D_FF)),
            _full((D_FF, D_MODEL)),
            _full((1, D_MODEL)),
        ],
        out_specs=pl.BlockSpec((tm, D_MODEL), lambda i: (i, 0)),
        out_shape=jax.ShapeDtypeStruct((rows, D_MODEL), F32),
        compiler_params=pltpu.CompilerParams(
            dimension_semantics=("arbitrary",), vmem_limit_bytes=VMEM_LIMIT_BYTES),
        name="ffn",
    )(h2d, p["norm_ffn_w"], p["w_up"], p["w_down"], p["norm_final_w"])


def _sample_proj_kernel(x_ref, nw_ref, wa_ref, wdt_ref, wb_ref, o_ref):
    xn = _rms(x_ref[...], nw_ref[...]).astype(BF16)
    o_ref[:, :OFF_DT] = _dot(xn, wa_ref[...])
    o_ref[:, OFF_DT:OFF_Q] = _dot(xn, wdt_ref[...])
    o_ref[:, OFF_Q:] = _dot(xn, wb_ref[...])


def _sample_proj(x2d, p):
    rows = x2d.shape[0]
    args = [x2d, p["norm_mix_w"], p["w_in_a"], p["w_in_dt"], p["w_in_b"]]
    return pl.pallas_call(
        _sample_proj_kernel,
        grid=(1,),
        in_specs=[_full(a.shape) for a in args],
        out_specs=_full((rows, IN_PROJ_PACKED)),
        out_shape=jax.ShapeDtypeStruct((rows, IN_PROJ_PACKED), F32),
        compiler_params=pltpu.CompilerParams(
            dimension_semantics=("arbitrary",), vmem_limit_bytes=VMEM_LIMIT_BYTES),
        name="sample_proj",
    )(*args)


def _sample_state_kernel(proj_ref, stc_ref, ssm_ref, k_ref, v_ref, lnw_ref, lnb_ref, ws_ref,
                         bs_ref, cw_ref, cb_ref, dtb_ref, alog_ref, dsk_ref, snw_ref,
                         gm_ref, yn_ref, xa_ref, convn_ref, ssmn_ref, gv_ref):
    nb = proj_ref.shape[0]
    proj = proj_ref[...]

    uv = _gelu(proj[:, OFF_UV:OFF_Z])
    v = _layernorm(uv[:, GM_WIDTH:], lnw_ref[...], lnb_ref[...])
    gv_ref[...] = v
    for g in range(GM_GROUPS):
        cols = slice(g * GM_GROUP_DIM, (g + 1) * GM_GROUP_DIM)
        mixed = v[:, cols] * ws_ref[g, 0:1, 0:1] + bs_ref[g:g + 1, 0:1]
        gm_ref[:, cols] = uv[:, cols] * mixed

    st = stc_ref[...]
    xnew = proj[:, OFF_XBC:OFF_DT]
    acc = st[:, 0:SSD_CONV_DIM] * cw_ref[0:1, :]
    for k in range(1, SSD_CONV - 1):
        acc = acc + st[:, k * SSD_CONV_DIM:(k + 1) * SSD_CONV_DIM] * cw_ref[k:k + 1, :]
    acc = acc + xnew * cw_ref[SSD_CONV - 1:SSD_CONV, :]
    convn_ref[:, 0:(SSD_CONV - 2) * SSD_CONV_DIM] = st[:, SSD_CONV_DIM:]
    convn_ref[:, (SSD_CONV - 2) * SSD_CONV_DIM:] = xnew
    xbc = _silu(acc + cb_ref[...])
    xs = xbc[:, :SSD_INNER]
    bm = xbc[:, SSD_INNER:SSD_INNER + SSD_GROUPS * SSD_STATE].astype(BF16)
    cm = xbc[:, SSD_INNER + SSD_GROUPS * SSD_STATE:].astype(BF16)

    dt = _softplus(proj[:, OFF_DT:OFF_Q] + dtb_ref[...])
    decay = jnp.exp(dt * -jnp.exp(alog_ref[...]))
    hh = lax.broadcasted_iota(jnp.int32, (DT_PAD, SSD_INNER), 0)
    cc = lax.broadcasted_iota(jnp.int32, (DT_PAD, SSD_INNER), 1)
    spread = (jnp.right_shift(cc, SSD_HEAD_DIM.bit_length() - 1) == hh).astype(BF16)
    dt_wide = sum(_dot(piece, spread) for piece in _split3(dt))
    xdt = xs * dt_wide

    rid = lax.broadcasted_iota(jnp.int32, (nb, 1), 0)
    scale = XA_HEAD_DIM ** -0.5
    kv_vregs = MEM_LEN * XA_HEADS // SUBLANES
    lane_ones = jnp.ones((XA_HEAD_DIM, XA_HEAD_DIM), BF16)
    qv = proj[:, OFF_Q:OFF_GATE]
    y = jnp.zeros((nb, SSD_INNER), F32)
    for bi in range(nb):
        mine = rid == bi
        x_row = jnp.where(mine, xdt, 0.0).astype(BF16)
        parts = []
        for g in range(SSD_GROUPS):
            cols = slice(g * SSD_GROUP_WIDTH, (g + 1) * SSD_GROUP_WIDTH)
            ncols = slice(g * SSD_STATE, (g + 1) * SSD_STATE)
            upd = _dot_tn(x_row[:, cols], bm[:, ncols])
            heads = range(g * SSD_HEADS // SSD_GROUPS, (g + 1) * SSD_HEADS // SSD_GROUPS)
            e_rows = jnp.concatenate(
                [jnp.broadcast_to(decay[bi:bi + 1, h:h + 1], (SSD_HEAD_DIM, SSD_STATE)) for h in heads],
                axis=0)
            h_new = ssm_ref[bi, cols, :] * e_rows + upd
            ssmn_ref[bi, cols, :] = h_new
            parts.append(_dot_nt(cm[:, ncols], h_new.astype(BF16)))
        y = y + jnp.where(mine, jnp.concatenate(parts, axis=-1), 0.0)

        q_rows = jnp.concatenate(
            [qv[bi:bi + 1, hd * XA_HEAD_DIM:(hd + 1) * XA_HEAD_DIM] for hd in range(XA_HEADS)]
            * (SUBLANES // XA_HEADS), axis=0)
        kq = (k_ref[bi].reshape(kv_vregs, SUBLANES, XA_HEAD_DIM) * q_rows[None]).astype(BF16)
        s = _dot(kq.reshape(kv_vregs * SUBLANES, XA_HEAD_DIM), lane_ones) * scale
        s = s.reshape(kv_vregs, SUBLANES, XA_HEAD_DIM)
        pexp = jnp.exp(s - _fold_heads(_reduce_leading(s, jnp.maximum), jnp.maximum)[None])
        o = _reduce_leading(pexp * v_ref[bi].reshape(kv_vregs, SUBLANES, XA_HEAD_DIM), jnp.add)
        o = _fold_heads(o, jnp.add) / _fold_heads(_reduce_leading(pexp, jnp.add), jnp.add)
        for hd in range(XA_HEADS):
            xa_ref[bi:bi + 1, hd * XA_HEAD_DIM:(hd + 1) * XA_HEAD_DIM] = o[hd:hd + 1, :]

    zs = _silu(proj[:, OFF_Z:OFF_XBC])
    yn_ref[...] = _group_rms((y + xs * dsk_ref[...]) * zs, snw_ref[...])


def _sample_state(proj, state_conv2d, state_ssm, cache_k, cache_v, p):
    n = proj.shape[0]
    nb = SAMPLE_BLOCK
    assert n % nb == 0
    conv_w = (SSD_CONV - 1) * SSD_CONV_DIM
    small = [p["gm_ln_w"], p["gm_ln_b"], p["gm_ws"], p["gm_bs"], p["conv_w"], p["conv_b"],
             p["dt_bias"], p["a_log"], p["d_skip"], p["ssd_norm_w"]]
    rows2 = lambda w: pl.BlockSpec((nb, w), lambda i: (i, 0))
    rows3 = lambda a, c: pl.BlockSpec((nb, a, c), lambda i: (i, 0, 0))
    cache = rows3(MEM_LEN * XA_HEADS, XA_HEAD_DIM)
    return pl.pallas_call(
        _sample_state_kernel,
        grid=(n // nb,),
        in_specs=[rows2(IN_PROJ_PACKED), rows2(conv_w), rows3(SSD_INNER, SSD_STATE), cache, cache]
                 + [_full(a.shape) for a in small],
        out_specs=[rows2(GM_WIDTH), rows2(SSD_INNER), rows2(XA_WIDTH), rows2(conv_w),
                   rows3(SSD_INNER, SSD_STATE), rows2(GM_WIDTH)],
        out_shape=[
            jax.ShapeDtypeStruct((n, GM_WIDTH), F32),
            jax.ShapeDtypeStruct((n, SSD_INNER), F32),
            jax.ShapeDtypeStruct((n, XA_WIDTH), F32),
            jax.ShapeDtypeStruct((n, conv_w), F32),
            jax.ShapeDtypeStruct((n, SSD_INNER, SSD_STATE), F32),
            jax.ShapeDtypeStruct((n, GM_WIDTH), F32),
        ],
        compiler_params=pltpu.CompilerParams(
            dimension_semantics=("arbitrary",), vmem_limit_bytes=VMEM_LIMIT_BYTES),
        name="sample_state",
    )(proj, state_conv2d, state_ssm, cache_k, cache_v, *small)


def _sample_merge_kernel(x_ref, proj_ref, gm_ref, yn_ref, xa_ref, wgm_ref, wssd_ref, wxa_ref,
                         wout_ref, h_ref):
    merged = None
    for i, (br_ref, w_ref) in enumerate(((gm_ref, wgm_ref), (yn_ref, wssd_ref), (xa_ref, wxa_ref))):
        gate = jax.nn.sigmoid(proj_ref[:, OFF_GATE + i * D_MODEL:OFF_GATE + (i + 1) * D_MODEL])
        term = gate * _dot(br_ref[...].astype(BF16), w_ref[...])
        merged = term if merged is None else merged + term
    h_ref[...] = x_ref[...] + _dot(merged.astype(BF16), wout_ref[...])


def _sample_merge(x2d, proj, gm, yn, xa, p):
    args = [x2d, proj, gm, yn, xa, p["w_br_gm"], p["w_br_ssd"], p["w_br_xa"], p["w_out"]]
    return pl.pallas_call(
        _sample_merge_kernel,
        grid=(1,),
        in_specs=[_full(a.shape) for a in args],
        out_specs=_full(x2d.shape),
        out_shape=jax.ShapeDtypeStruct(x2d.shape, F32),
        compiler_params=pltpu.CompilerParams(
            dimension_semantics=("arbitrary",), vmem_limit_bytes=VMEM_LIMIT_BYTES),
        name="sample_merge",
    )(*args)


def _pack_w_in_kernel(wt_ref, wa_ref, wdt_ref, wb_ref):
    rest = wb_ref.shape[1]
    for j in range(0, OFF_DT, PACK_COLS):
        wa_ref[:, j:j + PACK_COLS] = wt_ref[0, j:j + PACK_COLS, :].T.astype(BF16)
    dt_rows = wt_ref[0, OFF_DT:OFF_DT + DT_PAD, :]
    is_dt = lax.broadcasted_iota(jnp.int32, dt_rows.shape, 0) < SSD_HEADS
    wdt_ref[...] = jnp.where(is_dt, dt_rows, 0.0).T.astype(BF16)
    base = OFF_DT + SSD_HEADS
    for j in range(0, rest, PACK_COLS):
        wb_ref[:, j:j + PACK_COLS] = wt_ref[0, base + j:base + j + PACK_COLS, :].T.astype(BF16)


def _pack_w_in(w_in, l):
    d, width = w_in.shape[1:]
    rest = width - OFF_DT - SSD_HEADS
    assert rest == IN_PROJ_PACKED - OFF_Q and OFF_DT % PACK_COLS == 0 and rest % PACK_COLS == 0
    w_t = jnp.swapaxes(w_in, 1, 2)
    return pl.pallas_call(
        _pack_w_in_kernel,
        grid=(1,),
        in_specs=[pl.BlockSpec((1, width, d), lambda i: (l, 0, 0))],
        out_specs=[_full((d, OFF_DT)), _full((d, DT_PAD)), _full((d, rest))],
        out_shape=[jax.ShapeDtypeStruct((d, OFF_DT), BF16),
                   jax.ShapeDtypeStruct((d, DT_PAD), BF16),
                   jax.ShapeDtypeStruct((d, rest), BF16)],
        compiler_params=pltpu.CompilerParams(
            dimension_semantics=("arbitrary",), vmem_limit_bytes=VMEM_LIMIT_BYTES),
        name="pack_w_in",
    )(w_t)


def _prep_layer(l, norm_mix_w, w_in, gm_ln_w, gm_ln_b, gm_ws, gm_bs, conv_w, conv_b, dt_bias, a_log,
                d_skip, ssd_norm_w, mem_norm_w, w_mem_k, w_mem_v, w_br_gm, w_br_ssd, w_br_xa, w_out,
                norm_ffn_w, w_up, w_down, norm_final_w):
    row = lambda a: a.reshape(1, -1)
    pad_heads = lambda a: jnp.pad(a, (0, DT_PAD - SSD_HEADS)).reshape(1, DT_PAD)
    w_in_a, w_in_dt, w_in_b = _pack_w_in(w_in, l)
    return {
        "norm_mix_w": row(norm_mix_w[l]),
        "w_in_a": w_in_a, "w_in_dt": w_in_dt, "w_in_b": w_in_b,
        "gm_ln_w": row(gm_ln_w[l]), "gm_ln_b": row(gm_ln_b[l]),
        "gm_ws": gm_ws[l], "gm_bs": gm_bs[l], "gm_bs_t": gm_bs[l].T,
        "conv_w": conv_w[l], "conv_b": row(conv_b[l]),
        "dt_bias": pad_heads(dt_bias[l]), "a_log": pad_heads(a_log[l]),
        "d_skip": row(jnp.repeat(d_skip[l], SSD_HEAD_DIM)), "ssd_norm_w": row(ssd_norm_w[l]),
        "mem_norm_w": row(mem_norm_w[l]),
        "w_kv": jnp.concatenate([w_mem_k[l], w_mem_v[l]], axis=1).astype(BF16),
        "w_br_gm": w_br_gm[l].astype(BF16), "w_br_ssd": w_br_ssd[l].astype(BF16),
        "w_br_xa": w_br_xa[l].astype(BF16), "w_out": w_out[l].astype(BF16),
        "norm_ffn_w": row(norm_ffn_w[l]), "w_up": w_up[l].astype(BF16),
        "w_down": w_down[l].astype(BF16), "norm_final_w": row(norm_final_w),
    }


def kernel(x_prompt, x_sample, mem_prompt, cache_mem_k, cache_mem_v, state_conv, state_ssm, norm_mix_w, w_in, gm_ln_w, gm_ln_b, gm_ws, gm_bs, conv_w, conv_b, dt_bias, a_log, d_skip, ssd_norm_w, mem_norm_w, w_mem_k, w_mem_v, w_br_gm, w_br_ssd, w_br_xa, w_out, norm_ffn_w, w_up, w_down, norm_final_w):
    depth = w_in.shape[0]
    assert depth == 1, "the final norm is fused into the MLP kernel of the last (only) layer"
    b, seq, _ = x_prompt.shape
    n, dec_seq, _ = x_sample.shape
    assert dec_seq == 1
    p = _prep_layer(0, norm_mix_w, w_in, gm_ln_w, gm_ln_b, gm_ws, gm_bs, conv_w, conv_b, dt_bias,
                    a_log, d_skip, ssd_norm_w, mem_norm_w, w_mem_k, w_mem_v, w_br_gm, w_br_ssd,
                    w_br_xa, w_out, norm_ffn_w, w_up, w_down, norm_final_w)

    mem_k, mem_v = _memkv(mem_prompt.reshape(b * MEM_LEN, D_MODEL), p["mem_norm_w"], p["w_kv"])
    h1, conv_p, ssm_p = _mixer(x_prompt, mem_k.reshape(b, MEM_LEN, XA_WIDTH),
                               mem_v.reshape(b, MEM_LEN, XA_WIDTH), p)
    y_prompt = _ffn(h1.reshape(b * seq, D_MODEL), p).reshape(b, seq, D_MODEL)

    xs2d = x_sample.reshape(n, D_MODEL)
    proj = _sample_proj(xs2d, p)
    gm, yn, xa, conv_s, ssm_s, gv = _sample_state(
        proj, state_conv[0].reshape(n, (SSD_CONV - 1) * SSD_CONV_DIM), state_ssm[0].reshape(n, SSD_INNER, SSD_STATE),
        cache_mem_k[0].reshape(n, MEM_LEN * XA_HEADS, XA_HEAD_DIM),
        cache_mem_v[0].reshape(n, MEM_LEN * XA_HEADS, XA_HEAD_DIM), p)
    hs = _sample_merge(xs2d, proj, gm, yn, xa, p)
    y_sample = _ffn(hs, p).reshape(n, 1, D_MODEL)

    kv_shape = (1, b, MEM_LEN, XA_HEADS, XA_HEAD_DIM)
    state_shape = (SSD_HEADS, SSD_HEAD_DIM, SSD_STATE)
    return (y_prompt, y_sample,
            mem_k.reshape(kv_shape), mem_v.reshape(kv_shape),
            conv_p.reshape(1, b, SSD_CONV - 1, SSD_CONV_DIM), ssm_p.reshape((1, b) + state_shape),
            conv_s.reshape(1, n, SSD_CONV - 1, SSD_CONV_DIM), ssm_s.reshape((1, n) + state_shape),
            gv.reshape(1, n, 1, GM_WIDTH))
```

```python
import functools

import jax
import jax.numpy as jnp
from jax import lax
from jax.experimental import pallas as pl
from jax.experimental.pallas import tpu as pltpu

F32 = jnp.float32
BF16 = jnp.bfloat16

D_MODEL = 1024
GM_CHUNK = 128
GM_GROUPS = 4
GM_GROUP_DIM = 128
GM_WIDTH = GM_GROUPS * GM_GROUP_DIM
SSD_HEADS = 16
SSD_HEAD_DIM = 64
SSD_INNER = SSD_HEADS * SSD_HEAD_DIM
SSD_GROUPS = 2
SSD_GROUP_WIDTH = SSD_INNER // SSD_GROUPS
SSD_STATE = 128
SSD_CONV = 4
SSD_CHUNK = 128
SSD_CONV_DIM = SSD_INNER + 2 * SSD_GROUPS * SSD_STATE
MEM_LEN = 256
XA_HEADS = 4
XA_HEAD_DIM = 128
XA_WIDTH = XA_HEADS * XA_HEAD_DIM
N_BRANCH = 3
D_FF = 4 * D_MODEL
EPS = 1e-6

SUBLANES = 8
LANES = 128
HEADS_PER_LANE_TILE = LANES // SSD_HEAD_DIM
HEAD_PAIRS = SSD_HEADS // HEADS_PER_LANE_TILE
PAIRS_PER_GROUP = HEAD_PAIRS // SSD_GROUPS

DT_PAD = LANES
OFF_UV = 0
OFF_Z = OFF_UV + 2 * GM_WIDTH
OFF_XBC = OFF_Z + SSD_INNER
OFF_DT = OFF_XBC + SSD_CONV_DIM
OFF_Q = OFF_DT + DT_PAD
OFF_GATE = OFF_Q + XA_WIDTH
IN_PROJ_PACKED = OFF_GATE + N_BRANCH * D_MODEL

RAW_UV = 0
RAW_Z = RAW_UV + 2 * GM_WIDTH
RAW_DT = RAW_Z + SSD_INNER
RAW_GATE = RAW_DT + DT_PAD
RAW_WIDTH = RAW_GATE + N_BRANCH * D_MODEL

MIXER_ROWS = 256
PIECE_COLS = 256
MERGE_COLS = 512
FFN_ROWS = 1024
FFN_COLS = 1024
MEMKV_ROWS = 512
PACK_COLS = 512
SAMPLE_BLOCK = SUBLANES
VMEM_LIMIT_BYTES = 56 * 1024 * 1024


def _dot(a, b):
    return jnp.dot(a, b, preferred_element_type=F32)


def _dot_nt(a, b):
    return lax.dot_general(a, b, (((1,), (1,)), ((), ())), preferred_element_type=F32)


def _dot_tn(a, b):
    return lax.dot_general(a, b, (((0,), (0,)), ((), ())), preferred_element_type=F32)


def _rms(x, w):
    return x * lax.rsqrt(jnp.mean(x * x, axis=-1, keepdims=True) + EPS) * w


def _layernorm(x, w, b):
    xc = x - jnp.mean(x, axis=-1, keepdims=True)
    return xc * lax.rsqrt(jnp.mean(xc * xc, axis=-1, keepdims=True) + EPS) * w + b


def _gelu(x):
    return 0.5 * x * (1.0 + lax.erf(x * 0.7071067811865476))


def _silu(x):
    return x * jax.nn.sigmoid(x)


def _softplus(x):
    return jnp.maximum(x, 0.0) + jnp.log1p(jnp.exp(-jnp.abs(x)))


def _split3(x):
    p1 = x.astype(BF16)
    r1 = x - p1.astype(F32)
    p2 = r1.astype(BF16)
    p3 = (r1 - p2.astype(F32)).astype(BF16)
    return p1, p2, p3


REDUCE_CHAINS = 8


def _reduce_leading(x, op):
    n = x.shape[0]
    assert n % REDUCE_CHAINS == 0
    acc = [x[w] for w in range(REDUCE_CHAINS)]
    for i in range(REDUCE_CHAINS, n, REDUCE_CHAINS):
        acc = [op(a, x[i + w]) for w, a in enumerate(acc)]
    while len(acc) > 1:
        acc = [op(acc[2 * i], acc[2 * i + 1]) for i in range(len(acc) // 2)]
    return acc[0]


def _fold_heads(a, op):
    assert SUBLANES == 2 * XA_HEADS
    return op(a, pltpu.roll(a, XA_HEADS, axis=0))


def _group_rms(y, w):
    parts = []
    for g in range(SSD_GROUPS):
        cols = slice(g * SSD_GROUP_WIDTH, (g + 1) * SSD_GROUP_WIDTH)
        parts.append(_rms(y[:, cols], w[:, cols]))
    return jnp.concatenate(parts, axis=-1)


def _memkv_kernel(mem_ref, nw_ref, wkv_ref, k_ref, v_ref):
    mn = _rms(mem_ref[...], nw_ref[...]).astype(BF16)
    kv = _dot(mn, wkv_ref[...])
    k_ref[...] = kv[:, :XA_WIDTH]
    v_ref[...] = kv[:, XA_WIDTH:]


def _memkv(mem2d, norm_w, wkv):
    rows = mem2d.shape[0]
    tm = min(MEMKV_ROWS, rows)
    return pl.pallas_call(
        _memkv_kernel,
        grid=(rows // tm,),
        in_specs=[
            pl.BlockSpec((tm, D_MODEL), lambda i: (i, 0)),
            pl.BlockSpec((1, D_MODEL), lambda i: (0, 0)),
            pl.BlockSpec((D_MODEL, 2 * XA_WIDTH), lambda i: (0, 0)),
        ],
        out_specs=[
            pl.BlockSpec((tm, XA_WIDTH), lambda i: (i, 0)),
            pl.BlockSpec((tm, XA_WIDTH), lambda i: (i, 0)),
        ],
        out_shape=[jax.ShapeDtypeStruct((rows, XA_WIDTH), F32)] * 2,
        compiler_params=pltpu.CompilerParams(
            dimension_semantics=("arbitrary",), vmem_limit_bytes=VMEM_LIMIT_BYTES),
        name="memkv",
    )(mem2d, norm_w, wkv)


def _ssd_chunk_steps(rows, xs_s, bc_s, dt_s, neg_a, hT_s, y_s, scratch):
    acum_s, acumt_s, dtt_s, wc_s, cb_s, lhs_s, cd_s, xw_s, sg_s = scratch
    q = SSD_CHUNK

    def masks():
        ii = lax.broadcasted_iota(jnp.int32, (q, q), 0)
        jj = lax.broadcasted_iota(jnp.int32, (q, q), 1)
        low_half = lax.broadcasted_iota(jnp.int32, (q, LANES), 1) < SSD_HEAD_DIM
        low_half_row = lax.broadcasted_iota(jnp.int32, (1, LANES), 1) < SSD_HEAD_DIM
        return ii >= jj, low_half, low_half_row

    def prologue():
        causal, _, _ = masks()
        dt = dt_s[rows, :]
        d1, d2, d3 = _split3(dt * neg_a)
        tri = causal.astype(BF16)
        acum = _dot(tri, d1) + _dot(tri, d2) + _dot(tri, d3)
        acum_s[...] = acum
        acumt_s[...] = acum.T
        dtt_s[...] = dt.T
        wc_s[...] = dt * jnp.exp(acum[q - 1:q, :] - acum)

    def group_step(g):
        _, low_half, _ = masks()
        b_g = bc_s[rows, g * SSD_STATE:(g + 1) * SSD_STATE]
        c_g = bc_s[rows, (SSD_GROUPS + g) * SSD_STATE:(SSD_GROUPS + g + 1) * SSD_STATE]
        cb_s[g] = _dot_nt(c_g.astype(BF16), b_g.astype(BF16))
        for pr in range(PAIRS_PER_GROUP):
            pair = g * PAIRS_PER_GROUP + pr
            lanes = slice(pair * LANES, (pair + 1) * LANES)
            h0 = pair * HEADS_PER_LANE_TILE
            w_pair = jnp.where(low_half, jnp.broadcast_to(wc_s[:, h0:h0 + 1], (q, LANES)),
                               jnp.broadcast_to(wc_s[:, h0 + 1:h0 + 2], (q, LANES)))
            xw_s[:, lanes] = (xs_s[rows, lanes] * w_pair).astype(BF16)
        cols = slice(g * SSD_GROUP_WIDTH, (g + 1) * SSD_GROUP_WIDTH)
        sg_s[:, cols] = _dot(b_g.T.astype(BF16), xw_s[:, cols])

    def head_step(h):
        causal, _, _ = masks()
        g = h // (SSD_HEADS // SSD_GROUPS)
        c_g = bc_s[rows, (SSD_GROUPS + g) * SSD_STATE:(SSD_GROUPS + g + 1) * SSD_STATE]
        col = jnp.broadcast_to(acum_s[:, h:h + 1], (q, q))
        seg = jnp.where(causal, col - acumt_s[h:h + 1, :], -jnp.inf)
        lhs_s[h, :, 0:q] = (cb_s[g] * jnp.exp(seg) * dtt_s[h:h + 1, :]).astype(BF16)
        lhs_s[h, :, q:2 * q] = (c_g * jnp.exp(col)).astype(BF16)
        cd_s[h:h + 1, :] = jnp.exp(col[q - 1:q, :])

    def pair_step(pair):
        _, low_half, low_half_row = masks()
        lanes = slice(pair * LANES, (pair + 1) * LANES)
        h0 = pair * HEADS_PER_LANE_TILE
        h_prev = hT_s[:, lanes]
        rhs = jnp.concatenate([xs_s[rows, lanes].astype(BF16), h_prev.astype(BF16)], axis=0)
        y_s[rows, lanes] = jnp.where(low_half, _dot(lhs_s[h0], rhs), _dot(lhs_s[h0 + 1], rhs))
        decay = jnp.where(low_half_row, cd_s[h0:h0 + 1, :], cd_s[h0 + 1:h0 + 2, :])
        hT_s[:, lanes] = h_prev * decay + sg_s[:, lanes]

    steps = [(2, prologue)]
    steps += [(3, functools.partial(group_step, g)) for g in range(SSD_GROUPS)]
    steps += [(2, functools.partial(head_step, h)) for h in range(SSD_HEADS)]
    steps += [(1, functools.partial(pair_step, pair)) for pair in range(HEAD_PAIRS)]
    return steps


def _interleave(steps, pieces):
    total = sum(w for w, _ in steps)
    done = acc = 0
    for w, step in steps:
        acc += w
        upto = len(pieces) * acc // total
        for piece in pieces[done:upto]:
            piece()
        done = upto
        step()
    assert done == len(pieces)


def _in_proj_cols(w_refs, lo, hi):
    wa_ref, wdt_ref, wb_ref = w_refs
    if hi <= OFF_DT:
        return wa_ref[:, lo:hi]
    if lo >= OFF_Q:
        return wb_ref[:, lo - OFF_Q:hi - OFF_Q]
    assert (lo, hi) == (OFF_DT, OFF_Q)
    return wdt_ref[...]


def _mixer_kernel(x1_ref, x2_ref, k_ref, v_ref, nw_ref, wa_ref, wdt_ref, wb_ref, lnw_ref, lnb_ref, ws_ref,
                  bst_ref, cw_ref, cb_ref, dtb_ref, alog_ref, dsk_ref, snw_ref,
                  wgm_ref, wssd_ref, wxa_ref, wout_ref,
                  h_ref, conv_ref, ssm_ref,
                  xn_s, raw_s, q_s, ext_s, hT_s, xs_s, bc_s, dt_s, y_s, u_s, v_s, gm_s, yn_s, xa_s, merged_s,
                  zs_s, gate_s, qb_s, *ssd_scratch, tiles_per_seq):
    tm = x1_ref.shape[1]
    s = pl.program_id(0)
    t2 = lax.rem(jnp.maximum(s - 1, 0), tiles_per_seq)
    w_in = functools.partial(_in_proj_cols, (wa_ref, wdt_ref, wb_ref))
    head = SUBLANES
    pc = PIECE_COLS

    @pl.when(s == 0)
    def _():
        blk = 2 * SUBLANES

        def zero_rows(i, carry):
            rows = pl.ds(pl.multiple_of(i * blk, blk), blk)
            for buf in (raw_s, ext_s, gate_s, zs_s, qb_s):
                buf[rows, :] = jnp.zeros((blk, buf.shape[1]), buf.dtype)
            return carry

        lax.fori_loop(0, tm // blk, zero_rows, 0)
        ext_s[tm:tm + head, :] = jnp.zeros((head, SSD_CONV_DIM), F32)

    @pl.when(t2 == 0)
    def _():
        ext_s[0:head, :] = jnp.zeros((head, SSD_CONV_DIM), F32)
        hT_s[...] = jnp.zeros(hT_s.shape, F32)

    xn_s[...] = _rms(x1_ref[0], nw_ref[...]).astype(BF16)

    def project(lo, width):
        return _dot(xn_s[...], w_in(lo, lo + width))

    def raw_piece(dst, src, width):
        def run():
            raw_s[:, dst:dst + width] = project(src, width)
        return run

    def q_piece(j):
        def run():
            q_s[:, j:j + pc] = project(OFF_Q + j, pc).astype(BF16)
        return run

    def xbc_piece(j):
        def run():
            ext_s[head:head + tm, j:j + pc] = project(OFF_XBC + j, pc)
        return run

    for j in range(0, SSD_CONV_DIM, pc):
        cols = slice(j, j + pc)
        acc = ext_s[head - 3:head - 3 + tm, cols] * cw_ref[0:1, cols]
        for k in range(1, SSD_CONV):
            acc = acc + ext_s[head - 3 + k:head - 3 + k + tm, cols] * cw_ref[k:k + 1, cols]
        xbc = _silu(acc + cb_ref[:, cols])
        if j < SSD_INNER:
            xs_s[:, cols] = xbc
        else:
            bc_s[:, j - SSD_INNER:j - SSD_INNER + pc] = xbc
        conv_ref[0, :, cols] = ext_s[tm + head - 3:tm + head, cols]
        ext_s[0:head, cols] = ext_s[tm:tm + head, cols]
    dt_s[...] = _softplus(raw_s[:, RAW_DT:RAW_GATE] + dtb_ref[...])
    for j in range(0, GM_WIDTH, pc):
        u_s[:, j:j + pc] = _gelu(raw_s[:, RAW_UV + j:RAW_UV + j + pc])
    v = _gelu(raw_s[:, RAW_UV + GM_WIDTH:RAW_Z])
    v_s[...] = _layernorm(v, lnw_ref[...], lnb_ref[...]).astype(BF16)

    pieces = ([raw_piece(RAW_GATE + j, OFF_GATE + j, pc) for j in range(0, N_BRANCH * D_MODEL, pc)]
              + [raw_piece(RAW_Z + j, OFF_Z + j, pc) for j in range(0, SSD_INNER, pc)]
              + [q_piece(j) for j in range(0, XA_WIDTH, pc)]
              + [xbc_piece(j) for j in range(0, SSD_CONV_DIM, pc)] + [raw_piece(RAW_DT, OFF_DT, DT_PAD)]
              + [raw_piece(RAW_UV + j, OFF_UV + j, pc) for j in range(0, 2 * GM_WIDTH, pc)])
    steps = []
    neg_a = -jnp.exp(alog_ref[...])
    for c in range(tm // SSD_CHUNK):
        rows = slice(c * SSD_CHUNK, (c + 1) * SSD_CHUNK)
        steps += _ssd_chunk_steps(rows, xs_s, bc_s, dt_s, neg_a, hT_s, y_s, ssd_scratch)

    def gmlp_group(g):
        ii = lax.broadcasted_iota(jnp.int32, (GM_CHUNK, GM_CHUNK), 0)
        jj = lax.broadcasted_iota(jnp.int32, (GM_CHUNK, GM_CHUNK), 1)
        cols = slice(g * GM_GROUP_DIM, (g + 1) * GM_GROUP_DIM)
        w_tril = jnp.where(ii >= jj, ws_ref[g], 0.0).astype(BF16)
        bias = bst_ref[:, g:g + 1]
        for c in range(tm // GM_CHUNK):
            rows = slice(c * GM_CHUNK, (c + 1) * GM_CHUNK)
            mixed = _dot(w_tril, v_s[rows, cols]) + bias
            gm_s[rows, cols] = (u_s[rows, cols] * mixed).astype(BF16)

    steps += [(1, functools.partial(gmlp_group, g)) for g in range(GM_GROUPS)]

    def attn_head(hd):
        cols = slice(hd * XA_HEAD_DIM, (hd + 1) * XA_HEAD_DIM)
        sc = _dot_nt(qb_s[:, cols], k_ref[0, :, cols].astype(BF16)) * (XA_HEAD_DIM ** -0.5)
        p = jnp.exp(sc - jnp.max(sc, axis=-1, keepdims=True))
        o = _dot(p.astype(BF16), v_ref[0, :, cols].astype(BF16))
        xa_s[:, cols] = (o / jnp.sum(p, axis=-1, keepdims=True)).astype(BF16)

    steps += [(2, functools.partial(attn_head, hd)) for hd in range(XA_HEADS)]

    def ssd_out_group(g):
        cols = slice(g * SSD_GROUP_WIDTH, (g + 1) * SSD_GROUP_WIDTH)
        yv = (y_s[:, cols] + xs_s[:, cols] * dsk_ref[:, cols]) * zs_s[:, cols]
        yn_s[:, cols] = _rms(yv, snw_ref[:, cols]).astype(BF16)

    steps += [(2, functools.partial(ssd_out_group, g)) for g in range(SSD_GROUPS)]

    def merge_piece(i, br_s, w_ref, j):
        lo = i * D_MODEL + j
        term = gate_s[:, lo:lo + MERGE_COLS] * _dot(br_s[...], w_ref[:, j:j + MERGE_COLS])
        if i == 0:
            merged_s[:, j:j + MERGE_COLS] = term
        else:
            merged_s[:, j:j + MERGE_COLS] += term

    for i, (br_s, w_ref) in enumerate(((gm_s, wgm_ref), (yn_s, wssd_ref), (xa_s, wxa_ref))):
        steps += [(2, functools.partial(merge_piece, i, br_s, w_ref, j)) for j in range(0, D_MODEL, MERGE_COLS)]

    def out_proj():
        h_ref[0] = x2_ref[0] + _dot(merged_s[...].astype(BF16), wout_ref[...])

    steps.append((1, out_proj))
    _interleave(steps, pieces)

    for j in range(0, N_BRANCH * D_MODEL, pc):
        gate_s[:, j:j + pc] = jax.nn.sigmoid(raw_s[:, RAW_GATE + j:RAW_GATE + j + pc])
    for j in range(0, SSD_INNER, pc):
        zs_s[:, j:j + pc] = _silu(raw_s[:, RAW_Z + j:RAW_Z + j + pc])
    qb_s[...] = q_s[...]

    @pl.when(t2 == tiles_per_seq - 1)
    def _():
        for pair in range(HEAD_PAIRS):
            lanes = slice(pair * LANES, (pair + 1) * LANES)
            ssm_ref[0, lanes, :] = hT_s[:, lanes].T


def _full(shape):
    return pl.BlockSpec(shape, lambda *_: (0,) * len(shape))


def _mixer(x, mem_k, mem_v, p):
    b, seq, _ = x.shape
    tm = MIXER_ROWS
    assert seq % tm == 0 and tm % SSD_CHUNK == 0
    small = [p["norm_mix_w"], p["w_in_a"], p["w_in_dt"], p["w_in_b"],
             p["gm_ln_w"], p["gm_ln_b"], p["gm_ws"], p["gm_bs_t"],
             p["conv_w"], p["conv_b"], p["dt_bias"], p["a_log"], p["d_skip"], p["ssd_norm_w"],
             p["w_br_gm"], p["w_br_ssd"], p["w_br_xa"], p["w_out"]]
    tps = seq // tm
    n_tiles = b * tps
    tile1 = lambda s: jnp.minimum(s, n_tiles - 1)
    tile2 = lambda s: jnp.maximum(s - 1, 0)
    x_spec = lambda tile: pl.BlockSpec((1, tm, D_MODEL), lambda s: (tile(s) // tps, tile(s) % tps, 0))
    seq_spec = lambda r, c: pl.BlockSpec((1, r, c), lambda s: (tile2(s) // tps, 0, 0))
    return pl.pallas_call(
        functools.partial(_mixer_kernel, tiles_per_seq=tps),
        grid=(n_tiles + 1,),
        in_specs=[x_spec(tile1), x_spec(tile2), seq_spec(MEM_LEN, XA_WIDTH), seq_spec(MEM_LEN, XA_WIDTH)]
                 + [_full(a.shape) for a in small],
        out_specs=[x_spec(tile2), seq_spec(SSD_CONV - 1, SSD_CONV_DIM), seq_spec(SSD_INNER, SSD_STATE)],
        out_shape=[
            jax.ShapeDtypeStruct((b, seq, D_MODEL), F32),
            jax.ShapeDtypeStruct((b, SSD_CONV - 1, SSD_CONV_DIM), F32),
            jax.ShapeDtypeStruct((b, SSD_INNER, SSD_STATE), F32),
        ],
        scratch_shapes=[
            pltpu.VMEM((tm, D_MODEL), BF16),
            pltpu.VMEM((tm, RAW_WIDTH), F32),
            pltpu.VMEM((tm, XA_WIDTH), BF16),
            pltpu.VMEM((tm + SUBLANES, SSD_CONV_DIM), F32),
            pltpu.VMEM((SSD_STATE, SSD_INNER), F32),
            pltpu.VMEM((tm, SSD_INNER), F32),
            pltpu.VMEM((tm, 2 * SSD_GROUPS * SSD_STATE), F32),
            pltpu.VMEM((tm, DT_PAD), F32),
            pltpu.VMEM((tm, SSD_INNER), F32),
            pltpu.VMEM((tm, GM_WIDTH), F32),
            pltpu.VMEM((tm, GM_WIDTH), BF16),
            pltpu.VMEM((tm, GM_WIDTH), BF16),
            pltpu.VMEM((tm, SSD_INNER), BF16),
            pltpu.VMEM((tm, XA_WIDTH), BF16),
            pltpu.VMEM((tm, D_MODEL), F32),
            pltpu.VMEM((tm, SSD_INNER), F32),
            pltpu.VMEM((tm, N_BRANCH * D_MODEL), F32),
            pltpu.VMEM((tm, XA_WIDTH), BF16),
            pltpu.VMEM((SSD_CHUNK, DT_PAD), F32),
            pltpu.VMEM((DT_PAD, SSD_CHUNK), F32),
            pltpu.VMEM((DT_PAD, SSD_CHUNK), F32),
            pltpu.VMEM((SSD_CHUNK, DT_PAD), F32),
            pltpu.VMEM((SSD_GROUPS, SSD_CHUNK, SSD_CHUNK), F32),
            pltpu.VMEM((SSD_HEADS, SSD_CHUNK, 2 * SSD_CHUNK), BF16),
            pltpu.VMEM((SSD_HEADS, SSD_CHUNK), F32),
            pltpu.VMEM((SSD_CHUNK, SSD_INNER), BF16),
            pltpu.VMEM((SSD_STATE, SSD_INNER), F32),
        ],
        compiler_params=pltpu.CompilerParams(
            dimension_semantics=("arbitrary",), vmem_limit_bytes=VMEM_LIMIT_BYTES),
        name="prompt_mixer",
    )(x, x, mem_k, mem_v, *small)


def _ffn_kernel(h_ref, nw_ref, wup_ref, wdn_ref, fw_ref, o_ref):
    h = h_ref[...]
    hn = _rms(h, nw_ref[...]).astype(BF16)
    acc = h
    for j in range(D_FF // FFN_COLS):
        cols = slice(j * FFN_COLS, (j + 1) * FFN_COLS)
        a = jnp.square(jnp.maximum(_dot(hn, wup_ref[:, cols]), 0.0)).astype(BF16)
        acc = acc + _dot(a, wdn_ref[cols, :])
    o_ref[...] = _rms(acc, fw_ref[...])


def _ffn(h2d, p):
    rows = h2d.shape[0]
    tm = min(FFN_ROWS, rows)
    assert rows % tm == 0
    return pl.pallas_call(
        _ffn_kernel,
        grid=(rows // tm,),
        in_specs=[
            pl.BlockSpec((tm, D_MODEL), lambda i: (i, 0)),
            _full((1, D_MODEL)),
            _full((D_MODEL, D_FF)),
            _full((D_FF, D_MODEL)),
            _full((1, D_MODEL)),
        ],
        out_specs=pl.BlockSpec((tm, D_MODEL), lambda i: (i, 0)),
        out_shape=jax.ShapeDtypeStruct((rows, D_MODEL), F32),
        compiler_params=pltpu.CompilerParams(
            dimension_semantics=("arbitrary",), vmem_limit_bytes=VMEM_LIMIT_BYTES),
        name="ffn",
    )(h2d, p["norm_ffn_w"], p["w_up"], p["w_down"], p["norm_final_w"])


def _sample_proj_kernel(x_ref, nw_ref, wa_ref, wdt_ref, wb_ref, o_ref):
    xn = _rms(x_ref[...], nw_ref[...]).astype(BF16)
    o_ref[:, :OFF_DT] = _dot(xn, wa_ref[...])
    o_ref[:, OFF_DT:OFF_Q] = _dot(xn, wdt_ref[...])
    o_ref[:, OFF_Q:] = _dot(xn, wb_ref[...])


def _sample_proj(x2d, p):
    rows = x2d.shape[0]
    args = [x2d, p["norm_mix_w"], p["w_in_a"], p["w_in_dt"], p["w_in_b"]]
    return pl.pallas_call(
        _sample_proj_kernel,
        grid=(1,),
        in_specs=[_full(a.shape) for a in args],
        out_specs=_full((rows, IN_PROJ_PACKED)),
        out_shape=jax.ShapeDtypeStruct((rows, IN_PROJ_PACKED), F32),
        compiler_params=pltpu.CompilerParams(
            dimension_semantics=("arbitrary",), vmem_limit_bytes=VMEM_LIMIT_BYTES),
        name="sample_proj",
    )(*args)


def _sample_state_kernel(proj_ref, stc_ref, ssm_ref, k_ref, v_ref, lnw_ref, lnb_ref, ws_ref,
                         bs_ref, cw_ref, cb_ref, dtb_ref, alog_ref, dsk_ref, snw_ref,
                         gm_ref, yn_ref, xa_ref, convn_ref, ssmn_ref, gv_ref):
    nb = proj_ref.shape[0]
    proj = proj_ref[...]

    uv = _gelu(proj[:, OFF_UV:OFF_Z])
    v = _layernorm(uv[:, GM_WIDTH:], lnw_ref[...], lnb_ref[...])
    gv_ref[...] = v
    for g in range(GM_GROUPS):
        cols = slice(g * GM_GROUP_DIM, (g + 1) * GM_GROUP_DIM)
        mixed = v[:, cols] * ws_ref[g, 0:1, 0:1] + bs_ref[g:g + 1, 0:1]
        gm_ref[:, cols] = uv[:, cols] * mixed

    st = stc_ref[...]
    xnew = proj[:, OFF_XBC:OFF_DT]
    acc = st[:, 0:SSD_CONV_DIM] * cw_ref[0:1, :]
    for k in range(1, SSD_CONV - 1):
        acc = acc + st[:, k * SSD_CONV_DIM:(k + 1) * SSD_CONV_DIM] * cw_ref[k:k + 1, :]
    acc = acc + xnew * cw_ref[SSD_CONV - 1:SSD_CONV, :]
    convn_ref[:, 0:(SSD_CONV - 2) * SSD_CONV_DIM] = st[:, SSD_CONV_DIM:]
    convn_ref[:, (SSD_CONV - 2) * SSD_CONV_DIM:] = xnew
    xbc = _silu(acc + cb_ref[...])
    xs = xbc[:, :SSD_INNER]
    bm = xbc[:, SSD_INNER:SSD_INNER + SSD_GROUPS * SSD_STATE].astype(BF16)
    cm = xbc[:, SSD_INNER + SSD_GROUPS * SSD_STATE:].astype(BF16)

    dt = _softplus(proj[:, OFF_DT:OFF_Q] + dtb_ref[...])
    decay = jnp.exp(dt * -jnp.exp(alog_ref[...]))
    hh = lax.broadcasted_iota(jnp.int32, (DT_PAD, SSD_INNER), 0)
    cc = lax.broadcasted_iota(jnp.int32, (DT_PAD, SSD_INNER), 1)
    spread = (jnp.right_shift(cc, SSD_HEAD_DIM.bit_length() - 1) == hh).astype(BF16)
    dt_wide = sum(_dot(piece, spread) for piece in _split3(dt))
    xdt = xs * dt_wide

    rid = lax.broadcasted_iota(jnp.int32, (nb, 1), 0)
    scale = XA_HEAD_DIM ** -0.5
    kv_vregs = MEM_LEN * XA_HEADS // SUBLANES
    lane_ones = jnp.ones((XA_HEAD_DIM, XA_HEAD_DIM), BF16)
    qv = proj[:, OFF_Q:OFF_GATE]
    y = jnp.zeros((nb, SSD_INNER), F32)
    for bi in range(nb):
        mine = rid == bi
        x_row = jnp.where(mine, xdt, 0.0).astype(BF16)
        parts = []
        for g in range(SSD_GROUPS):
            cols = slice(g * SSD_GROUP_WIDTH, (g + 1) * SSD_GROUP_WIDTH)
            ncols = slice(g * SSD_STATE, (g + 1) * SSD_STATE)
            upd = _dot_tn(x_row[:, cols], bm[:, ncols])
            heads = range(g * SSD_HEADS // SSD_GROUPS, (g + 1) * SSD_HEADS // SSD_GROUPS)
            e_rows = jnp.concatenate(
                [jnp.broadcast_to(decay[bi:bi + 1, h:h + 1], (SSD_HEAD_DIM, SSD_STATE)) for h in heads],
                axis=0)
            h_new = ssm_ref[bi, cols, :] * e_rows + upd
            ssmn_ref[bi, cols, :] = h_new
            parts.append(_dot_nt(cm[:, ncols], h_new.astype(BF16)))
        y = y + jnp.where(mine, jnp.concatenate(parts, axis=-1), 0.0)

        q_rows = jnp.concatenate(
            [qv[bi:bi + 1, hd * XA_HEAD_DIM:(hd + 1) * XA_HEAD_DIM] for hd in range(XA_HEADS)]
            * (SUBLANES // XA_HEADS), axis=0)
        kq = (k_ref[bi].reshape(kv_vregs, SUBLANES, XA_HEAD_DIM) * q_rows[None]).astype(BF16)
        s = _dot(kq.reshape(kv_vregs * SUBLANES, XA_HEAD_DIM), lane_ones) * scale
        s = s.reshape(kv_vregs, SUBLANES, XA_HEAD_DIM)
        pexp = jnp.exp(s - _fold_heads(_reduce_leading(s, jnp.maximum), jnp.maximum)[None])
        o = _reduce_leading(pexp * v_ref[bi].reshape(kv_vregs, SUBLANES, XA_HEAD_DIM), jnp.add)
        o = _fold_heads(o, jnp.add) / _fold_heads(_reduce_leading(pexp, jnp.add), jnp.add)
        for hd in range(XA_HEADS):
            xa_ref[bi:bi + 1, hd * XA_HEAD_DIM:(hd + 1) * XA_HEAD_DIM] = o[hd:hd + 1, :]

    zs = _silu(proj[:, OFF_Z:OFF_XBC])
    yn_ref[...] = _group_rms((y + xs * dsk_ref[...]) * zs, snw_ref[...])


def _sample_state(proj, state_conv2d, state_ssm, cache_k, cache_v, p):
    n = proj.shape[0]
    nb = SAMPLE_BLOCK
    assert n % nb == 0
    conv_w = (SSD_CONV - 1) * SSD_CONV_DIM
    small = [p["gm_ln_w"], p["gm_ln_b"], p["gm_ws"], p["gm_bs"], p["conv_w"], p["conv_b"],
             p["dt_bias"], p["a_log"], p["d_skip"], p["ssd_norm_w"]]
    rows2 = lambda w: pl.BlockSpec((nb, w), lambda i: (i, 0))
    rows3 = lambda a, c: pl.BlockSpec((nb, a, c), lambda i: (i, 0, 0))
    cache = rows3(MEM_LEN * XA_HEADS, XA_HEAD_DIM)
    return pl.pallas_call(
        _sample_state_kernel,
        grid=(n // nb,),
        in_specs=[rows2(IN_PROJ_PACKED), rows2(conv_w), rows3(SSD_INNER, SSD_STATE), cache, cache]
                 + [_full(a.shape) for a in small],
        out_specs=[rows2(GM_WIDTH), rows2(SSD_INNER), rows2(XA_WIDTH), rows2(conv_w),
                   rows3(SSD_INNER, SSD_STATE), rows2(GM_WIDTH)],
        out_shape=[
            jax.ShapeDtypeStruct((n, GM_WIDTH), F32),
            jax.ShapeDtypeStruct((n, SSD_INNER), F32),
            jax.ShapeDtypeStruct((n, XA_WIDTH), F32),
            jax.ShapeDtypeStruct((n, conv_w), F32),
            jax.ShapeDtypeStruct((n, SSD_INNER, SSD_STATE), F32),
            jax.ShapeDtypeStruct((n, GM_WIDTH), F32),
        ],
        compiler_params=pltpu.CompilerParams(
            dimension_semantics=("arbitrary",), vmem_limit_bytes=VMEM_LIMIT_BYTES),
        name="sample_state",
    )(proj, state_conv2d, state_ssm, cache_k, cache_v, *small)


def _sample_merge_kernel(x_ref, proj_ref, gm_ref, yn_ref, xa_ref, wgm_ref, wssd_ref, wxa_ref,
                         wout_ref, h_ref):
    merged = None
    for i, (br_ref, w_ref) in enumerate(((gm_ref, wgm_ref), (yn_ref, wssd_ref), (xa_ref, wxa_ref))):
        gate = jax.nn.sigmoid(proj_ref[:, OFF_GATE + i * D_MODEL:OFF_GATE + (i + 1) * D_MODEL])
        term = gate * _dot(br_ref[...].astype(BF16), w_ref[...])
        merged = term if merged is None else merged + term
    h_ref[...] = x_ref[...] + _dot(merged.astype(BF16), wout_ref[...])


def _sample_merge(x2d, proj, gm, yn, xa, p):
    args = [x2d, proj, gm, yn, xa, p["w_br_gm"], p["w_br_ssd"], p["w_br_xa"], p["w_out"]]
    return pl.pallas_call(
        _sample_merge_kernel,
        grid=(1,),
        in_specs=[_full(a.shape) for a in args],
        out_specs=_full(x2d.shape),
        out_shape=jax.ShapeDtypeStruct(x2d.shape, F32),
        compiler_params=pltpu.CompilerParams(
            dimension_semantics=("arbitrary",), vmem_limit_bytes=VMEM_LIMIT_BYTES),
        name="sample_merge",
    )(*args)


def _pack_w_in_kernel(wt_ref, wa_ref, wdt_ref, wb_ref):
    rest = wb_ref.shape[1]
    for j in range(0, OFF_DT, PACK_COLS):
        wa_ref[:, j:j + PACK_COLS] = wt_ref[0, j:j + PACK_COLS, :].T.astype(BF16)
    dt_rows = wt_ref[0, OFF_DT:OFF_DT + DT_PAD, :]
    is_dt = lax.broadcasted_iota(jnp.int32, dt_rows.shape, 0) < SSD_HEADS
    wdt_ref[...] = jnp.where(is_dt, dt_rows, 0.0).T.astype(BF16)
    base = OFF_DT + SSD_HEADS
    for j in range(0, rest, PACK_COLS):
        wb_ref[:, j:j + PACK_COLS] = wt_ref[0, base + j:base + j + PACK_COLS, :].T.astype(BF16)


def _pack_w_in(w_in, l):
    d, width = w_in.shape[1:]
    rest = width - OFF_DT - SSD_HEADS
    assert rest == IN_PROJ_PACKED - OFF_Q and OFF_DT % PACK_COLS == 0 and rest % PACK_COLS == 0
    w_t = jnp.swapaxes(w_in, 1, 2)
    return pl.pallas_call(
        _pack_w_in_kernel,
        grid=(1,),
        in_specs=[pl.BlockSpec((1, width, d), lambda i: (l, 0, 0))],
        out_specs=[_full((d, OFF_DT)), _full((d, DT_PAD)), _full((d, rest))],
        out_shape=[jax.ShapeDtypeStruct((d, OFF_DT), BF16),
                   jax.ShapeDtypeStruct((d, DT_PAD), BF16),
                   jax.ShapeDtypeStruct((d, rest), BF16)],
        compiler_params=pltpu.CompilerParams(
            dimension_semantics=("arbitrary",), vmem_limit_bytes=VMEM_LIMIT_BYTES),
        name="pack_w_in",
    )(w_t)


def _prep_layer(l, norm_mix_w, w_in, gm_ln_w, gm_ln_b, gm_ws, gm_bs, conv_w, conv_b, dt_bias, a_log,
                d_skip, ssd_norm_w, mem_norm_w, w_mem_k, w_mem_v, w_br_gm, w_br_ssd, w_br_xa, w_out,
                norm_ffn_w, w_up, w_down, norm_final_w):
    row = lambda a: a.reshape(1, -1)
    pad_heads = lambda a: jnp.pad(a, (0, DT_PAD - SSD_HEADS)).reshape(1, DT_PAD)
    w_in_a, w_in_dt, w_in_b = _pack_w_in(w_in, l)
    return {
        "norm_mix_w": row(norm_mix_w[l]),
        "w_in_a": w_in_a, "w_in_dt": w_in_dt, "w_in_b": w_in_b,
        "gm_ln_w": row(gm_ln_w[l]), "gm_ln_b": row(gm_ln_b[l]),
        "gm_ws": gm_ws[l], "gm_bs": gm_bs[l], "gm_bs_t": gm_bs[l].T,
        "conv_w": conv_w[l], "conv_b": row(conv_b[l]),
        "dt_bias": pad_heads(dt_bias[l]), "a_log": pad_heads(a_log[l]),
        "d_skip": row(jnp.repeat(d_skip[l], SSD_HEAD_DIM)), "ssd_norm_w": row(ssd_norm_w[l]),
        "mem_norm_w": row(mem_norm_w[l]),
        "w_kv": jnp.concatenate([w_mem_k[l], w_mem_v[l]], axis=1).astype(BF16),
        "w_br_gm": w_br_gm[l].astype(BF16), "w_br_ssd": w_br_ssd[l].astype(BF16),
        "w_br_xa": w_br_xa[l].astype(BF16), "w_out": w_out[l].astype(BF16),
        "norm_ffn_w": row(norm_ffn_w[l]), "w_up": w_up[l].astype(BF16),
        "w_down": w_down[l].astype(BF16), "norm_final_w": row(norm_final_w),
    }


def kernel(x_prompt, x_sample, mem_prompt, cache_mem_k, cache_mem_v, state_conv, state_ssm, norm_mix_w, w_in, gm_ln_w, gm_ln_b, gm_ws, gm_bs, conv_w, conv_b, dt_bias, a_log, d_skip, ssd_norm_w, mem_norm_w, w_mem_k, w_mem_v, w_br_gm, w_br_ssd, w_br_xa, w_out, norm_ffn_w, w_up, w_down, norm_final_w):
    depth = w_in.shape[0]
    assert depth == 1, "the final norm is fused into the MLP kernel of the last (only) layer"
    b, seq, _ = x_prompt.shape
    n, dec_seq, _ = x_sample.shape
    assert dec_seq == 1
    p = _prep_layer(0, norm_mix_w, w_in, gm_ln_w, gm_ln_b, gm_ws, gm_bs, conv_w, conv_b, dt_bias,
                    a_log, d_skip, ssd_norm_w, mem_norm_w, w_mem_k, w_mem_v, w_br_gm, w_br_ssd,
                    w_br_xa, w_out, norm_ffn_w, w_up, w_down, norm_final_w)

    mem_k, mem_v = _memkv(mem_prompt.reshape(b * MEM_LEN, D_MODEL), p["mem_norm_w"], p["w_kv"])
    h1, conv_p, ssm_p = _mixer(x_prompt, mem_k.reshape(b, MEM_LEN, XA_WIDTH),
                               mem_v.reshape(b, MEM_LEN, XA_WIDTH), p)
    y_prompt = _ffn(h1.reshape(b * seq, D_MODEL), p).reshape(b, seq, D_MODEL)

    xs2d = x_sample.reshape(n, D_MODEL)
    proj = _sample_proj(xs2d, p)
    gm, yn, xa, conv_s, ssm_s, gv = _sample_state(
        proj, state_conv[0].reshape(n, (SSD_CONV - 1) * SSD_CONV_DIM), state_ssm[0].reshape(n, SSD_INNER, SSD_STATE),
        cache_mem_k[0].reshape(n, MEM_LEN * XA_HEADS, XA_HEAD_DIM),
        cache_mem_v[0].reshape(n, MEM_LEN * XA_HEADS, XA_HEAD_DIM), p)
    hs = _sample_merge(xs2d, proj, gm, yn, xa, p)
    y_sample = _ffn(hs, p).reshape(n, 1, D_MODEL)

    kv_shape = (1, b, MEM_LEN, XA_HEADS, XA_HEAD_DIM)
    state_shape = (SSD_HEADS, SSD_HEAD_DIM, SSD_STATE)
    return (y_prompt, y_sample,
            mem_k.reshape(kv_shape), mem_v.reshape(kv_shape),
            conv_p.reshape(1, b, SSD_CONV - 1, SSD_CONV_DIM), ssm_p.reshape((1, b) + state_shape),
            conv_s.reshape(1, n, SSD_CONV - 1, SSD_CONV_DIM), ssm_s.reshape((1, n) + state_shape),
            gv.reshape(1, n, 1, GM_WIDTH))
```

```python
import functools

import jax
import jax.numpy as jnp
from jax import lax
from jax.experimental import pallas as pl
from jax.experimental.pallas import tpu as pltpu

F32 = jnp.float32
BF16 = jnp.bfloat16

D_MODEL = 1024
GM_CHUNK = 128
GM_GROUPS = 4
GM_GROUP_DIM = 128
GM_WIDTH = GM_GROUPS * GM_GROUP_DIM
SSD_HEADS = 16
SSD_HEAD_DIM = 64
SSD_INNER = SSD_HEADS * SSD_HEAD_DIM
SSD_GROUPS = 2
SSD_GROUP_WIDTH = SSD_INNER // SSD_GROUPS
SSD_STATE = 128
SSD_CONV = 4
SSD_CHUNK = 128
SSD_CONV_DIM = SSD_INNER + 2 * SSD_GROUPS * SSD_STATE
MEM_LEN = 256
XA_HEADS = 4
XA_HEAD_DIM = 128
XA_WIDTH = XA_HEADS * XA_HEAD_DIM
N_BRANCH = 3
D_FF = 4 * D_MODEL
EPS = 1e-6

SUBLANES = 8
LANES = 128
HEADS_PER_LANE_TILE = LANES // SSD_HEAD_DIM
HEAD_PAIRS = SSD_HEADS // HEADS_PER_LANE_TILE
PAIRS_PER_GROUP = HEAD_PAIRS // SSD_GROUPS

DT_PAD = LANES
OFF_UV = 0
OFF_Z = OFF_UV + 2 * GM_WIDTH
OFF_XBC = OFF_Z + SSD_INNER
OFF_DT = OFF_XBC + SSD_CONV_DIM
OFF_Q = OFF_DT + DT_PAD
OFF_GATE = OFF_Q + XA_WIDTH
IN_PROJ_PACKED = OFF_GATE + N_BRANCH * D_MODEL

RAW_UV = 0
RAW_Z = RAW_UV + 2 * GM_WIDTH
RAW_DT = RAW_Z + SSD_INNER
RAW_GATE = RAW_DT + DT_PAD
RAW_WIDTH = RAW_GATE + N_BRANCH * D_MODEL

MIXER_ROWS = 256
PIECE_COLS = 256
MERGE_COLS = 512
FFN_ROWS = 1024
FFN_COLS = 1024
FFN_CAST_COLS = 512
FFN_CAST_STEPS = D_FF // FFN_CAST_COLS
MEMKV_ROWS = 512
PACK_COLS = 512
SAMPLE_BLOCK = SUBLANES
VMEM_LIMIT_BYTES = 56 * 1024 * 1024


def _dot(a, b):
    return jnp.dot(a, b, preferred_element_type=F32)


def _dot_nt(a, b):
    return lax.dot_general(a, b, (((1,), (1,)), ((), ())), preferred_element_type=F32)


def _dot_tn(a, b):
    return lax.dot_general(a, b, (((0,), (0,)), ((), ())), preferred_element_type=F32)


def _rms(x, w):
    return x * lax.rsqrt(jnp.mean(x * x, axis=-1, keepdims=True) + EPS) * w


def _layernorm(x, w, b):
    xc = x - jnp.mean(x, axis=-1, keepdims=True)
    return xc * lax.rsqrt(jnp.mean(xc * xc, axis=-1, keepdims=True) + EPS) * w + b


def _gelu(x):
    return 0.5 * x * (1.0 + lax.erf(x * 0.7071067811865476))


def _silu(x):
    return x * jax.nn.sigmoid(x)


def _softplus(x):
    return jnp.maximum(x, 0.0) + jnp.log1p(jnp.exp(-jnp.abs(x)))


def _split3(x):
    p1 = x.astype(BF16)
    r1 = x - p1.astype(F32)
    p2 = r1.astype(BF16)
    p3 = (r1 - p2.astype(F32)).astype(BF16)
    return p1, p2, p3


REDUCE_CHAINS = 8


def _reduce_leading(x, op):
    n = x.shape[0]
    assert n % REDUCE_CHAINS == 0
    acc = [x[w] for w in range(REDUCE_CHAINS)]
    for i in range(REDUCE_CHAINS, n, REDUCE_CHAINS):
        acc = [op(a, x[i + w]) for w, a in enumerate(acc)]
    while len(acc) > 1:
        acc = [op(acc[2 * i], acc[2 * i + 1]) for i in range(len(acc) // 2)]
    return acc[0]


def _fold_heads(a, op):
    assert SUBLANES == 2 * XA_HEADS
    return op(a, pltpu.roll(a, XA_HEADS, axis=0))


def _group_rms(y, w):
    parts = []
    for g in range(SSD_GROUPS):
        cols = slice(g * SSD_GROUP_WIDTH, (g + 1) * SSD_GROUP_WIDTH)
        parts.append(_rms(y[:, cols], w[:, cols]))
    return jnp.concatenate(parts, axis=-1)


def _memkv_kernel(mem_ref, nw_ref, wkv_ref, k_ref, v_ref):
    mn = _rms(mem_ref[...], nw_ref[...]).astype(BF16)
    kv = _dot(mn, wkv_ref[...])
    k_ref[...] = kv[:, :XA_WIDTH]
    v_ref[...] = kv[:, XA_WIDTH:]


def _memkv(mem2d, norm_w, wkv):
    rows = mem2d.shape[0]
    tm = min(MEMKV_ROWS, rows)
    return pl.pallas_call(
        _memkv_kernel,
        grid=(rows // tm,),
        in_specs=[
            pl.BlockSpec((tm, D_MODEL), lambda i: (i, 0)),
            pl.BlockSpec((1, D_MODEL), lambda i: (0, 0)),
            pl.BlockSpec((D_MODEL, 2 * XA_WIDTH), lambda i: (0, 0)),
        ],
        out_specs=[
            pl.BlockSpec((tm, XA_WIDTH), lambda i: (i, 0)),
            pl.BlockSpec((tm, XA_WIDTH), lambda i: (i, 0)),
        ],
        out_shape=[jax.ShapeDtypeStruct((rows, XA_WIDTH), F32)] * 2,
        compiler_params=pltpu.CompilerParams(
            dimension_semantics=("arbitrary",), vmem_limit_bytes=VMEM_LIMIT_BYTES),
        name="memkv",
    )(mem2d, norm_w, wkv)


def _ssd_chunk_steps(rows, xs_s, bc_s, dt_s, neg_a, hT_s, y_s, scratch):
    acum_s, acumt_s, dtt_s, wc_s, cb_s, lhs_s, cd_s, xw_s, sg_s = scratch
    q = SSD_CHUNK

    def masks():
        ii = lax.broadcasted_iota(jnp.int32, (q, q), 0)
        jj = lax.broadcasted_iota(jnp.int32, (q, q), 1)
        low_half = lax.broadcasted_iota(jnp.int32, (q, LANES), 1) < SSD_HEAD_DIM
        low_half_row = lax.broadcasted_iota(jnp.int32, (1, LANES), 1) < SSD_HEAD_DIM
        return ii >= jj, low_half, low_half_row

    def prologue():
        causal, _, _ = masks()
        dt = dt_s[rows, :]
        d1, d2, d3 = _split3(dt * neg_a)
        tri = causal.astype(BF16)
        acum = _dot(tri, d1) + _dot(tri, d2) + _dot(tri, d3)
        acum_s[...] = acum
        acumt_s[...] = acum.T
        dtt_s[...] = dt.T
        wc_s[...] = dt * jnp.exp(acum[q - 1:q, :] - acum)

    def group_step(g):
        _, low_half, _ = masks()
        b_g = bc_s[rows, g * SSD_STATE:(g + 1) * SSD_STATE]
        c_g = bc_s[rows, (SSD_GROUPS + g) * SSD_STATE:(SSD_GROUPS + g + 1) * SSD_STATE]
        cb_s[g] = _dot_nt(c_g.astype(BF16), b_g.astype(BF16))
        for pr in range(PAIRS_PER_GROUP):
            pair = g * PAIRS_PER_GROUP + pr
            lanes = slice(pair * LANES, (pair + 1) * LANES)
            h0 = pair * HEADS_PER_LANE_TILE
            w_pair = jnp.where(low_half, jnp.broadcast_to(wc_s[:, h0:h0 + 1], (q, LANES)),
                               jnp.broadcast_to(wc_s[:, h0 + 1:h0 + 2], (q, LANES)))
            xw_s[:, lanes] = (xs_s[rows, lanes] * w_pair).astype(BF16)
        cols = slice(g * SSD_GROUP_WIDTH, (g + 1) * SSD_GROUP_WIDTH)
        sg_s[:, cols] = _dot(b_g.T.astype(BF16), xw_s[:, cols])

    def head_step(h):
        causal, _, _ = masks()
        g = h // (SSD_HEADS // SSD_GROUPS)
        c_g = bc_s[rows, (SSD_GROUPS + g) * SSD_STATE:(SSD_GROUPS + g + 1) * SSD_STATE]
        col = jnp.broadcast_to(acum_s[:, h:h + 1], (q, q))
        seg = jnp.where(causal, col - acumt_s[h:h + 1, :], -jnp.inf)
        lhs_s[h, :, 0:q] = (cb_s[g] * jnp.exp(seg) * dtt_s[h:h + 1, :]).astype(BF16)
        lhs_s[h, :, q:2 * q] = (c_g * jnp.exp(col)).astype(BF16)
        cd_s[h:h + 1, :] = jnp.exp(col[q - 1:q, :])

    def pair_step(pair):
        _, low_half, low_half_row = masks()
        lanes = slice(pair * LANES, (pair + 1) * LANES)
        h0 = pair * HEADS_PER_LANE_TILE
        h_prev = hT_s[:, lanes]
        rhs = jnp.concatenate([xs_s[rows, lanes].astype(BF16), h_prev.astype(BF16)], axis=0)
        y_s[rows, lanes] = jnp.where(low_half, _dot(lhs_s[h0], rhs), _dot(lhs_s[h0 + 1], rhs))
        decay = jnp.where(low_half_row, cd_s[h0:h0 + 1, :], cd_s[h0 + 1:h0 + 2, :])
        hT_s[:, lanes] = h_prev * decay + sg_s[:, lanes]

    steps = [(2, prologue)]
    steps += [(3, functools.partial(group_step, g)) for g in range(SSD_GROUPS)]
    steps += [(2, functools.partial(head_step, h)) for h in range(SSD_HEADS)]
    steps += [(1, functools.partial(pair_step, pair)) for pair in range(HEAD_PAIRS)]
    return steps


def _interleave(steps, pieces):
    total = sum(w for w, _ in steps)
    done = acc = 0
    for w, step in steps:
        acc += w
        upto = len(pieces) * acc // total
        for piece in pieces[done:upto]:
            piece()
        done = upto
        step()
    assert done == len(pieces)


def _in_proj_cols(w_refs, lo, hi):
    wa_ref, wdt_ref, wb_ref = w_refs
    if hi <= OFF_DT:
        return wa_ref[:, lo:hi]
    if lo >= OFF_Q:
        return wb_ref[:, lo - OFF_Q:hi - OFF_Q]
    assert (lo, hi) == (OFF_DT, OFF_Q)
    return wdt_ref[...]


def _mixer_kernel(x1_ref, k_ref, v_ref, nw_ref, wa_ref, wdt_ref, wb_ref, lnw_ref, lnb_ref, ws_ref,
                  bst_ref, cw_ref, cb_ref, dtb_ref, alog_ref, dsk_ref, snw_ref,
                  wgm_ref, wssd_ref, wxa_ref, wout_ref,
                  h_ref, conv_ref, ssm_ref,
                  xn_s, raw_s, q_s, ext_s, hT_s, xs_s, bc_s, dt_s, y_s, u_s, v_s, gm_s, yn_s, xa_s, merged_s,
                  zs_s, gate_s, qb_s, xprev_s, *ssd_scratch, tiles_per_seq):
    tm = x1_ref.shape[1]
    s = pl.program_id(0)
    t2 = lax.rem(jnp.maximum(s - 1, 0), tiles_per_seq)
    w_in = functools.partial(_in_proj_cols, (wa_ref, wdt_ref, wb_ref))
    head = SUBLANES
    pc = PIECE_COLS

    @pl.when(s == 0)
    def _():
        blk = 2 * SUBLANES

        def zero_rows(i, carry):
            rows = pl.ds(pl.multiple_of(i * blk, blk), blk)
            for buf in (raw_s, ext_s, gate_s, zs_s, qb_s, xprev_s):
                buf[rows, :] = jnp.zeros((blk, buf.shape[1]), buf.dtype)
            return carry

        lax.fori_loop(0, tm // blk, zero_rows, 0)
        ext_s[tm:tm + head, :] = jnp.zeros((head, SSD_CONV_DIM), F32)

    @pl.when(t2 == 0)
    def _():
        ext_s[0:head, :] = jnp.zeros((head, SSD_CONV_DIM), F32)
        hT_s[...] = jnp.zeros(hT_s.shape, F32)

    xn_s[...] = _rms(x1_ref[0], nw_ref[...]).astype(BF16)

    def project(lo, width):
        return _dot(xn_s[...], w_in(lo, lo + width))

    def raw_piece(dst, src, width):
        def run():
            raw_s[:, dst:dst + width] = project(src, width)
        return run

    def q_piece(j):
        def run():
            q_s[:, j:j + pc] = project(OFF_Q + j, pc).astype(BF16)
        return run

    def xbc_piece(j):
        def run():
            ext_s[head:head + tm, j:j + pc] = project(OFF_XBC + j, pc)
        return run

    for j in range(0, SSD_CONV_DIM, pc):
        cols = slice(j, j + pc)
        acc = ext_s[head - 3:head - 3 + tm, cols] * cw_ref[0:1, cols]
        for k in range(1, SSD_CONV):
            acc = acc + ext_s[head - 3 + k:head - 3 + k + tm, cols] * cw_ref[k:k + 1, cols]
        xbc = _silu(acc + cb_ref[:, cols])
        if j < SSD_INNER:
            xs_s[:, cols] = xbc
        else:
            bc_s[:, j - SSD_INNER:j - SSD_INNER + pc] = xbc
        conv_ref[0, :, cols] = ext_s[tm + head - 3:tm + head, cols]
        ext_s[0:head, cols] = ext_s[tm:tm + head, cols]
    dt_s[...] = _softplus(raw_s[:, RAW_DT:RAW_GATE] + dtb_ref[...])
    for j in range(0, GM_WIDTH, pc):
        u_s[:, j:j + pc] = _gelu(raw_s[:, RAW_UV + j:RAW_UV + j + pc])
    v = _gelu(raw_s[:, RAW_UV + GM_WIDTH:RAW_Z])
    v_s[...] = _layernorm(v, lnw_ref[...], lnb_ref[...]).astype(BF16)

    pieces = ([raw_piece(RAW_GATE + j, OFF_GATE + j, pc) for j in range(0, N_BRANCH * D_MODEL, pc)]
              + [raw_piece(RAW_Z + j, OFF_Z + j, pc) for j in range(0, SSD_INNER, pc)]
              + [q_piece(j) for j in range(0, XA_WIDTH, pc)]
              + [xbc_piece(j) for j in range(0, SSD_CONV_DIM, pc)] + [raw_piece(RAW_DT, OFF_DT, DT_PAD)]
              + [raw_piece(RAW_UV + j, OFF_UV + j, pc) for j in range(0, 2 * GM_WIDTH, pc)])
    steps = []
    neg_a = -jnp.exp(alog_ref[...])
    for c in range(tm // SSD_CHUNK):
        rows = slice(c * SSD_CHUNK, (c + 1) * SSD_CHUNK)
        steps += _ssd_chunk_steps(rows, xs_s, bc_s, dt_s, neg_a, hT_s, y_s, ssd_scratch)

    def gmlp_group(g):
        ii = lax.broadcasted_iota(jnp.int32, (GM_CHUNK, GM_CHUNK), 0)
        jj = lax.broadcasted_iota(jnp.int32, (GM_CHUNK, GM_CHUNK), 1)
        cols = slice(g * GM_GROUP_DIM, (g + 1) * GM_GROUP_DIM)
        w_tril = jnp.where(ii >= jj, ws_ref[g], 0.0).astype(BF16)
        bias = bst_ref[:, g:g + 1]
        for c in range(tm // GM_CHUNK):
            rows = slice(c * GM_CHUNK, (c + 1) * GM_CHUNK)
            mixed = _dot(w_tril, v_s[rows, cols]) + bias
            gm_s[rows, cols] = (u_s[rows, cols] * mixed).astype(BF16)

    steps += [(1, functools.partial(gmlp_group, g)) for g in range(GM_GROUPS)]

    def attn_head(hd):
        cols = slice(hd * XA_HEAD_DIM, (hd + 1) * XA_HEAD_DIM)
        sc = _dot_nt(qb_s[:, cols], k_ref[0, :, cols].astype(BF16)) * (XA_HEAD_DIM ** -0.5)
        p = jnp.exp(sc - jnp.max(sc, axis=-1, keepdims=True))
        o = _dot(p.astype(BF16), v_ref[0, :, cols].astype(BF16))
        xa_s[:, cols] = (o / jnp.sum(p, axis=-1, keepdims=True)).astype(BF16)

    steps += [(2, functools.partial(attn_head, hd)) for hd in range(XA_HEADS)]

    def ssd_out_group(g):
        cols = slice(g * SSD_GROUP_WIDTH, (g + 1) * SSD_GROUP_WIDTH)
        yv = (y_s[:, cols] + xs_s[:, cols] * dsk_ref[:, cols]) * zs_s[:, cols]
        yn_s[:, cols] = _rms(yv, snw_ref[:, cols]).astype(BF16)

    steps += [(2, functools.partial(ssd_out_group, g)) for g in range(SSD_GROUPS)]

    def merge_piece(i, br_s, w_ref, j):
        lo = i * D_MODEL + j
        term = gate_s[:, lo:lo + MERGE_COLS] * _dot(br_s[...], w_ref[:, j:j + MERGE_COLS])
        if i == 0:
            merged_s[:, j:j + MERGE_COLS] = term
        else:
            merged_s[:, j:j + MERGE_COLS] += term

    for i, (br_s, w_ref) in enumerate(((gm_s, wgm_ref), (yn_s, wssd_ref), (xa_s, wxa_ref))):
        steps += [(2, functools.partial(merge_piece, i, br_s, w_ref, j)) for j in range(0, D_MODEL, MERGE_COLS)]

    def out_proj():
        h_ref[0] = xprev_s[...] + _dot(merged_s[...].astype(BF16), wout_ref[...])

    steps.append((1, out_proj))
    _interleave(steps, pieces)

    for j in range(0, N_BRANCH * D_MODEL, pc):
        gate_s[:, j:j + pc] = jax.nn.sigmoid(raw_s[:, RAW_GATE + j:RAW_GATE + j + pc])
    for j in range(0, SSD_INNER, pc):
        zs_s[:, j:j + pc] = _silu(raw_s[:, RAW_Z + j:RAW_Z + j + pc])
    qb_s[...] = q_s[...]
    xprev_s[...] = x1_ref[0]

    @pl.when(t2 == tiles_per_seq - 1)
    def _():
        for pair in range(HEAD_PAIRS):
            lanes = slice(pair * LANES, (pair + 1) * LANES)
            ssm_ref[0, lanes, :] = hT_s[:, lanes].T


def _full(shape):
    return pl.BlockSpec(shape, lambda *_: (0,) * len(shape))


def _mixer(x, mem_k, mem_v, p):
    b, seq, _ = x.shape
    tm = MIXER_ROWS
    assert seq % tm == 0 and tm % SSD_CHUNK == 0
    small = [p["norm_mix_w"], p["w_in_a"], p["w_in_dt"], p["w_in_b"],
             p["gm_ln_w"], p["gm_ln_b"], p["gm_ws"], p["gm_bs_t"],
             p["conv_w"], p["conv_b"], p["dt_bias"], p["a_log"], p["d_skip"], p["ssd_norm_w"],
             p["w_br_gm"], p["w_br_ssd"], p["w_br_xa"], p["w_out"]]
    tps = seq // tm
    n_tiles = b * tps
    tile1 = lambda s: jnp.minimum(s, n_tiles - 1)
    tile2 = lambda s: jnp.maximum(s - 1, 0)
    x_spec = lambda tile: pl.BlockSpec((1, tm, D_MODEL), lambda s: (tile(s) // tps, tile(s) % tps, 0))
    seq_spec = lambda r, c: pl.BlockSpec((1, r, c), lambda s: (tile2(s) // tps, 0, 0))
    return pl.pallas_call(
        functools.partial(_mixer_kernel, tiles_per_seq=tps),
        grid=(n_tiles + 1,),
        in_specs=[x_spec(tile1), seq_spec(MEM_LEN, XA_WIDTH), seq_spec(MEM_LEN, XA_WIDTH)]
                 + [_full(a.shape) for a in small],
        out_specs=[x_spec(tile2), seq_spec(SSD_CONV - 1, SSD_CONV_DIM), seq_spec(SSD_INNER, SSD_STATE)],
        out_shape=[
            jax.ShapeDtypeStruct((b, seq, D_MODEL), F32),
            jax.ShapeDtypeStruct((b, SSD_CONV - 1, SSD_CONV_DIM), F32),
            jax.ShapeDtypeStruct((b, SSD_INNER, SSD_STATE), F32),
        ],
        scratch_shapes=[
            pltpu.VMEM((tm, D_MODEL), BF16),
            pltpu.VMEM((tm, RAW_WIDTH), F32),
            pltpu.VMEM((tm, XA_WIDTH), BF16),
            pltpu.VMEM((tm + SUBLANES, SSD_CONV_DIM), F32),
            pltpu.VMEM((SSD_STATE, SSD_INNER), F32),
            pltpu.VMEM((tm, SSD_INNER), F32),
            pltpu.VMEM((tm, 2 * SSD_GROUPS * SSD_STATE), F32),
            pltpu.VMEM((tm, DT_PAD), F32),
            pltpu.VMEM((tm, SSD_INNER), F32),
            pltpu.VMEM((tm, GM_WIDTH), F32),
            pltpu.VMEM((tm, GM_WIDTH), BF16),
            pltpu.VMEM((tm, GM_WIDTH), BF16),
            pltpu.VMEM((tm, SSD_INNER), BF16),
            pltpu.VMEM((tm, XA_WIDTH), BF16),
            pltpu.VMEM((tm, D_MODEL), F32),
            pltpu.VMEM((tm, SSD_INNER), F32),
            pltpu.VMEM((tm, N_BRANCH * D_MODEL), F32),
            pltpu.VMEM((tm, XA_WIDTH), BF16),
            pltpu.VMEM((tm, D_MODEL), F32),
            pltpu.VMEM((SSD_CHUNK, DT_PAD), F32),
            pltpu.VMEM((DT_PAD, SSD_CHUNK), F32),
            pltpu.VMEM((DT_PAD, SSD_CHUNK), F32),
            pltpu.VMEM((SSD_CHUNK, DT_PAD), F32),
            pltpu.VMEM((SSD_GROUPS, SSD_CHUNK, SSD_CHUNK), F32),
            pltpu.VMEM((SSD_HEADS, SSD_CHUNK, 2 * SSD_CHUNK), BF16),
            pltpu.VMEM((SSD_HEADS, SSD_CHUNK), F32),
            pltpu.VMEM((SSD_CHUNK, SSD_INNER), BF16),
            pltpu.VMEM((SSD_STATE, SSD_INNER), F32),
        ],
        compiler_params=pltpu.CompilerParams(
            dimension_semantics=("arbitrary",), vmem_limit_bytes=VMEM_LIMIT_BYTES),
        name="prompt_mixer",
    )(x, mem_k, mem_v, *small)


def _ffn_kernel(h_ref, nw_ref, wup_ref, wdn_ref, fw_ref, o_ref):
    h = h_ref[...]
    hn = _rms(h, nw_ref[...]).astype(BF16)
    acc = h
    for j in range(D_FF // FFN_COLS):
        cols = slice(j * FFN_COLS, (j + 1) * FFN_COLS)
        a = jnp.square(jnp.maximum(_dot(hn, wup_ref[:, cols]), 0.0)).astype(BF16)
        acc = acc + _dot(a, wdn_ref[cols, :])
    o_ref[...] = _rms(acc, fw_ref[...])


def _ffn_cast_kernel(h_ref, nw_ref, wup_ref, wdn_ref, fw_ref, o_ref, wup_bf, wdn_bf):
    i = pl.program_id(0)
    for j in range(FFN_CAST_STEPS):
        @pl.when(i == j)
        def _(j=j):
            blk = slice(j * FFN_CAST_COLS, (j + 1) * FFN_CAST_COLS)
            wup_bf[:, blk] = wup_ref[0].astype(BF16)
            wdn_bf[blk, :] = wdn_ref[0].astype(BF16)

    @pl.when(i >= FFN_CAST_STEPS)
    def _():
        _ffn_kernel(h_ref, nw_ref, wup_bf, wdn_bf, fw_ref, o_ref)


def _ffn_cast(h2d, norm_w, w_up, w_down, final_w, l):
    rows = h2d.shape[0]
    tm = min(FFN_ROWS, rows)
    assert rows % tm == 0
    wblk = lambda i: jnp.minimum(i, FFN_CAST_STEPS - 1)
    tile = lambda i: (jnp.maximum(i - FFN_CAST_STEPS, 0), 0)
    return pl.pallas_call(
        _ffn_cast_kernel,
        grid=(FFN_CAST_STEPS + rows // tm,),
        in_specs=[
            pl.BlockSpec((tm, D_MODEL), tile),
            _full((1, D_MODEL)),
            pl.BlockSpec((1, D_MODEL, FFN_CAST_COLS), lambda i: (l, 0, wblk(i))),
            pl.BlockSpec((1, FFN_CAST_COLS, D_MODEL), lambda i: (l, wblk(i), 0)),
            _full((1, D_MODEL)),
        ],
        out_specs=[pl.BlockSpec((tm, D_MODEL), tile), _full((D_MODEL, D_FF)), _full((D_FF, D_MODEL))],
        out_shape=[jax.ShapeDtypeStruct((rows, D_MODEL), F32),
                   jax.ShapeDtypeStruct((D_MODEL, D_FF), BF16),
                   jax.ShapeDtypeStruct((D_FF, D_MODEL), BF16)],
        compiler_params=pltpu.CompilerParams(
            dimension_semantics=("arbitrary",), vmem_limit_bytes=VMEM_LIMIT_BYTES),
        name="ffn_cast",
    )(h2d, norm_w, w_up, w_down, final_w)


def _sample_proj_kernel(x_ref, nw_ref, wa_ref, wdt_ref, wb_ref, o_ref):
    xn = _rms(x_ref[...], nw_ref[...]).astype(BF16)
    o_ref[:, :OFF_DT] = _dot(xn, wa_ref[...])
    o_ref[:, OFF_DT:OFF_Q] = _dot(xn, wdt_ref[...])
    o_ref[:, OFF_Q:] = _dot(xn, wb_ref[...])


def _sample_proj(x2d, p):
    rows = x2d.shape[0]
    args = [x2d, p["norm_mix_w"], p["w_in_a"], p["w_in_dt"], p["w_in_b"]]
    return pl.pallas_call(
        _sample_proj_kernel,
        grid=(1,),
        in_specs=[_full(a.shape) for a in args],
        out_specs=_full((rows, IN_PROJ_PACKED)),
        out_shape=jax.ShapeDtypeStruct((rows, IN_PROJ_PACKED), F32),
        compiler_params=pltpu.CompilerParams(
            dimension_semantics=("arbitrary",), vmem_limit_bytes=VMEM_LIMIT_BYTES),
        name="sample_proj",
    )(*args)


def _sample_state_kernel(proj_ref, stc_ref, ssm_ref, k_ref, v_ref, lnw_ref, lnb_ref, ws_ref,
                         bs_ref, cw_ref, cb_ref, dtb_ref, alog_ref, dsk_ref, snw_ref,
                         gm_ref, yn_ref, xa_ref, convn_ref, ssmn_ref, gv_ref):
    nb = proj_ref.shape[0]
    proj = proj_ref[...]

    uv = _gelu(proj[:, OFF_UV:OFF_Z])
    v = _layernorm(uv[:, GM_WIDTH:], lnw_ref[...], lnb_ref[...])
    gv_ref[...] = v
    for g in range(GM_GROUPS):
        cols = slice(g * GM_GROUP_DIM, (g + 1) * GM_GROUP_DIM)
        mixed = v[:, cols] * ws_ref[g, 0:1, 0:1] + bs_ref[g:g + 1, 0:1]
        gm_ref[:, cols] = uv[:, cols] * mixed

    st = stc_ref[...]
    xnew = proj[:, OFF_XBC:OFF_DT]
    acc = st[:, 0:SSD_CONV_DIM] * cw_ref[0:1, :]
    for k in range(1, SSD_CONV - 1):
        acc = acc + st[:, k * SSD_CONV_DIM:(k + 1) * SSD_CONV_DIM] * cw_ref[k:k + 1, :]
    acc = acc + xnew * cw_ref[SSD_CONV - 1:SSD_CONV, :]
    convn_ref[:, 0:(SSD_CONV - 2) * SSD_CONV_DIM] = st[:, SSD_CONV_DIM:]
    convn_ref[:, (SSD_CONV - 2) * SSD_CONV_DIM:] = xnew
    xbc = _silu(acc + cb_ref[...])
    xs = xbc[:, :SSD_INNER]
    bm = xbc[:, SSD_INNER:SSD_INNER + SSD_GROUPS * SSD_STATE].astype(BF16)
    cm = xbc[:, SSD_INNER + SSD_GROUPS * SSD_STATE:].astype(BF16)

    dt = _softplus(proj[:, OFF_DT:OFF_Q] + dtb_ref[...])
    decay = jnp.exp(dt * -jnp.exp(alog_ref[...]))
    hh = lax.broadcasted_iota(jnp.int32, (DT_PAD, SSD_INNER), 0)
    cc = lax.broadcasted_iota(jnp.int32, (DT_PAD, SSD_INNER), 1)
    spread = (jnp.right_shift(cc, SSD_HEAD_DIM.bit_length() - 1) == hh).astype(BF16)
    dt_wide = sum(_dot(piece, spread) for piece in _split3(dt))
    xdt = xs * dt_wide

    rid = lax.broadcasted_iota(jnp.int32, (nb, 1), 0)
    scale = XA_HEAD_DIM ** -0.5
    kv_vregs = MEM_LEN * XA_HEADS // SUBLANES
    lane_ones = jnp.ones((XA_HEAD_DIM, XA_HEAD_DIM), BF16)
    qv = proj[:, OFF_Q:OFF_GATE]
    y = jnp.zeros((nb, SSD_INNER), F32)
    for bi in range(nb):
        mine = rid == bi
        x_row = jnp.where(mine, xdt, 0.0).astype(BF16)
        parts = []
        for g in range(SSD_GROUPS):
            cols = slice(g * SSD_GROUP_WIDTH, (g + 1) * SSD_GROUP_WIDTH)
            ncols = slice(g * SSD_STATE, (g + 1) * SSD_STATE)
            upd = _dot_tn(x_row[:, cols], bm[:, ncols])
            heads = range(g * SSD_HEADS // SSD_GROUPS, (g + 1) * SSD_HEADS // SSD_GROUPS)
            e_rows = jnp.concatenate(
                [jnp.broadcast_to(decay[bi:bi + 1, h:h + 1], (SSD_HEAD_DIM, SSD_STATE)) for h in heads],
                axis=0)
            h_new = ssm_ref[bi, cols, :] * e_rows + upd
            ssmn_ref[bi, cols, :] = h_new
            parts.append(_dot_nt(cm[:, ncols], h_new.astype(BF16)))
        y = y + jnp.where(mine, jnp.concatenate(parts, axis=-1), 0.0)

        q_rows = jnp.concatenate(
            [qv[bi:bi + 1, hd * XA_HEAD_DIM:(hd + 1) * XA_HEAD_DIM] for hd in range(XA_HEADS)]
            * (SUBLANES // XA_HEADS), axis=0)
        kq = (k_ref[bi].reshape(kv_vregs, SUBLANES, XA_HEAD_DIM) * q_rows[None]).astype(BF16)
        s = _dot(kq.reshape(kv_vregs * SUBLANES, XA_HEAD_DIM), lane_ones) * scale
        s = s.reshape(kv_vregs, SUBLANES, XA_HEAD_DIM)
        pexp = jnp.exp(s - _fold_heads(_reduce_leading(s, jnp.maximum), jnp.maximum)[None])
        o = _reduce_leading(pexp * v_ref[bi].reshape(kv_vregs, SUBLANES, XA_HEAD_DIM), jnp.add)
        o = _fold_heads(o, jnp.add) / _fold_heads(_reduce_leading(pexp, jnp.add), jnp.add)
        for hd in range(XA_HEADS):
            xa_ref[bi:bi + 1, hd * XA_HEAD_DIM:(hd + 1) * XA_HEAD_DIM] = o[hd:hd + 1, :]

    zs = _silu(proj[:, OFF_Z:OFF_XBC])
    yn_ref[...] = _group_rms((y + xs * dsk_ref[...]) * zs, snw_ref[...])


def _sample_state(proj, state_conv2d, state_ssm, cache_k, cache_v, p):
    n = proj.shape[0]
    nb = SAMPLE_BLOCK
    assert n % nb == 0
    conv_w = (SSD_CONV - 1) * SSD_CONV_DIM
    small = [p["gm_ln_w"], p["gm_ln_b"], p["gm_ws"], p["gm_bs"], p["conv_w"], p["conv_b"],
             p["dt_bias"], p["a_log"], p["d_skip"], p["ssd_norm_w"]]
    rows2 = lambda w: pl.BlockSpec((nb, w), lambda i: (i, 0))
    rows3 = lambda a, c: pl.BlockSpec((nb, a, c), lambda i: (i, 0, 0))
    cache = rows3(MEM_LEN * XA_HEADS, XA_HEAD_DIM)
    return pl.pallas_call(
        _sample_state_kernel,
        grid=(n // nb,),
        in_specs=[rows2(IN_PROJ_PACKED), rows2(conv_w), rows3(SSD_INNER, SSD_STATE), cache, cache]
                 + [_full(a.shape) for a in small],
        out_specs=[rows2(GM_WIDTH), rows2(SSD_INNER), rows2(XA_WIDTH), rows2(conv_w),
                   rows3(SSD_INNER, SSD_STATE), rows2(GM_WIDTH)],
        out_shape=[
            jax.ShapeDtypeStruct((n, GM_WIDTH), F32),
            jax.ShapeDtypeStruct((n, SSD_INNER), F32),
            jax.ShapeDtypeStruct((n, XA_WIDTH), F32),
            jax.ShapeDtypeStruct((n, conv_w), F32),
            jax.ShapeDtypeStruct((n, SSD_INNER, SSD_STATE), F32),
            jax.ShapeDtypeStruct((n, GM_WIDTH), F32),
        ],
        compiler_params=pltpu.CompilerParams(
            dimension_semantics=("arbitrary",), vmem_limit_bytes=VMEM_LIMIT_BYTES),
        name="sample_state",
    )(proj, state_conv2d, state_ssm, cache_k, cache_v, *small)


def _sample_tail_kernel(x_ref, proj_ref, gm_ref, yn_ref, xa_ref, wgm_ref, wssd_ref, wxa_ref, wout_ref,
                        nw_ref, wup_ref, wdn_ref, fw_ref, o_ref, hn_s, acc_s):
    j = pl.program_id(0)

    @pl.when(j == 0)
    def _():
        merged = None
        for i, (br_ref, w_ref) in enumerate(((gm_ref, wgm_ref), (yn_ref, wssd_ref), (xa_ref, wxa_ref))):
            gate = jax.nn.sigmoid(proj_ref[:, OFF_GATE + i * D_MODEL:OFF_GATE + (i + 1) * D_MODEL])
            term = gate * _dot(br_ref[...].astype(BF16), w_ref[...])
            merged = term if merged is None else merged + term
        h = x_ref[...] + _dot(merged.astype(BF16), wout_ref[...])
        hn_s[...] = _rms(h, nw_ref[...]).astype(BF16)
        acc_s[...] = h

    a = jnp.square(jnp.maximum(_dot(hn_s[...], wup_ref[...]), 0.0)).astype(BF16)
    acc_s[...] += _dot(a, wdn_ref[...])

    @pl.when(j == pl.num_programs(0) - 1)
    def _():
        o_ref[...] = _rms(acc_s[...], fw_ref[...])


def _sample_tail(x2d, proj, gm, yn, xa, p):
    rows = x2d.shape[0]
    head_args = [x2d, proj, gm, yn, xa, p["w_br_gm"], p["w_br_ssd"], p["w_br_xa"], p["w_out"], p["norm_ffn_w"]]
    return pl.pallas_call(
        _sample_tail_kernel,
        grid=(D_FF // FFN_COLS,),
        in_specs=[_full(a.shape) for a in head_args] + [
            pl.BlockSpec((D_MODEL, FFN_COLS), lambda j: (0, j)),
            pl.BlockSpec((FFN_COLS, D_MODEL), lambda j: (j, 0)),
            _full((1, D_MODEL)),
        ],
        out_specs=_full((rows, D_MODEL)),
        out_shape=jax.ShapeDtypeStruct((rows, D_MODEL), F32),
        scratch_shapes=[pltpu.VMEM((rows, D_MODEL), BF16), pltpu.VMEM((rows, D_MODEL), F32)],
        compiler_params=pltpu.CompilerParams(
            dimension_semantics=("arbitrary",), vmem_limit_bytes=VMEM_LIMIT_BYTES),
        name="sample_tail",
    )(*head_args, p["w_up"], p["w_down"], p["norm_final_w"])


def _pack_w_in_kernel(wt_ref, wa_ref, wdt_ref, wb_ref):
    rest = wb_ref.shape[1]
    for j in range(0, OFF_DT, PACK_COLS):
        wa_ref[:, j:j + PACK_COLS] = wt_ref[0, j:j + PACK_COLS, :].T.astype(BF16)
    dt_rows = wt_ref[0, OFF_DT:OFF_DT + DT_PAD, :]
    is_dt = lax.broadcasted_iota(jnp.int32, dt_rows.shape, 0) < SSD_HEADS
    wdt_ref[...] = jnp.where(is_dt, dt_rows, 0.0).T.astype(BF16)
    base = OFF_DT + SSD_HEADS
    for j in range(0, rest, PACK_COLS):
        wb_ref[:, j:j + PACK_COLS] = wt_ref[0, base + j:base + j + PACK_COLS, :].T.astype(BF16)


def _pack_w_in(w_in, l):
    d, width = w_in.shape[1:]
    rest = width - OFF_DT - SSD_HEADS
    assert rest == IN_PROJ_PACKED - OFF_Q and OFF_DT % PACK_COLS == 0 and rest % PACK_COLS == 0
    w_t = jnp.swapaxes(w_in, 1, 2)
    return pl.pallas_call(
        _pack_w_in_kernel,
        grid=(1,),
        in_specs=[pl.BlockSpec((1, width, d), lambda i: (l, 0, 0))],
        out_specs=[_full((d, OFF_DT)), _full((d, DT_PAD)), _full((d, rest))],
        out_shape=[jax.ShapeDtypeStruct((d, OFF_DT), BF16),
                   jax.ShapeDtypeStruct((d, DT_PAD), BF16),
                   jax.ShapeDtypeStruct((d, rest), BF16)],
        compiler_params=pltpu.CompilerParams(
            dimension_semantics=("arbitrary",), vmem_limit_bytes=VMEM_LIMIT_BYTES),
        name="pack_w_in",
    )(w_t)


def _prep_layer(l, norm_mix_w, w_in, gm_ln_w, gm_ln_b, gm_ws, gm_bs, conv_w, conv_b, dt_bias, a_log,
                d_skip, ssd_norm_w, mem_norm_w, w_mem_k, w_mem_v, w_br_gm, w_br_ssd, w_br_xa, w_out,
                norm_ffn_w, w_up, w_down, norm_final_w):
    row = lambda a: a.reshape(1, -1)
    pad_heads = lambda a: jnp.pad(a, (0, DT_PAD - SSD_HEADS)).reshape(1, DT_PAD)
    w_in_a, w_in_dt, w_in_b = _pack_w_in(w_in, l)
    return {
        "norm_mix_w": row(norm_mix_w[l]),
        "w_in_a": w_in_a, "w_in_dt": w_in_dt, "w_in_b": w_in_b,
        "gm_ln_w": row(gm_ln_w[l]), "gm_ln_b": row(gm_ln_b[l]),
        "gm_ws": gm_ws[l], "gm_bs": gm_bs[l], "gm_bs_t": gm_bs[l].T,
        "conv_w": conv_w[l], "conv_b": row(conv_b[l]),
        "dt_bias": pad_heads(dt_bias[l]), "a_log": pad_heads(a_log[l]),
        "d_skip": row(jnp.repeat(d_skip[l], SSD_HEAD_DIM)), "ssd_norm_w": row(ssd_norm_w[l]),
        "mem_norm_w": row(mem_norm_w[l]),
        "w_kv": jnp.concatenate([w_mem_k[l], w_mem_v[l]], axis=1).astype(BF16),
        "w_br_gm": w_br_gm[l].astype(BF16), "w_br_ssd": w_br_ssd[l].astype(BF16),
        "w_br_xa": w_br_xa[l].astype(BF16), "w_out": w_out[l].astype(BF16),
        "norm_ffn_w": row(norm_ffn_w[l]), "norm_final_w": row(norm_final_w),
    }


def kernel(x_prompt, x_sample, mem_prompt, cache_mem_k, cache_mem_v, state_conv, state_ssm, norm_mix_w, w_in, gm_ln_w, gm_ln_b, gm_ws, gm_bs, conv_w, conv_b, dt_bias, a_log, d_skip, ssd_norm_w, mem_norm_w, w_mem_k, w_mem_v, w_br_gm, w_br_ssd, w_br_xa, w_out, norm_ffn_w, w_up, w_down, norm_final_w):
    depth = w_in.shape[0]
    assert depth == 1, "the final norm is fused into the MLP kernel of the last (only) layer"
    b, seq, _ = x_prompt.shape
    n, dec_seq, _ = x_sample.shape
    assert dec_seq == 1
    p = _prep_layer(0, norm_mix_w, w_in, gm_ln_w, gm_ln_b, gm_ws, gm_bs, conv_w, conv_b, dt_bias,
                    a_log, d_skip, ssd_norm_w, mem_norm_w, w_mem_k, w_mem_v, w_br_gm, w_br_ssd,
                    w_br_xa, w_out, norm_ffn_w, w_up, w_down, norm_final_w)

    mem_k, mem_v = _memkv(mem_prompt.reshape(b * MEM_LEN, D_MODEL), p["mem_norm_w"], p["w_kv"])
    h1, conv_p, ssm_p = _mixer(x_prompt, mem_k.reshape(b, MEM_LEN, XA_WIDTH),
                               mem_v.reshape(b, MEM_LEN, XA_WIDTH), p)
    y_prompt, p["w_up"], p["w_down"] = _ffn_cast(h1.reshape(b * seq, D_MODEL), p["norm_ffn_w"], w_up, w_down,
                                                 p["norm_final_w"], 0)
    y_prompt = y_prompt.reshape(b, seq, D_MODEL)

    xs2d = x_sample.reshape(n, D_MODEL)
    proj = _sample_proj(xs2d, p)
    gm, yn, xa, conv_s, ssm_s, gv = _sample_state(
        proj, state_conv[0].reshape(n, (SSD_CONV - 1) * SSD_CONV_DIM), state_ssm[0].reshape(n, SSD_INNER, SSD_STATE),
        cache_mem_k[0].reshape(n, MEM_LEN * XA_HEADS, XA_HEAD_DIM),
        cache_mem_v[0].reshape(n, MEM_LEN * XA_HEADS, XA_HEAD_DIM), p)
    y_sample = _sample_tail(xs2d, proj, gm, yn, xa, p).reshape(n, 1, D_MODEL)

    kv_shape = (1, b, MEM_LEN, XA_HEADS, XA_HEAD_DIM)
    state_shape = (SSD_HEADS, SSD_HEAD_DIM, SSD_STATE)
    return (y_prompt, y_sample,
            mem_k.reshape(kv_shape), mem_v.reshape(kv_shape),
            conv_p.reshape(1, b, SSD_CONV - 1, SSD_CONV_DIM), ssm_p.reshape((1, b) + state_shape),
            conv_s.reshape(1, n, SSD_CONV - 1, SSD_CONV_DIM), ssm_s.reshape((1, n) + state_shape),
            gv.reshape(1, n, 1, GM_WIDTH))
```

```python
import functools

import jax
import jax.numpy as jnp
from jax import lax
from jax.experimental import pallas as pl
from jax.experimental.pallas import tpu as pltpu

F32 = jnp.float32
BF16 = jnp.bfloat16

D_MODEL = 1024
GM_CHUNK = 128
GM_GROUPS = 4
GM_GROUP_DIM = 128
GM_WIDTH = GM_GROUPS * GM_GROUP_DIM
SSD_HEADS = 16
SSD_HEAD_DIM = 64
SSD_INNER = SSD_HEADS * SSD_HEAD_DIM
SSD_GROUPS = 2
SSD_GROUP_WIDTH = SSD_INNER // SSD_GROUPS
SSD_STATE = 128
SSD_CONV = 4
SSD_CHUNK = 128
SSD_CONV_DIM = SSD_INNER + 2 * SSD_GROUPS * SSD_STATE
MEM_LEN = 256
XA_HEADS = 4
XA_HEAD_DIM = 128
XA_WIDTH = XA_HEADS * XA_HEAD_DIM
N_BRANCH = 3
D_FF = 4 * D_MODEL
EPS = 1e-6

SUBLANES = 8
LANES = 128
HEADS_PER_LANE_TILE = LANES // SSD_HEAD_DIM
HEAD_PAIRS = SSD_HEADS // HEADS_PER_LANE_TILE
PAIRS_PER_GROUP = HEAD_PAIRS // SSD_GROUPS

DT_PAD = LANES
OFF_UV = 0
OFF_Z = OFF_UV + 2 * GM_WIDTH
OFF_XBC = OFF_Z + SSD_INNER
OFF_DT = OFF_XBC + SSD_CONV_DIM
OFF_Q = OFF_DT + DT_PAD
OFF_GATE = OFF_Q + XA_WIDTH
IN_PROJ_PACKED = OFF_GATE + N_BRANCH * D_MODEL

RAW_UV = 0
RAW_Z = RAW_UV + 2 * GM_WIDTH
RAW_DT = RAW_Z + SSD_INNER
RAW_GATE = RAW_DT + DT_PAD
RAW_WIDTH = RAW_GATE + N_BRANCH * D_MODEL

MIXER_ROWS = 256
PIECE_COLS = 256
MERGE_COLS = 512
FFN_ROWS = 1024
FFN_COLS = 1024
FFN_CAST_COLS = 512
FFN_CAST_STEPS = D_FF // FFN_CAST_COLS
MEMKV_ROWS = 512
PACK_COLS = 512
SAMPLE_BLOCK = SUBLANES
VMEM_LIMIT_BYTES = 56 * 1024 * 1024
FUSED_VMEM_LIMIT_BYTES = 60 * 1024 * 1024


def _dot(a, b):
    return jnp.dot(a, b, preferred_element_type=F32)


def _dot_nt(a, b):
    return lax.dot_general(a, b, (((1,), (1,)), ((), ())), preferred_element_type=F32)


def _dot_tn(a, b):
    return lax.dot_general(a, b, (((0,), (0,)), ((), ())), preferred_element_type=F32)


def _rms(x, w):
    return x * lax.rsqrt(jnp.mean(x * x, axis=-1, keepdims=True) + EPS) * w


def _layernorm(x, w, b):
    xc = x - jnp.mean(x, axis=-1, keepdims=True)
    return xc * lax.rsqrt(jnp.mean(xc * xc, axis=-1, keepdims=True) + EPS) * w + b


def _gelu(x):
    return 0.5 * x * (1.0 + lax.erf(x * 0.7071067811865476))


def _silu(x):
    return x * jax.nn.sigmoid(x)


def _softplus(x):
    return jnp.maximum(x, 0.0) + jnp.log1p(jnp.exp(-jnp.abs(x)))


def _split3(x):
    p1 = x.astype(BF16)
    r1 = x - p1.astype(F32)
    p2 = r1.astype(BF16)
    p3 = (r1 - p2.astype(F32)).astype(BF16)
    return p1, p2, p3


REDUCE_CHAINS = 8


def _reduce_leading(x, op):
    n = x.shape[0]
    assert n % REDUCE_CHAINS == 0
    acc = [x[w] for w in range(REDUCE_CHAINS)]
    for i in range(REDUCE_CHAINS, n, REDUCE_CHAINS):
        acc = [op(a, x[i + w]) for w, a in enumerate(acc)]
    while len(acc) > 1:
        acc = [op(acc[2 * i], acc[2 * i + 1]) for i in range(len(acc) // 2)]
    return acc[0]


def _fold_heads(a, op):
    assert SUBLANES == 2 * XA_HEADS
    return op(a, pltpu.roll(a, XA_HEADS, axis=0))


def _group_rms(y, w):
    parts = []
    for g in range(SSD_GROUPS):
        cols = slice(g * SSD_GROUP_WIDTH, (g + 1) * SSD_GROUP_WIDTH)
        parts.append(_rms(y[:, cols], w[:, cols]))
    return jnp.concatenate(parts, axis=-1)


def _memkv_kernel(mem_ref, nw_ref, wkv_ref, k_ref, v_ref):
    mn = _rms(mem_ref[...], nw_ref[...]).astype(BF16)
    kv = _dot(mn, wkv_ref[...])
    k_ref[...] = kv[:, :XA_WIDTH]
    v_ref[...] = kv[:, XA_WIDTH:]


def _memkv(mem2d, norm_w, wkv):
    rows = mem2d.shape[0]
    tm = min(MEMKV_ROWS, rows)
    return pl.pallas_call(
        _memkv_kernel,
        grid=(rows // tm,),
        in_specs=[
            pl.BlockSpec((tm, D_MODEL), lambda i: (i, 0)),
            pl.BlockSpec((1, D_MODEL), lambda i: (0, 0)),
            pl.BlockSpec((D_MODEL, 2 * XA_WIDTH), lambda i: (0, 0)),
        ],
        out_specs=[
            pl.BlockSpec((tm, XA_WIDTH), lambda i: (i, 0)),
            pl.BlockSpec((tm, XA_WIDTH), lambda i: (i, 0)),
        ],
        out_shape=[jax.ShapeDtypeStruct((rows, XA_WIDTH), F32)] * 2,
        compiler_params=pltpu.CompilerParams(
            dimension_semantics=("arbitrary",), vmem_limit_bytes=VMEM_LIMIT_BYTES),
        name="memkv",
    )(mem2d, norm_w, wkv)


def _ssd_chunk_steps(rows, xs_s, bc_s, dt_s, neg_a, hT_s, y_s, scratch):
    acum_s, acumt_s, dtt_s, wc_s, cb_s, lhs_s, cd_s, xw_s, sg_s = scratch
    q = SSD_CHUNK

    def masks():
        ii = lax.broadcasted_iota(jnp.int32, (q, q), 0)
        jj = lax.broadcasted_iota(jnp.int32, (q, q), 1)
        low_half = lax.broadcasted_iota(jnp.int32, (q, LANES), 1) < SSD_HEAD_DIM
        low_half_row = lax.broadcasted_iota(jnp.int32, (1, LANES), 1) < SSD_HEAD_DIM
        return ii >= jj, low_half, low_half_row

    def prologue():
        causal, _, _ = masks()
        dt = dt_s[rows, :]
        d1, d2, d3 = _split3(dt * neg_a)
        tri = causal.astype(BF16)
        acum = _dot(tri, d1) + _dot(tri, d2) + _dot(tri, d3)
        acum_s[...] = acum
        acumt_s[...] = acum.T
        dtt_s[...] = dt.T
        wc_s[...] = dt * jnp.exp(acum[q - 1:q, :] - acum)

    def group_step(g):
        _, low_half, _ = masks()
        b_g = bc_s[rows, g * SSD_STATE:(g + 1) * SSD_STATE]
        c_g = bc_s[rows, (SSD_GROUPS + g) * SSD_STATE:(SSD_GROUPS + g + 1) * SSD_STATE]
        cb_s[g] = _dot_nt(c_g.astype(BF16), b_g.astype(BF16))
        for pr in range(PAIRS_PER_GROUP):
            pair = g * PAIRS_PER_GROUP + pr
            lanes = slice(pair * LANES, (pair + 1) * LANES)
            h0 = pair * HEADS_PER_LANE_TILE
            w_pair = jnp.where(low_half, jnp.broadcast_to(wc_s[:, h0:h0 + 1], (q, LANES)),
                               jnp.broadcast_to(wc_s[:, h0 + 1:h0 + 2], (q, LANES)))
            xw_s[:, lanes] = (xs_s[rows, lanes] * w_pair).astype(BF16)
        cols = slice(g * SSD_GROUP_WIDTH, (g + 1) * SSD_GROUP_WIDTH)
        sg_s[:, cols] = _dot(b_g.T.astype(BF16), xw_s[:, cols])

    def head_step(h):
        causal, _, _ = masks()
        g = h // (SSD_HEADS // SSD_GROUPS)
        c_g = bc_s[rows, (SSD_GROUPS + g) * SSD_STATE:(SSD_GROUPS + g + 1) * SSD_STATE]
        col = jnp.broadcast_to(acum_s[:, h:h + 1], (q, q))
        seg = jnp.where(causal, col - acumt_s[h:h + 1, :], -jnp.inf)
        lhs_s[h, :, 0:q] = (cb_s[g] * jnp.exp(seg) * dtt_s[h:h + 1, :]).astype(BF16)
        lhs_s[h, :, q:2 * q] = (c_g * jnp.exp(col)).astype(BF16)
        cd_s[h:h + 1, :] = jnp.exp(col[q - 1:q, :])

    def pair_step(pair):
        _, low_half, low_half_row = masks()
        lanes = slice(pair * LANES, (pair + 1) * LANES)
        h0 = pair * HEADS_PER_LANE_TILE
        h_prev = hT_s[:, lanes]
        rhs = jnp.concatenate([xs_s[rows, lanes].astype(BF16), h_prev.astype(BF16)], axis=0)
        y_s[rows, lanes] = jnp.where(low_half, _dot(lhs_s[h0], rhs), _dot(lhs_s[h0 + 1], rhs))
        decay = jnp.where(low_half_row, cd_s[h0:h0 + 1, :], cd_s[h0 + 1:h0 + 2, :])
        hT_s[:, lanes] = h_prev * decay + sg_s[:, lanes]

    steps = [(2, prologue)]
    steps += [(3, functools.partial(group_step, g)) for g in range(SSD_GROUPS)]
    steps += [(2, functools.partial(head_step, h)) for h in range(SSD_HEADS)]
    steps += [(1, functools.partial(pair_step, pair)) for pair in range(HEAD_PAIRS)]
    return steps


def _interleave(steps, pieces):
    total = sum(w for w, _ in steps)
    done = acc = 0
    for w, step in steps:
        acc += w
        upto = len(pieces) * acc // total
        for piece in pieces[done:upto]:
            piece()
        done = upto
        step()
    assert done == len(pieces)


def _in_proj_cols(w_refs, lo, hi):
    wa_ref, wdt_ref, wb_ref = w_refs
    if hi <= OFF_DT:
        return wa_ref[:, lo:hi]
    if lo >= OFF_Q:
        return wb_ref[:, lo - OFF_Q:hi - OFF_Q]
    assert (lo, hi) == (OFF_DT, OFF_Q)
    return wdt_ref[...]


def _mixer_kernel(x1_ref, k_ref, v_ref, nw_ref, wa_ref, wdt_ref, wb_ref, lnw_ref, lnb_ref, ws_ref,
                  bst_ref, cw_ref, cb_ref, dtb_ref, alog_ref, dsk_ref, snw_ref,
                  wgm_ref, wssd_ref, wxa_ref, wout_ref,
                  h_ref, conv_ref, ssm_ref,
                  xn_s, raw_s, q_s, ext_s, hT_s, xs_s, bc_s, dt_s, y_s, u_s, v_s, gm_s, yn_s, xa_s, merged_s,
                  zs_s, gate_s, qb_s, xprev_s, *ssd_scratch, tiles_per_seq):
    tm = x1_ref.shape[1]
    s = pl.program_id(0)
    t2 = lax.rem(jnp.maximum(s - 1, 0), tiles_per_seq)
    w_in = functools.partial(_in_proj_cols, (wa_ref, wdt_ref, wb_ref))
    head = SUBLANES
    pc = PIECE_COLS

    @pl.when(s == 0)
    def _():
        blk = 2 * SUBLANES

        def zero_rows(i, carry):
            rows = pl.ds(pl.multiple_of(i * blk, blk), blk)
            for buf in (raw_s, ext_s, gate_s, zs_s, qb_s, xprev_s):
                buf[rows, :] = jnp.zeros((blk, buf.shape[1]), buf.dtype)
            return carry

        lax.fori_loop(0, tm // blk, zero_rows, 0)
        ext_s[tm:tm + head, :] = jnp.zeros((head, SSD_CONV_DIM), F32)

    @pl.when(t2 == 0)
    def _():
        ext_s[0:head, :] = jnp.zeros((head, SSD_CONV_DIM), F32)
        hT_s[...] = jnp.zeros(hT_s.shape, F32)

    xn_s[...] = _rms(x1_ref[0], nw_ref[...]).astype(BF16)

    def project(lo, width):
        return _dot(xn_s[...], w_in(lo, lo + width))

    def raw_piece(dst, src, width):
        def run():
            raw_s[:, dst:dst + width] = project(src, width)
        return run

    def q_piece(j):
        def run():
            q_s[:, j:j + pc] = project(OFF_Q + j, pc).astype(BF16)
        return run

    def xbc_piece(j):
        def run():
            ext_s[head:head + tm, j:j + pc] = project(OFF_XBC + j, pc)
        return run

    for j in range(0, SSD_CONV_DIM, pc):
        cols = slice(j, j + pc)
        acc = ext_s[head - 3:head - 3 + tm, cols] * cw_ref[0:1, cols]
        for k in range(1, SSD_CONV):
            acc = acc + ext_s[head - 3 + k:head - 3 + k + tm, cols] * cw_ref[k:k + 1, cols]
        xbc = _silu(acc + cb_ref[:, cols])
        if j < SSD_INNER:
            xs_s[:, cols] = xbc
        else:
            bc_s[:, j - SSD_INNER:j - SSD_INNER + pc] = xbc
        conv_ref[0, :, cols] = ext_s[tm + head - 3:tm + head, cols]
        ext_s[0:head, cols] = ext_s[tm:tm + head, cols]
    dt_s[...] = _softplus(raw_s[:, RAW_DT:RAW_GATE] + dtb_ref[...])
    for j in range(0, GM_WIDTH, pc):
        u_s[:, j:j + pc] = _gelu(raw_s[:, RAW_UV + j:RAW_UV + j + pc])
    v = _gelu(raw_s[:, RAW_UV + GM_WIDTH:RAW_Z])
    v_s[...] = _layernorm(v, lnw_ref[...], lnb_ref[...]).astype(BF16)

    pieces = ([raw_piece(RAW_GATE + j, OFF_GATE + j, pc) for j in range(0, N_BRANCH * D_MODEL, pc)]
              + [raw_piece(RAW_Z + j, OFF_Z + j, pc) for j in range(0, SSD_INNER, pc)]
              + [q_piece(j) for j in range(0, XA_WIDTH, pc)]
              + [xbc_piece(j) for j in range(0, SSD_CONV_DIM, pc)] + [raw_piece(RAW_DT, OFF_DT, DT_PAD)]
              + [raw_piece(RAW_UV + j, OFF_UV + j, pc) for j in range(0, 2 * GM_WIDTH, pc)])
    steps = []
    neg_a = -jnp.exp(alog_ref[...])
    for c in range(tm // SSD_CHUNK):
        rows = slice(c * SSD_CHUNK, (c + 1) * SSD_CHUNK)
        steps += _ssd_chunk_steps(rows, xs_s, bc_s, dt_s, neg_a, hT_s, y_s, ssd_scratch)

    def gmlp_group(g):
        ii = lax.broadcasted_iota(jnp.int32, (GM_CHUNK, GM_CHUNK), 0)
        jj = lax.broadcasted_iota(jnp.int32, (GM_CHUNK, GM_CHUNK), 1)
        cols = slice(g * GM_GROUP_DIM, (g + 1) * GM_GROUP_DIM)
        w_tril = jnp.where(ii >= jj, ws_ref[g], 0.0).astype(BF16)
        bias = bst_ref[:, g:g + 1]
        for c in range(tm // GM_CHUNK):
            rows = slice(c * GM_CHUNK, (c + 1) * GM_CHUNK)
            mixed = _dot(w_tril, v_s[rows, cols]) + bias
            gm_s[rows, cols] = (u_s[rows, cols] * mixed).astype(BF16)

    steps += [(1, functools.partial(gmlp_group, g)) for g in range(GM_GROUPS)]

    def attn_head(hd):
        cols = slice(hd * XA_HEAD_DIM, (hd + 1) * XA_HEAD_DIM)
        sc = _dot_nt(qb_s[:, cols], k_ref[0, :, cols].astype(BF16)) * (XA_HEAD_DIM ** -0.5)
        p = jnp.exp(sc - jnp.max(sc, axis=-1, keepdims=True))
        o = _dot(p.astype(BF16), v_ref[0, :, cols].astype(BF16))
        xa_s[:, cols] = (o / jnp.sum(p, axis=-1, keepdims=True)).astype(BF16)

    steps += [(2, functools.partial(attn_head, hd)) for hd in range(XA_HEADS)]

    def ssd_out_group(g):
        cols = slice(g * SSD_GROUP_WIDTH, (g + 1) * SSD_GROUP_WIDTH)
        yv = (y_s[:, cols] + xs_s[:, cols] * dsk_ref[:, cols]) * zs_s[:, cols]
        yn_s[:, cols] = _rms(yv, snw_ref[:, cols]).astype(BF16)

    steps += [(2, functools.partial(ssd_out_group, g)) for g in range(SSD_GROUPS)]

    def merge_piece(i, br_s, w_ref, j):
        lo = i * D_MODEL + j
        term = gate_s[:, lo:lo + MERGE_COLS] * _dot(br_s[...], w_ref[:, j:j + MERGE_COLS])
        if i == 0:
            merged_s[:, j:j + MERGE_COLS] = term
        else:
            merged_s[:, j:j + MERGE_COLS] += term

    for i, (br_s, w_ref) in enumerate(((gm_s, wgm_ref), (yn_s, wssd_ref), (xa_s, wxa_ref))):
        steps += [(2, functools.partial(merge_piece, i, br_s, w_ref, j)) for j in range(0, D_MODEL, MERGE_COLS)]

    def out_proj():
        h_ref[0] = xprev_s[...] + _dot(merged_s[...].astype(BF16), wout_ref[...])

    steps.append((1, out_proj))
    _interleave(steps, pieces)

    for j in range(0, N_BRANCH * D_MODEL, pc):
        gate_s[:, j:j + pc] = jax.nn.sigmoid(raw_s[:, RAW_GATE + j:RAW_GATE + j + pc])
    for j in range(0, SSD_INNER, pc):
        zs_s[:, j:j + pc] = _silu(raw_s[:, RAW_Z + j:RAW_Z + j + pc])
    qb_s[...] = q_s[...]
    xprev_s[...] = x1_ref[0]

    @pl.when(t2 == tiles_per_seq - 1)
    def _():
        for pair in range(HEAD_PAIRS):
            lanes = slice(pair * LANES, (pair + 1) * LANES)
            ssm_ref[0, lanes, :] = hT_s[:, lanes].T


def _full(shape):
    return pl.BlockSpec(shape, lambda *_: (0,) * len(shape))


def _mixer(x, mem_k, mem_v, p):
    b, seq, _ = x.shape
    tm = MIXER_ROWS
    assert seq % tm == 0 and tm % SSD_CHUNK == 0
    small = [p["norm_mix_w"], p["w_in_a"], p["w_in_dt"], p["w_in_b"],
             p["gm_ln_w"], p["gm_ln_b"], p["gm_ws"], p["gm_bs_t"],
             p["conv_w"], p["conv_b"], p["dt_bias"], p["a_log"], p["d_skip"], p["ssd_norm_w"],
             p["w_br_gm"], p["w_br_ssd"], p["w_br_xa"], p["w_out"]]
    tps = seq // tm
    n_tiles = b * tps
    tile1 = lambda s: jnp.minimum(s, n_tiles - 1)
    tile2 = lambda s: jnp.maximum(s - 1, 0)
    x_spec = lambda tile: pl.BlockSpec((1, tm, D_MODEL), lambda s: (tile(s) // tps, tile(s) % tps, 0))
    seq_spec = lambda r, c: pl.BlockSpec((1, r, c), lambda s: (tile2(s) // tps, 0, 0))
    return pl.pallas_call(
        functools.partial(_mixer_kernel, tiles_per_seq=tps),
        grid=(n_tiles + 1,),
        in_specs=[x_spec(tile1), seq_spec(MEM_LEN, XA_WIDTH), seq_spec(MEM_LEN, XA_WIDTH)]
                 + [_full(a.shape) for a in small],
        out_specs=[x_spec(tile2), seq_spec(SSD_CONV - 1, SSD_CONV_DIM), seq_spec(SSD_INNER, SSD_STATE)],
        out_shape=[
            jax.ShapeDtypeStruct((b, seq, D_MODEL), F32),
            jax.ShapeDtypeStruct((b, SSD_CONV - 1, SSD_CONV_DIM), F32),
            jax.ShapeDtypeStruct((b, SSD_INNER, SSD_STATE), F32),
        ],
        scratch_shapes=[
            pltpu.VMEM((tm, D_MODEL), BF16),
            pltpu.VMEM((tm, RAW_WIDTH), F32),
            pltpu.VMEM((tm, XA_WIDTH), BF16),
            pltpu.VMEM((tm + SUBLANES, SSD_CONV_DIM), F32),
            pltpu.VMEM((SSD_STATE, SSD_INNER), F32),
            pltpu.VMEM((tm, SSD_INNER), F32),
            pltpu.VMEM((tm, 2 * SSD_GROUPS * SSD_STATE), F32),
            pltpu.VMEM((tm, DT_PAD), F32),
            pltpu.VMEM((tm, SSD_INNER), F32),
            pltpu.VMEM((tm, GM_WIDTH), F32),
            pltpu.VMEM((tm, GM_WIDTH), BF16),
            pltpu.VMEM((tm, GM_WIDTH), BF16),
            pltpu.VMEM((tm, SSD_INNER), BF16),
            pltpu.VMEM((tm, XA_WIDTH), BF16),
            pltpu.VMEM((tm, D_MODEL), F32),
            pltpu.VMEM((tm, SSD_INNER), F32),
            pltpu.VMEM((tm, N_BRANCH * D_MODEL), F32),
            pltpu.VMEM((tm, XA_WIDTH), BF16),
            pltpu.VMEM((tm, D_MODEL), F32),
            pltpu.VMEM((SSD_CHUNK, DT_PAD), F32),
            pltpu.VMEM((DT_PAD, SSD_CHUNK), F32),
            pltpu.VMEM((DT_PAD, SSD_CHUNK), F32),
            pltpu.VMEM((SSD_CHUNK, DT_PAD), F32),
            pltpu.VMEM((SSD_GROUPS, SSD_CHUNK, SSD_CHUNK), F32),
            pltpu.VMEM((SSD_HEADS, SSD_CHUNK, 2 * SSD_CHUNK), BF16),
            pltpu.VMEM((SSD_HEADS, SSD_CHUNK), F32),
            pltpu.VMEM((SSD_CHUNK, SSD_INNER), BF16),
            pltpu.VMEM((SSD_STATE, SSD_INNER), F32),
        ],
        compiler_params=pltpu.CompilerParams(
            dimension_semantics=("arbitrary",), vmem_limit_bytes=VMEM_LIMIT_BYTES),
        name="prompt_mixer",
    )(x, mem_k, mem_v, *small)


def _ffn_kernel(h_ref, nw_ref, wup_ref, wdn_ref, fw_ref, o_ref, side=()):
    h = h_ref[...]
    hn = _rms(h, nw_ref[...]).astype(BF16)
    acc = h
    for j in range(D_FF // FFN_COLS):
        cols = slice(j * FFN_COLS, (j + 1) * FFN_COLS)
        a = jnp.square(jnp.maximum(_dot(hn, wup_ref[:, cols]), 0.0)).astype(BF16)
        acc = acc + _dot(a, wdn_ref[cols, :])
        if j < len(side):
            side[j]()
    o_ref[...] = _rms(acc, fw_ref[...])


def _sample_proj_kernel(x_ref, nw_ref, wa_ref, wdt_ref, wb_ref, o_ref):
    xn = _rms(x_ref[...], nw_ref[...]).astype(BF16)
    o_ref[:, :OFF_DT] = _dot(xn, wa_ref[...])
    o_ref[:, OFF_DT:OFF_Q] = _dot(xn, wdt_ref[...])
    o_ref[:, OFF_Q:] = _dot(xn, wb_ref[...])


def _sample_proj(x2d, p):
    rows = x2d.shape[0]
    args = [x2d, p["norm_mix_w"], p["w_in_a"], p["w_in_dt"], p["w_in_b"]]
    return pl.pallas_call(
        _sample_proj_kernel,
        grid=(1,),
        in_specs=[_full(a.shape) for a in args],
        out_specs=_full((rows, IN_PROJ_PACKED)),
        out_shape=jax.ShapeDtypeStruct((rows, IN_PROJ_PACKED), F32),
        compiler_params=pltpu.CompilerParams(
            dimension_semantics=("arbitrary",), vmem_limit_bytes=VMEM_LIMIT_BYTES),
        name="sample_proj",
    )(*args)


def _sample_steps(proj_ref, stc_ref, ssm_ref, k_ref, v_ref, lnw_ref, lnb_ref, ws_ref,
                  bs_ref, cw_ref, cb_ref, dtb_ref, alog_ref, dsk_ref, snw_ref,
                  gm_ref, convn_ref, ssmn_ref, gv_ref, first_row):
    nb = proj_ref.shape[0]
    nseq = ssm_ref.shape[0]
    rid = lax.broadcasted_iota(jnp.int32, (nb, 1), 0)
    scale = XA_HEAD_DIM ** -0.5
    kv_vregs = MEM_LEN * XA_HEADS // SUBLANES
    st = {}

    def preamble():
        proj = proj_ref[...]
        uv = _gelu(proj[:, OFF_UV:OFF_Z])
        v = _layernorm(uv[:, GM_WIDTH:], lnw_ref[...], lnb_ref[...])
        gv_ref[...] = v
        for g in range(GM_GROUPS):
            cols = slice(g * GM_GROUP_DIM, (g + 1) * GM_GROUP_DIM)
            mixed = v[:, cols] * ws_ref[g, 0:1, 0:1] + bs_ref[g:g + 1, 0:1]
            gm_ref[:, cols] = uv[:, cols] * mixed

        sc = stc_ref[...]
        xnew = proj[:, OFF_XBC:OFF_DT]
        acc = sc[:, 0:SSD_CONV_DIM] * cw_ref[0:1, :]
        for k in range(1, SSD_CONV - 1):
            acc = acc + sc[:, k * SSD_CONV_DIM:(k + 1) * SSD_CONV_DIM] * cw_ref[k:k + 1, :]
        acc = acc + xnew * cw_ref[SSD_CONV - 1:SSD_CONV, :]
        convn_ref[:, 0:(SSD_CONV - 2) * SSD_CONV_DIM] = sc[:, SSD_CONV_DIM:]
        convn_ref[:, (SSD_CONV - 2) * SSD_CONV_DIM:] = xnew
        xbc = _silu(acc + cb_ref[...])
        st["xs"] = xs = xbc[:, :SSD_INNER]
        st["bm"] = xbc[:, SSD_INNER:SSD_INNER + SSD_GROUPS * SSD_STATE].astype(BF16)
        st["cm"] = xbc[:, SSD_INNER + SSD_GROUPS * SSD_STATE:].astype(BF16)

        dt = _softplus(proj[:, OFF_DT:OFF_Q] + dtb_ref[...])
        st["decay"] = jnp.exp(dt * -jnp.exp(alog_ref[...]))
        hh = lax.broadcasted_iota(jnp.int32, (DT_PAD, SSD_INNER), 0)
        cc = lax.broadcasted_iota(jnp.int32, (DT_PAD, SSD_INNER), 1)
        spread = (jnp.right_shift(cc, SSD_HEAD_DIM.bit_length() - 1) == hh).astype(BF16)
        dt_wide = sum(_dot(piece, spread) for piece in _split3(dt))
        st["xdt"] = xs * dt_wide
        st["qv"] = proj[:, OFF_Q:OFF_GATE]
        st["zs"] = _silu(proj[:, OFF_Z:OFF_XBC])
        st["y"] = jnp.zeros((nb, SSD_INNER), F32)
        st["xa"] = jnp.zeros((nb, XA_WIDTH), F32)

    def seq_step(k):
        row = first_row + k
        mine = rid == row
        pick = lambda a: jnp.sum(jnp.where(mine, a, 0.0), axis=0, keepdims=True)
        decay_row = pick(st["decay"])
        x_row = jnp.where(mine, st["xdt"], 0.0).astype(BF16)
        parts = []
        for g in range(SSD_GROUPS):
            cols = slice(g * SSD_GROUP_WIDTH, (g + 1) * SSD_GROUP_WIDTH)
            ncols = slice(g * SSD_STATE, (g + 1) * SSD_STATE)
            upd = _dot_tn(x_row[:, cols], st["bm"][:, ncols])
            heads = range(g * SSD_HEADS // SSD_GROUPS, (g + 1) * SSD_HEADS // SSD_GROUPS)
            e_rows = jnp.concatenate(
                [jnp.broadcast_to(decay_row[:, h:h + 1], (SSD_HEAD_DIM, SSD_STATE)) for h in heads],
                axis=0)
            h_new = ssm_ref[k, cols, :] * e_rows + upd
            ssmn_ref[k, cols, :] = h_new
            parts.append(_dot_nt(st["cm"][:, ncols], h_new.astype(BF16)))
        st["y"] = st["y"] + jnp.where(mine, jnp.concatenate(parts, axis=-1), 0.0)

        q_row = pick(st["qv"])
        q_rows = jnp.concatenate(
            [q_row[:, hd * XA_HEAD_DIM:(hd + 1) * XA_HEAD_DIM] for hd in range(XA_HEADS)]
            * (SUBLANES // XA_HEADS), axis=0)
        kq = (k_ref[k].reshape(kv_vregs, SUBLANES, XA_HEAD_DIM) * q_rows[None]).astype(BF16)
        lane_ones = jnp.ones((XA_HEAD_DIM, XA_HEAD_DIM), BF16)
        s = _dot(kq.reshape(kv_vregs * SUBLANES, XA_HEAD_DIM), lane_ones) * scale
        s = s.reshape(kv_vregs, SUBLANES, XA_HEAD_DIM)
        pexp = jnp.exp(s - _fold_heads(_reduce_leading(s, jnp.maximum), jnp.maximum)[None])
        o = _reduce_leading(pexp * v_ref[k].reshape(kv_vregs, SUBLANES, XA_HEAD_DIM), jnp.add)
        o = _fold_heads(o, jnp.add) / _fold_heads(_reduce_leading(pexp, jnp.add), jnp.add)
        xa_row = jnp.concatenate([o[hd:hd + 1, :] for hd in range(XA_HEADS)], axis=-1)
        st["xa"] = st["xa"] + jnp.where(mine, xa_row, 0.0)

    def epilogue():
        yn = _group_rms((st["y"] + st["xs"] * dsk_ref[...]) * st["zs"], snw_ref[...])
        return yn, st["xa"], (rid >= first_row) & (rid < first_row + nseq)

    return preamble, [functools.partial(seq_step, k) for k in range(nseq)], epilogue


N_SAMPLE_INS = 15


def _ffn_sample_kernel(h_ref, nw_ref, wup_ref, wdn_ref, fw_ref, *rest):
    s_ins, (o_ref, wup_bf, wdn_bf), s_outs = rest[:N_SAMPLE_INS], rest[N_SAMPLE_INS:N_SAMPLE_INS + 3], rest[N_SAMPLE_INS + 3:]
    gm_ref, yn_ref, xa_ref, convn_ref, ssmn_ref, gv_ref = s_outs
    i = pl.program_id(0)
    for j in range(FFN_CAST_STEPS):
        @pl.when(i == j)
        def _(j=j):
            blk = slice(j * FFN_CAST_COLS, (j + 1) * FFN_CAST_COLS)
            wup_bf[:, blk] = wup_ref[0].astype(BF16)
            wdn_bf[blk, :] = wdn_ref[0].astype(BF16)

    @pl.when(i >= FFN_CAST_STEPS)
    def _():
        nseq = ssmn_ref.shape[0]
        half = lax.rem(i - FFN_CAST_STEPS, 2)
        preamble, seq_steps, epilogue = _sample_steps(
            *s_ins, gm_ref, convn_ref, ssmn_ref, gv_ref, first_row=half * nseq)
        assert len(seq_steps) == D_FF // FFN_COLS
        preamble()
        _ffn_kernel(h_ref, nw_ref, wup_bf, wdn_bf, fw_ref, o_ref, side=seq_steps)
        yn, xa, valid = epilogue()

        @pl.when(half == 0)
        def _():
            yn_ref[...] = yn
            xa_ref[...] = xa

        @pl.when(half == 1)
        def _():
            yn_ref[...] = jnp.where(valid, yn, yn_ref[...])
            xa_ref[...] = jnp.where(valid, xa, xa_ref[...])


def _ffn_sample(h2d, norm_w, w_up, w_down, final_w, l, proj, state_conv2d, state_ssm, cache_k, cache_v, p):
    rows = h2d.shape[0]
    n = proj.shape[0]
    nb = SAMPLE_BLOCK
    nseq = nb // 2
    tiles = 2 * (n // nb)
    tm = rows // tiles
    assert rows == tm * tiles and n % nb == 0 and tm % SUBLANES == 0
    conv_w = (SSD_CONV - 1) * SSD_CONV_DIM
    small = [p["gm_ln_w"], p["gm_ln_b"], p["gm_ws"], p["gm_bs"], p["conv_w"], p["conv_b"],
             p["dt_bias"], p["a_log"], p["d_skip"], p["ssd_norm_w"]]
    assert 5 + len(small) == N_SAMPLE_INS
    wblk = lambda i: jnp.minimum(i, FFN_CAST_STEPS - 1)
    step = lambda i: jnp.maximum(i - FFN_CAST_STEPS, 0)
    tile = pl.BlockSpec((tm, D_MODEL), lambda i: (step(i), 0))
    rows2 = lambda w: pl.BlockSpec((nb, w), lambda i: (step(i) // 2, 0))
    seq3 = lambda a, c: pl.BlockSpec((nseq, a, c), lambda i: (step(i), 0, 0))
    cache = seq3(MEM_LEN * XA_HEADS, XA_HEAD_DIM)
    return pl.pallas_call(
        _ffn_sample_kernel,
        grid=(FFN_CAST_STEPS + tiles,),
        in_specs=[
            tile,
            _full((1, D_MODEL)),
            pl.BlockSpec((1, D_MODEL, FFN_CAST_COLS), lambda i: (l, 0, wblk(i))),
            pl.BlockSpec((1, FFN_CAST_COLS, D_MODEL), lambda i: (l, wblk(i), 0)),
            _full((1, D_MODEL)),
            rows2(IN_PROJ_PACKED), rows2(conv_w), seq3(SSD_INNER, SSD_STATE), cache, cache,
        ] + [_full(a.shape) for a in small],
        out_specs=[tile, _full((D_MODEL, D_FF)), _full((D_FF, D_MODEL)),
                   rows2(GM_WIDTH), rows2(SSD_INNER), rows2(XA_WIDTH), rows2(conv_w),
                   seq3(SSD_INNER, SSD_STATE), rows2(GM_WIDTH)],
        out_shape=[
            jax.ShapeDtypeStruct((rows, D_MODEL), F32),
            jax.ShapeDtypeStruct((D_MODEL, D_FF), BF16),
            jax.ShapeDtypeStruct((D_FF, D_MODEL), BF16),
            jax.ShapeDtypeStruct((n, GM_WIDTH), F32),
            jax.ShapeDtypeStruct((n, SSD_INNER), F32),
            jax.ShapeDtypeStruct((n, XA_WIDTH), F32),
            jax.ShapeDtypeStruct((n, conv_w), F32),
            jax.ShapeDtypeStruct((n, SSD_INNER, SSD_STATE), F32),
            jax.ShapeDtypeStruct((n, GM_WIDTH), F32),
        ],
        compiler_params=pltpu.CompilerParams(
            dimension_semantics=("arbitrary",), vmem_limit_bytes=FUSED_VMEM_LIMIT_BYTES),
        name="ffn_sample",
    )(h2d, norm_w, w_up, w_down, final_w, proj, state_conv2d, state_ssm, cache_k, cache_v, *small)


def _sample_tail_kernel(x_ref, proj_ref, gm_ref, yn_ref, xa_ref, wgm_ref, wssd_ref, wxa_ref, wout_ref,
                        nw_ref, wup_ref, wdn_ref, fw_ref, o_ref, hn_s, acc_s):
    j = pl.program_id(0)

    @pl.when(j == 0)
    def _():
        merged = None
        for i, (br_ref, w_ref) in enumerate(((gm_ref, wgm_ref), (yn_ref, wssd_ref), (xa_ref, wxa_ref))):
            gate = jax.nn.sigmoid(proj_ref[:, OFF_GATE + i * D_MODEL:OFF_GATE + (i + 1) * D_MODEL])
            term = gate * _dot(br_ref[...].astype(BF16), w_ref[...])
            merged = term if merged is None else merged + term
        h = x_ref[...] + _dot(merged.astype(BF16), wout_ref[...])
        hn_s[...] = _rms(h, nw_ref[...]).astype(BF16)
        acc_s[...] = h

    a = jnp.square(jnp.maximum(_dot(hn_s[...], wup_ref[...]), 0.0)).astype(BF16)
    acc_s[...] += _dot(a, wdn_ref[...])

    @pl.when(j == pl.num_programs(0) - 1)
    def _():
        o_ref[...] = _rms(acc_s[...], fw_ref[...])


def _sample_tail(x2d, proj, gm, yn, xa, p):
    rows = x2d.shape[0]
    head_args = [x2d, proj, gm, yn, xa, p["w_br_gm"], p["w_br_ssd"], p["w_br_xa"], p["w_out"], p["norm_ffn_w"]]
    return pl.pallas_call(
        _sample_tail_kernel,
        grid=(D_FF // FFN_COLS,),
        in_specs=[_full(a.shape) for a in head_args] + [
            pl.BlockSpec((D_MODEL, FFN_COLS), lambda j: (0, j)),
            pl.BlockSpec((FFN_COLS, D_MODEL), lambda j: (j, 0)),
            _full((1, D_MODEL)),
        ],
        out_specs=_full((rows, D_MODEL)),
        out_shape=jax.ShapeDtypeStruct((rows, D_MODEL), F32),
        scratch_shapes=[pltpu.VMEM((rows, D_MODEL), BF16), pltpu.VMEM((rows, D_MODEL), F32)],
        compiler_params=pltpu.CompilerParams(
            dimension_semantics=("arbitrary",), vmem_limit_bytes=VMEM_LIMIT_BYTES),
        name="sample_tail",
    )(*head_args, p["w_up"], p["w_down"], p["norm_final_w"])


def _pack_w_in_kernel(wt_ref, wa_ref, wdt_ref, wb_ref):
    rest = wb_ref.shape[1]
    for j in range(0, OFF_DT, PACK_COLS):
        wa_ref[:, j:j + PACK_COLS] = wt_ref[0, j:j + PACK_COLS, :].T.astype(BF16)
    dt_rows = wt_ref[0, OFF_DT:OFF_DT + DT_PAD, :]
    is_dt = lax.broadcasted_iota(jnp.int32, dt_rows.shape, 0) < SSD_HEADS
    wdt_ref[...] = jnp.where(is_dt, dt_rows, 0.0).T.astype(BF16)
    base = OFF_DT + SSD_HEADS
    for j in range(0, rest, PACK_COLS):
        wb_ref[:, j:j + PACK_COLS] = wt_ref[0, base + j:base + j + PACK_COLS, :].T.astype(BF16)


def _pack_w_in(w_in, l):
    d, width = w_in.shape[1:]
    rest = width - OFF_DT - SSD_HEADS
    assert rest == IN_PROJ_PACKED - OFF_Q and OFF_DT % PACK_COLS == 0 and rest % PACK_COLS == 0
    w_t = jnp.swapaxes(w_in, 1, 2)
    return pl.pallas_call(
        _pack_w_in_kernel,
        grid=(1,),
        in_specs=[pl.BlockSpec((1, width, d), lambda i: (l, 0, 0))],
        out_specs=[_full((d, OFF_DT)), _full((d, DT_PAD)), _full((d, rest))],
        out_shape=[jax.ShapeDtypeStruct((d, OFF_DT), BF16),
                   jax.ShapeDtypeStruct((d, DT_PAD), BF16),
                   jax.ShapeDtypeStruct((d, rest), BF16)],
        compiler_params=pltpu.CompilerParams(
            dimension_semantics=("arbitrary",), vmem_limit_bytes=VMEM_LIMIT_BYTES),
        name="pack_w_in",
    )(w_t)


def _prep_layer(l, norm_mix_w, w_in, gm_ln_w, gm_ln_b, gm_ws, gm_bs, conv_w, conv_b, dt_bias, a_log,
                d_skip, ssd_norm_w, mem_norm_w, w_mem_k, w_mem_v, w_br_gm, w_br_ssd, w_br_xa, w_out,
                norm_ffn_w, w_up, w_down, norm_final_w):
    row = lambda a: a.reshape(1, -1)
    pad_heads = lambda a: jnp.pad(a, (0, DT_PAD - SSD_HEADS)).reshape(1, DT_PAD)
    w_in_a, w_in_dt, w_in_b = _pack_w_in(w_in, l)
    return {
        "norm_mix_w": row(norm_mix_w[l]),
        "w_in_a": w_in_a, "w_in_dt": w_in_dt, "w_in_b": w_in_b,
        "gm_ln_w": row(gm_ln_w[l]), "gm_ln_b": row(gm_ln_b[l]),
        "gm_ws": gm_ws[l], "gm_bs": gm_bs[l], "gm_bs_t": gm_bs[l].T,
        "conv_w": conv_w[l], "conv_b": row(conv_b[l]),
        "dt_bias": pad_heads(dt_bias[l]), "a_log": pad_heads(a_log[l]),
        "d_skip": row(jnp.repeat(d_skip[l], SSD_HEAD_DIM)), "ssd_norm_w": row(ssd_norm_w[l]),
        "mem_norm_w": row(mem_norm_w[l]),
        "w_kv": jnp.concatenate([w_mem_k[l], w_mem_v[l]], axis=1).astype(BF16),
        "w_br_gm": w_br_gm[l].astype(BF16), "w_br_ssd": w_br_ssd[l].astype(BF16),
        "w_br_xa": w_br_xa[l].astype(BF16), "w_out": w_out[l].astype(BF16),
        "norm_ffn_w": row(norm_ffn_w[l]), "norm_final_w": row(norm_final_w),
    }


def kernel(x_prompt, x_sample, mem_prompt, cache_mem_k, cache_mem_v, state_conv, state_ssm, norm_mix_w, w_in, gm_ln_w, gm_ln_b, gm_ws, gm_bs, conv_w, conv_b, dt_bias, a_log, d_skip, ssd_norm_w, mem_norm_w, w_mem_k, w_mem_v, w_br_gm, w_br_ssd, w_br_xa, w_out, norm_ffn_w, w_up, w_down, norm_final_w):
    depth = w_in.shape[0]
    assert depth == 1, "the final norm is fused into the MLP kernel of the last (only) layer"
    b, seq, _ = x_prompt.shape
    n, dec_seq, _ = x_sample.shape
    assert dec_seq == 1
    p = _prep_layer(0, norm_mix_w, w_in, gm_ln_w, gm_ln_b, gm_ws, gm_bs, conv_w, conv_b, dt_bias,
                    a_log, d_skip, ssd_norm_w, mem_norm_w, w_mem_k, w_mem_v, w_br_gm, w_br_ssd,
                    w_br_xa, w_out, norm_ffn_w, w_up, w_down, norm_final_w)

    mem_k, mem_v = _memkv(mem_prompt.reshape(b * MEM_LEN, D_MODEL), p["mem_norm_w"], p["w_kv"])
    h1, conv_p, ssm_p = _mixer(x_prompt, mem_k.reshape(b, MEM_LEN, XA_WIDTH),
                               mem_v.reshape(b, MEM_LEN, XA_WIDTH), p)

    xs2d = x_sample.reshape(n, D_MODEL)
    proj = _sample_proj(xs2d, p)
    y_prompt, p["w_up"], p["w_down"], gm, yn, xa, conv_s, ssm_s, gv = _ffn_sample(
        h1.reshape(b * seq, D_MODEL), p["norm_ffn_w"], w_up, w_down, p["norm_final_w"], 0,
        proj, state_conv[0].reshape(n, (SSD_CONV - 1) * SSD_CONV_DIM), state_ssm[0].reshape(n, SSD_INNER, SSD_STATE),
        cache_mem_k[0].reshape(n, MEM_LEN * XA_HEADS, XA_HEAD_DIM),
        cache_mem_v[0].reshape(n, MEM_LEN * XA_HEADS, XA_HEAD_DIM), p)
    y_prompt = y_prompt.reshape(b, seq, D_MODEL)
    y_sample = _sample_tail(xs2d, proj, gm, yn, xa, p).reshape(n, 1, D_MODEL)

    kv_shape = (1, b, MEM_LEN, XA_HEADS, XA_HEAD_DIM)
    state_shape = (SSD_HEADS, SSD_HEAD_DIM, SSD_STATE)
    return (y_prompt, y_sample,
            mem_k.reshape(kv_shape), mem_v.reshape(kv_shape),
            conv_p.reshape(1, b, SSD_CONV - 1, SSD_CONV_DIM), ssm_p.reshape((1, b) + state_shape),
            conv_s.reshape(1, n, SSD_CONV - 1, SSD_CONV_DIM), ssm_s.reshape((1, n) + state_shape),
            gv.reshape(1, n, 1, GM_WIDTH))
```

```python
import functools

import jax
import jax.numpy as jnp
from jax import lax
from jax.experimental import pallas as pl
from jax.experimental.pallas import tpu as pltpu

F32 = jnp.float32
BF16 = jnp.bfloat16

D_MODEL = 1024
GM_CHUNK = 128
GM_GROUPS = 4
GM_GROUP_DIM = 128
GM_WIDTH = GM_GROUPS * GM_GROUP_DIM
SSD_HEADS = 16
SSD_HEAD_DIM = 64
SSD_INNER = SSD_HEADS * SSD_HEAD_DIM
SSD_GROUPS = 2
SSD_GROUP_WIDTH = SSD_INNER // SSD_GROUPS
SSD_STATE = 128
SSD_CONV = 4
SSD_CHUNK = 128
SSD_CONV_DIM = SSD_INNER + 2 * SSD_GROUPS * SSD_STATE
MEM_LEN = 256
XA_HEADS = 4
XA_HEAD_DIM = 128
XA_WIDTH = XA_HEADS * XA_HEAD_DIM
N_BRANCH = 3
D_FF = 4 * D_MODEL
EPS = 1e-6

SUBLANES = 8
LANES = 128
HEADS_PER_LANE_TILE = LANES // SSD_HEAD_DIM
HEAD_PAIRS = SSD_HEADS // HEADS_PER_LANE_TILE
PAIRS_PER_GROUP = HEAD_PAIRS // SSD_GROUPS

DT_PAD = LANES
OFF_UV = 0
OFF_Z = OFF_UV + 2 * GM_WIDTH
OFF_XBC = OFF_Z + SSD_INNER
OFF_DT = OFF_XBC + SSD_CONV_DIM
OFF_Q = OFF_DT + DT_PAD
OFF_GATE = OFF_Q + XA_WIDTH
IN_PROJ_PACKED = OFF_GATE + N_BRANCH * D_MODEL

RAW_UV = 0
RAW_Z = RAW_UV + 2 * GM_WIDTH
RAW_DT = RAW_Z + SSD_INNER
RAW_GATE = RAW_DT + DT_PAD
RAW_WIDTH = RAW_GATE + N_BRANCH * D_MODEL

MIXER_ROWS = 256
PIECE_COLS = 256
MERGE_COLS = 512
FFN_ROWS = 1024
FFN_COLS = 1024
MEMKV_ROWS = 512
PACK_COLS = 512
SAMPLE_BLOCK = SUBLANES
VMEM_LIMIT_BYTES = 56 * 1024 * 1024


def _dot(a, b):
    return jnp.dot(a, b, preferred_element_type=F32)


def _dot_nt(a, b):
    return lax.dot_general(a, b, (((1,), (1,)), ((), ())), preferred_element_type=F32)


def _dot_tn(a, b):
    return lax.dot_general(a, b, (((0,), (0,)), ((), ())), preferred_element_type=F32)


def _rms(x, w):
    return x * lax.rsqrt(jnp.mean(x * x, axis=-1, keepdims=True) + EPS) * w


def _layernorm(x, w, b):
    xc = x - jnp.mean(x, axis=-1, keepdims=True)
    return xc * lax.rsqrt(jnp.mean(xc * xc, axis=-1, keepdims=True) + EPS) * w + b


def _gelu(x):
    return 0.5 * x * (1.0 + lax.erf(x * 0.7071067811865476))


def _silu(x):
    return x * jax.nn.sigmoid(x)


def _softplus(x):
    return jnp.maximum(x, 0.0) + jnp.log1p(jnp.exp(-jnp.abs(x)))


def _split3(x):
    p1 = x.astype(BF16)
    r1 = x - p1.astype(F32)
    p2 = r1.astype(BF16)
    p3 = (r1 - p2.astype(F32)).astype(BF16)
    return p1, p2, p3


REDUCE_CHAINS = 8


def _reduce_leading(x, op):
    n = x.shape[0]
    assert n % REDUCE_CHAINS == 0
    acc = [x[w] for w in range(REDUCE_CHAINS)]
    for i in range(REDUCE_CHAINS, n, REDUCE_CHAINS):
        acc = [op(a, x[i + w]) for w, a in enumerate(acc)]
    while len(acc) > 1:
        acc = [op(acc[2 * i], acc[2 * i + 1]) for i in range(len(acc) // 2)]
    return acc[0]


def _fold_heads(a, op):
    assert SUBLANES == 2 * XA_HEADS
    return op(a, pltpu.roll(a, XA_HEADS, axis=0))


def _group_rms(y, w):
    parts = []
    for g in range(SSD_GROUPS):
        cols = slice(g * SSD_GROUP_WIDTH, (g + 1) * SSD_GROUP_WIDTH)
        parts.append(_rms(y[:, cols], w[:, cols]))
    return jnp.concatenate(parts, axis=-1)


def _memkv_kernel(mem_ref, nw_ref, wkv_ref, k_ref, v_ref):
    mn = _rms(mem_ref[...], nw_ref[...]).astype(BF16)
    kv = _dot(mn, wkv_ref[...])
    k_ref[...] = kv[:, :XA_WIDTH]
    v_ref[...] = kv[:, XA_WIDTH:]


def _memkv(mem2d, norm_w, wkv):
    rows = mem2d.shape[0]
    tm = min(MEMKV_ROWS, rows)
    return pl.pallas_call(
        _memkv_kernel,
        grid=(rows // tm,),
        in_specs=[
            pl.BlockSpec((tm, D_MODEL), lambda i: (i, 0)),
            pl.BlockSpec((1, D_MODEL), lambda i: (0, 0)),
            pl.BlockSpec((D_MODEL, 2 * XA_WIDTH), lambda i: (0, 0)),
        ],
        out_specs=[
            pl.BlockSpec((tm, XA_WIDTH), lambda i: (i, 0)),
            pl.BlockSpec((tm, XA_WIDTH), lambda i: (i, 0)),
        ],
        out_shape=[jax.ShapeDtypeStruct((rows, XA_WIDTH), F32)] * 2,
        compiler_params=pltpu.CompilerParams(
            dimension_semantics=("arbitrary",), vmem_limit_bytes=VMEM_LIMIT_BYTES),
        name="memkv",
    )(mem2d, norm_w, wkv)


def _ssd_chunk_steps(rows, xs_s, bc_s, dt_s, neg_a, hT_s, y_s, scratch):
    acum_s, acumt_s, dtt_s, wc_s, cb_s, lhs_s, cd_s, xw_s, sg_s = scratch
    q = SSD_CHUNK

    def masks():
        ii = lax.broadcasted_iota(jnp.int32, (q, q), 0)
        jj = lax.broadcasted_iota(jnp.int32, (q, q), 1)
        low_half = lax.broadcasted_iota(jnp.int32, (q, LANES), 1) < SSD_HEAD_DIM
        low_half_row = lax.broadcasted_iota(jnp.int32, (1, LANES), 1) < SSD_HEAD_DIM
        return ii >= jj, low_half, low_half_row

    def prologue():
        causal, _, _ = masks()
        dt = dt_s[rows, :]
        d1, d2, d3 = _split3(dt * neg_a)
        tri = causal.astype(BF16)
        acum = _dot(tri, d1) + _dot(tri, d2) + _dot(tri, d3)
        acum_s[...] = acum
        acumt_s[...] = acum.T
        dtt_s[...] = dt.T
        wc_s[...] = dt * jnp.exp(acum[q - 1:q, :] - acum)

    def group_step(g):
        _, low_half, _ = masks()
        b_g = bc_s[rows, g * SSD_STATE:(g + 1) * SSD_STATE]
        c_g = bc_s[rows, (SSD_GROUPS + g) * SSD_STATE:(SSD_GROUPS + g + 1) * SSD_STATE]
        cb_s[g] = _dot_nt(c_g.astype(BF16), b_g.astype(BF16))
        for pr in range(PAIRS_PER_GROUP):
            pair = g * PAIRS_PER_GROUP + pr
            lanes = slice(pair * LANES, (pair + 1) * LANES)
            h0 = pair * HEADS_PER_LANE_TILE
            w_pair = jnp.where(low_half, jnp.broadcast_to(wc_s[:, h0:h0 + 1], (q, LANES)),
                               jnp.broadcast_to(wc_s[:, h0 + 1:h0 + 2], (q, LANES)))
            xw_s[:, lanes] = (xs_s[rows, lanes] * w_pair).astype(BF16)
        cols = slice(g * SSD_GROUP_WIDTH, (g + 1) * SSD_GROUP_WIDTH)
        sg_s[:, cols] = _dot(b_g.T.astype(BF16), xw_s[:, cols])

    def head_step(h):
        causal, _, _ = masks()
        g = h // (SSD_HEADS // SSD_GROUPS)
        c_g = bc_s[rows, (SSD_GROUPS + g) * SSD_STATE:(SSD_GROUPS + g + 1) * SSD_STATE]
        col = jnp.broadcast_to(acum_s[:, h:h + 1], (q, q))
        seg = jnp.where(causal, col - acumt_s[h:h + 1, :], -jnp.inf)
        lhs_s[h, :, 0:q] = (cb_s[g] * jnp.exp(seg) * dtt_s[h:h + 1, :]).astype(BF16)
        lhs_s[h, :, q:2 * q] = (c_g * jnp.exp(col)).astype(BF16)
        cd_s[h:h + 1, :] = jnp.exp(col[q - 1:q, :])

    def pair_step(pair):
        _, low_half, low_half_row = masks()
        lanes = slice(pair * LANES, (pair + 1) * LANES)
        h0 = pair * HEADS_PER_LANE_TILE
        h_prev = hT_s[:, lanes]
        rhs = jnp.concatenate([xs_s[rows, lanes].astype(BF16), h_prev.astype(BF16)], axis=0)
        y_s[rows, lanes] = jnp.where(low_half, _dot(lhs_s[h0], rhs), _dot(lhs_s[h0 + 1], rhs))
        decay = jnp.where(low_half_row, cd_s[h0:h0 + 1, :], cd_s[h0 + 1:h0 + 2, :])
        hT_s[:, lanes] = h_prev * decay + sg_s[:, lanes]

    steps = [(2, prologue)]
    steps += [(3, functools.partial(group_step, g)) for g in range(SSD_GROUPS)]
    steps += [(2, functools.partial(head_step, h)) for h in range(SSD_HEADS)]
    steps += [(1, functools.partial(pair_step, pair)) for pair in range(HEAD_PAIRS)]
    return steps


def _interleave(steps, pieces):
    total = sum(w for w, _ in steps)
    done = acc = 0
    for w, step in steps:
        acc += w
        upto = len(pieces) * acc // total
        for piece in pieces[done:upto]:
            piece()
        done = upto
        step()
    assert done == len(pieces)


def _in_proj_cols(w_refs, lo, hi):
    wa_ref, wdt_ref, wb_ref = w_refs
    if hi <= OFF_DT:
        return wa_ref[:, lo:hi]
    if lo >= OFF_Q:
        return wb_ref[:, lo - OFF_Q:hi - OFF_Q]
    assert (lo, hi) == (OFF_DT, OFF_Q)
    return wdt_ref[...]


def _mixer_kernel(x1_ref, k_ref, v_ref, nw_ref, wa_ref, wdt_ref, wb_ref, lnw_ref, lnb_ref, ws_ref,
                  bst_ref, cw_ref, cb_ref, dtb_ref, alog_ref, dsk_ref, snw_ref,
                  wgm_ref, wssd_ref, wxa_ref, wout_ref, wup_ref, wdn_ref,
                  h_ref, conv_ref, ssm_ref, wup_bf_ref, wdn_bf_ref,
                  xn_s, raw_s, q_s, ext_s, hT_s, xs_s, bc_s, dt_s, y_s, u_s, v_s, gm_s, yn_s, xa_s, merged_s,
                  zs_s, gate_s, qb_s, xprev_s, *ssd_scratch, tiles_per_seq):
    tm = x1_ref.shape[1]
    s = pl.program_id(0)
    t2 = lax.rem(jnp.maximum(s - 1, 0), tiles_per_seq)
    w_in = functools.partial(_in_proj_cols, (wa_ref, wdt_ref, wb_ref))
    head = SUBLANES
    pc = PIECE_COLS

    @pl.when(s == 0)
    def _():
        blk = 2 * SUBLANES

        def zero_rows(i, carry):
            rows = pl.ds(pl.multiple_of(i * blk, blk), blk)
            for buf in (raw_s, ext_s, gate_s, zs_s, qb_s, xprev_s):
                buf[rows, :] = jnp.zeros((blk, buf.shape[1]), buf.dtype)
            return carry

        lax.fori_loop(0, tm // blk, zero_rows, 0)
        ext_s[tm:tm + head, :] = jnp.zeros((head, SSD_CONV_DIM), F32)

    @pl.when(t2 == 0)
    def _():
        ext_s[0:head, :] = jnp.zeros((head, SSD_CONV_DIM), F32)
        hT_s[...] = jnp.zeros(hT_s.shape, F32)

    xn_s[...] = _rms(x1_ref[0], nw_ref[...]).astype(BF16)

    wup_bf_ref[...] = wup_ref[0].astype(BF16)
    wdn_bf_ref[...] = wdn_ref[0].astype(BF16)

    def project(lo, width):
        return _dot(xn_s[...], w_in(lo, lo + width))

    def raw_piece(dst, src, width):
        def run():
            raw_s[:, dst:dst + width] = project(src, width)
        return run

    def q_piece(j):
        def run():
            q_s[:, j:j + pc] = project(OFF_Q + j, pc).astype(BF16)
        return run

    def xbc_piece(j):
        def run():
            ext_s[head:head + tm, j:j + pc] = project(OFF_XBC + j, pc)
        return run

    for j in range(0, SSD_CONV_DIM, pc):
        cols = slice(j, j + pc)
        acc = ext_s[head - 3:head - 3 + tm, cols] * cw_ref[0:1, cols]
        for k in range(1, SSD_CONV):
            acc = acc + ext_s[head - 3 + k:head - 3 + k + tm, cols] * cw_ref[k:k + 1, cols]
        xbc = _silu(acc + cb_ref[:, cols])
        if j < SSD_INNER:
            xs_s[:, cols] = xbc
        else:
            bc_s[:, j - SSD_INNER:j - SSD_INNER + pc] = xbc
        conv_ref[0, :, cols] = ext_s[tm + head - 3:tm + head, cols]
        ext_s[0:head, cols] = ext_s[tm:tm + head, cols]
    dt_s[...] = _softplus(raw_s[:, RAW_DT:RAW_GATE] + dtb_ref[...])
    for j in range(0, GM_WIDTH, pc):
        u_s[:, j:j + pc] = _gelu(raw_s[:, RAW_UV + j:RAW_UV + j + pc])
    v = _gelu(raw_s[:, RAW_UV + GM_WIDTH:RAW_Z])
    v_s[...] = _layernorm(v, lnw_ref[...], lnb_ref[...]).astype(BF16)

    pieces = ([raw_piece(RAW_GATE + j, OFF_GATE + j, pc) for j in range(0, N_BRANCH * D_MODEL, pc)]
              + [raw_piece(RAW_Z + j, OFF_Z + j, pc) for j in range(0, SSD_INNER, pc)]
              + [q_piece(j) for j in range(0, XA_WIDTH, pc)]
              + [xbc_piece(j) for j in range(0, SSD_CONV_DIM, pc)] + [raw_piece(RAW_DT, OFF_DT, DT_PAD)]
              + [raw_piece(RAW_UV + j, OFF_UV + j, pc) for j in range(0, 2 * GM_WIDTH, pc)])
    steps = []
    neg_a = -jnp.exp(alog_ref[...])
    for c in range(tm // SSD_CHUNK):
        rows = slice(c * SSD_CHUNK, (c + 1) * SSD_CHUNK)
        steps += _ssd_chunk_steps(rows, xs_s, bc_s, dt_s, neg_a, hT_s, y_s, ssd_scratch)

    def gmlp_group(g):
        ii = lax.broadcasted_iota(jnp.int32, (GM_CHUNK, GM_CHUNK), 0)
        jj = lax.broadcasted_iota(jnp.int32, (GM_CHUNK, GM_CHUNK), 1)
        cols = slice(g * GM_GROUP_DIM, (g + 1) * GM_GROUP_DIM)
        w_tril = jnp.where(ii >= jj, ws_ref[g], 0.0).astype(BF16)
        bias = bst_ref[:, g:g + 1]
        for c in range(tm // GM_CHUNK):
            rows = slice(c * GM_CHUNK, (c + 1) * GM_CHUNK)
            mixed = _dot(w_tril, v_s[rows, cols]) + bias
            gm_s[rows, cols] = (u_s[rows, cols] * mixed).astype(BF16)

    steps += [(1, functools.partial(gmlp_group, g)) for g in range(GM_GROUPS)]

    def attn_head(hd):
        cols = slice(hd * XA_HEAD_DIM, (hd + 1) * XA_HEAD_DIM)
        sc = _dot_nt(qb_s[:, cols], k_ref[0, :, cols].astype(BF16)) * (XA_HEAD_DIM ** -0.5)
        p = jnp.exp(sc - jnp.max(sc, axis=-1, keepdims=True))
        o = _dot(p.astype(BF16), v_ref[0, :, cols].astype(BF16))
        xa_s[:, cols] = (o / jnp.sum(p, axis=-1, keepdims=True)).astype(BF16)

    steps += [(2, functools.partial(attn_head, hd)) for hd in range(XA_HEADS)]

    def ssd_out_group(g):
        cols = slice(g * SSD_GROUP_WIDTH, (g + 1) * SSD_GROUP_WIDTH)
        yv = (y_s[:, cols] + xs_s[:, cols] * dsk_ref[:, cols]) * zs_s[:, cols]
        yn_s[:, cols] = _rms(yv, snw_ref[:, cols]).astype(BF16)

    steps += [(2, functools.partial(ssd_out_group, g)) for g in range(SSD_GROUPS)]

    def merge_piece(i, br_s, w_ref, j):
        lo = i * D_MODEL + j
        term = gate_s[:, lo:lo + MERGE_COLS] * _dot(br_s[...], w_ref[:, j:j + MERGE_COLS])
        if i == 0:
            merged_s[:, j:j + MERGE_COLS] = term
        else:
            merged_s[:, j:j + MERGE_COLS] += term

    for i, (br_s, w_ref) in enumerate(((gm_s, wgm_ref), (yn_s, wssd_ref), (xa_s, wxa_ref))):
        steps += [(2, functools.partial(merge_piece, i, br_s, w_ref, j)) for j in range(0, D_MODEL, MERGE_COLS)]

    def out_proj():
        h_ref[0] = xprev_s[...] + _dot(merged_s[...].astype(BF16), wout_ref[...])

    steps.append((1, out_proj))
    _interleave(steps, pieces)

    for j in range(0, N_BRANCH * D_MODEL, pc):
        gate_s[:, j:j + pc] = jax.nn.sigmoid(raw_s[:, RAW_GATE + j:RAW_GATE + j + pc])
    for j in range(0, SSD_INNER, pc):
        zs_s[:, j:j + pc] = _silu(raw_s[:, RAW_Z + j:RAW_Z + j + pc])
    qb_s[...] = q_s[...]
    xprev_s[...] = x1_ref[0]

    @pl.when(t2 == tiles_per_seq - 1)
    def _():
        for pair in range(HEAD_PAIRS):
            lanes = slice(pair * LANES, (pair + 1) * LANES)
            ssm_ref[0, lanes, :] = hT_s[:, lanes].T


def _full(shape):
    return pl.BlockSpec(shape, lambda *_: (0,) * len(shape))


def _mixer(x, mem_k, mem_v, p, w_up, w_down, l):
    b, seq, _ = x.shape
    tm = MIXER_ROWS
    assert seq % tm == 0 and tm % SSD_CHUNK == 0
    small = [p["norm_mix_w"], p["w_in_a"], p["w_in_dt"], p["w_in_b"],
             p["gm_ln_w"], p["gm_ln_b"], p["gm_ws"], p["gm_bs_t"],
             p["conv_w"], p["conv_b"], p["dt_bias"], p["a_log"], p["d_skip"], p["ssd_norm_w"],
             p["w_br_gm"], p["w_br_ssd"], p["w_br_xa"], p["w_out"]]
    tps = seq // tm
    n_tiles = b * tps
    tile1 = lambda s: jnp.minimum(s, n_tiles - 1)
    tile2 = lambda s: jnp.maximum(s - 1, 0)
    x_spec = lambda tile: pl.BlockSpec((1, tm, D_MODEL), lambda s: (tile(s) // tps, tile(s) % tps, 0))
    seq_spec = lambda r, c: pl.BlockSpec((1, r, c), lambda s: (tile2(s) // tps, 0, 0))
    up_rows, dn_rows = D_MODEL // n_tiles, D_FF // n_tiles
    assert D_MODEL == up_rows * n_tiles and up_rows % (2 * SUBLANES) == 0
    return pl.pallas_call(
        functools.partial(_mixer_kernel, tiles_per_seq=tps),
        grid=(n_tiles + 1,),
        in_specs=[x_spec(tile1), seq_spec(MEM_LEN, XA_WIDTH), seq_spec(MEM_LEN, XA_WIDTH)]
                 + [_full(a.shape) for a in small]
                 + [pl.BlockSpec((1, up_rows, D_FF), lambda s: (l, tile1(s), 0)),
                    pl.BlockSpec((1, dn_rows, D_MODEL), lambda s: (l, tile1(s), 0))],
        out_specs=[x_spec(tile2), seq_spec(SSD_CONV - 1, SSD_CONV_DIM), seq_spec(SSD_INNER, SSD_STATE),
                   pl.BlockSpec((up_rows, D_FF), lambda s: (tile1(s), 0)),
                   pl.BlockSpec((dn_rows, D_MODEL), lambda s: (tile1(s), 0))],
        out_shape=[
            jax.ShapeDtypeStruct((b, seq, D_MODEL), F32),
            jax.ShapeDtypeStruct((b, SSD_CONV - 1, SSD_CONV_DIM), F32),
            jax.ShapeDtypeStruct((b, SSD_INNER, SSD_STATE), F32),
            jax.ShapeDtypeStruct((D_MODEL, D_FF), BF16),
            jax.ShapeDtypeStruct((D_FF, D_MODEL), BF16),
        ],
        scratch_shapes=[
            pltpu.VMEM((tm, D_MODEL), BF16),
            pltpu.VMEM((tm, RAW_WIDTH), F32),
            pltpu.VMEM((tm, XA_WIDTH), BF16),
            pltpu.VMEM((tm + SUBLANES, SSD_CONV_DIM), F32),
            pltpu.VMEM((SSD_STATE, SSD_INNER), F32),
            pltpu.VMEM((tm, SSD_INNER), F32),
            pltpu.VMEM((tm, 2 * SSD_GROUPS * SSD_STATE), F32),
            pltpu.VMEM((tm, DT_PAD), F32),
            pltpu.VMEM((tm, SSD_INNER), F32),
            pltpu.VMEM((tm, GM_WIDTH), F32),
            pltpu.VMEM((tm, GM_WIDTH), BF16),
            pltpu.VMEM((tm, GM_WIDTH), BF16),
            pltpu.VMEM((tm, SSD_INNER), BF16),
            pltpu.VMEM((tm, XA_WIDTH), BF16),
            pltpu.VMEM((tm, D_MODEL), F32),
            pltpu.VMEM((tm, SSD_INNER), F32),
            pltpu.VMEM((tm, N_BRANCH * D_MODEL), F32),
            pltpu.VMEM((tm, XA_WIDTH), BF16),
            pltpu.VMEM((tm, D_MODEL), F32),
            pltpu.VMEM((SSD_CHUNK, DT_PAD), F32),
            pltpu.VMEM((DT_PAD, SSD_CHUNK), F32),
            pltpu.VMEM((DT_PAD, SSD_CHUNK), F32),
            pltpu.VMEM((SSD_CHUNK, DT_PAD), F32),
            pltpu.VMEM((SSD_GROUPS, SSD_CHUNK, SSD_CHUNK), F32),
            pltpu.VMEM((SSD_HEADS, SSD_CHUNK, 2 * SSD_CHUNK), BF16),
            pltpu.VMEM((SSD_HEADS, SSD_CHUNK), F32),
            pltpu.VMEM((SSD_CHUNK, SSD_INNER), BF16),
            pltpu.VMEM((SSD_STATE, SSD_INNER), F32),
        ],
        compiler_params=pltpu.CompilerParams(
            dimension_semantics=("arbitrary",), vmem_limit_bytes=VMEM_LIMIT_BYTES),
        name="prompt_mixer",
    )(x, mem_k, mem_v, *small, w_up, w_down)


def _ffn_kernel(h_ref, nw_ref, wup_ref, wdn_ref, fw_ref, o_ref):
    h = h_ref[...]
    hn = _rms(h, nw_ref[...]).astype(BF16)
    acc = h
    for j in range(D_FF // FFN_COLS):
        cols = slice(j * FFN_COLS, (j + 1) * FFN_COLS)
        a = jnp.square(jnp.maximum(_dot(hn, wup_ref[:, cols]), 0.0)).astype(BF16)
        acc = acc + _dot(a, wdn_ref[cols, :])
    o_ref[...] = _rms(acc, fw_ref[...])


def _ffn(h2d, p):
    rows = h2d.shape[0]
    tm = min(FFN_ROWS, rows)
    assert rows % tm == 0
    return pl.pallas_call(
        _ffn_kernel,
        grid=(rows // tm,),
        in_specs=[
            pl.BlockSpec((tm, D_MODEL), lambda i: (i, 0)),
            _full((1, D_MODEL)),
            _full((D_MODEL, D_FF)),
            _full((D_FF, D_MODEL)),
            _full((1, D_MODEL)),
        ],
        out_specs=pl.BlockSpec((tm, D_MODEL), lambda i: (i, 0)),
        out_shape=jax.ShapeDtypeStruct((rows, D_MODEL), F32),
        compiler_params=pltpu.CompilerParams(
            dimension_semantics=("arbitrary",), vmem_limit_bytes=VMEM_LIMIT_BYTES),
        name="ffn",
    )(h2d, p["norm_ffn_w"], p["w_up"], p["w_down"], p["norm_final_w"])


def _sample_proj_kernel(x_ref, nw_ref, wa_ref, wdt_ref, wb_ref, o_ref):
    xn = _rms(x_ref[...], nw_ref[...]).astype(BF16)
    o_ref[:, :OFF_DT] = _dot(xn, wa_ref[...])
    o_ref[:, OFF_DT:OFF_Q] = _dot(xn, wdt_ref[...])
    o_ref[:, OFF_Q:] = _dot(xn, wb_ref[...])


def _sample_proj(x2d, p):
    rows = x2d.shape[0]
    args = [x2d, p["norm_mix_w"], p["w_in_a"], p["w_in_dt"], p["w_in_b"]]
    return pl.pallas_call(
        _sample_proj_kernel,
        grid=(1,),
        in_specs=[_full(a.shape) for a in args],
        out_specs=_full((rows, IN_PROJ_PACKED)),
        out_shape=jax.ShapeDtypeStruct((rows, IN_PROJ_PACKED), F32),
        compiler_params=pltpu.CompilerParams(
            dimension_semantics=("arbitrary",), vmem_limit_bytes=VMEM_LIMIT_BYTES),
        name="sample_proj",
    )(*args)


def _sample_state_kernel(proj_ref, stc_ref, ssm_ref, k_ref, v_ref, lnw_ref, lnb_ref, ws_ref,
                         bs_ref, cw_ref, cb_ref, dtb_ref, alog_ref, dsk_ref, snw_ref,
                         gm_ref, yn_ref, xa_ref, convn_ref, ssmn_ref, gv_ref):
    nb = proj_ref.shape[0]
    proj = proj_ref[...]

    uv = _gelu(proj[:, OFF_UV:OFF_Z])
    v = _layernorm(uv[:, GM_WIDTH:], lnw_ref[...], lnb_ref[...])
    gv_ref[...] = v
    for g in range(GM_GROUPS):
        cols = slice(g * GM_GROUP_DIM, (g + 1) * GM_GROUP_DIM)
        mixed = v[:, cols] * ws_ref[g, 0:1, 0:1] + bs_ref[g:g + 1, 0:1]
        gm_ref[:, cols] = uv[:, cols] * mixed

    st = stc_ref[...]
    xnew = proj[:, OFF_XBC:OFF_DT]
    acc = st[:, 0:SSD_CONV_DIM] * cw_ref[0:1, :]
    for k in range(1, SSD_CONV - 1):
        acc = acc + st[:, k * SSD_CONV_DIM:(k + 1) * SSD_CONV_DIM] * cw_ref[k:k + 1, :]
    acc = acc + xnew * cw_ref[SSD_CONV - 1:SSD_CONV, :]
    convn_ref[:, 0:(SSD_CONV - 2) * SSD_CONV_DIM] = st[:, SSD_CONV_DIM:]
    convn_ref[:, (SSD_CONV - 2) * SSD_CONV_DIM:] = xnew
    xbc = _silu(acc + cb_ref[...])
    xs = xbc[:, :SSD_INNER]
    bm = xbc[:, SSD_INNER:SSD_INNER + SSD_GROUPS * SSD_STATE].astype(BF16)
    cm = xbc[:, SSD_INNER + SSD_GROUPS * SSD_STATE:].astype(BF16)

    dt = _softplus(proj[:, OFF_DT:OFF_Q] + dtb_ref[...])
    decay = jnp.exp(dt * -jnp.exp(alog_ref[...]))
    hh = lax.broadcasted_iota(jnp.int32, (DT_PAD, SSD_INNER), 0)
    cc = lax.broadcasted_iota(jnp.int32, (DT_PAD, SSD_INNER), 1)
    spread = (jnp.right_shift(cc, SSD_HEAD_DIM.bit_length() - 1) == hh).astype(BF16)
    dt_wide = sum(_dot(piece, spread) for piece in _split3(dt))
    xdt = xs * dt_wide

    rid = lax.broadcasted_iota(jnp.int32, (nb, 1), 0)
    scale = XA_HEAD_DIM ** -0.5
    kv_vregs = MEM_LEN * XA_HEADS // SUBLANES
    lane_ones = jnp.ones((XA_HEAD_DIM, XA_HEAD_DIM), BF16)
    qv = proj[:, OFF_Q:OFF_GATE]
    y = jnp.zeros((nb, SSD_INNER), F32)
    for bi in range(nb):
        mine = rid == bi
        x_row = jnp.where(mine, xdt, 0.0).astype(BF16)
        parts = []
        for g in range(SSD_GROUPS):
            cols = slice(g * SSD_GROUP_WIDTH, (g + 1) * SSD_GROUP_WIDTH)
            ncols = slice(g * SSD_STATE, (g + 1) * SSD_STATE)
            upd = _dot_tn(x_row[:, cols], bm[:, ncols])
            heads = range(g * SSD_HEADS // SSD_GROUPS, (g + 1) * SSD_HEADS // SSD_GROUPS)
            e_rows = jnp.concatenate(
                [jnp.broadcast_to(decay[bi:bi + 1, h:h + 1], (SSD_HEAD_DIM, SSD_STATE)) for h in heads],
                axis=0)
            h_new = ssm_ref[bi, cols, :] * e_rows + upd
            ssmn_ref[bi, cols, :] = h_new
            parts.append(_dot_nt(cm[:, ncols], h_new.astype(BF16)))
        y = y + jnp.where(mine, jnp.concatenate(parts, axis=-1), 0.0)

        q_rows = jnp.concatenate(
            [qv[bi:bi + 1, hd * XA_HEAD_DIM:(hd + 1) * XA_HEAD_DIM] for hd in range(XA_HEADS)]
            * (SUBLANES // XA_HEADS), axis=0)
        kq = (k_ref[bi].reshape(kv_vregs, SUBLANES, XA_HEAD_DIM) * q_rows[None]).astype(BF16)
        s = _dot(kq.reshape(kv_vregs * SUBLANES, XA_HEAD_DIM), lane_ones) * scale
        s = s.reshape(kv_vregs, SUBLANES, XA_HEAD_DIM)
        pexp = jnp.exp(s - _fold_heads(_reduce_leading(s, jnp.maximum), jnp.maximum)[None])
        o = _reduce_leading(pexp * v_ref[bi].reshape(kv_vregs, SUBLANES, XA_HEAD_DIM), jnp.add)
        o = _fold_heads(o, jnp.add) / _fold_heads(_reduce_leading(pexp, jnp.add), jnp.add)
        for hd in range(XA_HEADS):
            xa_ref[bi:bi + 1, hd * XA_HEAD_DIM:(hd + 1) * XA_HEAD_DIM] = o[hd:hd + 1, :]

    zs = _silu(proj[:, OFF_Z:OFF_XBC])
    yn_ref[...] = _group_rms((y + xs * dsk_ref[...]) * zs, snw_ref[...])


def _sample_state(proj, state_conv2d, state_ssm, cache_k, cache_v, p):
    n = proj.shape[0]
    nb = SAMPLE_BLOCK
    assert n % nb == 0
    conv_w = (SSD_CONV - 1) * SSD_CONV_DIM
    small = [p["gm_ln_w"], p["gm_ln_b"], p["gm_ws"], p["gm_bs"], p["conv_w"], p["conv_b"],
             p["dt_bias"], p["a_log"], p["d_skip"], p["ssd_norm_w"]]
    rows2 = lambda w: pl.BlockSpec((nb, w), lambda i: (i, 0))
    rows3 = lambda a, c: pl.BlockSpec((nb, a, c), lambda i: (i, 0, 0))
    cache = rows3(MEM_LEN * XA_HEADS, XA_HEAD_DIM)
    return pl.pallas_call(
        _sample_state_kernel,
        grid=(n // nb,),
        in_specs=[rows2(IN_PROJ_PACKED), rows2(conv_w), rows3(SSD_INNER, SSD_STATE), cache, cache]
                 + [_full(a.shape) for a in small],
        out_specs=[rows2(GM_WIDTH), rows2(SSD_INNER), rows2(XA_WIDTH), rows2(conv_w),
                   rows3(SSD_INNER, SSD_STATE), rows2(GM_WIDTH)],
        out_shape=[
            jax.ShapeDtypeStruct((n, GM_WIDTH), F32),
            jax.ShapeDtypeStruct((n, SSD_INNER), F32),
            jax.ShapeDtypeStruct((n, XA_WIDTH), F32),
            jax.ShapeDtypeStruct((n, conv_w), F32),
            jax.ShapeDtypeStruct((n, SSD_INNER, SSD_STATE), F32),
            jax.ShapeDtypeStruct((n, GM_WIDTH), F32),
        ],
        compiler_params=pltpu.CompilerParams(
            dimension_semantics=("arbitrary",), vmem_limit_bytes=VMEM_LIMIT_BYTES),
        name="sample_state",
    )(proj, state_conv2d, state_ssm, cache_k, cache_v, *small)


def _sample_tail_kernel(x_ref, proj_ref, gm_ref, yn_ref, xa_ref, wgm_ref, wssd_ref, wxa_ref, wout_ref,
                        nw_ref, wup_ref, wdn_ref, fw_ref, o_ref, hn_s, acc_s):
    j = pl.program_id(0)

    @pl.when(j == 0)
    def _():
        merged = None
        for i, (br_ref, w_ref) in enumerate(((gm_ref, wgm_ref), (yn_ref, wssd_ref), (xa_ref, wxa_ref))):
            gate = jax.nn.sigmoid(proj_ref[:, OFF_GATE + i * D_MODEL:OFF_GATE + (i + 1) * D_MODEL])
            term = gate * _dot(br_ref[...].astype(BF16), w_ref[...])
            merged = term if merged is None else merged + term
        h = x_ref[...] + _dot(merged.astype(BF16), wout_ref[...])
        hn_s[...] = _rms(h, nw_ref[...]).astype(BF16)
        acc_s[...] = h

    a = jnp.square(jnp.maximum(_dot(hn_s[...], wup_ref[...]), 0.0)).astype(BF16)
    acc_s[...] += _dot(a, wdn_ref[...])

    @pl.when(j == pl.num_programs(0) - 1)
    def _():
        o_ref[...] = _rms(acc_s[...], fw_ref[...])


def _sample_tail(x2d, proj, gm, yn, xa, p):
    rows = x2d.shape[0]
    head_args = [x2d, proj, gm, yn, xa, p["w_br_gm"], p["w_br_ssd"], p["w_br_xa"], p["w_out"], p["norm_ffn_w"]]
    return pl.pallas_call(
        _sample_tail_kernel,
        grid=(D_FF // FFN_COLS,),
        in_specs=[_full(a.shape) for a in head_args] + [
            pl.BlockSpec((D_MODEL, FFN_COLS), lambda j: (0, j)),
            pl.BlockSpec((FFN_COLS, D_MODEL), lambda j: (j, 0)),
            _full((1, D_MODEL)),
        ],
        out_specs=_full((rows, D_MODEL)),
        out_shape=jax.ShapeDtypeStruct((rows, D_MODEL), F32),
        scratch_shapes=[pltpu.VMEM((rows, D_MODEL), BF16), pltpu.VMEM((rows, D_MODEL), F32)],
        compiler_params=pltpu.CompilerParams(
            dimension_semantics=("arbitrary",), vmem_limit_bytes=VMEM_LIMIT_BYTES),
        name="sample_tail",
    )(*head_args, p["w_up"], p["w_down"], p["norm_final_w"])


def _pack_w_in_kernel(wt_ref, wa_ref, wdt_ref, wb_ref):
    rest = wb_ref.shape[1]
    for j in range(0, OFF_DT, PACK_COLS):
        wa_ref[:, j:j + PACK_COLS] = wt_ref[0, j:j + PACK_COLS, :].T.astype(BF16)
    dt_rows = wt_ref[0, OFF_DT:OFF_DT + DT_PAD, :]
    is_dt = lax.broadcasted_iota(jnp.int32, dt_rows.shape, 0) < SSD_HEADS
    wdt_ref[...] = jnp.where(is_dt, dt_rows, 0.0).T.astype(BF16)
    base = OFF_DT + SSD_HEADS
    for j in range(0, rest, PACK_COLS):
        wb_ref[:, j:j + PACK_COLS] = wt_ref[0, base + j:base + j + PACK_COLS, :].T.astype(BF16)


def _pack_w_in(w_in, l):
    d, width = w_in.shape[1:]
    rest = width - OFF_DT - SSD_HEADS
    assert rest == IN_PROJ_PACKED - OFF_Q and OFF_DT % PACK_COLS == 0 and rest % PACK_COLS == 0
    w_t = jnp.swapaxes(w_in, 1, 2)
    return pl.pallas_call(
        _pack_w_in_kernel,
        grid=(1,),
        in_specs=[pl.BlockSpec((1, width, d), lambda i: (l, 0, 0))],
        out_specs=[_full((d, OFF_DT)), _full((d, DT_PAD)), _full((d, rest))],
        out_shape=[jax.ShapeDtypeStruct((d, OFF_DT), BF16),
                   jax.ShapeDtypeStruct((d, DT_PAD), BF16),
                   jax.ShapeDtypeStruct((d, rest), BF16)],
        compiler_params=pltpu.CompilerParams(
            dimension_semantics=("arbitrary",), vmem_limit_bytes=VMEM_LIMIT_BYTES),
        name="pack_w_in",
    )(w_t)


def _prep_layer(l, norm_mix_w, w_in, gm_ln_w, gm_ln_b, gm_ws, gm_bs, conv_w, conv_b, dt_bias, a_log,
                d_skip, ssd_norm_w, mem_norm_w, w_mem_k, w_mem_v, w_br_gm, w_br_ssd, w_br_xa, w_out,
                norm_ffn_w, w_up, w_down, norm_final_w):
    row = lambda a: a.reshape(1, -1)
    pad_heads = lambda a: jnp.pad(a, (0, DT_PAD - SSD_HEADS)).reshape(1, DT_PAD)
    w_in_a, w_in_dt, w_in_b = _pack_w_in(w_in, l)
    return {
        "norm_mix_w": row(norm_mix_w[l]),
        "w_in_a": w_in_a, "w_in_dt": w_in_dt, "w_in_b": w_in_b,
        "gm_ln_w": row(gm_ln_w[l]), "gm_ln_b": row(gm_ln_b[l]),
        "gm_ws": gm_ws[l], "gm_bs": gm_bs[l], "gm_bs_t": gm_bs[l].T,
        "conv_w": conv_w[l], "conv_b": row(conv_b[l]),
        "dt_bias": pad_heads(dt_bias[l]), "a_log": pad_heads(a_log[l]),
        "d_skip": row(jnp.repeat(d_skip[l], SSD_HEAD_DIM)), "ssd_norm_w": row(ssd_norm_w[l]),
        "mem_norm_w": row(mem_norm_w[l]),
        "w_kv": jnp.concatenate([w_mem_k[l], w_mem_v[l]], axis=1).astype(BF16),
        "w_br_gm": w_br_gm[l].astype(BF16), "w_br_ssd": w_br_ssd[l].astype(BF16),
        "w_br_xa": w_br_xa[l].astype(BF16), "w_out": w_out[l].astype(BF16),
        "norm_ffn_w": row(norm_ffn_w[l]), "norm_final_w": row(norm_final_w),
    }


def kernel(x_prompt, x_sample, mem_prompt, cache_mem_k, cache_mem_v, state_conv, state_ssm, norm_mix_w, w_in, gm_ln_w, gm_ln_b, gm_ws, gm_bs, conv_w, conv_b, dt_bias, a_log, d_skip, ssd_norm_w, mem_norm_w, w_mem_k, w_mem_v, w_br_gm, w_br_ssd, w_br_xa, w_out, norm_ffn_w, w_up, w_down, norm_final_w):
    depth = w_in.shape[0]
    assert depth == 1, "the final norm is fused into the MLP kernel of the last (only) layer"
    b, seq, _ = x_prompt.shape
    n, dec_seq, _ = x_sample.shape
    assert dec_seq == 1
    p = _prep_layer(0, norm_mix_w, w_in, gm_ln_w, gm_ln_b, gm_ws, gm_bs, conv_w, conv_b, dt_bias,
                    a_log, d_skip, ssd_norm_w, mem_norm_w, w_mem_k, w_mem_v, w_br_gm, w_br_ssd,
                    w_br_xa, w_out, norm_ffn_w, w_up, w_down, norm_final_w)

    mem_k, mem_v = _memkv(mem_prompt.reshape(b * MEM_LEN, D_MODEL), p["mem_norm_w"], p["w_kv"])
    h1, conv_p, ssm_p, p["w_up"], p["w_down"] = _mixer(
        x_prompt, mem_k.reshape(b, MEM_LEN, XA_WIDTH), mem_v.reshape(b, MEM_LEN, XA_WIDTH), p, w_up, w_down, 0)
    y_prompt = _ffn(h1.reshape(b * seq, D_MODEL), p).reshape(b, seq, D_MODEL)

    xs2d = x_sample.reshape(n, D_MODEL)
    proj = _sample_proj(xs2d, p)
    gm, yn, xa, conv_s, ssm_s, gv = _sample_state(
        proj, state_conv[0].reshape(n, (SSD_CONV - 1) * SSD_CONV_DIM), state_ssm[0].reshape(n, SSD_INNER, SSD_STATE),
        cache_mem_k[0].reshape(n, MEM_LEN * XA_HEADS, XA_HEAD_DIM),
        cache_mem_v[0].reshape(n, MEM_LEN * XA_HEADS, XA_HEAD_DIM), p)
    y_sample = _sample_tail(xs2d, proj, gm, yn, xa, p).reshape(n, 1, D_MODEL)

    kv_shape = (1, b, MEM_LEN, XA_HEADS, XA_HEAD_DIM)
    state_shape = (SSD_HEADS, SSD_HEAD_DIM, SSD_STATE)
    return (y_prompt, y_sample,
            mem_k.reshape(kv_shape), mem_v.reshape(kv_shape),
            conv_p.reshape(1, b, SSD_CONV - 1, SSD_CONV_DIM), ssm_p.reshape((1, b) + state_shape),
            conv_s.reshape(1, n, SSD_CONV - 1, SSD_CONV_DIM), ssm_s.reshape((1, n) + state_shape),
            gv.reshape(1, n, 1, GM_WIDTH))
```

```python
import functools

import jax
import jax.numpy as jnp
from jax import lax
from jax.experimental import pallas as pl
from jax.experimental.pallas import tpu as pltpu

F32 = jnp.float32
BF16 = jnp.bfloat16

D_MODEL = 1024
GM_CHUNK = 128
GM_GROUPS = 4
GM_GROUP_DIM = 128
GM_WIDTH = GM_GROUPS * GM_GROUP_DIM
SSD_HEADS = 16
SSD_HEAD_DIM = 64
SSD_INNER = SSD_HEADS * SSD_HEAD_DIM
SSD_GROUPS = 2
SSD_GROUP_WIDTH = SSD_INNER // SSD_GROUPS
SSD_STATE = 128
SSD_CONV = 4
SSD_CHUNK = 128
SSD_CONV_DIM = SSD_INNER + 2 * SSD_GROUPS * SSD_STATE
MEM_LEN = 256
XA_HEADS = 4
XA_HEAD_DIM = 128
XA_WIDTH = XA_HEADS * XA_HEAD_DIM
N_BRANCH = 3
D_FF = 4 * D_MODEL
EPS = 1e-6

SUBLANES = 8
LANES = 128
HEADS_PER_LANE_TILE = LANES // SSD_HEAD_DIM
HEAD_PAIRS = SSD_HEADS // HEADS_PER_LANE_TILE
PAIRS_PER_GROUP = HEAD_PAIRS // SSD_GROUPS

DT_PAD = LANES
OFF_UV = 0
OFF_Z = OFF_UV + 2 * GM_WIDTH
OFF_XBC = OFF_Z + SSD_INNER
OFF_DT = OFF_XBC + SSD_CONV_DIM
OFF_Q = OFF_DT + DT_PAD
OFF_GATE = OFF_Q + XA_WIDTH
IN_PROJ_PACKED = OFF_GATE + N_BRANCH * D_MODEL

RAW_UV = 0
RAW_Z = RAW_UV + 2 * GM_WIDTH
RAW_DT = RAW_Z + SSD_INNER
RAW_GATE = RAW_DT + DT_PAD
RAW_WIDTH = RAW_GATE + N_BRANCH * D_MODEL

MIXER_ROWS = 256
PIECE_COLS = 256
MERGE_COLS = 512
FFN_ROWS = 1024
FFN_COLS = 1024
MEMKV_ROWS = 512
PACK_COLS = 512
SAMPLE_BLOCK = SUBLANES
VMEM_LIMIT_BYTES = 56 * 1024 * 1024


def _dot(a, b):
    return jnp.dot(a, b, preferred_element_type=F32)


def _dot_nt(a, b):
    return lax.dot_general(a, b, (((1,), (1,)), ((), ())), preferred_element_type=F32)


def _dot_tn(a, b):
    return lax.dot_general(a, b, (((0,), (0,)), ((), ())), preferred_element_type=F32)


def _rms(x, w):
    return x * lax.rsqrt(jnp.mean(x * x, axis=-1, keepdims=True) + EPS) * w


def _layernorm(x, w, b):
    xc = x - jnp.mean(x, axis=-1, keepdims=True)
    return xc * lax.rsqrt(jnp.mean(xc * xc, axis=-1, keepdims=True) + EPS) * w + b


def _gelu(x):
    return 0.5 * x * (1.0 + lax.erf(x * 0.7071067811865476))


def _silu(x):
    return x * jax.nn.sigmoid(x)


def _softplus(x):
    return jnp.maximum(x, 0.0) + jnp.log1p(jnp.exp(-jnp.abs(x)))


def _split3(x):
    p1 = x.astype(BF16)
    r1 = x - p1.astype(F32)
    p2 = r1.astype(BF16)
    p3 = (r1 - p2.astype(F32)).astype(BF16)
    return p1, p2, p3


REDUCE_CHAINS = 8


def _reduce_leading(x, op):
    n = x.shape[0]
    assert n % REDUCE_CHAINS == 0
    acc = [x[w] for w in range(REDUCE_CHAINS)]
    for i in range(REDUCE_CHAINS, n, REDUCE_CHAINS):
        acc = [op(a, x[i + w]) for w, a in enumerate(acc)]
    while len(acc) > 1:
        acc = [op(acc[2 * i], acc[2 * i + 1]) for i in range(len(acc) // 2)]
    return acc[0]


def _fold_heads(a, op):
    assert SUBLANES == 2 * XA_HEADS
    return op(a, pltpu.roll(a, XA_HEADS, axis=0))


def _group_rms(y, w):
    parts = []
    for g in range(SSD_GROUPS):
        cols = slice(g * SSD_GROUP_WIDTH, (g + 1) * SSD_GROUP_WIDTH)
        parts.append(_rms(y[:, cols], w[:, cols]))
    return jnp.concatenate(parts, axis=-1)


def _memkv_kernel(mem_ref, nw_ref, wkv_ref, k_ref, v_ref):
    mn = _rms(mem_ref[...], nw_ref[...]).astype(BF16)
    kv = _dot(mn, wkv_ref[...])
    k_ref[...] = kv[:, :XA_WIDTH]
    v_ref[...] = kv[:, XA_WIDTH:]


def _memkv(mem2d, norm_w, wkv):
    rows = mem2d.shape[0]
    tm = min(MEMKV_ROWS, rows)
    return pl.pallas_call(
        _memkv_kernel,
        grid=(rows // tm,),
        in_specs=[
            pl.BlockSpec((tm, D_MODEL), lambda i: (i, 0)),
            pl.BlockSpec((1, D_MODEL), lambda i: (0, 0)),
            pl.BlockSpec((D_MODEL, 2 * XA_WIDTH), lambda i: (0, 0)),
        ],
        out_specs=[
            pl.BlockSpec((tm, XA_WIDTH), lambda i: (i, 0)),
            pl.BlockSpec((tm, XA_WIDTH), lambda i: (i, 0)),
        ],
        out_shape=[jax.ShapeDtypeStruct((rows, XA_WIDTH), F32)] * 2,
        compiler_params=pltpu.CompilerParams(
            dimension_semantics=("arbitrary",), vmem_limit_bytes=VMEM_LIMIT_BYTES),
        name="memkv",
    )(mem2d, norm_w, wkv)


def _ssd_chunk_steps(rows, xs_s, bc_s, dt_s, neg_a, hT_s, y_s, scratch):
    acum_s, acumt_s, dtt_s, wc_s, cb_s, lhs_s, cd_s, xw_s, sg_s = scratch
    q = SSD_CHUNK

    def masks():
        ii = lax.broadcasted_iota(jnp.int32, (q, q), 0)
        jj = lax.broadcasted_iota(jnp.int32, (q, q), 1)
        low_half = lax.broadcasted_iota(jnp.int32, (q, LANES), 1) < SSD_HEAD_DIM
        low_half_row = lax.broadcasted_iota(jnp.int32, (1, LANES), 1) < SSD_HEAD_DIM
        return ii >= jj, low_half, low_half_row

    def prologue():
        causal, _, _ = masks()
        dt = dt_s[rows, :]
        d1, d2, d3 = _split3(dt * neg_a)
        tri = causal.astype(BF16)
        acum = _dot(tri, d1) + _dot(tri, d2) + _dot(tri, d3)
        acum_s[...] = acum
        acumt_s[...] = acum.T
        dtt_s[...] = dt.T
        wc_s[...] = dt * jnp.exp(acum[q - 1:q, :] - acum)

    def group_step(g):
        _, low_half, _ = masks()
        b_g = bc_s[rows, g * SSD_STATE:(g + 1) * SSD_STATE]
        c_g = bc_s[rows, (SSD_GROUPS + g) * SSD_STATE:(SSD_GROUPS + g + 1) * SSD_STATE]
        cb_s[g] = _dot_nt(c_g.astype(BF16), b_g.astype(BF16))
        for pr in range(PAIRS_PER_GROUP):
            pair = g * PAIRS_PER_GROUP + pr
            lanes = slice(pair * LANES, (pair + 1) * LANES)
            h0 = pair * HEADS_PER_LANE_TILE
            w_pair = jnp.where(low_half, jnp.broadcast_to(wc_s[:, h0:h0 + 1], (q, LANES)),
                               jnp.broadcast_to(wc_s[:, h0 + 1:h0 + 2], (q, LANES)))
            xw_s[:, lanes] = (xs_s[rows, lanes] * w_pair).astype(BF16)
        cols = slice(g * SSD_GROUP_WIDTH, (g + 1) * SSD_GROUP_WIDTH)
        sg_s[:, cols] = _dot(b_g.T.astype(BF16), xw_s[:, cols])

    def head_step(h):
        causal, _, _ = masks()
        g = h // (SSD_HEADS // SSD_GROUPS)
        c_g = bc_s[rows, (SSD_GROUPS + g) * SSD_STATE:(SSD_GROUPS + g + 1) * SSD_STATE]
        col = jnp.broadcast_to(acum_s[:, h:h + 1], (q, q))
        seg = jnp.where(causal, col - acumt_s[h:h + 1, :], -jnp.inf)
        lhs_s[h, :, 0:q] = (cb_s[g] * jnp.exp(seg) * dtt_s[h:h + 1, :]).astype(BF16)
        lhs_s[h, :, q:2 * q] = (c_g * jnp.exp(col)).astype(BF16)
        cd_s[h:h + 1, :] = jnp.exp(col[q - 1:q, :])

    def pair_step(pair):
        _, low_half, low_half_row = masks()
        lanes = slice(pair * LANES, (pair + 1) * LANES)
        h0 = pair * HEADS_PER_LANE_TILE
        h_prev = hT_s[:, lanes]
        rhs = jnp.concatenate([xs_s[rows, lanes].astype(BF16), h_prev.astype(BF16)], axis=0)
        y_s[rows, lanes] = jnp.where(low_half, _dot(lhs_s[h0], rhs), _dot(lhs_s[h0 + 1], rhs))
        decay = jnp.where(low_half_row, cd_s[h0:h0 + 1, :], cd_s[h0 + 1:h0 + 2, :])
        hT_s[:, lanes] = h_prev * decay + sg_s[:, lanes]

    steps = [(2, prologue)]
    steps += [(3, functools.partial(group_step, g)) for g in range(SSD_GROUPS)]
    steps += [(2, functools.partial(head_step, h)) for h in range(SSD_HEADS)]
    steps += [(1, functools.partial(pair_step, pair)) for pair in range(HEAD_PAIRS)]
    return steps


def _interleave(steps, pieces):
    total = sum(w for w, _ in steps)
    done = acc = 0
    for w, step in steps:
        acc += w
        upto = len(pieces) * acc // total
        for piece in pieces[done:upto]:
            piece()
        done = upto
        step()
    assert done == len(pieces)


def _in_proj_cols(w_refs, lo, hi):
    wa_ref, wdt_ref, wb_ref = w_refs
    if hi <= OFF_DT:
        return wa_ref[:, lo:hi]
    if lo >= OFF_Q:
        return wb_ref[:, lo - OFF_Q:hi - OFF_Q]
    assert (lo, hi) == (OFF_DT, OFF_Q)
    return wdt_ref[...]


def _mixer_kernel(x1_ref, k_ref, v_ref, nw_ref, wa_ref, wdt_ref, wb_ref, lnw_ref, lnb_ref, ws_ref,
                  bst_ref, cw_ref, cb_ref, dtb_ref, alog_ref, dsk_ref, snw_ref,
                  wgm_ref, wssd_ref, wxa_ref, wout_ref, wup_ref, wdn_ref,
                  h_ref, conv_ref, ssm_ref, wup_bf_ref, wdn_bf_ref, krow_ref, vrow_ref,
                  xn_s, raw_s, q_s, ext_s, hT_s, xs_s, bc_s, dt_s, y_s, u_s, v_s, gm_s, yn_s, xa_s, merged_s,
                  zs_s, gate_s, qb_s, xprev_s, *ssd_scratch, tiles_per_seq):
    tm = x1_ref.shape[1]
    s = pl.program_id(0)
    t2 = lax.rem(jnp.maximum(s - 1, 0), tiles_per_seq)
    w_in = functools.partial(_in_proj_cols, (wa_ref, wdt_ref, wb_ref))
    head = SUBLANES
    pc = PIECE_COLS

    @pl.when(s == 0)
    def _():
        blk = 2 * SUBLANES

        def zero_rows(i, carry):
            rows = pl.ds(pl.multiple_of(i * blk, blk), blk)
            for buf in (raw_s, ext_s, gate_s, zs_s, qb_s, xprev_s):
                buf[rows, :] = jnp.zeros((blk, buf.shape[1]), buf.dtype)
            return carry

        lax.fori_loop(0, tm // blk, zero_rows, 0)
        ext_s[tm:tm + head, :] = jnp.zeros((head, SSD_CONV_DIM), F32)

    @pl.when(t2 == 0)
    def _():
        ext_s[0:head, :] = jnp.zeros((head, SSD_CONV_DIM), F32)
        hT_s[...] = jnp.zeros(hT_s.shape, F32)
        for hd in range(XA_HEADS):
            cols = slice(hd * XA_HEAD_DIM, (hd + 1) * XA_HEAD_DIM)
            krow_ref[0, pl.ds(hd, MEM_LEN, stride=XA_HEADS), :] = k_ref[0, :, cols]
            vrow_ref[0, pl.ds(hd, MEM_LEN, stride=XA_HEADS), :] = v_ref[0, :, cols]

    xn_s[...] = _rms(x1_ref[0], nw_ref[...]).astype(BF16)

    wup_bf_ref[...] = wup_ref[0].astype(BF16)
    wdn_bf_ref[...] = wdn_ref[0].astype(BF16)

    def project(lo, width):
        return _dot(xn_s[...], w_in(lo, lo + width))

    def raw_piece(dst, src, width):
        def run():
            raw_s[:, dst:dst + width] = project(src, width)
        return run

    def q_piece(j):
        def run():
            q_s[:, j:j + pc] = project(OFF_Q + j, pc).astype(BF16)
        return run

    def xbc_piece(j):
        def run():
            ext_s[head:head + tm, j:j + pc] = project(OFF_XBC + j, pc)
        return run

    for j in range(0, SSD_CONV_DIM, pc):
        cols = slice(j, j + pc)
        acc = ext_s[head - 3:head - 3 + tm, cols] * cw_ref[0:1, cols]
        for k in range(1, SSD_CONV):
            acc = acc + ext_s[head - 3 + k:head - 3 + k + tm, cols] * cw_ref[k:k + 1, cols]
        xbc = _silu(acc + cb_ref[:, cols])
        if j < SSD_INNER:
            xs_s[:, cols] = xbc
        else:
            bc_s[:, j - SSD_INNER:j - SSD_INNER + pc] = xbc
        conv_ref[0, :, cols] = ext_s[tm + head - 3:tm + head, cols]
        ext_s[0:head, cols] = ext_s[tm:tm + head, cols]
    dt_s[...] = _softplus(raw_s[:, RAW_DT:RAW_GATE] + dtb_ref[...])
    for j in range(0, GM_WIDTH, pc):
        u_s[:, j:j + pc] = _gelu(raw_s[:, RAW_UV + j:RAW_UV + j + pc])
    v = _gelu(raw_s[:, RAW_UV + GM_WIDTH:RAW_Z])
    v_s[...] = _layernorm(v, lnw_ref[...], lnb_ref[...]).astype(BF16)

    pieces = ([raw_piece(RAW_GATE + j, OFF_GATE + j, pc) for j in range(0, N_BRANCH * D_MODEL, pc)]
              + [raw_piece(RAW_Z + j, OFF_Z + j, pc) for j in range(0, SSD_INNER, pc)]
              + [q_piece(j) for j in range(0, XA_WIDTH, pc)]
              + [xbc_piece(j) for j in range(0, SSD_CONV_DIM, pc)] + [raw_piece(RAW_DT, OFF_DT, DT_PAD)]
              + [raw_piece(RAW_UV + j, OFF_UV + j, pc) for j in range(0, 2 * GM_WIDTH, pc)])
    steps = []
    neg_a = -jnp.exp(alog_ref[...])
    for c in range(tm // SSD_CHUNK):
        rows = slice(c * SSD_CHUNK, (c + 1) * SSD_CHUNK)
        steps += _ssd_chunk_steps(rows, xs_s, bc_s, dt_s, neg_a, hT_s, y_s, ssd_scratch)

    def gmlp_group(g):
        ii = lax.broadcasted_iota(jnp.int32, (GM_CHUNK, GM_CHUNK), 0)
        jj = lax.broadcasted_iota(jnp.int32, (GM_CHUNK, GM_CHUNK), 1)
        cols = slice(g * GM_GROUP_DIM, (g + 1) * GM_GROUP_DIM)
        w_tril = jnp.where(ii >= jj, ws_ref[g], 0.0).astype(BF16)
        bias = bst_ref[:, g:g + 1]
        for c in range(tm // GM_CHUNK):
            rows = slice(c * GM_CHUNK, (c + 1) * GM_CHUNK)
            mixed = _dot(w_tril, v_s[rows, cols]) + bias
            gm_s[rows, cols] = (u_s[rows, cols] * mixed).astype(BF16)

    steps += [(1, functools.partial(gmlp_group, g)) for g in range(GM_GROUPS)]

    def attn_head(hd):
        cols = slice(hd * XA_HEAD_DIM, (hd + 1) * XA_HEAD_DIM)
        sc = _dot_nt(qb_s[:, cols], k_ref[0, :, cols].astype(BF16)) * (XA_HEAD_DIM ** -0.5)
        p = jnp.exp(sc - jnp.max(sc, axis=-1, keepdims=True))
        o = _dot(p.astype(BF16), v_ref[0, :, cols].astype(BF16))
        xa_s[:, cols] = (o / jnp.sum(p, axis=-1, keepdims=True)).astype(BF16)

    steps += [(2, functools.partial(attn_head, hd)) for hd in range(XA_HEADS)]

    def ssd_out_group(g):
        cols = slice(g * SSD_GROUP_WIDTH, (g + 1) * SSD_GROUP_WIDTH)
        yv = (y_s[:, cols] + xs_s[:, cols] * dsk_ref[:, cols]) * zs_s[:, cols]
        yn_s[:, cols] = _rms(yv, snw_ref[:, cols]).astype(BF16)

    steps += [(2, functools.partial(ssd_out_group, g)) for g in range(SSD_GROUPS)]

    def merge_piece(i, br_s, w_ref, j):
        lo = i * D_MODEL + j
        term = gate_s[:, lo:lo + MERGE_COLS] * _dot(br_s[...], w_ref[:, j:j + MERGE_COLS])
        if i == 0:
            merged_s[:, j:j + MERGE_COLS] = term
        else:
            merged_s[:, j:j + MERGE_COLS] += term

    for i, (br_s, w_ref) in enumerate(((gm_s, wgm_ref), (yn_s, wssd_ref), (xa_s, wxa_ref))):
        steps += [(2, functools.partial(merge_piece, i, br_s, w_ref, j)) for j in range(0, D_MODEL, MERGE_COLS)]

    def out_proj():
        h_ref[0] = xprev_s[...] + _dot(merged_s[...].astype(BF16), wout_ref[...])

    steps.append((1, out_proj))
    _interleave(steps, pieces)

    for j in range(0, N_BRANCH * D_MODEL, pc):
        gate_s[:, j:j + pc] = jax.nn.sigmoid(raw_s[:, RAW_GATE + j:RAW_GATE + j + pc])
    for j in range(0, SSD_INNER, pc):
        zs_s[:, j:j + pc] = _silu(raw_s[:, RAW_Z + j:RAW_Z + j + pc])
    qb_s[...] = q_s[...]
    xprev_s[...] = x1_ref[0]

    @pl.when(t2 == tiles_per_seq - 1)
    def _():
        for pair in range(HEAD_PAIRS):
            lanes = slice(pair * LANES, (pair + 1) * LANES)
            ssm_ref[0, lanes, :] = hT_s[:, lanes].T


def _full(shape):
    return pl.BlockSpec(shape, lambda *_: (0,) * len(shape))


def _mixer(x, mem_k, mem_v, p, w_up, w_down, l):
    b, seq, _ = x.shape
    tm = MIXER_ROWS
    assert seq % tm == 0 and tm % SSD_CHUNK == 0
    small = [p["norm_mix_w"], p["w_in_a"], p["w_in_dt"], p["w_in_b"],
             p["gm_ln_w"], p["gm_ln_b"], p["gm_ws"], p["gm_bs_t"],
             p["conv_w"], p["conv_b"], p["dt_bias"], p["a_log"], p["d_skip"], p["ssd_norm_w"],
             p["w_br_gm"], p["w_br_ssd"], p["w_br_xa"], p["w_out"]]
    tps = seq // tm
    n_tiles = b * tps
    tile1 = lambda s: jnp.minimum(s, n_tiles - 1)
    tile2 = lambda s: jnp.maximum(s - 1, 0)
    x_spec = lambda tile: pl.BlockSpec((1, tm, D_MODEL), lambda s: (tile(s) // tps, tile(s) % tps, 0))
    seq_spec = lambda r, c: pl.BlockSpec((1, r, c), lambda s: (tile2(s) // tps, 0, 0))
    up_rows, dn_rows = D_MODEL // n_tiles, D_FF // n_tiles
    assert D_MODEL == up_rows * n_tiles and up_rows % (2 * SUBLANES) == 0
    return pl.pallas_call(
        functools.partial(_mixer_kernel, tiles_per_seq=tps),
        grid=(n_tiles + 1,),
        in_specs=[x_spec(tile1), seq_spec(MEM_LEN, XA_WIDTH), seq_spec(MEM_LEN, XA_WIDTH)]
                 + [_full(a.shape) for a in small]
                 + [pl.BlockSpec((1, up_rows, D_FF), lambda s: (l, tile1(s), 0)),
                    pl.BlockSpec((1, dn_rows, D_MODEL), lambda s: (l, tile1(s), 0))],
        out_specs=[x_spec(tile2), seq_spec(SSD_CONV - 1, SSD_CONV_DIM), seq_spec(SSD_INNER, SSD_STATE),
                   pl.BlockSpec((up_rows, D_FF), lambda s: (tile1(s), 0)),
                   pl.BlockSpec((dn_rows, D_MODEL), lambda s: (tile1(s), 0)),
                   seq_spec(MEM_LEN * XA_HEADS, XA_HEAD_DIM), seq_spec(MEM_LEN * XA_HEADS, XA_HEAD_DIM)],
        out_shape=[
            jax.ShapeDtypeStruct((b, seq, D_MODEL), F32),
            jax.ShapeDtypeStruct((b, SSD_CONV - 1, SSD_CONV_DIM), F32),
            jax.ShapeDtypeStruct((b, SSD_INNER, SSD_STATE), F32),
            jax.ShapeDtypeStruct((D_MODEL, D_FF), BF16),
            jax.ShapeDtypeStruct((D_FF, D_MODEL), BF16),
            jax.ShapeDtypeStruct((b, MEM_LEN * XA_HEADS, XA_HEAD_DIM), F32),
            jax.ShapeDtypeStruct((b, MEM_LEN * XA_HEADS, XA_HEAD_DIM), F32),
        ],
        scratch_shapes=[
            pltpu.VMEM((tm, D_MODEL), BF16),
            pltpu.VMEM((tm, RAW_WIDTH), F32),
            pltpu.VMEM((tm, XA_WIDTH), BF16),
            pltpu.VMEM((tm + SUBLANES, SSD_CONV_DIM), F32),
            pltpu.VMEM((SSD_STATE, SSD_INNER), F32),
            pltpu.VMEM((tm, SSD_INNER), F32),
            pltpu.VMEM((tm, 2 * SSD_GROUPS * SSD_STATE), F32),
            pltpu.VMEM((tm, DT_PAD), F32),
            pltpu.VMEM((tm, SSD_INNER), F32),
            pltpu.VMEM((tm, GM_WIDTH), F32),
            pltpu.VMEM((tm, GM_WIDTH), BF16),
            pltpu.VMEM((tm, GM_WIDTH), BF16),
            pltpu.VMEM((tm, SSD_INNER), BF16),
            pltpu.VMEM((tm, XA_WIDTH), BF16),
            pltpu.VMEM((tm, D_MODEL), F32),
            pltpu.VMEM((tm, SSD_INNER), F32),
            pltpu.VMEM((tm, N_BRANCH * D_MODEL), F32),
            pltpu.VMEM((tm, XA_WIDTH), BF16),
            pltpu.VMEM((tm, D_MODEL), F32),
            pltpu.VMEM((SSD_CHUNK, DT_PAD), F32),
            pltpu.VMEM((DT_PAD, SSD_CHUNK), F32),
            pltpu.VMEM((DT_PAD, SSD_CHUNK), F32),
            pltpu.VMEM((SSD_CHUNK, DT_PAD), F32),
            pltpu.VMEM((SSD_GROUPS, SSD_CHUNK, SSD_CHUNK), F32),
            pltpu.VMEM((SSD_HEADS, SSD_CHUNK, 2 * SSD_CHUNK), BF16),
            pltpu.VMEM((SSD_HEADS, SSD_CHUNK), F32),
            pltpu.VMEM((SSD_CHUNK, SSD_INNER), BF16),
            pltpu.VMEM((SSD_STATE, SSD_INNER), F32),
        ],
        compiler_params=pltpu.CompilerParams(
            dimension_semantics=("arbitrary",), vmem_limit_bytes=VMEM_LIMIT_BYTES),
        name="prompt_mixer",
    )(x, mem_k, mem_v, *small, w_up, w_down)


def _ffn_kernel(h_ref, nw_ref, wup_ref, wdn_ref, fw_ref, o_ref):
    h = h_ref[...]
    hn = _rms(h, nw_ref[...]).astype(BF16)
    acc = h
    for j in range(D_FF // FFN_COLS):
        cols = slice(j * FFN_COLS, (j + 1) * FFN_COLS)
        a = jnp.square(jnp.maximum(_dot(hn, wup_ref[:, cols]), 0.0)).astype(BF16)
        acc = acc + _dot(a, wdn_ref[cols, :])
    o_ref[...] = _rms(acc, fw_ref[...])


def _ffn(h2d, p):
    rows = h2d.shape[0]
    tm = min(FFN_ROWS, rows)
    assert rows % tm == 0
    return pl.pallas_call(
        _ffn_kernel,
        grid=(rows // tm,),
        in_specs=[
            pl.BlockSpec((tm, D_MODEL), lambda i: (i, 0)),
            _full((1, D_MODEL)),
            _full((D_MODEL, D_FF)),
            _full((D_FF, D_MODEL)),
            _full((1, D_MODEL)),
        ],
        out_specs=pl.BlockSpec((tm, D_MODEL), lambda i: (i, 0)),
        out_shape=jax.ShapeDtypeStruct((rows, D_MODEL), F32),
        compiler_params=pltpu.CompilerParams(
            dimension_semantics=("arbitrary",), vmem_limit_bytes=VMEM_LIMIT_BYTES),
        name="ffn",
    )(h2d, p["norm_ffn_w"], p["w_up"], p["w_down"], p["norm_final_w"])


def _sample_proj_kernel(x_ref, nw_ref, wa_ref, wdt_ref, wb_ref, o_ref):
    xn = _rms(x_ref[...], nw_ref[...]).astype(BF16)
    o_ref[:, :OFF_DT] = _dot(xn, wa_ref[...])
    o_ref[:, OFF_DT:OFF_Q] = _dot(xn, wdt_ref[...])
    o_ref[:, OFF_Q:] = _dot(xn, wb_ref[...])


def _sample_proj(x2d, p):
    rows = x2d.shape[0]
    args = [x2d, p["norm_mix_w"], p["w_in_a"], p["w_in_dt"], p["w_in_b"]]
    return pl.pallas_call(
        _sample_proj_kernel,
        grid=(1,),
        in_specs=[_full(a.shape) for a in args],
        out_specs=_full((rows, IN_PROJ_PACKED)),
        out_shape=jax.ShapeDtypeStruct((rows, IN_PROJ_PACKED), F32),
        compiler_params=pltpu.CompilerParams(
            dimension_semantics=("arbitrary",), vmem_limit_bytes=VMEM_LIMIT_BYTES),
        name="sample_proj",
    )(*args)


def _sample_state_kernel(proj_ref, stc_ref, ssm_ref, k_ref, v_ref, lnw_ref, lnb_ref, ws_ref,
                         bs_ref, cw_ref, cb_ref, dtb_ref, alog_ref, dsk_ref, snw_ref,
                         gm_ref, yn_ref, xa_ref, convn_ref, ssmn_ref, gv_ref):
    nb = proj_ref.shape[0]
    proj = proj_ref[...]

    uv = _gelu(proj[:, OFF_UV:OFF_Z])
    v = _layernorm(uv[:, GM_WIDTH:], lnw_ref[...], lnb_ref[...])
    gv_ref[...] = v
    for g in range(GM_GROUPS):
        cols = slice(g * GM_GROUP_DIM, (g + 1) * GM_GROUP_DIM)
        mixed = v[:, cols] * ws_ref[g, 0:1, 0:1] + bs_ref[g:g + 1, 0:1]
        gm_ref[:, cols] = uv[:, cols] * mixed

    st = stc_ref[...]
    xnew = proj[:, OFF_XBC:OFF_DT]
    acc = st[:, 0:SSD_CONV_DIM] * cw_ref[0:1, :]
    for k in range(1, SSD_CONV - 1):
        acc = acc + st[:, k * SSD_CONV_DIM:(k + 1) * SSD_CONV_DIM] * cw_ref[k:k + 1, :]
    acc = acc + xnew * cw_ref[SSD_CONV - 1:SSD_CONV, :]
    convn_ref[:, 0:(SSD_CONV - 2) * SSD_CONV_DIM] = st[:, SSD_CONV_DIM:]
    convn_ref[:, (SSD_CONV - 2) * SSD_CONV_DIM:] = xnew
    xbc = _silu(acc + cb_ref[...])
    xs = xbc[:, :SSD_INNER]
    bm = xbc[:, SSD_INNER:SSD_INNER + SSD_GROUPS * SSD_STATE].astype(BF16)
    cm = xbc[:, SSD_INNER + SSD_GROUPS * SSD_STATE:].astype(BF16)

    dt = _softplus(proj[:, OFF_DT:OFF_Q] + dtb_ref[...])
    decay = jnp.exp(dt * -jnp.exp(alog_ref[...]))
    hh = lax.broadcasted_iota(jnp.int32, (DT_PAD, SSD_INNER), 0)
    cc = lax.broadcasted_iota(jnp.int32, (DT_PAD, SSD_INNER), 1)
    spread = (jnp.right_shift(cc, SSD_HEAD_DIM.bit_length() - 1) == hh).astype(BF16)
    dt_wide = sum(_dot(piece, spread) for piece in _split3(dt))
    xdt = xs * dt_wide

    rid = lax.broadcasted_iota(jnp.int32, (nb, 1), 0)
    scale = XA_HEAD_DIM ** -0.5
    kv_vregs = MEM_LEN * XA_HEADS // SUBLANES
    lane_ones = jnp.ones((XA_HEAD_DIM, XA_HEAD_DIM), BF16)
    qv = proj[:, OFF_Q:OFF_GATE]
    y = jnp.zeros((nb, SSD_INNER), F32)
    for bi in range(nb):
        mine = rid == bi
        x_row = jnp.where(mine, xdt, 0.0).astype(BF16)
        parts = []
        for g in range(SSD_GROUPS):
            cols = slice(g * SSD_GROUP_WIDTH, (g + 1) * SSD_GROUP_WIDTH)
            ncols = slice(g * SSD_STATE, (g + 1) * SSD_STATE)
            upd = _dot_tn(x_row[:, cols], bm[:, ncols])
            heads = range(g * SSD_HEADS // SSD_GROUPS, (g + 1) * SSD_HEADS // SSD_GROUPS)
            e_rows = jnp.concatenate(
                [jnp.broadcast_to(decay[bi:bi + 1, h:h + 1], (SSD_HEAD_DIM, SSD_STATE)) for h in heads],
                axis=0)
            h_new = ssm_ref[bi, cols, :] * e_rows + upd
            ssmn_ref[bi, cols, :] = h_new
            parts.append(_dot_nt(cm[:, ncols], h_new.astype(BF16)))
        y = y + jnp.where(mine, jnp.concatenate(parts, axis=-1), 0.0)

        q_rows = jnp.concatenate(
            [qv[bi:bi + 1, hd * XA_HEAD_DIM:(hd + 1) * XA_HEAD_DIM] for hd in range(XA_HEADS)]
            * (SUBLANES // XA_HEADS), axis=0)
        kq = (k_ref[bi].reshape(kv_vregs, SUBLANES, XA_HEAD_DIM) * q_rows[None]).astype(BF16)
        s = _dot(kq.reshape(kv_vregs * SUBLANES, XA_HEAD_DIM), lane_ones) * scale
        s = s.reshape(kv_vregs, SUBLANES, XA_HEAD_DIM)
        pexp = jnp.exp(s - _fold_heads(_reduce_leading(s, jnp.maximum), jnp.maximum)[None])
        o = _reduce_leading(pexp * v_ref[bi].reshape(kv_vregs, SUBLANES, XA_HEAD_DIM), jnp.add)
        o = _fold_heads(o, jnp.add) / _fold_heads(_reduce_leading(pexp, jnp.add), jnp.add)
        for hd in range(XA_HEADS):
            xa_ref[bi:bi + 1, hd * XA_HEAD_DIM:(hd + 1) * XA_HEAD_DIM] = o[hd:hd + 1, :]

    zs = _silu(proj[:, OFF_Z:OFF_XBC])
    yn_ref[...] = _group_rms((y + xs * dsk_ref[...]) * zs, snw_ref[...])


def _sample_state(proj, state_conv2d, state_ssm, cache_k, cache_v, p):
    n = proj.shape[0]
    nb = SAMPLE_BLOCK
    assert n % nb == 0
    conv_w = (SSD_CONV - 1) * SSD_CONV_DIM
    small = [p["gm_ln_w"], p["gm_ln_b"], p["gm_ws"], p["gm_bs"], p["conv_w"], p["conv_b"],
             p["dt_bias"], p["a_log"], p["d_skip"], p["ssd_norm_w"]]
    rows2 = lambda w: pl.BlockSpec((nb, w), lambda i: (i, 0))
    rows3 = lambda a, c: pl.BlockSpec((nb, a, c), lambda i: (i, 0, 0))
    cache = rows3(MEM_LEN * XA_HEADS, XA_HEAD_DIM)
    return pl.pallas_call(
        _sample_state_kernel,
        grid=(n // nb,),
        in_specs=[rows2(IN_PROJ_PACKED), rows2(conv_w), rows3(SSD_INNER, SSD_STATE), cache, cache]
                 + [_full(a.shape) for a in small],
        out_specs=[rows2(GM_WIDTH), rows2(SSD_INNER), rows2(XA_WIDTH), rows2(conv_w),
                   rows3(SSD_INNER, SSD_STATE), rows2(GM_WIDTH)],
        out_shape=[
            jax.ShapeDtypeStruct((n, GM_WIDTH), F32),
            jax.ShapeDtypeStruct((n, SSD_INNER), F32),
            jax.ShapeDtypeStruct((n, XA_WIDTH), F32),
            jax.ShapeDtypeStruct((n, conv_w), F32),
            jax.ShapeDtypeStruct((n, SSD_INNER, SSD_STATE), F32),
            jax.ShapeDtypeStruct((n, GM_WIDTH), F32),
        ],
        compiler_params=pltpu.CompilerParams(
            dimension_semantics=("arbitrary",), vmem_limit_bytes=VMEM_LIMIT_BYTES),
        name="sample_state",
    )(proj, state_conv2d, state_ssm, cache_k, cache_v, *small)


def _sample_tail_kernel(x_ref, proj_ref, gm_ref, yn_ref, xa_ref, wgm_ref, wssd_ref, wxa_ref, wout_ref,
                        nw_ref, wup_ref, wdn_ref, fw_ref, o_ref, hn_s, acc_s):
    j = pl.program_id(0)

    @pl.when(j == 0)
    def _():
        merged = None
        for i, (br_ref, w_ref) in enumerate(((gm_ref, wgm_ref), (yn_ref, wssd_ref), (xa_ref, wxa_ref))):
            gate = jax.nn.sigmoid(proj_ref[:, OFF_GATE + i * D_MODEL:OFF_GATE + (i + 1) * D_MODEL])
            term = gate * _dot(br_ref[...].astype(BF16), w_ref[...])
            merged = term if merged is None else merged + term
        h = x_ref[...] + _dot(merged.astype(BF16), wout_ref[...])
        hn_s[...] = _rms(h, nw_ref[...]).astype(BF16)
        acc_s[...] = h

    a = jnp.square(jnp.maximum(_dot(hn_s[...], wup_ref[...]), 0.0)).astype(BF16)
    acc_s[...] += _dot(a, wdn_ref[...])

    @pl.when(j == pl.num_programs(0) - 1)
    def _():
        o_ref[...] = _rms(acc_s[...], fw_ref[...])


def _sample_tail(x2d, proj, gm, yn, xa, p):
    rows = x2d.shape[0]
    head_args = [x2d, proj, gm, yn, xa, p["w_br_gm"], p["w_br_ssd"], p["w_br_xa"], p["w_out"], p["norm_ffn_w"]]
    return pl.pallas_call(
        _sample_tail_kernel,
        grid=(D_FF // FFN_COLS,),
        in_specs=[_full(a.shape) for a in head_args] + [
            pl.BlockSpec((D_MODEL, FFN_COLS), lambda j: (0, j)),
            pl.BlockSpec((FFN_COLS, D_MODEL), lambda j: (j, 0)),
            _full((1, D_MODEL)),
        ],
        out_specs=_full((rows, D_MODEL)),
        out_shape=jax.ShapeDtypeStruct((rows, D_MODEL), F32),
        scratch_shapes=[pltpu.VMEM((rows, D_MODEL), BF16), pltpu.VMEM((rows, D_MODEL), F32)],
        compiler_params=pltpu.CompilerParams(
            dimension_semantics=("arbitrary",), vmem_limit_bytes=VMEM_LIMIT_BYTES),
        name="sample_tail",
    )(*head_args, p["w_up"], p["w_down"], p["norm_final_w"])


def _pack_w_in_kernel(wt_ref, wa_ref, wdt_ref, wb_ref):
    rest = wb_ref.shape[1]
    for j in range(0, OFF_DT, PACK_COLS):
        wa_ref[:, j:j + PACK_COLS] = wt_ref[0, j:j + PACK_COLS, :].T.astype(BF16)
    dt_rows = wt_ref[0, OFF_DT:OFF_DT + DT_PAD, :]
    is_dt = lax.broadcasted_iota(jnp.int32, dt_rows.shape, 0) < SSD_HEADS
    wdt_ref[...] = jnp.where(is_dt, dt_rows, 0.0).T.astype(BF16)
    base = OFF_DT + SSD_HEADS
    for j in range(0, rest, PACK_COLS):
        wb_ref[:, j:j + PACK_COLS] = wt_ref[0, base + j:base + j + PACK_COLS, :].T.astype(BF16)


def _pack_w_in(w_in, l):
    d, width = w_in.shape[1:]
    rest = width - OFF_DT - SSD_HEADS
    assert rest == IN_PROJ_PACKED - OFF_Q and OFF_DT % PACK_COLS == 0 and rest % PACK_COLS == 0
    w_t = jnp.swapaxes(w_in, 1, 2)
    return pl.pallas_call(
        _pack_w_in_kernel,
        grid=(1,),
        in_specs=[pl.BlockSpec((1, width, d), lambda i: (l, 0, 0))],
        out_specs=[_full((d, OFF_DT)), _full((d, DT_PAD)), _full((d, rest))],
        out_shape=[jax.ShapeDtypeStruct((d, OFF_DT), BF16),
                   jax.ShapeDtypeStruct((d, DT_PAD), BF16),
                   jax.ShapeDtypeStruct((d, rest), BF16)],
        compiler_params=pltpu.CompilerParams(
            dimension_semantics=("arbitrary",), vmem_limit_bytes=VMEM_LIMIT_BYTES),
        name="pack_w_in",
    )(w_t)


def _prep_layer(l, norm_mix_w, w_in, gm_ln_w, gm_ln_b, gm_ws, gm_bs, conv_w, conv_b, dt_bias, a_log,
                d_skip, ssd_norm_w, mem_norm_w, w_mem_k, w_mem_v, w_br_gm, w_br_ssd, w_br_xa, w_out,
                norm_ffn_w, w_up, w_down, norm_final_w):
    row = lambda a: a.reshape(1, -1)
    pad_heads = lambda a: jnp.pad(a, (0, DT_PAD - SSD_HEADS)).reshape(1, DT_PAD)
    w_in_a, w_in_dt, w_in_b = _pack_w_in(w_in, l)
    return {
        "norm_mix_w": row(norm_mix_w[l]),
        "w_in_a": w_in_a, "w_in_dt": w_in_dt, "w_in_b": w_in_b,
        "gm_ln_w": row(gm_ln_w[l]), "gm_ln_b": row(gm_ln_b[l]),
        "gm_ws": gm_ws[l], "gm_bs": gm_bs[l], "gm_bs_t": gm_bs[l].T,
        "conv_w": conv_w[l], "conv_b": row(conv_b[l]),
        "dt_bias": pad_heads(dt_bias[l]), "a_log": pad_heads(a_log[l]),
        "d_skip": row(jnp.repeat(d_skip[l], SSD_HEAD_DIM)), "ssd_norm_w": row(ssd_norm_w[l]),
        "mem_norm_w": row(mem_norm_w[l]),
        "w_kv": jnp.concatenate([w_mem_k[l], w_mem_v[l]], axis=1).astype(BF16),
        "w_br_gm": w_br_gm[l].astype(BF16), "w_br_ssd": w_br_ssd[l].astype(BF16),
        "w_br_xa": w_br_xa[l].astype(BF16), "w_out": w_out[l].astype(BF16),
        "norm_ffn_w": row(norm_ffn_w[l]), "norm_final_w": row(norm_final_w),
    }


def kernel(x_prompt, x_sample, mem_prompt, cache_mem_k, cache_mem_v, state_conv, state_ssm, norm_mix_w, w_in, gm_ln_w, gm_ln_b, gm_ws, gm_bs, conv_w, conv_b, dt_bias, a_log, d_skip, ssd_norm_w, mem_norm_w, w_mem_k, w_mem_v, w_br_gm, w_br_ssd, w_br_xa, w_out, norm_ffn_w, w_up, w_down, norm_final_w):
    depth = w_in.shape[0]
    assert depth == 1, "the final norm is fused into the MLP kernel of the last (only) layer"
    b, seq, _ = x_prompt.shape
    n, dec_seq, _ = x_sample.shape
    assert dec_seq == 1
    p = _prep_layer(0, norm_mix_w, w_in, gm_ln_w, gm_ln_b, gm_ws, gm_bs, conv_w, conv_b, dt_bias,
                    a_log, d_skip, ssd_norm_w, mem_norm_w, w_mem_k, w_mem_v, w_br_gm, w_br_ssd,
                    w_br_xa, w_out, norm_ffn_w, w_up, w_down, norm_final_w)

    mem_k, mem_v = _memkv(mem_prompt.reshape(b * MEM_LEN, D_MODEL), p["mem_norm_w"], p["w_kv"])
    h1, conv_p, ssm_p, p["w_up"], p["w_down"], k_rows, v_rows = _mixer(
        x_prompt, mem_k.reshape(b, MEM_LEN, XA_WIDTH), mem_v.reshape(b, MEM_LEN, XA_WIDTH), p, w_up, w_down, 0)
    y_prompt = _ffn(h1.reshape(b * seq, D_MODEL), p).reshape(b, seq, D_MODEL)

    xs2d = x_sample.reshape(n, D_MODEL)
    proj = _sample_proj(xs2d, p)
    gm, yn, xa, conv_s, ssm_s, gv = _sample_state(
        proj, state_conv[0].reshape(n, (SSD_CONV - 1) * SSD_CONV_DIM), state_ssm[0].reshape(n, SSD_INNER, SSD_STATE),
        cache_mem_k[0].reshape(n, MEM_LEN * XA_HEADS, XA_HEAD_DIM),
        cache_mem_v[0].reshape(n, MEM_LEN * XA_HEADS, XA_HEAD_DIM), p)
    y_sample = _sample_tail(xs2d, proj, gm, yn, xa, p).reshape(n, 1, D_MODEL)

    kv_shape = (1, b, MEM_LEN, XA_HEADS, XA_HEAD_DIM)
    state_shape = (SSD_HEADS, SSD_HEAD_DIM, SSD_STATE)
    return (y_prompt, y_sample,
            k_rows.reshape(kv_shape), v_rows.reshape(kv_shape),
            conv_p.reshape(1, b, SSD_CONV - 1, SSD_CONV_DIM), ssm_p.reshape((1, b) + state_shape),
            conv_s.reshape(1, n, SSD_CONV - 1, SSD_CONV_DIM), ssm_s.reshape((1, n) + state_shape),
            gv.reshape(1, n, 1, GM_WIDTH))
```

```python
import functools

import jax
import jax.numpy as jnp
from jax import lax
from jax.experimental import pallas as pl
from jax.experimental.pallas import tpu as pltpu

F32 = jnp.float32
BF16 = jnp.bfloat16

D_MODEL = 1024
GM_CHUNK = 128
GM_GROUPS = 4
GM_GROUP_DIM = 128
GM_WIDTH = GM_GROUPS * GM_GROUP_DIM
SSD_HEADS = 16
SSD_HEAD_DIM = 64
SSD_INNER = SSD_HEADS * SSD_HEAD_DIM
SSD_GROUPS = 2
SSD_GROUP_WIDTH = SSD_INNER // SSD_GROUPS
SSD_STATE = 128
SSD_CONV = 4
SSD_CHUNK = 128
SSD_CONV_DIM = SSD_INNER + 2 * SSD_GROUPS * SSD_STATE
MEM_LEN = 256
XA_HEADS = 4
XA_HEAD_DIM = 128
XA_WIDTH = XA_HEADS * XA_HEAD_DIM
N_BRANCH = 3
D_FF = 4 * D_MODEL
EPS = 1e-6

SUBLANES = 8
LANES = 128
HEADS_PER_LANE_TILE = LANES // SSD_HEAD_DIM
HEAD_PAIRS = SSD_HEADS // HEADS_PER_LANE_TILE
PAIRS_PER_GROUP = HEAD_PAIRS // SSD_GROUPS

DT_PAD = LANES
OFF_UV = 0
OFF_Z = OFF_UV + 2 * GM_WIDTH
OFF_XBC = OFF_Z + SSD_INNER
OFF_DT = OFF_XBC + SSD_CONV_DIM
OFF_Q = OFF_DT + DT_PAD
OFF_GATE = OFF_Q + XA_WIDTH
IN_PROJ_PACKED = OFF_GATE + N_BRANCH * D_MODEL

RAW_UV = 0
RAW_Z = RAW_UV + 2 * GM_WIDTH
RAW_DT = RAW_Z + SSD_INNER
RAW_GATE = RAW_DT + DT_PAD
RAW_WIDTH = RAW_GATE + N_BRANCH * D_MODEL

MIXER_ROWS = 256
PIECE_COLS = 256
MERGE_COLS = 512
FFN_ROWS = 1024
FFN_COLS = 1024
MEMKV_ROWS = 512
PACK_COLS = 512
SAMPLE_BLOCK = SUBLANES
VMEM_LIMIT_BYTES = 56 * 1024 * 1024


def _dot(a, b):
    return jnp.dot(a, b, preferred_element_type=F32)


def _dot_nt(a, b):
    return lax.dot_general(a, b, (((1,), (1,)), ((), ())), preferred_element_type=F32)


def _dot_tn(a, b):
    return lax.dot_general(a, b, (((0,), (0,)), ((), ())), preferred_element_type=F32)


def _rms(x, w):
    return x * lax.rsqrt(jnp.mean(x * x, axis=-1, keepdims=True) + EPS) * w


def _layernorm(x, w, b):
    xc = x - jnp.mean(x, axis=-1, keepdims=True)
    return xc * lax.rsqrt(jnp.mean(xc * xc, axis=-1, keepdims=True) + EPS) * w + b


def _gelu(x):
    return 0.5 * x * (1.0 + lax.erf(x * 0.7071067811865476))


def _silu(x):
    return x * jax.nn.sigmoid(x)


def _softplus(x):
    return jnp.maximum(x, 0.0) + jnp.log1p(jnp.exp(-jnp.abs(x)))


def _split3(x):
    p1 = x.astype(BF16)
    r1 = x - p1.astype(F32)
    p2 = r1.astype(BF16)
    p3 = (r1 - p2.astype(F32)).astype(BF16)
    return p1, p2, p3


REDUCE_CHAINS = 8


def _reduce_leading(x, op):
    n = x.shape[0]
    assert n % REDUCE_CHAINS == 0
    acc = [x[w] for w in range(REDUCE_CHAINS)]
    for i in range(REDUCE_CHAINS, n, REDUCE_CHAINS):
        acc = [op(a, x[i + w]) for w, a in enumerate(acc)]
    while len(acc) > 1:
        acc = [op(acc[2 * i], acc[2 * i + 1]) for i in range(len(acc) // 2)]
    return acc[0]


def _fold_heads(a, op):
    assert SUBLANES == 2 * XA_HEADS
    return op(a, pltpu.roll(a, XA_HEADS, axis=0))


def _group_rms(y, w):
    parts = []
    for g in range(SSD_GROUPS):
        cols = slice(g * SSD_GROUP_WIDTH, (g + 1) * SSD_GROUP_WIDTH)
        parts.append(_rms(y[:, cols], w[:, cols]))
    return jnp.concatenate(parts, axis=-1)


def _memkv_kernel(mem_ref, nw_ref, wkv_ref, k_ref, v_ref):
    mn = _rms(mem_ref[...], nw_ref[...]).astype(BF16)
    kv = _dot(mn, wkv_ref[...])
    k_ref[...] = kv[:, :XA_WIDTH]
    v_ref[...] = kv[:, XA_WIDTH:]


def _memkv(mem2d, norm_w, wkv):
    rows = mem2d.shape[0]
    tm = min(MEMKV_ROWS, rows)
    return pl.pallas_call(
        _memkv_kernel,
        grid=(rows // tm,),
        in_specs=[
            pl.BlockSpec((tm, D_MODEL), lambda i: (i, 0)),
            pl.BlockSpec((1, D_MODEL), lambda i: (0, 0)),
            pl.BlockSpec((D_MODEL, 2 * XA_WIDTH), lambda i: (0, 0)),
        ],
        out_specs=[
            pl.BlockSpec((tm, XA_WIDTH), lambda i: (i, 0)),
            pl.BlockSpec((tm, XA_WIDTH), lambda i: (i, 0)),
        ],
        out_shape=[jax.ShapeDtypeStruct((rows, XA_WIDTH), F32)] * 2,
        compiler_params=pltpu.CompilerParams(
            dimension_semantics=("arbitrary",), vmem_limit_bytes=VMEM_LIMIT_BYTES),
        name="memkv",
    )(mem2d, norm_w, wkv)


def _ssd_chunk_steps(rows, xs_s, bc_s, dt_s, neg_a, hT_s, y_s, scratch):
    acum_s, acumt_s, dtt_s, wc_s, cb_s, lhs_s, cd_s, xw_s, sg_s = scratch
    q = SSD_CHUNK

    def masks():
        ii = lax.broadcasted_iota(jnp.int32, (q, q), 0)
        jj = lax.broadcasted_iota(jnp.int32, (q, q), 1)
        low_half = lax.broadcasted_iota(jnp.int32, (q, LANES), 1) < SSD_HEAD_DIM
        low_half_row = lax.broadcasted_iota(jnp.int32, (1, LANES), 1) < SSD_HEAD_DIM
        return ii >= jj, low_half, low_half_row

    def prologue():
        causal, _, _ = masks()
        dt = dt_s[rows, :]
        d1, d2, d3 = _split3(dt * neg_a)
        tri = causal.astype(BF16)
        acum = _dot(tri, d1) + _dot(tri, d2) + _dot(tri, d3)
        acum_s[...] = acum
        acumt_s[...] = acum.T
        dtt_s[...] = dt.T
        wc_s[...] = dt * jnp.exp(acum[q - 1:q, :] - acum)

    def group_step(g):
        _, low_half, _ = masks()
        b_g = bc_s[rows, g * SSD_STATE:(g + 1) * SSD_STATE]
        c_g = bc_s[rows, (SSD_GROUPS + g) * SSD_STATE:(SSD_GROUPS + g + 1) * SSD_STATE]
        cb_s[g] = _dot_nt(c_g.astype(BF16), b_g.astype(BF16))
        for pr in range(PAIRS_PER_GROUP):
            pair = g * PAIRS_PER_GROUP + pr
            lanes = slice(pair * LANES, (pair + 1) * LANES)
            h0 = pair * HEADS_PER_LANE_TILE
            w_pair = jnp.where(low_half, jnp.broadcast_to(wc_s[:, h0:h0 + 1], (q, LANES)),
                               jnp.broadcast_to(wc_s[:, h0 + 1:h0 + 2], (q, LANES)))
            xw_s[:, lanes] = (xs_s[rows, lanes] * w_pair).astype(BF16)
        cols = slice(g * SSD_GROUP_WIDTH, (g + 1) * SSD_GROUP_WIDTH)
        sg_s[:, cols] = _dot(b_g.T.astype(BF16), xw_s[:, cols])

    def head_step(h):
        causal, _, _ = masks()
        g = h // (SSD_HEADS // SSD_GROUPS)
        c_g = bc_s[rows, (SSD_GROUPS + g) * SSD_STATE:(SSD_GROUPS + g + 1) * SSD_STATE]
        col = jnp.broadcast_to(acum_s[:, h:h + 1], (q, q))
        seg = jnp.where(causal, col - acumt_s[h:h + 1, :], -jnp.inf)
        lhs_s[h, :, 0:q] = (cb_s[g] * jnp.exp(seg) * dtt_s[h:h + 1, :]).astype(BF16)
        lhs_s[h, :, q:2 * q] = (c_g * jnp.exp(col)).astype(BF16)
        cd_s[h:h + 1, :] = jnp.exp(col[q - 1:q, :])

    def pair_step(pair):
        _, low_half, low_half_row = masks()
        lanes = slice(pair * LANES, (pair + 1) * LANES)
        h0 = pair * HEADS_PER_LANE_TILE
        h_prev = hT_s[:, lanes]
        rhs = jnp.concatenate([xs_s[rows, lanes].astype(BF16), h_prev.astype(BF16)], axis=0)
        y_s[rows, lanes] = jnp.where(low_half, _dot(lhs_s[h0], rhs), _dot(lhs_s[h0 + 1], rhs))
        decay = jnp.where(low_half_row, cd_s[h0:h0 + 1, :], cd_s[h0 + 1:h0 + 2, :])
        hT_s[:, lanes] = h_prev * decay + sg_s[:, lanes]

    steps = [(2, prologue)]
    steps += [(3, functools.partial(group_step, g)) for g in range(SSD_GROUPS)]
    steps += [(2, functools.partial(head_step, h)) for h in range(SSD_HEADS)]
    steps += [(1, functools.partial(pair_step, pair)) for pair in range(HEAD_PAIRS)]
    return steps


def _interleave(steps, pieces):
    total = sum(w for w, _ in steps)
    done = acc = 0
    for w, step in steps:
        acc += w
        upto = len(pieces) * acc // total
        for piece in pieces[done:upto]:
            piece()
        done = upto
        step()
    assert done == len(pieces)


def _in_proj_cols(w_refs, lo, hi):
    wa_ref, wdt_ref, wb_ref = w_refs
    if hi <= OFF_DT:
        return wa_ref[:, lo:hi]
    if lo >= OFF_Q:
        return wb_ref[:, lo - OFF_Q:hi - OFF_Q]
    assert (lo, hi) == (OFF_DT, OFF_Q)
    return wdt_ref[...]


def _mixer_kernel(x1_ref, k_ref, v_ref, nw_ref, wa_ref, wdt_ref, wb_ref, lnw_ref, lnb_ref, ws_ref,
                  bst_ref, cw_ref, cb_ref, dtb_ref, alog_ref, dsk_ref, snw_ref,
                  wgm_ref, wssd_ref, wxa_ref, wout_ref, wup_ref, wdn_ref,
                  h_ref, conv_ref, ssm_ref, wup_bf_ref, wdn_bf_ref, krow_ref, vrow_ref,
                  xn_s, raw_s, q_s, ext_s, hT_s, xs_s, bc_s, dt_s, y_s, u_s, v_s, gm_s, yn_s, xa_s, merged_s,
                  zs_s, gate_s, qb_s, xprev_s, *ssd_scratch, tiles_per_seq):
    tm = x1_ref.shape[1]
    s = pl.program_id(0)
    t2 = lax.rem(jnp.maximum(s - 1, 0), tiles_per_seq)
    w_in = functools.partial(_in_proj_cols, (wa_ref, wdt_ref, wb_ref))
    head = SUBLANES
    pc = PIECE_COLS

    @pl.when(s == 0)
    def _():
        blk = 2 * SUBLANES

        def zero_rows(i, carry):
            rows = pl.ds(pl.multiple_of(i * blk, blk), blk)
            for buf in (raw_s, ext_s, gate_s, zs_s, qb_s, xprev_s):
                buf[rows, :] = jnp.zeros((blk, buf.shape[1]), buf.dtype)
            return carry

        lax.fori_loop(0, tm // blk, zero_rows, 0)
        ext_s[tm:tm + head, :] = jnp.zeros((head, SSD_CONV_DIM), F32)

    @pl.when(t2 == 0)
    def _():
        ext_s[0:head, :] = jnp.zeros((head, SSD_CONV_DIM), F32)
        hT_s[...] = jnp.zeros(hT_s.shape, F32)
        for hd in range(XA_HEADS):
            cols = slice(hd * XA_HEAD_DIM, (hd + 1) * XA_HEAD_DIM)
            krow_ref[0, pl.ds(hd, MEM_LEN, stride=XA_HEADS), :] = k_ref[0, :, cols]
            vrow_ref[0, pl.ds(hd, MEM_LEN, stride=XA_HEADS), :] = v_ref[0, :, cols]

    xn_s[...] = _rms(x1_ref[0], nw_ref[...]).astype(BF16)

    wup_bf_ref[...] = wup_ref[0].astype(BF16)
    wdn_bf_ref[...] = wdn_ref[0].astype(BF16)

    def project(lo, width):
        return _dot(xn_s[...], w_in(lo, lo + width))

    def raw_piece(dst, src, width):
        def run():
            raw_s[:, dst:dst + width] = project(src, width)
        return run

    def q_piece(j):
        def run():
            q_s[:, j:j + pc] = project(OFF_Q + j, pc).astype(BF16)
        return run

    def xbc_piece(j):
        def run():
            ext_s[head:head + tm, j:j + pc] = project(OFF_XBC + j, pc)
        return run

    for j in range(0, SSD_CONV_DIM, pc):
        cols = slice(j, j + pc)
        acc = ext_s[head - 3:head - 3 + tm, cols] * cw_ref[0:1, cols]
        for k in range(1, SSD_CONV):
            acc = acc + ext_s[head - 3 + k:head - 3 + k + tm, cols] * cw_ref[k:k + 1, cols]
        xbc = _silu(acc + cb_ref[:, cols])
        if j < SSD_INNER:
            xs_s[:, cols] = xbc
        else:
            bc_s[:, j - SSD_INNER:j - SSD_INNER + pc] = xbc
        conv_ref[0, :, cols] = ext_s[tm + head - 3:tm + head, cols]
        ext_s[0:head, cols] = ext_s[tm:tm + head, cols]
    dt_s[...] = _softplus(raw_s[:, RAW_DT:RAW_GATE] + dtb_ref[...])
    for j in range(0, GM_WIDTH, pc):
        u_s[:, j:j + pc] = _gelu(raw_s[:, RAW_UV + j:RAW_UV + j + pc])
    v = _gelu(raw_s[:, RAW_UV + GM_WIDTH:RAW_Z])
    v_s[...] = _layernorm(v, lnw_ref[...], lnb_ref[...]).astype(BF16)

    pieces = ([raw_piece(RAW_GATE + j, OFF_GATE + j, pc) for j in range(0, N_BRANCH * D_MODEL, pc)]
              + [raw_piece(RAW_Z + j, OFF_Z + j, pc) for j in range(0, SSD_INNER, pc)]
              + [q_piece(j) for j in range(0, XA_WIDTH, pc)]
              + [xbc_piece(j) for j in range(0, SSD_CONV_DIM, pc)] + [raw_piece(RAW_DT, OFF_DT, DT_PAD)]
              + [raw_piece(RAW_UV + j, OFF_UV + j, pc) for j in range(0, 2 * GM_WIDTH, pc)])
    steps = []
    neg_a = -jnp.exp(alog_ref[...])
    for c in range(tm // SSD_CHUNK):
        rows = slice(c * SSD_CHUNK, (c + 1) * SSD_CHUNK)
        steps += _ssd_chunk_steps(rows, xs_s, bc_s, dt_s, neg_a, hT_s, y_s, ssd_scratch)

    def gmlp_group(g):
        ii = lax.broadcasted_iota(jnp.int32, (GM_CHUNK, GM_CHUNK), 0)
        jj = lax.broadcasted_iota(jnp.int32, (GM_CHUNK, GM_CHUNK), 1)
        cols = slice(g * GM_GROUP_DIM, (g + 1) * GM_GROUP_DIM)
        w_tril = jnp.where(ii >= jj, ws_ref[g], 0.0).astype(BF16)
        bias = bst_ref[:, g:g + 1]
        for c in range(tm // GM_CHUNK):
            rows = slice(c * GM_CHUNK, (c + 1) * GM_CHUNK)
            mixed = _dot(w_tril, v_s[rows, cols]) + bias
            gm_s[rows, cols] = (u_s[rows, cols] * mixed).astype(BF16)

    steps += [(1, functools.partial(gmlp_group, g)) for g in range(GM_GROUPS)]

    def attn_head(hd):
        cols = slice(hd * XA_HEAD_DIM, (hd + 1) * XA_HEAD_DIM)
        sc = _dot_nt(qb_s[:, cols], k_ref[0, :, cols].astype(BF16)) * (XA_HEAD_DIM ** -0.5)
        p = jnp.exp(sc - jnp.max(sc, axis=-1, keepdims=True))
        o = _dot(p.astype(BF16), v_ref[0, :, cols].astype(BF16))
        xa_s[:, cols] = (o / jnp.sum(p, axis=-1, keepdims=True)).astype(BF16)

    steps += [(2, functools.partial(attn_head, hd)) for hd in range(XA_HEADS)]

    def ssd_out_group(g):
        cols = slice(g * SSD_GROUP_WIDTH, (g + 1) * SSD_GROUP_WIDTH)
        yv = (y_s[:, cols] + xs_s[:, cols] * dsk_ref[:, cols]) * zs_s[:, cols]
        yn_s[:, cols] = _rms(yv, snw_ref[:, cols]).astype(BF16)

    steps += [(2, functools.partial(ssd_out_group, g)) for g in range(SSD_GROUPS)]

    def merge_piece(i, br_s, w_ref, j):
        lo = i * D_MODEL + j
        term = gate_s[:, lo:lo + MERGE_COLS] * _dot(br_s[...], w_ref[:, j:j + MERGE_COLS])
        if i == 0:
            merged_s[:, j:j + MERGE_COLS] = term
        else:
            merged_s[:, j:j + MERGE_COLS] += term

    for i, (br_s, w_ref) in enumerate(((gm_s, wgm_ref), (yn_s, wssd_ref), (xa_s, wxa_ref))):
        steps += [(2, functools.partial(merge_piece, i, br_s, w_ref, j)) for j in range(0, D_MODEL, MERGE_COLS)]

    def out_proj():
        h_ref[0] = xprev_s[...] + _dot(merged_s[...].astype(BF16), wout_ref[...])

    steps.append((1, out_proj))
    _interleave(steps, pieces)

    for j in range(0, N_BRANCH * D_MODEL, pc):
        gate_s[:, j:j + pc] = jax.nn.sigmoid(raw_s[:, RAW_GATE + j:RAW_GATE + j + pc])
    for j in range(0, SSD_INNER, pc):
        zs_s[:, j:j + pc] = _silu(raw_s[:, RAW_Z + j:RAW_Z + j + pc])
    qb_s[...] = q_s[...]
    xprev_s[...] = x1_ref[0]

    @pl.when(t2 == tiles_per_seq - 1)
    def _():
        for pair in range(HEAD_PAIRS):
            lanes = slice(pair * LANES, (pair + 1) * LANES)
            ssm_ref[0, lanes, :] = hT_s[:, lanes].T


def _full(shape):
    return pl.BlockSpec(shape, lambda *_: (0,) * len(shape))


def _mixer(x, mem_k, mem_v, p, w_up, w_down, l):
    b, seq, _ = x.shape
    tm = MIXER_ROWS
    assert seq % tm == 0 and tm % SSD_CHUNK == 0
    small = [p["norm_mix_w"], p["w_in_a"], p["w_in_dt"], p["w_in_b"],
             p["gm_ln_w"], p["gm_ln_b"], p["gm_ws"], p["gm_bs_t"],
             p["conv_w"], p["conv_b"], p["dt_bias"], p["a_log"], p["d_skip"], p["ssd_norm_w"],
             p["w_br_gm"], p["w_br_ssd"], p["w_br_xa"], p["w_out"]]
    tps = seq // tm
    n_tiles = b * tps
    tile1 = lambda s: jnp.minimum(s, n_tiles - 1)
    tile2 = lambda s: jnp.maximum(s - 1, 0)
    x_spec = lambda tile: pl.BlockSpec((1, tm, D_MODEL), lambda s: (tile(s) // tps, tile(s) % tps, 0))
    seq_spec = lambda r, c: pl.BlockSpec((1, r, c), lambda s: (tile2(s) // tps, 0, 0))
    up_rows, dn_rows = D_MODEL // n_tiles, D_FF // n_tiles
    assert D_MODEL == up_rows * n_tiles and up_rows % (2 * SUBLANES) == 0
    return pl.pallas_call(
        functools.partial(_mixer_kernel, tiles_per_seq=tps),
        grid=(n_tiles + 1,),
        in_specs=[x_spec(tile1), seq_spec(MEM_LEN, XA_WIDTH), seq_spec(MEM_LEN, XA_WIDTH)]
                 + [_full(a.shape) for a in small]
                 + [pl.BlockSpec((1, up_rows, D_FF), lambda s: (l, tile1(s), 0)),
                    pl.BlockSpec((1, dn_rows, D_MODEL), lambda s: (l, tile1(s), 0))],
        out_specs=[x_spec(tile2), seq_spec(SSD_CONV - 1, SSD_CONV_DIM), seq_spec(SSD_INNER, SSD_STATE),
                   pl.BlockSpec((up_rows, D_FF), lambda s: (tile1(s), 0)),
                   pl.BlockSpec((dn_rows, D_MODEL), lambda s: (tile1(s), 0)),
                   seq_spec(MEM_LEN * XA_HEADS, XA_HEAD_DIM), seq_spec(MEM_LEN * XA_HEADS, XA_HEAD_DIM)],
        out_shape=[
            jax.ShapeDtypeStruct((b, seq, D_MODEL), F32),
            jax.ShapeDtypeStruct((b, SSD_CONV - 1, SSD_CONV_DIM), F32),
            jax.ShapeDtypeStruct((b, SSD_INNER, SSD_STATE), F32),
            jax.ShapeDtypeStruct((D_MODEL, D_FF), BF16),
            jax.ShapeDtypeStruct((D_FF, D_MODEL), BF16),
            jax.ShapeDtypeStruct((b, MEM_LEN * XA_HEADS, XA_HEAD_DIM), F32),
            jax.ShapeDtypeStruct((b, MEM_LEN * XA_HEADS, XA_HEAD_DIM), F32),
        ],
        scratch_shapes=[
            pltpu.VMEM((tm, D_MODEL), BF16),
            pltpu.VMEM((tm, RAW_WIDTH), F32),
            pltpu.VMEM((tm, XA_WIDTH), BF16),
            pltpu.VMEM((tm + SUBLANES, SSD_CONV_DIM), F32),
            pltpu.VMEM((SSD_STATE, SSD_INNER), F32),
            pltpu.VMEM((tm, SSD_INNER), F32),
            pltpu.VMEM((tm, 2 * SSD_GROUPS * SSD_STATE), F32),
            pltpu.VMEM((tm, DT_PAD), F32),
            pltpu.VMEM((tm, SSD_INNER), F32),
            pltpu.VMEM((tm, GM_WIDTH), F32),
            pltpu.VMEM((tm, GM_WIDTH), BF16),
            pltpu.VMEM((tm, GM_WIDTH), BF16),
            pltpu.VMEM((tm, SSD_INNER), BF16),
            pltpu.VMEM((tm, XA_WIDTH), BF16),
            pltpu.VMEM((tm, D_MODEL), F32),
            pltpu.VMEM((tm, SSD_INNER), F32),
            pltpu.VMEM((tm, N_BRANCH * D_MODEL), F32),
            pltpu.VMEM((tm, XA_WIDTH), BF16),
            pltpu.VMEM((tm, D_MODEL), F32),
            pltpu.VMEM((SSD_CHUNK, DT_PAD), F32),
            pltpu.VMEM((DT_PAD, SSD_CHUNK), F32),
            pltpu.VMEM((DT_PAD, SSD_CHUNK), F32),
            pltpu.VMEM((SSD_CHUNK, DT_PAD), F32),
            pltpu.VMEM((SSD_GROUPS, SSD_CHUNK, SSD_CHUNK), F32),
            pltpu.VMEM((SSD_HEADS, SSD_CHUNK, 2 * SSD_CHUNK), BF16),
            pltpu.VMEM((SSD_HEADS, SSD_CHUNK), F32),
            pltpu.VMEM((SSD_CHUNK, SSD_INNER), BF16),
            pltpu.VMEM((SSD_STATE, SSD_INNER), F32),
        ],
        compiler_params=pltpu.CompilerParams(
            dimension_semantics=("arbitrary",), vmem_limit_bytes=VMEM_LIMIT_BYTES),
        name="prompt_mixer",
    )(x, mem_k, mem_v, *small, w_up, w_down)


def _ffn_kernel(h_ref, nw_ref, wup_ref, wdn_ref, fw_ref, o_ref):
    h = h_ref[...]
    hn = _rms(h, nw_ref[...]).astype(BF16)
    acc = h
    for j in range(D_FF // FFN_COLS):
        cols = slice(j * FFN_COLS, (j + 1) * FFN_COLS)
        a = jnp.square(jnp.maximum(_dot(hn, wup_ref[:, cols]), 0.0)).astype(BF16)
        acc = acc + _dot(a, wdn_ref[cols, :])
    o_ref[...] = _rms(acc, fw_ref[...])


def _ffn(h2d, p):
    rows = h2d.shape[0]
    tm = min(FFN_ROWS, rows)
    assert rows % tm == 0
    return pl.pallas_call(
        _ffn_kernel,
        grid=(rows // tm,),
        in_specs=[
            pl.BlockSpec((tm, D_MODEL), lambda i: (i, 0)),
            _full((1, D_MODEL)),
            _full((D_MODEL, D_FF)),
            _full((D_FF, D_MODEL)),
            _full((1, D_MODEL)),
        ],
        out_specs=pl.BlockSpec((tm, D_MODEL), lambda i: (i, 0)),
        out_shape=jax.ShapeDtypeStruct((rows, D_MODEL), F32),
        compiler_params=pltpu.CompilerParams(
            dimension_semantics=("arbitrary",), vmem_limit_bytes=VMEM_LIMIT_BYTES),
        name="ffn",
    )(h2d, p["norm_ffn_w"], p["w_up"], p["w_down"], p["norm_final_w"])


SAMPLE_PROJ_STEPS = 4


def _sample_proj_kernel(x_ref, nw_ref, wa_ref, wdt_ref, wb_ref, o_ref, xn_s):
    i = pl.program_id(0)
    blk_a, blk_b = wa_ref.shape[1], wb_ref.shape[1]

    @pl.when(i == 0)
    def _():
        xn_s[...] = _rms(x_ref[...], nw_ref[...]).astype(BF16)
        o_ref[:, OFF_DT:OFF_Q] = _dot(xn_s[...], wdt_ref[...])

    o_ref[:, pl.ds(pl.multiple_of(i * blk_a, LANES), blk_a)] = _dot(xn_s[...], wa_ref[...])
    o_ref[:, pl.ds(pl.multiple_of(OFF_Q + i * blk_b, LANES), blk_b)] = _dot(xn_s[...], wb_ref[...])


def _sample_proj(x2d, p):
    rows = x2d.shape[0]
    args = [x2d, p["norm_mix_w"], p["w_in_a"], p["w_in_dt"], p["w_in_b"]]
    steps = SAMPLE_PROJ_STEPS
    blk_a, blk_b = OFF_DT // steps, (IN_PROJ_PACKED - OFF_Q) // steps
    assert blk_a % LANES == 0 and blk_b % LANES == 0
    assert blk_a * steps == OFF_DT and blk_b * steps == IN_PROJ_PACKED - OFF_Q
    return pl.pallas_call(
        _sample_proj_kernel,
        grid=(steps,),
        in_specs=[_full(x2d.shape), _full(p["norm_mix_w"].shape),
                  pl.BlockSpec((D_MODEL, blk_a), lambda i: (0, i)),
                  _full(p["w_in_dt"].shape),
                  pl.BlockSpec((D_MODEL, blk_b), lambda i: (0, i))],
        out_specs=_full((rows, IN_PROJ_PACKED)),
        out_shape=jax.ShapeDtypeStruct((rows, IN_PROJ_PACKED), F32),
        scratch_shapes=[pltpu.VMEM((rows, D_MODEL), BF16)],
        compiler_params=pltpu.CompilerParams(
            dimension_semantics=("arbitrary",), vmem_limit_bytes=VMEM_LIMIT_BYTES),
        name="sample_proj",
    )(*args)


def _sample_state_kernel(proj_ref, stc_ref, ssm_ref, k_ref, v_ref, lnw_ref, lnb_ref, ws_ref,
                         bs_ref, cw_ref, cb_ref, dtb_ref, alog_ref, dsk_ref, snw_ref,
                         gm_ref, yn_ref, xa_ref, convn_ref, ssmn_ref, gv_ref):
    nb = proj_ref.shape[0]
    proj = proj_ref[...]

    uv = _gelu(proj[:, OFF_UV:OFF_Z])
    v = _layernorm(uv[:, GM_WIDTH:], lnw_ref[...], lnb_ref[...])
    gv_ref[...] = v
    for g in range(GM_GROUPS):
        cols = slice(g * GM_GROUP_DIM, (g + 1) * GM_GROUP_DIM)
        mixed = v[:, cols] * ws_ref[g, 0:1, 0:1] + bs_ref[g:g + 1, 0:1]
        gm_ref[:, cols] = uv[:, cols] * mixed

    st = stc_ref[...]
    xnew = proj[:, OFF_XBC:OFF_DT]
    acc = st[:, 0:SSD_CONV_DIM] * cw_ref[0:1, :]
    for k in range(1, SSD_CONV - 1):
        acc = acc + st[:, k * SSD_CONV_DIM:(k + 1) * SSD_CONV_DIM] * cw_ref[k:k + 1, :]
    acc = acc + xnew * cw_ref[SSD_CONV - 1:SSD_CONV, :]
    convn_ref[:, 0:(SSD_CONV - 2) * SSD_CONV_DIM] = st[:, SSD_CONV_DIM:]
    convn_ref[:, (SSD_CONV - 2) * SSD_CONV_DIM:] = xnew
    xbc = _silu(acc + cb_ref[...])
    xs = xbc[:, :SSD_INNER]
    bm = xbc[:, SSD_INNER:SSD_INNER + SSD_GROUPS * SSD_STATE].astype(BF16)
    cm = xbc[:, SSD_INNER + SSD_GROUPS * SSD_STATE:].astype(BF16)

    dt = _softplus(proj[:, OFF_DT:OFF_Q] + dtb_ref[...])
    decay = jnp.exp(dt * -jnp.exp(alog_ref[...]))
    hh = lax.broadcasted_iota(jnp.int32, (DT_PAD, SSD_INNER), 0)
    cc = lax.broadcasted_iota(jnp.int32, (DT_PAD, SSD_INNER), 1)
    spread = (jnp.right_shift(cc, SSD_HEAD_DIM.bit_length() - 1) == hh).astype(BF16)
    dt_wide = sum(_dot(piece, spread) for piece in _split3(dt))
    xdt = xs * dt_wide

    rid = lax.broadcasted_iota(jnp.int32, (nb, 1), 0)
    scale = XA_HEAD_DIM ** -0.5
    kv_vregs = MEM_LEN * XA_HEADS // SUBLANES
    lane_ones = jnp.ones((XA_HEAD_DIM, XA_HEAD_DIM), BF16)
    qv = proj[:, OFF_Q:OFF_GATE]
    y = jnp.zeros((nb, SSD_INNER), F32)
    for bi in range(nb):
        mine = rid == bi
        x_row = jnp.where(mine, xdt, 0.0).astype(BF16)
        parts = []
        for g in range(SSD_GROUPS):
            cols = slice(g * SSD_GROUP_WIDTH, (g + 1) * SSD_GROUP_WIDTH)
            ncols = slice(g * SSD_STATE, (g + 1) * SSD_STATE)
            upd = _dot_tn(x_row[:, cols], bm[:, ncols])
            heads = range(g * SSD_HEADS // SSD_GROUPS, (g + 1) * SSD_HEADS // SSD_GROUPS)
            e_rows = jnp.concatenate(
                [jnp.broadcast_to(decay[bi:bi + 1, h:h + 1], (SSD_HEAD_DIM, SSD_STATE)) for h in heads],
                axis=0)
            h_new = ssm_ref[bi, cols, :] * e_rows + upd
            ssmn_ref[bi, cols, :] = h_new
            parts.append(_dot_nt(cm[:, ncols], h_new.astype(BF16)))
        y = y + jnp.where(mine, jnp.concatenate(parts, axis=-1), 0.0)

        q_rows = jnp.concatenate(
            [qv[bi:bi + 1, hd * XA_HEAD_DIM:(hd + 1) * XA_HEAD_DIM] for hd in range(XA_HEADS)]
            * (SUBLANES // XA_HEADS), axis=0)
        kq = (k_ref[bi].reshape(kv_vregs, SUBLANES, XA_HEAD_DIM) * q_rows[None]).astype(BF16)
        s = _dot(kq.reshape(kv_vregs * SUBLANES, XA_HEAD_DIM), lane_ones) * scale
        s = s.reshape(kv_vregs, SUBLANES, XA_HEAD_DIM)
        pexp = jnp.exp(s - _fold_heads(_reduce_leading(s, jnp.maximum), jnp.maximum)[None])
        o = _reduce_leading(pexp * v_ref[bi].reshape(kv_vregs, SUBLANES, XA_HEAD_DIM), jnp.add)
        o = _fold_heads(o, jnp.add) / _fold_heads(_reduce_leading(pexp, jnp.add), jnp.add)
        for hd in range(XA_HEADS):
            xa_ref[bi:bi + 1, hd * XA_HEAD_DIM:(hd + 1) * XA_HEAD_DIM] = o[hd:hd + 1, :]

    zs = _silu(proj[:, OFF_Z:OFF_XBC])
    yn_ref[...] = _group_rms((y + xs * dsk_ref[...]) * zs, snw_ref[...])


def _sample_state(proj, state_conv2d, state_ssm, cache_k, cache_v, p):
    n = proj.shape[0]
    nb = SAMPLE_BLOCK
    assert n % nb == 0
    conv_w = (SSD_CONV - 1) * SSD_CONV_DIM
    small = [p["gm_ln_w"], p["gm_ln_b"], p["gm_ws"], p["gm_bs"], p["conv_w"], p["conv_b"],
             p["dt_bias"], p["a_log"], p["d_skip"], p["ssd_norm_w"]]
    rows2 = lambda w: pl.BlockSpec((nb, w), lambda i: (i, 0))
    rows3 = lambda a, c: pl.BlockSpec((nb, a, c), lambda i: (i, 0, 0))
    cache = rows3(MEM_LEN * XA_HEADS, XA_HEAD_DIM)
    return pl.pallas_call(
        _sample_state_kernel,
        grid=(n // nb,),
        in_specs=[rows2(IN_PROJ_PACKED), rows2(conv_w), rows3(SSD_INNER, SSD_STATE), cache, cache]
                 + [_full(a.shape) for a in small],
        out_specs=[rows2(GM_WIDTH), rows2(SSD_INNER), rows2(XA_WIDTH), rows2(conv_w),
                   rows3(SSD_INNER, SSD_STATE), rows2(GM_WIDTH)],
        out_shape=[
            jax.ShapeDtypeStruct((n, GM_WIDTH), F32),
            jax.ShapeDtypeStruct((n, SSD_INNER), F32),
            jax.ShapeDtypeStruct((n, XA_WIDTH), F32),
            jax.ShapeDtypeStruct((n, conv_w), F32),
            jax.ShapeDtypeStruct((n, SSD_INNER, SSD_STATE), F32),
            jax.ShapeDtypeStruct((n, GM_WIDTH), F32),
        ],
        compiler_params=pltpu.CompilerParams(
            dimension_semantics=("arbitrary",), vmem_limit_bytes=VMEM_LIMIT_BYTES),
        name="sample_state",
    )(proj, state_conv2d, state_ssm, cache_k, cache_v, *small)


def _sample_tail_kernel(x_ref, proj_ref, gm_ref, yn_ref, xa_ref, wgm_ref, wssd_ref, wxa_ref, wout_ref,
                        nw_ref, wup_ref, wdn_ref, fw_ref, o_ref, hn_s, acc_s):
    j = pl.program_id(0)

    @pl.when(j == 0)
    def _():
        merged = None
        for i, (br_ref, w_ref) in enumerate(((gm_ref, wgm_ref), (yn_ref, wssd_ref), (xa_ref, wxa_ref))):
            gate = jax.nn.sigmoid(proj_ref[:, OFF_GATE + i * D_MODEL:OFF_GATE + (i + 1) * D_MODEL])
            term = gate * _dot(br_ref[...].astype(BF16), w_ref[...])
            merged = term if merged is None else merged + term
        h = x_ref[...] + _dot(merged.astype(BF16), wout_ref[...])
        hn_s[...] = _rms(h, nw_ref[...]).astype(BF16)
        acc_s[...] = h

    a = jnp.square(jnp.maximum(_dot(hn_s[...], wup_ref[...]), 0.0)).astype(BF16)
    acc_s[...] += _dot(a, wdn_ref[...])

    @pl.when(j == pl.num_programs(0) - 1)
    def _():
        o_ref[...] = _rms(acc_s[...], fw_ref[...])


def _sample_tail(x2d, proj, gm, yn, xa, p):
    rows = x2d.shape[0]
    head_args = [x2d, proj, gm, yn, xa, p["w_br_gm"], p["w_br_ssd"], p["w_br_xa"], p["w_out"], p["norm_ffn_w"]]
    return pl.pallas_call(
        _sample_tail_kernel,
        grid=(D_FF // FFN_COLS,),
        in_specs=[_full(a.shape) for a in head_args] + [
            pl.BlockSpec((D_MODEL, FFN_COLS), lambda j: (0, j)),
            pl.BlockSpec((FFN_COLS, D_MODEL), lambda j: (j, 0)),
            _full((1, D_MODEL)),
        ],
        out_specs=_full((rows, D_MODEL)),
        out_shape=jax.ShapeDtypeStruct((rows, D_MODEL), F32),
        scratch_shapes=[pltpu.VMEM((rows, D_MODEL), BF16), pltpu.VMEM((rows, D_MODEL), F32)],
        compiler_params=pltpu.CompilerParams(
            dimension_semantics=("arbitrary",), vmem_limit_bytes=VMEM_LIMIT_BYTES),
        name="sample_tail",
    )(*head_args, p["w_up"], p["w_down"], p["norm_final_w"])


def _pack_w_in_kernel(wt_ref, wa_ref, wdt_ref, wb_ref):
    rest = wb_ref.shape[1]
    for j in range(0, OFF_DT, PACK_COLS):
        wa_ref[:, j:j + PACK_COLS] = wt_ref[0, j:j + PACK_COLS, :].T.astype(BF16)
    dt_rows = wt_ref[0, OFF_DT:OFF_DT + DT_PAD, :]
    is_dt = lax.broadcasted_iota(jnp.int32, dt_rows.shape, 0) < SSD_HEADS
    wdt_ref[...] = jnp.where(is_dt, dt_rows, 0.0).T.astype(BF16)
    base = OFF_DT + SSD_HEADS
    for j in range(0, rest, PACK_COLS):
        wb_ref[:, j:j + PACK_COLS] = wt_ref[0, base + j:base + j + PACK_COLS, :].T.astype(BF16)


def _pack_w_in(w_in, l):
    d, width = w_in.shape[1:]
    rest = width - OFF_DT - SSD_HEADS
    assert rest == IN_PROJ_PACKED - OFF_Q and OFF_DT % PACK_COLS == 0 and rest % PACK_COLS == 0
    w_t = jnp.swapaxes(w_in, 1, 2)
    return pl.pallas_call(
        _pack_w_in_kernel,
        grid=(1,),
        in_specs=[pl.BlockSpec((1, width, d), lambda i: (l, 0, 0))],
        out_specs=[_full((d, OFF_DT)), _full((d, DT_PAD)), _full((d, rest))],
        out_shape=[jax.ShapeDtypeStruct((d, OFF_DT), BF16),
                   jax.ShapeDtypeStruct((d, DT_PAD), BF16),
                   jax.ShapeDtypeStruct((d, rest), BF16)],
        compiler_params=pltpu.CompilerParams(
            dimension_semantics=("arbitrary",), vmem_limit_bytes=VMEM_LIMIT_BYTES),
        name="pack_w_in",
    )(w_t)


def _prep_layer(l, norm_mix_w, w_in, gm_ln_w, gm_ln_b, gm_ws, gm_bs, conv_w, conv_b, dt_bias, a_log,
                d_skip, ssd_norm_w, mem_norm_w, w_mem_k, w_mem_v, w_br_gm, w_br_ssd, w_br_xa, w_out,
                norm_ffn_w, w_up, w_down, norm_final_w):
    row = lambda a: a.reshape(1, -1)
    pad_heads = lambda a: jnp.pad(a, (0, DT_PAD - SSD_HEADS)).reshape(1, DT_PAD)
    w_in_a, w_in_dt, w_in_b = _pack_w_in(w_in, l)
    return {
        "norm_mix_w": row(norm_mix_w[l]),
        "w_in_a": w_in_a, "w_in_dt": w_in_dt, "w_in_b": w_in_b,
        "gm_ln_w": row(gm_ln_w[l]), "gm_ln_b": row(gm_ln_b[l]),
        "gm_ws": gm_ws[l], "gm_bs": gm_bs[l], "gm_bs_t": gm_bs[l].T,
        "conv_w": conv_w[l], "conv_b": row(conv_b[l]),
        "dt_bias": pad_heads(dt_bias[l]), "a_log": pad_heads(a_log[l]),
        "d_skip": row(jnp.repeat(d_skip[l], SSD_HEAD_DIM)), "ssd_norm_w": row(ssd_norm_w[l]),
        "mem_norm_w": row(mem_norm_w[l]),
        "w_kv": jnp.concatenate([w_mem_k[l], w_mem_v[l]], axis=1).astype(BF16),
        "w_br_gm": w_br_gm[l].astype(BF16), "w_br_ssd": w_br_ssd[l].astype(BF16),
        "w_br_xa": w_br_xa[l].astype(BF16), "w_out": w_out[l].astype(BF16),
        "norm_ffn_w": row(norm_ffn_w[l]), "norm_final_w": row(norm_final_w),
    }


def kernel(x_prompt, x_sample, mem_prompt, cache_mem_k, cache_mem_v, state_conv, state_ssm, norm_mix_w, w_in, gm_ln_w, gm_ln_b, gm_ws, gm_bs, conv_w, conv_b, dt_bias, a_log, d_skip, ssd_norm_w, mem_norm_w, w_mem_k, w_mem_v, w_br_gm, w_br_ssd, w_br_xa, w_out, norm_ffn_w, w_up, w_down, norm_final_w):
    depth = w_in.shape[0]
    assert depth == 1, "the final norm is fused into the MLP kernel of the last (only) layer"
    b, seq, _ = x_prompt.shape
    n, dec_seq, _ = x_sample.shape
    assert dec_seq == 1
    p = _prep_layer(0, norm_mix_w, w_in, gm_ln_w, gm_ln_b, gm_ws, gm_bs, conv_w, conv_b, dt_bias,
                    a_log, d_skip, ssd_norm_w, mem_norm_w, w_mem_k, w_mem_v, w_br_gm, w_br_ssd,
                    w_br_xa, w_out, norm_ffn_w, w_up, w_down, norm_final_w)

    mem_k, mem_v = _memkv(mem_prompt.reshape(b * MEM_LEN, D_MODEL), p["mem_norm_w"], p["w_kv"])
    h1, conv_p, ssm_p, p["w_up"], p["w_down"], k_rows, v_rows = _mixer(
        x_prompt, mem_k.reshape(b, MEM_LEN, XA_WIDTH), mem_v.reshape(b, MEM_LEN, XA_WIDTH), p, w_up, w_down, 0)
    y_prompt = _ffn(h1.reshape(b * seq, D_MODEL), p).reshape(b, seq, D_MODEL)

    xs2d = x_sample.reshape(n, D_MODEL)
    proj = _sample_proj(xs2d, p)
    gm, yn, xa, conv_s, ssm_s, gv = _sample_state(
        proj, state_conv[0].reshape(n, (SSD_CONV - 1) * SSD_CONV_DIM), state_ssm[0].reshape(n, SSD_INNER, SSD_STATE),
        cache_mem_k[0].reshape(n, MEM_LEN * XA_HEADS, XA_HEAD_DIM),
        cache_mem_v[0].reshape(n, MEM_LEN * XA_HEADS, XA_HEAD_DIM), p)
    y_sample = _sample_tail(xs2d, proj, gm, yn, xa, p).reshape(n, 1, D_MODEL)

    kv_shape = (1, b, MEM_LEN, XA_HEADS, XA_HEAD_DIM)
    state_shape = (SSD_HEADS, SSD_HEAD_DIM, SSD_STATE)
    return (y_prompt, y_sample,
            k_rows.reshape(kv_shape), v_rows.reshape(kv_shape),
            conv_p.reshape(1, b, SSD_CONV - 1, SSD_CONV_DIM), ssm_p.reshape((1, b) + state_shape),
            conv_s.reshape(1, n, SSD_CONV - 1, SSD_CONV_DIM), ssm_s.reshape((1, n) + state_shape),
            gv.reshape(1, n, 1, GM_WIDTH))
```

```python
import functools

import jax
import jax.numpy as jnp
from jax import lax
from jax.experimental import pallas as pl
from jax.experimental.pallas import tpu as pltpu

F32 = jnp.float32
BF16 = jnp.bfloat16

D_MODEL = 1024
GM_CHUNK = 128
GM_GROUPS = 4
GM_GROUP_DIM = 128
GM_WIDTH = GM_GROUPS * GM_GROUP_DIM
SSD_HEADS = 16
SSD_HEAD_DIM = 64
SSD_INNER = SSD_HEADS * SSD_HEAD_DIM
SSD_GROUPS = 2
SSD_GROUP_WIDTH = SSD_INNER // SSD_GROUPS
SSD_STATE = 128
SSD_CONV = 4
SSD_CHUNK = 128
SSD_CONV_DIM = SSD_INNER + 2 * SSD_GROUPS * SSD_STATE
MEM_LEN = 256
XA_HEADS = 4
XA_HEAD_DIM = 128
XA_WIDTH = XA_HEADS * XA_HEAD_DIM
N_BRANCH = 3
D_FF = 4 * D_MODEL
EPS = 1e-6

SUBLANES = 8
LANES = 128
HEADS_PER_LANE_TILE = LANES // SSD_HEAD_DIM
HEAD_PAIRS = SSD_HEADS // HEADS_PER_LANE_TILE
PAIRS_PER_GROUP = HEAD_PAIRS // SSD_GROUPS

DT_PAD = LANES
OFF_UV = 0
OFF_Z = OFF_UV + 2 * GM_WIDTH
OFF_XBC = OFF_Z + SSD_INNER
OFF_DT = OFF_XBC + SSD_CONV_DIM
OFF_Q = OFF_DT + DT_PAD
OFF_GATE = OFF_Q + XA_WIDTH
IN_PROJ_PACKED = OFF_GATE + N_BRANCH * D_MODEL

RAW_UV = 0
RAW_Z = RAW_UV + 2 * GM_WIDTH
RAW_DT = RAW_Z + SSD_INNER
RAW_GATE = RAW_DT + DT_PAD
RAW_WIDTH = RAW_GATE + N_BRANCH * D_MODEL

MIXER_ROWS = 256
PIECE_COLS = 256
MERGE_COLS = 512
FFN_ROWS = 1024
FFN_COLS = 1024
MEMKV_ROWS = 512
PACK_COLS = 512
SAMPLE_BLOCK = SUBLANES
VMEM_LIMIT_BYTES = 56 * 1024 * 1024


def _dot(a, b):
    return jnp.dot(a, b, preferred_element_type=F32)


def _dot_nt(a, b):
    return lax.dot_general(a, b, (((1,), (1,)), ((), ())), preferred_element_type=F32)


def _dot_tn(a, b):
    return lax.dot_general(a, b, (((0,), (0,)), ((), ())), preferred_element_type=F32)


def _rms(x, w):
    return x * lax.rsqrt(jnp.mean(x * x, axis=-1, keepdims=True) + EPS) * w


def _layernorm(x, w, b):
    xc = x - jnp.mean(x, axis=-1, keepdims=True)
    return xc * lax.rsqrt(jnp.mean(xc * xc, axis=-1, keepdims=True) + EPS) * w + b


def _gelu(x):
    return 0.5 * x * (1.0 + lax.erf(x * 0.7071067811865476))


def _silu(x):
    return x * jax.nn.sigmoid(x)


def _softplus(x):
    return jnp.maximum(x, 0.0) + jnp.log1p(jnp.exp(-jnp.abs(x)))


def _split3(x):
    p1 = x.astype(BF16)
    r1 = x - p1.astype(F32)
    p2 = r1.astype(BF16)
    p3 = (r1 - p2.astype(F32)).astype(BF16)
    return p1, p2, p3


REDUCE_CHAINS = 8


def _reduce_leading(x, op):
    n = x.shape[0]
    assert n % REDUCE_CHAINS == 0
    acc = [x[w] for w in range(REDUCE_CHAINS)]
    for i in range(REDUCE_CHAINS, n, REDUCE_CHAINS):
        acc = [op(a, x[i + w]) for w, a in enumerate(acc)]
    while len(acc) > 1:
        acc = [op(acc[2 * i], acc[2 * i + 1]) for i in range(len(acc) // 2)]
    return acc[0]


def _fold_heads(a, op):
    assert SUBLANES == 2 * XA_HEADS
    return op(a, pltpu.roll(a, XA_HEADS, axis=0))


def _group_rms(y, w):
    parts = []
    for g in range(SSD_GROUPS):
        cols = slice(g * SSD_GROUP_WIDTH, (g + 1) * SSD_GROUP_WIDTH)
        parts.append(_rms(y[:, cols], w[:, cols]))
    return jnp.concatenate(parts, axis=-1)


def _memkv_kernel(mem_ref, nw_ref, wk_ref, wv_ref, k_ref, v_ref, wkv_s):
    @pl.when(pl.program_id(0) == 0)
    def _():
        wkv_s[:, :XA_WIDTH] = wk_ref[0].astype(BF16)
        wkv_s[:, XA_WIDTH:] = wv_ref[0].astype(BF16)

    mn = _rms(mem_ref[...], nw_ref[...]).astype(BF16)
    kv = _dot(mn, wkv_s[...])
    k_ref[...] = kv[:, :XA_WIDTH]
    v_ref[...] = kv[:, XA_WIDTH:]


def _memkv(mem2d, norm_w, w_k, w_v, l):
    rows = mem2d.shape[0]
    tm = min(MEMKV_ROWS, rows)
    return pl.pallas_call(
        _memkv_kernel,
        grid=(rows // tm,),
        in_specs=[
            pl.BlockSpec((tm, D_MODEL), lambda i: (i, 0)),
            pl.BlockSpec((1, D_MODEL), lambda i: (0, 0)),
            pl.BlockSpec((1, D_MODEL, XA_WIDTH), lambda i: (l, 0, 0)),
            pl.BlockSpec((1, D_MODEL, XA_WIDTH), lambda i: (l, 0, 0)),
        ],
        out_specs=[
            pl.BlockSpec((tm, XA_WIDTH), lambda i: (i, 0)),
            pl.BlockSpec((tm, XA_WIDTH), lambda i: (i, 0)),
        ],
        out_shape=[jax.ShapeDtypeStruct((rows, XA_WIDTH), F32)] * 2,
        scratch_shapes=[pltpu.VMEM((D_MODEL, 2 * XA_WIDTH), BF16)],
        compiler_params=pltpu.CompilerParams(
            dimension_semantics=("arbitrary",), vmem_limit_bytes=VMEM_LIMIT_BYTES),
        name="memkv",
    )(mem2d, norm_w, w_k, w_v)


def _ssd_chunk_steps(rows, xs_s, bc_s, dt_s, neg_a, hT_s, y_s, scratch):
    acum_s, acumt_s, dtt_s, wc_s, cb_s, lhs_s, cd_s, xw_s, sg_s = scratch
    q = SSD_CHUNK

    def masks():
        ii = lax.broadcasted_iota(jnp.int32, (q, q), 0)
        jj = lax.broadcasted_iota(jnp.int32, (q, q), 1)
        low_half = lax.broadcasted_iota(jnp.int32, (q, LANES), 1) < SSD_HEAD_DIM
        low_half_row = lax.broadcasted_iota(jnp.int32, (1, LANES), 1) < SSD_HEAD_DIM
        return ii >= jj, low_half, low_half_row

    def prologue():
        causal, _, _ = masks()
        dt = dt_s[rows, :]
        d1, d2, d3 = _split3(dt * neg_a)
        tri = causal.astype(BF16)
        acum = _dot(tri, d1) + _dot(tri, d2) + _dot(tri, d3)
        acum_s[...] = acum
        acumt_s[...] = acum.T
        dtt_s[...] = dt.T
        wc_s[...] = dt * jnp.exp(acum[q - 1:q, :] - acum)

    def group_step(g):
        _, low_half, _ = masks()
        b_g = bc_s[rows, g * SSD_STATE:(g + 1) * SSD_STATE]
        c_g = bc_s[rows, (SSD_GROUPS + g) * SSD_STATE:(SSD_GROUPS + g + 1) * SSD_STATE]
        cb_s[g] = _dot_nt(c_g.astype(BF16), b_g.astype(BF16))
        for pr in range(PAIRS_PER_GROUP):
            pair = g * PAIRS_PER_GROUP + pr
            lanes = slice(pair * LANES, (pair + 1) * LANES)
            h0 = pair * HEADS_PER_LANE_TILE
            w_pair = jnp.where(low_half, jnp.broadcast_to(wc_s[:, h0:h0 + 1], (q, LANES)),
                               jnp.broadcast_to(wc_s[:, h0 + 1:h0 + 2], (q, LANES)))
            xw_s[:, lanes] = (xs_s[rows, lanes] * w_pair).astype(BF16)
        cols = slice(g * SSD_GROUP_WIDTH, (g + 1) * SSD_GROUP_WIDTH)
        sg_s[:, cols] = _dot(b_g.T.astype(BF16), xw_s[:, cols])

    def head_step(h):
        causal, _, _ = masks()
        g = h // (SSD_HEADS // SSD_GROUPS)
        c_g = bc_s[rows, (SSD_GROUPS + g) * SSD_STATE:(SSD_GROUPS + g + 1) * SSD_STATE]
        col = jnp.broadcast_to(acum_s[:, h:h + 1], (q, q))
        seg = jnp.where(causal, col - acumt_s[h:h + 1, :], -jnp.inf)
        lhs_s[h, :, 0:q] = (cb_s[g] * jnp.exp(seg) * dtt_s[h:h + 1, :]).astype(BF16)
        lhs_s[h, :, q:2 * q] = (c_g * jnp.exp(col)).astype(BF16)
        cd_s[h:h + 1, :] = jnp.exp(col[q - 1:q, :])

    def pair_step(pair):
        _, low_half, low_half_row = masks()
        lanes = slice(pair * LANES, (pair + 1) * LANES)
        h0 = pair * HEADS_PER_LANE_TILE
        h_prev = hT_s[:, lanes]
        rhs = jnp.concatenate([xs_s[rows, lanes].astype(BF16), h_prev.astype(BF16)], axis=0)
        y_s[rows, lanes] = jnp.where(low_half, _dot(lhs_s[h0], rhs), _dot(lhs_s[h0 + 1], rhs))
        decay = jnp.where(low_half_row, cd_s[h0:h0 + 1, :], cd_s[h0 + 1:h0 + 2, :])
        hT_s[:, lanes] = h_prev * decay + sg_s[:, lanes]

    steps = [(2, prologue)]
    steps += [(3, functools.partial(group_step, g)) for g in range(SSD_GROUPS)]
    steps += [(2, functools.partial(head_step, h)) for h in range(SSD_HEADS)]
    steps += [(1, functools.partial(pair_step, pair)) for pair in range(HEAD_PAIRS)]
    return steps


def _interleave(steps, pieces):
    total = sum(w for w, _ in steps)
    done = acc = 0
    for w, step in steps:
        acc += w
        upto = len(pieces) * acc // total
        for piece in pieces[done:upto]:
            piece()
        done = upto
        step()
    assert done == len(pieces)


def _in_proj_cols(w_refs, lo, hi):
    wa_ref, wdt_ref, wb_ref = w_refs
    if hi <= OFF_DT:
        return wa_ref[:, lo:hi]
    if lo >= OFF_Q:
        return wb_ref[:, lo - OFF_Q:hi - OFF_Q]
    assert (lo, hi) == (OFF_DT, OFF_Q)
    return wdt_ref[...]


def _mixer_kernel(x1_ref, k_ref, v_ref, nw_ref, wa_ref, wdt_ref, wb_ref, lnw_ref, lnb_ref, ws_ref,
                  bst_ref, cw_ref, cb_ref, dtb_ref, alog_ref, dsk_ref, snw_ref,
                  wgm_ref, wssd_ref, wxa_ref, wout_ref, wup_ref, wdn_ref,
                  h_ref, conv_ref, ssm_ref, wup_bf_ref, wdn_bf_ref, krow_ref, vrow_ref,
                  xn_s, raw_s, q_s, ext_s, hT_s, xs_s, bc_s, dt_s, y_s, u_s, v_s, gm_s, yn_s, xa_s, merged_s,
                  zs_s, gate_s, qb_s, xprev_s, *ssd_scratch, tiles_per_seq):
    tm = x1_ref.shape[1]
    s = pl.program_id(0)
    t2 = lax.rem(jnp.maximum(s - 1, 0), tiles_per_seq)
    w_in = functools.partial(_in_proj_cols, (wa_ref, wdt_ref, wb_ref))
    head = SUBLANES
    pc = PIECE_COLS

    @pl.when(s == 0)
    def _():
        blk = 2 * SUBLANES

        def zero_rows(i, carry):
            rows = pl.ds(pl.multiple_of(i * blk, blk), blk)
            for buf in (raw_s, ext_s, gate_s, zs_s, qb_s, xprev_s):
                buf[rows, :] = jnp.zeros((blk, buf.shape[1]), buf.dtype)
            return carry

        lax.fori_loop(0, tm // blk, zero_rows, 0)
        ext_s[tm:tm + head, :] = jnp.zeros((head, SSD_CONV_DIM), F32)

    @pl.when(t2 == 0)
    def _():
        ext_s[0:head, :] = jnp.zeros((head, SSD_CONV_DIM), F32)
        hT_s[...] = jnp.zeros(hT_s.shape, F32)
        for hd in range(XA_HEADS):
            cols = slice(hd * XA_HEAD_DIM, (hd + 1) * XA_HEAD_DIM)
            krow_ref[0, pl.ds(hd, MEM_LEN, stride=XA_HEADS), :] = k_ref[0, :, cols]
            vrow_ref[0, pl.ds(hd, MEM_LEN, stride=XA_HEADS), :] = v_ref[0, :, cols]

    xn_s[...] = _rms(x1_ref[0], nw_ref[...]).astype(BF16)

    wup_bf_ref[...] = wup_ref[0].astype(BF16)
    wdn_bf_ref[...] = wdn_ref[0].astype(BF16)

    def project(lo, width):
        return _dot(xn_s[...], w_in(lo, lo + width))

    def raw_piece(dst, src, width):
        def run():
            raw_s[:, dst:dst + width] = project(src, width)
        return run

    def q_piece(j):
        def run():
            q_s[:, j:j + pc] = project(OFF_Q + j, pc).astype(BF16)
        return run

    def xbc_piece(j):
        def run():
            ext_s[head:head + tm, j:j + pc] = project(OFF_XBC + j, pc)
        return run

    for j in range(0, SSD_CONV_DIM, pc):
        cols = slice(j, j + pc)
        acc = ext_s[head - 3:head - 3 + tm, cols] * cw_ref[0:1, cols]
        for k in range(1, SSD_CONV):
            acc = acc + ext_s[head - 3 + k:head - 3 + k + tm, cols] * cw_ref[k:k + 1, cols]
        xbc = _silu(acc + cb_ref[:, cols])
        if j < SSD_INNER:
            xs_s[:, cols] = xbc
        else:
            bc_s[:, j - SSD_INNER:j - SSD_INNER + pc] = xbc
        conv_ref[0, :, cols] = ext_s[tm + head - 3:tm + head, cols]
        ext_s[0:head, cols] = ext_s[tm:tm + head, cols]
    dt_s[...] = _softplus(raw_s[:, RAW_DT:RAW_GATE] + dtb_ref[...])
    for j in range(0, GM_WIDTH, pc):
        u_s[:, j:j + pc] = _gelu(raw_s[:, RAW_UV + j:RAW_UV + j + pc])
    v = _gelu(raw_s[:, RAW_UV + GM_WIDTH:RAW_Z])
    v_s[...] = _layernorm(v, lnw_ref[...], lnb_ref[...]).astype(BF16)

    pieces = ([raw_piece(RAW_GATE + j, OFF_GATE + j, pc) for j in range(0, N_BRANCH * D_MODEL, pc)]
              + [raw_piece(RAW_Z + j, OFF_Z + j, pc) for j in range(0, SSD_INNER, pc)]
              + [q_piece(j) for j in range(0, XA_WIDTH, pc)]
              + [xbc_piece(j) for j in range(0, SSD_CONV_DIM, pc)] + [raw_piece(RAW_DT, OFF_DT, DT_PAD)]
              + [raw_piece(RAW_UV + j, OFF_UV + j, pc) for j in range(0, 2 * GM_WIDTH, pc)])
    steps = []
    neg_a = -jnp.exp(alog_ref[...])
    for c in range(tm // SSD_CHUNK):
        rows = slice(c * SSD_CHUNK, (c + 1) * SSD_CHUNK)
        steps += _ssd_chunk_steps(rows, xs_s, bc_s, dt_s, neg_a, hT_s, y_s, ssd_scratch)

    def gmlp_group(g):
        ii = lax.broadcasted_iota(jnp.int32, (GM_CHUNK, GM_CHUNK), 0)
        jj = lax.broadcasted_iota(jnp.int32, (GM_CHUNK, GM_CHUNK), 1)
        cols = slice(g * GM_GROUP_DIM, (g + 1) * GM_GROUP_DIM)
        w_tril = jnp.where(ii >= jj, ws_ref[g], 0.0).astype(BF16)
        bias = bst_ref[:, g:g + 1]
        for c in range(tm // GM_CHUNK):
            rows = slice(c * GM_CHUNK, (c + 1) * GM_CHUNK)
            mixed = _dot(w_tril, v_s[rows, cols]) + bias
            gm_s[rows, cols] = (u_s[rows, cols] * mixed).astype(BF16)

    steps += [(1, functools.partial(gmlp_group, g)) for g in range(GM_GROUPS)]

    def attn_head(hd):
        cols = slice(hd * XA_HEAD_DIM, (hd + 1) * XA_HEAD_DIM)
        sc = _dot_nt(qb_s[:, cols], k_ref[0, :, cols].astype(BF16)) * (XA_HEAD_DIM ** -0.5)
        p = jnp.exp(sc - jnp.max(sc, axis=-1, keepdims=True))
        o = _dot(p.astype(BF16), v_ref[0, :, cols].astype(BF16))
        xa_s[:, cols] = (o / jnp.sum(p, axis=-1, keepdims=True)).astype(BF16)

    steps += [(2, functools.partial(attn_head, hd)) for hd in range(XA_HEADS)]

    def ssd_out_group(g):
        cols = slice(g * SSD_GROUP_WIDTH, (g + 1) * SSD_GROUP_WIDTH)
        yv = (y_s[:, cols] + xs_s[:, cols] * dsk_ref[:, cols]) * zs_s[:, cols]
        yn_s[:, cols] = _rms(yv, snw_ref[:, cols]).astype(BF16)

    steps += [(2, functools.partial(ssd_out_group, g)) for g in range(SSD_GROUPS)]

    def merge_piece(i, br_s, w_ref, j):
        lo = i * D_MODEL + j
        term = gate_s[:, lo:lo + MERGE_COLS] * _dot(br_s[...], w_ref[:, j:j + MERGE_COLS])
        if i == 0:
            merged_s[:, j:j + MERGE_COLS] = term
        else:
            merged_s[:, j:j + MERGE_COLS] += term

    for i, (br_s, w_ref) in enumerate(((gm_s, wgm_ref), (yn_s, wssd_ref), (xa_s, wxa_ref))):
        steps += [(2, functools.partial(merge_piece, i, br_s, w_ref, j)) for j in range(0, D_MODEL, MERGE_COLS)]

    def out_proj():
        h_ref[0] = xprev_s[...] + _dot(merged_s[...].astype(BF16), wout_ref[...])

    steps.append((1, out_proj))
    _interleave(steps, pieces)

    for j in range(0, N_BRANCH * D_MODEL, pc):
        gate_s[:, j:j + pc] = jax.nn.sigmoid(raw_s[:, RAW_GATE + j:RAW_GATE + j + pc])
    for j in range(0, SSD_INNER, pc):
        zs_s[:, j:j + pc] = _silu(raw_s[:, RAW_Z + j:RAW_Z + j + pc])
    qb_s[...] = q_s[...]
    xprev_s[...] = x1_ref[0]

    @pl.when(t2 == tiles_per_seq - 1)
    def _():
        for pair in range(HEAD_PAIRS):
            lanes = slice(pair * LANES, (pair + 1) * LANES)
            ssm_ref[0, lanes, :] = hT_s[:, lanes].T


def _full(shape):
    return pl.BlockSpec(shape, lambda *_: (0,) * len(shape))


def _mixer(x, mem_k, mem_v, p, w_up, w_down, l):
    b, seq, _ = x.shape
    tm = MIXER_ROWS
    assert seq % tm == 0 and tm % SSD_CHUNK == 0
    small = [p["norm_mix_w"], p["w_in_a"], p["w_in_dt"], p["w_in_b"],
             p["gm_ln_w"], p["gm_ln_b"], p["gm_ws"], p["gm_bs_t"],
             p["conv_w"], p["conv_b"], p["dt_bias"], p["a_log"], p["d_skip"], p["ssd_norm_w"],
             p["w_br_gm"], p["w_br_ssd"], p["w_br_xa"], p["w_out"]]
    tps = seq // tm
    n_tiles = b * tps
    tile1 = lambda s: jnp.minimum(s, n_tiles - 1)
    tile2 = lambda s: jnp.maximum(s - 1, 0)
    x_spec = lambda tile: pl.BlockSpec((1, tm, D_MODEL), lambda s: (tile(s) // tps, tile(s) % tps, 0))
    seq_spec = lambda r, c: pl.BlockSpec((1, r, c), lambda s: (tile2(s) // tps, 0, 0))
    up_rows, dn_rows = D_MODEL // n_tiles, D_FF // n_tiles
    assert D_MODEL == up_rows * n_tiles and up_rows % (2 * SUBLANES) == 0
    return pl.pallas_call(
        functools.partial(_mixer_kernel, tiles_per_seq=tps),
        grid=(n_tiles + 1,),
        in_specs=[x_spec(tile1), seq_spec(MEM_LEN, XA_WIDTH), seq_spec(MEM_LEN, XA_WIDTH)]
                 + [_full(a.shape) for a in small]
                 + [pl.BlockSpec((1, up_rows, D_FF), lambda s: (l, tile1(s), 0)),
                    pl.BlockSpec((1, dn_rows, D_MODEL), lambda s: (l, tile1(s), 0))],
        out_specs=[x_spec(tile2), seq_spec(SSD_CONV - 1, SSD_CONV_DIM), seq_spec(SSD_INNER, SSD_STATE),
                   pl.BlockSpec((up_rows, D_FF), lambda s: (tile1(s), 0)),
                   pl.BlockSpec((dn_rows, D_MODEL), lambda s: (tile1(s), 0)),
                   seq_spec(MEM_LEN * XA_HEADS, XA_HEAD_DIM), seq_spec(MEM_LEN * XA_HEADS, XA_HEAD_DIM)],
        out_shape=[
            jax.ShapeDtypeStruct((b, seq, D_MODEL), F32),
            jax.ShapeDtypeStruct((b, SSD_CONV - 1, SSD_CONV_DIM), F32),
            jax.ShapeDtypeStruct((b, SSD_INNER, SSD_STATE), F32),
            jax.ShapeDtypeStruct((D_MODEL, D_FF), BF16),
            jax.ShapeDtypeStruct((D_FF, D_MODEL), BF16),
            jax.ShapeDtypeStruct((b, MEM_LEN * XA_HEADS, XA_HEAD_DIM), F32),
            jax.ShapeDtypeStruct((b, MEM_LEN * XA_HEADS, XA_HEAD_DIM), F32),
        ],
        scratch_shapes=[
            pltpu.VMEM((tm, D_MODEL), BF16),
            pltpu.VMEM((tm, RAW_WIDTH), F32),
            pltpu.VMEM((tm, XA_WIDTH), BF16),
            pltpu.VMEM((tm + SUBLANES, SSD_CONV_DIM), F32),
            pltpu.VMEM((SSD_STATE, SSD_INNER), F32),
            pltpu.VMEM((tm, SSD_INNER), F32),
            pltpu.VMEM((tm, 2 * SSD_GROUPS * SSD_STATE), F32),
            pltpu.VMEM((tm, DT_PAD), F32),
            pltpu.VMEM((tm, SSD_INNER), F32),
            pltpu.VMEM((tm, GM_WIDTH), F32),
            pltpu.VMEM((tm, GM_WIDTH), BF16),
            pltpu.VMEM((tm, GM_WIDTH), BF16),
            pltpu.VMEM((tm, SSD_INNER), BF16),
            pltpu.VMEM((tm, XA_WIDTH), BF16),
            pltpu.VMEM((tm, D_MODEL), F32),
            pltpu.VMEM((tm, SSD_INNER), F32),
            pltpu.VMEM((tm, N_BRANCH * D_MODEL), F32),
            pltpu.VMEM((tm, XA_WIDTH), BF16),
            pltpu.VMEM((tm, D_MODEL), F32),
            pltpu.VMEM((SSD_CHUNK, DT_PAD), F32),
            pltpu.VMEM((DT_PAD, SSD_CHUNK), F32),
            pltpu.VMEM((DT_PAD, SSD_CHUNK), F32),
            pltpu.VMEM((SSD_CHUNK, DT_PAD), F32),
            pltpu.VMEM((SSD_GROUPS, SSD_CHUNK, SSD_CHUNK), F32),
            pltpu.VMEM((SSD_HEADS, SSD_CHUNK, 2 * SSD_CHUNK), BF16),
            pltpu.VMEM((SSD_HEADS, SSD_CHUNK), F32),
            pltpu.VMEM((SSD_CHUNK, SSD_INNER), BF16),
            pltpu.VMEM((SSD_STATE, SSD_INNER), F32),
        ],
        compiler_params=pltpu.CompilerParams(
            dimension_semantics=("arbitrary",), vmem_limit_bytes=VMEM_LIMIT_BYTES),
        name="prompt_mixer",
    )(x, mem_k, mem_v, *small, w_up, w_down)


def _ffn_kernel(h_ref, nw_ref, wup_ref, wdn_ref, fw_ref, o_ref):
    h = h_ref[...]
    hn = _rms(h, nw_ref[...]).astype(BF16)
    acc = h
    for j in range(D_FF // FFN_COLS):
        cols = slice(j * FFN_COLS, (j + 1) * FFN_COLS)
        a = jnp.square(jnp.maximum(_dot(hn, wup_ref[:, cols]), 0.0)).astype(BF16)
        acc = acc + _dot(a, wdn_ref[cols, :])
    o_ref[...] = _rms(acc, fw_ref[...])


def _ffn(h2d, p):
    rows = h2d.shape[0]
    tm = min(FFN_ROWS, rows)
    assert rows % tm == 0
    return pl.pallas_call(
        _ffn_kernel,
        grid=(rows // tm,),
        in_specs=[
            pl.BlockSpec((tm, D_MODEL), lambda i: (i, 0)),
            _full((1, D_MODEL)),
            _full((D_MODEL, D_FF)),
            _full((D_FF, D_MODEL)),
            _full((1, D_MODEL)),
        ],
        out_specs=pl.BlockSpec((tm, D_MODEL), lambda i: (i, 0)),
        out_shape=jax.ShapeDtypeStruct((rows, D_MODEL), F32),
        compiler_params=pltpu.CompilerParams(
            dimension_semantics=("arbitrary",), vmem_limit_bytes=VMEM_LIMIT_BYTES),
        name="ffn",
    )(h2d, p["norm_ffn_w"], p["w_up"], p["w_down"], p["norm_final_w"])


SAMPLE_PROJ_STEPS = 4


def _sample_proj_kernel(x_ref, nw_ref, wa_ref, wdt_ref, wb_ref, o_ref, xn_s):
    i = pl.program_id(0)
    blk_a, blk_b = wa_ref.shape[1], wb_ref.shape[1]

    @pl.when(i == 0)
    def _():
        xn_s[...] = _rms(x_ref[...], nw_ref[...]).astype(BF16)
        o_ref[:, OFF_DT:OFF_Q] = _dot(xn_s[...], wdt_ref[...])

    o_ref[:, pl.ds(pl.multiple_of(i * blk_a, LANES), blk_a)] = _dot(xn_s[...], wa_ref[...])
    o_ref[:, pl.ds(pl.multiple_of(OFF_Q + i * blk_b, LANES), blk_b)] = _dot(xn_s[...], wb_ref[...])


def _sample_proj(x2d, p):
    rows = x2d.shape[0]
    args = [x2d, p["norm_mix_w"], p["w_in_a"], p["w_in_dt"], p["w_in_b"]]
    steps = SAMPLE_PROJ_STEPS
    blk_a, blk_b = OFF_DT // steps, (IN_PROJ_PACKED - OFF_Q) // steps
    assert blk_a % LANES == 0 and blk_b % LANES == 0
    assert blk_a * steps == OFF_DT and blk_b * steps == IN_PROJ_PACKED - OFF_Q
    return pl.pallas_call(
        _sample_proj_kernel,
        grid=(steps,),
        in_specs=[_full(x2d.shape), _full(p["norm_mix_w"].shape),
                  pl.BlockSpec((D_MODEL, blk_a), lambda i: (0, i)),
                  _full(p["w_in_dt"].shape),
                  pl.BlockSpec((D_MODEL, blk_b), lambda i: (0, i))],
        out_specs=_full((rows, IN_PROJ_PACKED)),
        out_shape=jax.ShapeDtypeStruct((rows, IN_PROJ_PACKED), F32),
        scratch_shapes=[pltpu.VMEM((rows, D_MODEL), BF16)],
        compiler_params=pltpu.CompilerParams(
            dimension_semantics=("arbitrary",), vmem_limit_bytes=VMEM_LIMIT_BYTES),
        name="sample_proj",
    )(*args)


def _sample_state_kernel(proj_ref, stc_ref, ssm_ref, k_ref, v_ref, lnw_ref, lnb_ref, ws_ref,
                         bs_ref, cw_ref, cb_ref, dtb_ref, alog_ref, dsk_ref, snw_ref,
                         gm_ref, yn_ref, xa_ref, convn_ref, ssmn_ref, gv_ref):
    nb = proj_ref.shape[0]
    proj = proj_ref[...]

    uv = _gelu(proj[:, OFF_UV:OFF_Z])
    v = _layernorm(uv[:, GM_WIDTH:], lnw_ref[...], lnb_ref[...])
    gv_ref[...] = v
    for g in range(GM_GROUPS):
        cols = slice(g * GM_GROUP_DIM, (g + 1) * GM_GROUP_DIM)
        mixed = v[:, cols] * ws_ref[g, 0:1, 0:1] + bs_ref[g:g + 1, 0:1]
        gm_ref[:, cols] = uv[:, cols] * mixed

    st = stc_ref[...]
    xnew = proj[:, OFF_XBC:OFF_DT]
    acc = st[:, 0:SSD_CONV_DIM] * cw_ref[0:1, :]
    for k in range(1, SSD_CONV - 1):
        acc = acc + st[:, k * SSD_CONV_DIM:(k + 1) * SSD_CONV_DIM] * cw_ref[k:k + 1, :]
    acc = acc + xnew * cw_ref[SSD_CONV - 1:SSD_CONV, :]
    convn_ref[:, 0:(SSD_CONV - 2) * SSD_CONV_DIM] = st[:, SSD_CONV_DIM:]
    convn_ref[:, (SSD_CONV - 2) * SSD_CONV_DIM:] = xnew
    xbc = _silu(acc + cb_ref[...])
    xs = xbc[:, :SSD_INNER]
    bm = xbc[:, SSD_INNER:SSD_INNER + SSD_GROUPS * SSD_STATE].astype(BF16)
    cm = xbc[:, SSD_INNER + SSD_GROUPS * SSD_STATE:].astype(BF16)

    dt = _softplus(proj[:, OFF_DT:OFF_Q] + dtb_ref[...])
    decay = jnp.exp(dt * -jnp.exp(alog_ref[...]))
    hh = lax.broadcasted_iota(jnp.int32, (DT_PAD, SSD_INNER), 0)
    cc = lax.broadcasted_iota(jnp.int32, (DT_PAD, SSD_INNER), 1)
    spread = (jnp.right_shift(cc, SSD_HEAD_DIM.bit_length() - 1) == hh).astype(BF16)
    dt_wide = sum(_dot(piece, spread) for piece in _split3(dt))
    xdt = xs * dt_wide

    rid = lax.broadcasted_iota(jnp.int32, (nb, 1), 0)
    scale = XA_HEAD_DIM ** -0.5
    kv_vregs = MEM_LEN * XA_HEADS // SUBLANES
    lane_ones = jnp.ones((XA_HEAD_DIM, XA_HEAD_DIM), BF16)
    qv = proj[:, OFF_Q:OFF_GATE]
    y = jnp.zeros((nb, SSD_INNER), F32)
    for bi in range(nb):
        mine = rid == bi
        x_row = jnp.where(mine, xdt, 0.0).astype(BF16)
        parts = []
        for g in range(SSD_GROUPS):
            cols = slice(g * SSD_GROUP_WIDTH, (g + 1) * SSD_GROUP_WIDTH)
            ncols = slice(g * SSD_STATE, (g + 1) * SSD_STATE)
            upd = _dot_tn(x_row[:, cols], bm[:, ncols])
            heads = range(g * SSD_HEADS // SSD_GROUPS, (g + 1) * SSD_HEADS // SSD_GROUPS)
            e_rows = jnp.concatenate(
                [jnp.broadcast_to(decay[bi:bi + 1, h:h + 1], (SSD_HEAD_DIM, SSD_STATE)) for h in heads],
                axis=0)
            h_new = ssm_ref[bi, cols, :] * e_rows + upd
            ssmn_ref[bi, cols, :] = h_new
            parts.append(_dot_nt(cm[:, ncols], h_new.astype(BF16)))
        y = y + jnp.where(mine, jnp.concatenate(parts, axis=-1), 0.0)

        q_rows = jnp.concatenate(
            [qv[bi:bi + 1, hd * XA_HEAD_DIM:(hd + 1) * XA_HEAD_DIM] for hd in range(XA_HEADS)]
            * (SUBLANES // XA_HEADS), axis=0)
        kq = (k_ref[bi].reshape(kv_vregs, SUBLANES, XA_HEAD_DIM) * q_rows[None]).astype(BF16)
        s = _dot(kq.reshape(kv_vregs * SUBLANES, XA_HEAD_DIM), lane_ones) * scale
        s = s.reshape(kv_vregs, SUBLANES, XA_HEAD_DIM)
        pexp = jnp.exp(s - _fold_heads(_reduce_leading(s, jnp.maximum), jnp.maximum)[None])
        o = _reduce_leading(pexp * v_ref[bi].reshape(kv_vregs, SUBLANES, XA_HEAD_DIM), jnp.add)
        o = _fold_heads(o, jnp.add) / _fold_heads(_reduce_leading(pexp, jnp.add), jnp.add)
        for hd in range(XA_HEADS):
            xa_ref[bi:bi + 1, hd * XA_HEAD_DIM:(hd + 1) * XA_HEAD_DIM] = o[hd:hd + 1, :]

    zs = _silu(proj[:, OFF_Z:OFF_XBC])
    yn_ref[...] = _group_rms((y + xs * dsk_ref[...]) * zs, snw_ref[...])


def _sample_state(proj, state_conv2d, state_ssm, cache_k, cache_v, p):
    n = proj.shape[0]
    nb = SAMPLE_BLOCK
    assert n % nb == 0
    conv_w = (SSD_CONV - 1) * SSD_CONV_DIM
    small = [p["gm_ln_w"], p["gm_ln_b"], p["gm_ws"], p["gm_bs"], p["conv_w"], p["conv_b"],
             p["dt_bias"], p["a_log"], p["d_skip"], p["ssd_norm_w"]]
    rows2 = lambda w: pl.BlockSpec((nb, w), lambda i: (i, 0))
    rows3 = lambda a, c: pl.BlockSpec((nb, a, c), lambda i: (i, 0, 0))
    cache = rows3(MEM_LEN * XA_HEADS, XA_HEAD_DIM)
    return pl.pallas_call(
        _sample_state_kernel,
        grid=(n // nb,),
        in_specs=[rows2(IN_PROJ_PACKED), rows2(conv_w), rows3(SSD_INNER, SSD_STATE), cache, cache]
                 + [_full(a.shape) for a in small],
        out_specs=[rows2(GM_WIDTH), rows2(SSD_INNER), rows2(XA_WIDTH), rows2(conv_w),
                   rows3(SSD_INNER, SSD_STATE), rows2(GM_WIDTH)],
        out_shape=[
            jax.ShapeDtypeStruct((n, GM_WIDTH), F32),
            jax.ShapeDtypeStruct((n, SSD_INNER), F32),
            jax.ShapeDtypeStruct((n, XA_WIDTH), F32),
            jax.ShapeDtypeStruct((n, conv_w), F32),
            jax.ShapeDtypeStruct((n, SSD_INNER, SSD_STATE), F32),
            jax.ShapeDtypeStruct((n, GM_WIDTH), F32),
        ],
        compiler_params=pltpu.CompilerParams(
            dimension_semantics=("arbitrary",), vmem_limit_bytes=VMEM_LIMIT_BYTES),
        name="sample_state",
    )(proj, state_conv2d, state_ssm, cache_k, cache_v, *small)


def _sample_tail_kernel(x_ref, proj_ref, gm_ref, yn_ref, xa_ref, wgm_ref, wssd_ref, wxa_ref, wout_ref,
                        nw_ref, wup_ref, wdn_ref, fw_ref, o_ref, hn_s, acc_s):
    j = pl.program_id(0)

    @pl.when(j == 0)
    def _():
        merged = None
        for i, (br_ref, w_ref) in enumerate(((gm_ref, wgm_ref), (yn_ref, wssd_ref), (xa_ref, wxa_ref))):
            gate = jax.nn.sigmoid(proj_ref[:, OFF_GATE + i * D_MODEL:OFF_GATE + (i + 1) * D_MODEL])
            term = gate * _dot(br_ref[...].astype(BF16), w_ref[...])
            merged = term if merged is None else merged + term
        h = x_ref[...] + _dot(merged.astype(BF16), wout_ref[...])
        hn_s[...] = _rms(h, nw_ref[...]).astype(BF16)
        acc_s[...] = h

    a = jnp.square(jnp.maximum(_dot(hn_s[...], wup_ref[...]), 0.0)).astype(BF16)
    acc_s[...] += _dot(a, wdn_ref[...])

    @pl.when(j == pl.num_programs(0) - 1)
    def _():
        o_ref[...] = _rms(acc_s[...], fw_ref[...])


def _sample_tail(x2d, proj, gm, yn, xa, p):
    rows = x2d.shape[0]
    head_args = [x2d, proj, gm, yn, xa, p["w_br_gm"], p["w_br_ssd"], p["w_br_xa"], p["w_out"], p["norm_ffn_w"]]
    return pl.pallas_call(
        _sample_tail_kernel,
        grid=(D_FF // FFN_COLS,),
        in_specs=[_full(a.shape) for a in head_args] + [
            pl.BlockSpec((D_MODEL, FFN_COLS), lambda j: (0, j)),
            pl.BlockSpec((FFN_COLS, D_MODEL), lambda j: (j, 0)),
            _full((1, D_MODEL)),
        ],
        out_specs=_full((rows, D_MODEL)),
        out_shape=jax.ShapeDtypeStruct((rows, D_MODEL), F32),
        scratch_shapes=[pltpu.VMEM((rows, D_MODEL), BF16), pltpu.VMEM((rows, D_MODEL), F32)],
        compiler_params=pltpu.CompilerParams(
            dimension_semantics=("arbitrary",), vmem_limit_bytes=VMEM_LIMIT_BYTES),
        name="sample_tail",
    )(*head_args, p["w_up"], p["w_down"], p["norm_final_w"])


def _pack_w_in_kernel(wt_ref, wa_ref, wdt_ref, wb_ref):
    rest = wb_ref.shape[1]
    for j in range(0, OFF_DT, PACK_COLS):
        wa_ref[:, j:j + PACK_COLS] = wt_ref[0, j:j + PACK_COLS, :].T.astype(BF16)
    dt_rows = wt_ref[0, OFF_DT:OFF_DT + DT_PAD, :]
    is_dt = lax.broadcasted_iota(jnp.int32, dt_rows.shape, 0) < SSD_HEADS
    wdt_ref[...] = jnp.where(is_dt, dt_rows, 0.0).T.astype(BF16)
    base = OFF_DT + SSD_HEADS
    for j in range(0, rest, PACK_COLS):
        wb_ref[:, j:j + PACK_COLS] = wt_ref[0, base + j:base + j + PACK_COLS, :].T.astype(BF16)


def _pack_w_in(w_in, l):
    d, width = w_in.shape[1:]
    rest = width - OFF_DT - SSD_HEADS
    assert rest == IN_PROJ_PACKED - OFF_Q and OFF_DT % PACK_COLS == 0 and rest % PACK_COLS == 0
    w_t = jnp.swapaxes(w_in, 1, 2)
    return pl.pallas_call(
        _pack_w_in_kernel,
        grid=(1,),
        in_specs=[pl.BlockSpec((1, width, d), lambda i: (l, 0, 0))],
        out_specs=[_full((d, OFF_DT)), _full((d, DT_PAD)), _full((d, rest))],
        out_shape=[jax.ShapeDtypeStruct((d, OFF_DT), BF16),
                   jax.ShapeDtypeStruct((d, DT_PAD), BF16),
                   jax.ShapeDtypeStruct((d, rest), BF16)],
        compiler_params=pltpu.CompilerParams(
            dimension_semantics=("arbitrary",), vmem_limit_bytes=VMEM_LIMIT_BYTES),
        name="pack_w_in",
    )(w_t)


def _prep_layer(l, norm_mix_w, w_in, gm_ln_w, gm_ln_b, gm_ws, gm_bs, conv_w, conv_b, dt_bias, a_log,
                d_skip, ssd_norm_w, mem_norm_w, w_mem_k, w_mem_v, w_br_gm, w_br_ssd, w_br_xa, w_out,
                norm_ffn_w, w_up, w_down, norm_final_w):
    row = lambda a: a.reshape(1, -1)
    pad_heads = lambda a: jnp.pad(a, (0, DT_PAD - SSD_HEADS)).reshape(1, DT_PAD)
    w_in_a, w_in_dt, w_in_b = _pack_w_in(w_in, l)
    return {
        "norm_mix_w": row(norm_mix_w[l]),
        "w_in_a": w_in_a, "w_in_dt": w_in_dt, "w_in_b": w_in_b,
        "gm_ln_w": row(gm_ln_w[l]), "gm_ln_b": row(gm_ln_b[l]),
        "gm_ws": gm_ws[l], "gm_bs": gm_bs[l], "gm_bs_t": gm_bs[l].T,
        "conv_w": conv_w[l], "conv_b": row(conv_b[l]),
        "dt_bias": pad_heads(dt_bias[l]), "a_log": pad_heads(a_log[l]),
        "d_skip": row(jnp.repeat(d_skip[l], SSD_HEAD_DIM)), "ssd_norm_w": row(ssd_norm_w[l]),
        "mem_norm_w": row(mem_norm_w[l]),
        "w_br_gm": w_br_gm[l].astype(BF16), "w_br_ssd": w_br_ssd[l].astype(BF16),
        "w_br_xa": w_br_xa[l].astype(BF16), "w_out": w_out[l].astype(BF16),
        "norm_ffn_w": row(norm_ffn_w[l]), "norm_final_w": row(norm_final_w),
    }


def kernel(x_prompt, x_sample, mem_prompt, cache_mem_k, cache_mem_v, state_conv, state_ssm, norm_mix_w, w_in, gm_ln_w, gm_ln_b, gm_ws, gm_bs, conv_w, conv_b, dt_bias, a_log, d_skip, ssd_norm_w, mem_norm_w, w_mem_k, w_mem_v, w_br_gm, w_br_ssd, w_br_xa, w_out, norm_ffn_w, w_up, w_down, norm_final_w):
    depth = w_in.shape[0]
    assert depth == 1, "the final norm is fused into the MLP kernel of the last (only) layer"
    b, seq, _ = x_prompt.shape
    n, dec_seq, _ = x_sample.shape
    assert dec_seq == 1
    p = _prep_layer(0, norm_mix_w, w_in, gm_ln_w, gm_ln_b, gm_ws, gm_bs, conv_w, conv_b, dt_bias,
                    a_log, d_skip, ssd_norm_w, mem_norm_w, w_mem_k, w_mem_v, w_br_gm, w_br_ssd,
                    w_br_xa, w_out, norm_ffn_w, w_up, w_down, norm_final_w)

    mem_k, mem_v = _memkv(mem_prompt.reshape(b * MEM_LEN, D_MODEL), p["mem_norm_w"], w_mem_k, w_mem_v, 0)
    h1, conv_p, ssm_p, p["w_up"], p["w_down"], k_rows, v_rows = _mixer(
        x_prompt, mem_k.reshape(b, MEM_LEN, XA_WIDTH), mem_v.reshape(b, MEM_LEN, XA_WIDTH), p, w_up, w_down, 0)
    y_prompt = _ffn(h1.reshape(b * seq, D_MODEL), p).reshape(b, seq, D_MODEL)

    xs2d = x_sample.reshape(n, D_MODEL)
    proj = _sample_proj(xs2d, p)
    gm, yn, xa, conv_s, ssm_s, gv = _sample_state(
        proj, state_conv[0].reshape(n, (SSD_CONV - 1) * SSD_CONV_DIM), state_ssm[0].reshape(n, SSD_INNER, SSD_STATE),
        cache_mem_k[0].reshape(n, MEM_LEN * XA_HEADS, XA_HEAD_DIM),
        cache_mem_v[0].reshape(n, MEM_LEN * XA_HEADS, XA_HEAD_DIM), p)
    y_sample = _sample_tail(xs2d, proj, gm, yn, xa, p).reshape(n, 1, D_MODEL)

    kv_shape = (1, b, MEM_LEN, XA_HEADS, XA_HEAD_DIM)
    state_shape = (SSD_HEADS, SSD_HEAD_DIM, SSD_STATE)
    return (y_prompt, y_sample,
            k_rows.reshape(kv_shape), v_rows.reshape(kv_shape),
            conv_p.reshape(1, b, SSD_CONV - 1, SSD_CONV_DIM), ssm_p.reshape((1, b) + state_shape),
            conv_s.reshape(1, n, SSD_CONV - 1, SSD_CONV_DIM), ssm_s.reshape((1, n) + state_shape),
            gv.reshape(1, n, 1, GM_WIDTH))
```
